```python
import math
import jax, jax.numpy as jnp
from jax import lax
import numpy as np

D_MODEL = 2048
BATCH = 16
SEQ = 256
DEPTH = 2
DEC_BATCH = 4
DEC_SEQ = 4096
PAST_LEN = 512

GRID_W = 64
N_HEADS = 8
HEAD_DIM = D_MODEL // (2 * N_HEADS)
V_DIM = 2 * HEAD_DIM
ROPE_BASE = 10000.0
Q_BLOCK = 128
N_POOL_GROUPS = 4
POOL_WINDOWS = (2, 4, 8, 16)
POOL_GROUP_DIM = D_MODEL // N_POOL_GROUPS
D_FF = 11 * D_MODEL // 4
N_EXPERTS = 8
TOP_K = 2
D_FF_EXPERT = D_FF // 2
N_EVEN = (DEPTH + 1) // 2
N_ODD = DEPTH // 2
EPS = 1e-6

kernel_name = "diffattn_pool_moe_context_prefix_trunk"

F32 = jnp.float32


def rms_norm(x, g):
    xf = x.astype(F32)
    y = xf * lax.rsqrt(jnp.mean(xf * xf, axis=-1, keepdims=True) + EPS)
    return (y * g.astype(F32)).astype(x.dtype)


def adaln_mod(cond, w, b):
    m = (jax.nn.silu(cond) @ w + b)[:, None, :]
    return jnp.split(m, 6, axis=-1)


def modulate(h, shift, scale):
    return h * (1 + scale) + shift


def axial_rope_tables(rows):
    r = jnp.repeat(jnp.arange(rows, dtype=F32), GRID_W)
    col = jnp.tile(jnp.arange(GRID_W, dtype=F32), rows)
    n_freq = HEAD_DIM // 4
    freqs = ROPE_BASE ** (-jnp.arange(n_freq, dtype=F32) / n_freq)
    ang = jnp.stack([r[:, None] * freqs, col[:, None] * freqs], axis=1)
    return jnp.cos(ang), jnp.sin(ang)


def apply_axial_rope(x, cos, sin):
    xr = x.astype(F32).reshape(*x.shape[:-1], 2, 2, HEAD_DIM // 4)
    x1, x2 = xr[..., 0, :], xr[..., 1, :]
    cs, sn = cos[:, None, None], sin[:, None, None]
    out = jnp.stack([x1 * cs - x2 * sn, x2 * cs + x1 * sn], axis=-2)
    return out.reshape(x.shape).astype(x.dtype)


def diff_qkv(h, w_qkv, q_g, k_g):
    b, l, _ = h.shape
    q, k, v = jnp.split(h @ w_qkv, 3, axis=-1)
    q = rms_norm(q.reshape(b, l, N_HEADS, 2, HEAD_DIM), q_g)
    k = rms_norm(k.reshape(b, l, N_HEADS, 2, HEAD_DIM), k_g)
    return q, k, v.reshape(b, l, N_HEADS, V_DIM)


def diff_lambda(lp, lambda_init):
    lp = lp.astype(F32)
    return jnp.exp(jnp.sum(lp[0] * lp[1])) - jnp.exp(jnp.sum(lp[2] * lp[3])) + lambda_init


def diff_attend(q, k, v, lam):
    s = jnp.einsum('bqhcd,bkhcd->bhcqk', q, k).astype(F32) * (HEAD_DIM ** -0.5)
    p = jax.nn.softmax(s, axis=-1)
    a = p[:, :, 0] - lam * p[:, :, 1]
    return jnp.einsum('bhqk,bkhe->bqhe', a.astype(v.dtype), v)


def diff_attend_blocked(q, k, v, lam):
    b, l = q.shape[:2]
    nb = l // Q_BLOCK
    qb = jnp.moveaxis(q.reshape(b, nb, Q_BLOCK, N_HEADS, 2, HEAD_DIM), 1, 0)
    ob = lax.map(lambda qq: diff_attend(qq, k, v, lam), qb)
    return jnp.moveaxis(ob, 0, 1).reshape(b, l, N_HEADS, V_DIM)


def diff_out(o, subln_g, lambda_init, w_o):
    b, l = o.shape[:2]
    o = rms_norm(o, subln_g) * (1.0 - lambda_init)
    return o.reshape(b, l, D_MODEL) @ w_o


def pool_mixer(h, w_groups, scale):
    b, l, _ = h.shape
    hf = h.astype(F32).reshape(b, l, N_POOL_GROUPS, POOL_GROUP_DIM)
    csum = jnp.concatenate([jnp.zeros((b, 1, N_POOL_GROUPS, POOL_GROUP_DIM), F32),
                            jnp.cumsum(hf, axis=1)], axis=1)
    t = jnp.arange(l)
    means = []
    for g, w in enumerate(POOL_WINDOWS):
        lo = jnp.clip(t - w // 2, 0, l - 1)
        hi = jnp.clip(t + w // 2 - 1, 0, l - 1)
        cnt = (hi - lo + 1).astype(F32)
        s = csum[:, hi + 1, g] - csum[:, lo, g]
        means.append(s / cnt[None, :, None])
    pooled = (jnp.stack(means, axis=2) - hf).astype(h.dtype)
    y = jnp.einsum('blgc,gce->blge', pooled, w_groups).reshape(b, l, D_MODEL)
    return y * scale


def swiglu(h, w_gu, w_down):
    g, u = jnp.split(h @ w_gu, 2, axis=-1)
    return (jax.nn.silu(g) * u) @ w_down


def moe_swiglu(h, w_router, b_router, w_gu, w_down):
    shp = h.shape
    t = h.reshape(-1, D_MODEL)
    logits = (t @ w_router).astype(F32) + b_router.astype(F32)
    top_v, top_i = lax.top_k(logits, TOP_K)
    gates = jax.nn.softmax(top_v, axis=-1)
    combine = jnp.einsum('tk,tke->te', gates, jax.nn.one_hot(top_i, N_EXPERTS, dtype=F32))
    out = jnp.zeros(t.shape, F32)
    for e in range(N_EXPERTS):
        out = out + combine[:, e:e + 1] * swiglu(t, w_gu[e], w_down[e]).astype(F32)
    return out.astype(h.dtype).reshape(shp)


def setup_inputs(seed: int = 0) -> dict:
    key = jax.random.key(seed)
    ks = jax.random.split(key, 24)

    def nrm(k, shape, s):
        return jax.random.normal(k, shape, F32) * s

    return {
        "x_prompt": nrm(ks[0], (BATCH, SEQ, D_MODEL), 1.0),
        "x_sample": nrm(ks[1], (DEC_BATCH, DEC_SEQ, D_MODEL), 1.0),
        "c": nrm(ks[2], (DEC_BATCH, D_MODEL), 1.0),
        "cache_k": nrm(ks[3], (DEC_BATCH, N_EVEN, PAST_LEN, N_HEADS, V_DIM), 1.0),
        "cache_v": nrm(ks[4], (DEC_BATCH, N_EVEN, PAST_LEN, N_HEADS, V_DIM), 1.0),
        "c_ctx": nrm(ks[5], (D_MODEL,), 1.0),
        "ada_w": nrm(ks[6], (DEPTH, D_MODEL, 6 * D_MODEL), 0.5 * D_MODEL ** -0.5),
        "ada_b": nrm(ks[7], (DEPTH, 6 * D_MODEL), 0.02),
        "norm1_g": 1.0 + nrm(ks[8], (DEPTH, D_MODEL), 0.02),
        "norm2_g": 1.0 + nrm(ks[9], (DEPTH, D_MODEL), 0.02),
        "attn_w_qkv": nrm(ks[10], (N_EVEN, D_MODEL, 3 * D_MODEL), D_MODEL ** -0.5),
        "attn_w_o": nrm(ks[11], (N_EVEN, D_MODEL, D_MODEL), D_MODEL ** -0.5),
        "attn_q_norm": 1.0 + nrm(ks[12], (N_EVEN, HEAD_DIM), 0.02),
        "attn_k_norm": 1.0 + nrm(ks[13], (N_EVEN, HEAD_DIM), 0.02),
        "attn_lambda": nrm(ks[14], (N_EVEN, 4, HEAD_DIM), 0.1),
        "attn_subln_g": 1.0 + nrm(ks[15], (N_EVEN, V_DIM), 0.02),
        "pool_w": nrm(ks[16], (N_ODD, N_POOL_GROUPS, POOL_GROUP_DIM, POOL_GROUP_DIM), POOL_GROUP_DIM ** -0.5),
        "pool_scale": 1.0 + nrm(ks[17], (N_ODD, D_MODEL), 0.1),
        "ffn_w_gu": nrm(ks[18], (N_EVEN, D_MODEL, 2 * D_FF), D_MODEL ** -0.5),
        "ffn_w_down": nrm(ks[19], (N_EVEN, D_FF, D_MODEL), D_FF ** -0.5),
        "moe_w_router": nrm(ks[20], (N_ODD, D_MODEL, N_EXPERTS), D_MODEL ** -0.5),
        "moe_b_router": nrm(ks[21], (N_ODD, N_EXPERTS), 0.01),
        "moe_w_gu": nrm(ks[22], (N_ODD, N_EXPERTS, D_MODEL, 2 * D_FF_EXPERT), D_MODEL ** -0.5),
        "moe_w_down": nrm(ks[23], (N_ODD, N_EXPERTS, D_FF_EXPERT, D_MODEL), D_FF_EXPERT ** -0.5),
    }


def reference(x_prompt, x_sample, c, cache_k, cache_v, c_ctx, ada_w, ada_b, norm1_g, norm2_g,
              attn_w_qkv, attn_w_o, attn_q_norm, attn_k_norm, attn_lambda, attn_subln_g,
              pool_w, pool_scale, ffn_w_gu, ffn_w_down,
              moe_w_router, moe_b_router, moe_w_gu, moe_w_down):
    rows = x_sample.shape[1] // GRID_W
    cos, sin = axial_rope_tables(rows)
    b_ctx, l_ctx = x_prompt.shape[:2]
    b_dec, l_past = cache_k.shape[0], cache_k.shape[2]
    xp, xs = x_prompt, x_sample
    new_k, new_v = [], []
    for i in range(DEPTH):
        j = i // 2
        p_sh1, p_sc1, p_g1, p_sh2, p_sc2, p_g2 = adaln_mod(c_ctx[None, :], ada_w[i], ada_b[i])
        s_sh1, s_sc1, s_g1, s_sh2, s_sc2, s_g2 = adaln_mod(c, ada_w[i], ada_b[i])
        hp = modulate(rms_norm(xp, norm1_g[i]), p_sh1, p_sc1)
        hs = modulate(rms_norm(xs, norm1_g[i]), s_sh1, s_sc1)
        if i % 2 == 0:
            lambda_init = 0.8 - 0.6 * math.exp(-0.3 * i)
            lam = diff_lambda(attn_lambda[j], lambda_init)
            qp, kp, vp = diff_qkv(hp, attn_w_qkv[j], attn_q_norm[j], attn_k_norm[j])
            mp = diff_out(diff_attend(qp, kp, vp, lam), attn_subln_g[j], lambda_init, attn_w_o[j])
            new_k.append(kp.reshape(b_ctx, l_ctx, N_HEADS, V_DIM))
            new_v.append(vp)
            qs, ks_, vs = diff_qkv(hs, attn_w_qkv[j], attn_q_norm[j], attn_k_norm[j])
            qs = apply_axial_rope(qs, cos, sin)
            ks_ = apply_axial_rope(ks_, cos, sin)
            k_all = jnp.concatenate(
                [cache_k[:, j].reshape(b_dec, l_past, N_HEADS, 2, HEAD_DIM).astype(ks_.dtype), ks_], axis=1)
            v_all = jnp.concatenate([cache_v[:, j].astype(vs.dtype), vs], axis=1)
            ms = diff_out(diff_attend_blocked(qs, k_all, v_all, lam), attn_subln_g[j], lambda_init, attn_w_o[j])
        else:
            mp = pool_mixer(hp, pool_w[j], pool_scale[j])
            ms = pool_mixer(hs, pool_w[j], pool_scale[j])
        xp = xp + p_g1 * mp
        xs = xs + s_g1 * ms
        hp = modulate(rms_norm(xp, norm2_g[i]), p_sh2, p_sc2)
        hs = modulate(rms_norm(xs, norm2_g[i]), s_sh2, s_sc2)
        if i % 2 == 0:
            fp = swiglu(hp, ffn_w_gu[j], ffn_w_down[j])
            fs = swiglu(hs, ffn_w_gu[j], ffn_w_down[j])
        else:
            fp = moe_swiglu(hp, moe_w_router[j], moe_b_router[j], moe_w_gu[j], moe_w_down[j])
            fs = moe_swiglu(hs, moe_w_router[j], moe_b_router[j], moe_w_gu[j], moe_w_down[j])
        xp = xp + p_g2 * fp
        xs = xs + s_g2 * fs
    state_k = jnp.stack(new_k, axis=1)
    state_v = jnp.stack(new_v, axis=1)
    return (xp, xs, state_k, state_v)
```

```python
import functools
import math

import jax
import jax.numpy as jnp
from jax import lax
from jax.experimental import pallas as pl
from jax.experimental.pallas import tpu as pltpu

F32 = jnp.float32
BF16 = jnp.bfloat16

GRID_W = 64
ROPE_BASE = 10000.0
POOL_WINDOWS = (2, 4, 8, 16)
TOP_K = 2
EPS = 1e-6
LOG2E = 1.4426950408889634

LANES = 128
VMEM_LIMIT = 52 * 2**20

ROUTE_CHUNK = 256
ROW_TILE = 256
GEMM_ROWS = 512


def _lambda_init(layer):
    return 0.8 - 0.6 * math.exp(-0.3 * layer)


def _tile(n, pref, mult):
    best = None
    t = mult
    while t <= min(n, pref):
        if n % t == 0:
            best = t
        t += mult
    if best is None:
        raise ValueError(f"no tile for {n} (multiple of {mult}, <= {pref})")
    return best


def _params(semantics):
    return pltpu.CompilerParams(dimension_semantics=semantics, vmem_limit_bytes=VMEM_LIMIT)


def _dot(a, b):
    return jnp.dot(a, b, preferred_element_type=F32)


def _mask_bf16(m):
    return jnp.where(m, 1.0, 0.0).astype(BF16)


def _split_bf16(x):
    hi = x.astype(BF16)
    lo = (x - hi.astype(F32)).astype(BF16)
    return hi, lo


def _norm_modulate(x, g, scale, shift):
    ms = jnp.mean(x * x, axis=-1, keepdims=True)
    return x * lax.rsqrt(ms + EPS) * (g * (1.0 + scale)) + shift


def _adaln_kernel(cond_ref, w_ref, b_ref, o_ref):
    c = cond_ref[...]
    s = c * jax.nn.sigmoid(c)
    s_hi, s_lo = _split_bf16(s)
    w_hi, w_lo = _split_bf16(w_ref[...])
    o_ref[...] = _dot(s_hi, w_hi) + _dot(s_lo, w_hi) + _dot(s_hi, w_lo) + b_ref[...]


def _adaln_mods(cond, ada_w, ada_b):
    depth, d, n = ada_w.shape
    rows = cond.shape[0]
    tn = _tile(n, 512, LANES)
    return pl.pallas_call(
        _adaln_kernel,
        grid=(depth, n // tn),
        in_specs=[
            pl.BlockSpec((rows, d), lambda l, j: (0, 0)),
            pl.BlockSpec((None, d, tn), lambda l, j: (l, 0, j)),
            pl.BlockSpec((None, 1, tn), lambda l, j: (l, 0, j)),
        ],
        out_specs=pl.BlockSpec((None, rows, tn), lambda l, j: (l, 0, j)),
        out_shape=jax.ShapeDtypeStruct((depth, rows, n), F32),
        compiler_params=_params(("arbitrary", "arbitrary")),
        name="adaln_mods",
    )(cond, ada_w, ada_b.reshape(depth, 1, n))


class _Mods:
    SHIFT1, SCALE1, GATE1, SHIFT2, SCALE2, GATE2 = range(6)

    def __init__(self, table, rows, layer):
        self.table = table
        self.rows = rows
        self.layer = layer

    def spec(self, which, row_fn, width=None, col_fn=None):
        d = self.table.shape[-1]
        width = d if width is None else width
        base = self.layer * self.rows

        def index(*ids):
            col = 0 if col_fn is None else col_fn(*ids)
            return ((base + row_fn(*ids)) * 6 + which, 0, col)

        return pl.BlockSpec((None, 1, width), index)


def _qkv_kernel(*refs, norm, rope, tn):
    x_ref, sh_ref, sc_ref, g_ref, w_ref = refs[:5]
    rest = list(refs[5:])
    gain_ref = rest.pop(0) if norm else None
    cos_ref, sin_up_ref, sin_dn_ref = (rest.pop(0), rest.pop(0), rest.pop(0)) if rope else (None,) * 3
    o_ref, h_scr = rest

    @pl.when(pl.program_id(1) == 0)
    def _():
        h = _norm_modulate(x_ref[...], g_ref[...], sc_ref[...], sh_ref[...])
        h_scr[...] = h.astype(BF16)

    acc = _dot(h_scr[...], w_ref[...])
    if not norm:
        o_ref[...] = acc.astype(o_ref.dtype)
        return
    for g in range(tn // LANES):
        y = acc[:, g * LANES:(g + 1) * LANES]
        y = y * lax.rsqrt(jnp.mean(y * y, axis=-1, keepdims=True) + EPS) * gain_ref[...]
        if rope:
            y = (y * cos_ref[...] + pltpu.roll(y, LANES - LANES // 4, 1) * sin_up_ref[...]
                 + pltpu.roll(y, LANES // 4, 1) * sin_dn_ref[...])
        o_ref[:, g * LANES:(g + 1) * LANES] = y.astype(o_ref.dtype)


def _qkv_section(x, mods, norm_g, w, section, gain, tables, *, cond_base, tok_per_cond, out_dtype):
    t, d = x.shape
    tm = _tile(math.gcd(t, tok_per_cond), 1024, 8)
    tn = _tile(d, 512, LANES)
    col0 = section * (d // tn)
    norm = gain is not None
    rope = tables is not None
    row = lambda i, j: cond_base + (i * tm) // tok_per_cond
    in_specs = [
        pl.BlockSpec((tm, d), lambda i, j: (i, 0)),
        mods.spec(_Mods.SHIFT1, row),
        mods.spec(_Mods.SCALE1, row),
        pl.BlockSpec((1, d), lambda i, j: (0, 0)),
        pl.BlockSpec((d, tn), lambda i, j: (0, col0 + j)),
    ]
    args = [x, mods.table, mods.table, norm_g, w]
    if norm:
        in_specs.append(pl.BlockSpec((1, LANES), lambda i, j: (0, 0)))
        args.append(gain)
    if rope:
        nblk = tables[0].shape[0] // tm
        for tab in tables:
            in_specs.append(pl.BlockSpec((tm, LANES), lambda i, j: (i % nblk, 0)))
            args.append(tab)
    return pl.pallas_call(
        functools.partial(_qkv_kernel, norm=norm, rope=rope, tn=tn),
        grid=(t // tm, d // tn),
        in_specs=in_specs,
        out_specs=pl.BlockSpec((tm, tn), lambda i, j: (i, j)),
        out_shape=jax.ShapeDtypeStruct((t, d), out_dtype),
        scratch_shapes=[pltpu.VMEM((tm, d), BF16)],
        compiler_params=_params(("arbitrary", "arbitrary")),
        name=f"qkv_section{section}",
    )(*args)


def _rope_tables(seq):
    n_freq = LANES // 4
    pos = jnp.arange(seq, dtype=jnp.int32)
    r = (pos // GRID_W).astype(F32)
    col = (pos % GRID_W).astype(F32)
    freqs = ROPE_BASE ** (-jnp.arange(n_freq, dtype=F32) / n_freq)
    zero = jnp.zeros((seq, n_freq), F32)
    cos_t, sin_up, sin_dn = [], [], []
    for p in (r, col):
        ang = p[:, None] * freqs
        cos, sin = jnp.cos(ang), jnp.sin(ang)
        cos_t += [cos, cos]
        sin_up += [-sin, zero]
        sin_dn += [zero, sin]
    return tuple(jnp.concatenate(t, axis=1) for t in (cos_t, sin_up, sin_dn))


def _attn_kernel(*refs, n_cache, n_chunks, tk, lambda_init):
    lam_ref, g_ref, q_ref = refs[:3]
    if n_cache:
        kc_ref, vc_ref, k_ref, v_ref, o_ref, acc_ref, m_ref, l_ref = refs[3:]
    else:
        k_ref, v_ref, o_ref, acc_ref, m_ref, l_ref = refs[3:]
    q = q_ref[...]
    qs = (q[:, :LANES], q[:, LANES:])
    m_ref[...] = jnp.full(m_ref.shape, -jnp.inf, F32)
    l_ref[...] = jnp.zeros(l_ref.shape, F32)
    acc_ref[...] = jnp.zeros(acc_ref.shape, F32)

    def update(kblk, vblk):
        for c in range(2):
            s = lax.dot_general(qs[c], kblk[:, c * LANES:(c + 1) * LANES],
                                (((1,), (1,)), ((), ())), preferred_element_type=F32)
            m_prev = m_ref[c]
            m_new = jnp.maximum(m_prev, jnp.max(s, axis=-1, keepdims=True))
            alpha = jnp.exp2(m_prev - m_new)
            p = jnp.exp2(s - m_new)
            l_ref[c] = alpha * l_ref[c] + jnp.sum(p, axis=-1, keepdims=True)
            acc_ref[c] = alpha * acc_ref[c] + _dot(p.astype(BF16), vblk)
            m_ref[c] = m_new

    if n_cache:
        update(kc_ref[...].astype(BF16), vc_ref[...].astype(BF16))

    def body(i, carry):
        off = pl.multiple_of(i * tk, tk)
        update(k_ref[pl.ds(off, tk), :].astype(BF16), v_ref[pl.ds(off, tk), :].astype(BF16))
        return carry

    lax.fori_loop(0, n_chunks, body, 0)

    lp = lam_ref[...]
    lam = (jnp.exp(jnp.sum(lp[0:1] * lp[1:2], axis=-1, keepdims=True))
           - jnp.exp(jnp.sum(lp[2:3] * lp[3:4], axis=-1, keepdims=True)) + lambda_init)
    o = acc_ref[0] / l_ref[0] - lam * (acc_ref[1] / l_ref[1])
    o = o * lax.rsqrt(jnp.mean(o * o, axis=-1, keepdims=True) + EPS) * g_ref[...]
    o_ref[...] = (o * (1.0 - lambda_init)).astype(o_ref.dtype)


def _diff_attention(q, k, v, cache_k, cache_v, lam_params, subln_g, *, batch, seq, layer):
    t, d = q.shape
    vd = subln_g.shape[-1]
    heads = d // vd
    n_cache = 0 if cache_k is None else cache_k.shape[0] // batch
    tq = _tile(seq, 512, 8)
    tk = _tile(seq, 512, 8)
    nq = seq // tq
    kv_spec = pl.BlockSpec((seq, vd), lambda b, h, i: (b, h))
    in_specs = [
        pl.BlockSpec(lam_params.shape, lambda b, h, i: (0, 0)),
        pl.BlockSpec((1, vd), lambda b, h, i: (0, 0)),
        pl.BlockSpec((tq, vd), lambda b, h, i: (b * nq + i, h)),
    ]
    args = [lam_params, subln_g, q]
    if n_cache:
        in_specs += [pl.BlockSpec((n_cache, vd), lambda b, h, i: (b, h))] * 2
        args += [cache_k, cache_v]
    in_specs += [kv_spec, kv_spec]
    args += [k, v]
    return pl.pallas_call(
        functools.partial(_attn_kernel, n_cache=n_cache, n_chunks=seq // tk, tk=tk,
                          lambda_init=_lambda_init(layer)),
        grid=(batch, heads, nq),
        in_specs=in_specs,
        out_specs=pl.BlockSpec((tq, vd), lambda b, h, i: (b * nq + i, h)),
        out_shape=jax.ShapeDtypeStruct((t, d), BF16),
        scratch_shapes=[pltpu.VMEM((2, tq, vd), F32), pltpu.VMEM((2, tq, 1), F32),
                        pltpu.VMEM((2, tq, 1), F32)],
        compiler_params=_params(("arbitrary", "arbitrary", "arbitrary")),
        name=f"diff_attention_{'latent' if n_cache else 'context'}",
    )(*args)


def _out_proj_kernel(*refs, aliased):
    o_ref, w_ref, x_ref, gate_ref = refs[:4]
    out_ref = refs[-1]
    out_ref[...] = x_ref[...] + gate_ref[...] * _dot(o_ref[...], w_ref[...])


def _out_proj_residual(o, w, x, mods, prev, *, total_rows, row_off, cond_base, tok_per_cond):
    t, d = x.shape
    tm = _tile(math.gcd(math.gcd(t, tok_per_cond), row_off or t), 1024, 8)
    tn = _tile(d, 512, LANES)
    blk_off = row_off // tm
    row = lambda i, j: cond_base + (i * tm) // tok_per_cond
    in_specs = [
        pl.BlockSpec((tm, d), lambda i, j: (i, 0)),
        pl.BlockSpec((d, tn), lambda i, j: (0, j)),
        pl.BlockSpec((tm, tn), lambda i, j: (i, j)),
        mods.spec(_Mods.GATE1, row, width=tn, col_fn=lambda i, j: j),
    ]
    args = [o, w, x, mods.table]
    aliases = {}
    if prev is not None:
        in_specs.append(pl.BlockSpec(memory_space=pl.ANY))
        args.append(prev)
        aliases = {4: 0}
    return pl.pallas_call(
        functools.partial(_out_proj_kernel, aliased=prev is not None),
        grid=(t // tm, d // tn),
        in_specs=in_specs,
        out_specs=pl.BlockSpec((tm, tn), lambda i, j: (blk_off + i, j)),
        out_shape=jax.ShapeDtypeStruct((total_rows, d), F32),
        input_output_aliases=aliases,
        compiler_params=_params(("arbitrary", "arbitrary")),
        name="attn_out_proj",
    )(*args)


def _ffn_kernel(x_ref, sh_ref, sc_ref, gate_ref, g_ref, wg_ref, wu_ref, wd_ref, o_ref, h_scr, acc_scr):
    f = pl.program_id(1)

    @pl.when(f == 0)
    def _():
        h = _norm_modulate(x_ref[...], g_ref[...], sc_ref[...], sh_ref[...])
        h_scr[...] = h.astype(BF16)

    h = h_scr[...]
    gte = _dot(h, wg_ref[...])
    up = _dot(h, wu_ref[...])
    act = (gte * jax.nn.sigmoid(gte) * up).astype(BF16)
    part = _dot(act, wd_ref[...])

    @pl.when(f == 0)
    def _():
        acc_scr[...] = part

    @pl.when(f > 0)
    def _():
        acc_scr[...] += part

    @pl.when(f == pl.num_programs(1) - 1)
    def _():
        o_ref[...] = x_ref[...] + gate_ref[...] * acc_scr[...]


def _cond_row_joint(tm, n_prompt, tok_per_cond):
    def row(i, *_):
        tok = i * tm
        return jnp.where(tok < n_prompt, 0, 1 + (tok - n_prompt) // tok_per_cond)
    return row


def _dense_ffn(x, mods, norm_g, w_gu, w_down, *, n_prompt, tok_per_cond):
    t, d = x.shape
    ff = w_down.shape[0]
    tm = _tile(math.gcd(n_prompt, tok_per_cond), 512, 8)
    tf = _tile(ff, 512, LANES)
    nf = ff // tf
    row = _cond_row_joint(tm, n_prompt, tok_per_cond)
    return pl.pallas_call(
        _ffn_kernel,
        grid=(t // tm, nf),
        in_specs=[
            pl.BlockSpec((tm, d), lambda i, f: (i, 0)),
            mods.spec(_Mods.SHIFT2, row),
            mods.spec(_Mods.SCALE2, row),
            mods.spec(_Mods.GATE2, row),
            pl.BlockSpec((1, d), lambda i, f: (0, 0)),
            pl.BlockSpec((d, tf), lambda i, f: (0, f)),
            pl.BlockSpec((d, tf), lambda i, f: (0, nf + f)),
            pl.BlockSpec((tf, d), lambda i, f: (f, 0)),
        ],
        out_specs=pl.BlockSpec((tm, d), lambda i, f: (i, 0)),
        out_shape=jax.ShapeDtypeStruct((t, d), F32),
        scratch_shapes=[pltpu.VMEM((tm, d), BF16), pltpu.VMEM((tm, d), F32)],
        compiler_params=_params(("arbitrary", "arbitrary")),
        name="dense_ffn",
    )(x, mods.table, mods.table, mods.table, norm_g, w_gu, w_gu, w_down)


def _pool_kernel(x_ref, xp_ref, xn_ref, sh_ref, sc_ref, gate_ref, g_ref, w_ref, ps_ref, o_ref, *,
                 tp, halo, n_prompt_tiles, prompt_tiles_per_seq, sample_tiles_per_seq, group_dim):
    i = pl.program_id(0)
    is_prompt = i < n_prompt_tiles
    tiles_per_seq = jnp.where(is_prompt, prompt_tiles_per_seq, sample_tiles_per_seq)
    local = lax.rem(jnp.where(is_prompt, i, i - n_prompt_tiles), tiles_per_seq)
    first = local == 0
    last = local == tiles_per_seq - 1
    seq_len = tiles_per_seq * tp

    g, sc, sh = g_ref[...], sc_ref[...], sh_ref[...]
    x = x_ref[...]
    h = _norm_modulate(x, g, sc, sh)
    h_prev = jnp.where(first, 0.0, _norm_modulate(xp_ref[...], g, sc, sh))
    h_next = jnp.where(last, 0.0, _norm_modulate(xn_ref[...], g, sc, sh))
    d = h.shape[-1]
    h_halo = jnp.concatenate([h_prev, h_next, jnp.zeros((LANES - 2 * halo, d), F32)], axis=0)

    t_idx = lax.broadcasted_iota(jnp.int32, (tp, tp), 0)
    s_idx = lax.broadcasted_iota(jnp.int32, (tp, tp), 1)
    t_h = lax.broadcasted_iota(jnp.int32, (tp, LANES), 0)
    u_h = lax.broadcasted_iota(jnp.int32, (tp, LANES), 1)
    off_h = jnp.where(u_h < halo, u_h - halo, tp + u_h - halo)
    pos = local * tp + lax.broadcasted_iota(jnp.int32, (tp, 1), 0)

    for grp, win in enumerate(POOL_WINDOWS):
        lo, hi = win // 2, win // 2 - 1
        cols = slice(grp * group_dim, (grp + 1) * group_dim)
        band = _mask_bf16((s_idx >= t_idx - lo) & (s_idx <= t_idx + hi))
        band_h = _mask_bf16((u_h < 2 * halo) & (off_h >= t_h - lo) & (off_h <= t_h + hi))
        c_hi, c_lo = _split_bf16(h[:, cols])
        e_hi, e_lo = _split_bf16(h_halo[:, cols])
        total = (_dot(band, c_hi) + _dot(band, c_lo)) + (_dot(band_h, e_hi) + _dot(band_h, e_lo))
        cnt = (jnp.minimum(pos + hi, seq_len - 1) - jnp.maximum(pos - lo, 0) + 1).astype(F32)
        pooled = (total / cnt - h[:, cols]).astype(BF16)
        y = _dot(pooled, w_ref[grp]) * ps_ref[:, cols]
        o_ref[:, cols] = x[:, cols] + gate_ref[:, cols] * y


def _pool_mixer(x, mods, norm_g, pool_w, pool_scale, *, n_prompt, prompt_seq, sample_seq):
    t, d = x.shape
    halo = max(POOL_WINDOWS) // 2
    tp = _tile(math.gcd(prompt_seq, sample_seq), 256, 8)
    n_groups, group_dim = pool_w.shape[0], pool_w.shape[1]
    hb = tp // halo
    n_halo_blocks = t // halo
    row = _cond_row_joint(tp, n_prompt, sample_seq)
    return pl.pallas_call(
        functools.partial(_pool_kernel, tp=tp, halo=halo, n_prompt_tiles=n_prompt // tp,
                          prompt_tiles_per_seq=prompt_seq // tp,
                          sample_tiles_per_seq=sample_seq // tp, group_dim=group_dim),
        grid=(t // tp,),
        in_specs=[
            pl.BlockSpec((tp, d), lambda i: (i, 0)),
            pl.BlockSpec((halo, d), lambda i: (jnp.maximum(i * hb - 1, 0), 0)),
            pl.BlockSpec((halo, d), lambda i: (jnp.minimum((i + 1) * hb, n_halo_blocks - 1), 0)),
            mods.spec(_Mods.SHIFT1, row),
            mods.spec(_Mods.SCALE1, row),
            mods.spec(_Mods.GATE1, row),
            pl.BlockSpec((1, d), lambda i: (0, 0)),
            pl.BlockSpec((n_groups, group_dim, group_dim), lambda i: (0, 0, 0)),
            pl.BlockSpec((1, d), lambda i: (0, 0)),
        ],
        out_specs=pl.BlockSpec((tp, d), lambda i: (i, 0)),
        out_shape=jax.ShapeDtypeStruct((t, d), F32),
        compiler_params=_params(("arbitrary",)),
        name="pool_mixer",
    )(x, x, x, mods.table, mods.table, mods.table, norm_g, pool_w, pool_scale)


def _router_kernel(x_ref, sh_ref, sc_ref, g_ref, wr_ref, br_ref, h_ref, route_ref, cnt_ref):
    h = _norm_modulate(x_ref[...], g_ref[...], sc_ref[...], sh_ref[...])
    h_ref[...] = h.astype(BF16)
    h_hi, h_lo = _split_bf16(h)
    w_hi, w_lo = _split_bf16(wr_ref[...])
    logits = _dot(h_hi, w_hi) + _dot(h_lo, w_hi) + _dot(h_hi, w_lo) + br_ref[...]

    rows = logits.shape[0]
    lane = lax.broadcasted_iota(jnp.int32, logits.shape, 1).astype(F32)
    m1 = jnp.max(logits, axis=-1, keepdims=True)
    i1 = jnp.min(jnp.where(logits == m1, lane, float(LANES)), axis=-1, keepdims=True)
    oh1 = lane == i1
    rest = jnp.where(oh1, -jnp.inf, logits)
    m2 = jnp.max(rest, axis=-1, keepdims=True)
    i2 = jnp.min(jnp.where(rest == m2, lane, float(LANES)), axis=-1, keepdims=True)
    oh2 = lane == i2
    e = jnp.exp(m2 - m1)
    gate_a = 1.0 / (1.0 + e)
    gate_b = e / (1.0 + e)

    sel = jnp.where(oh1 | oh2, 1.0, 0.0)
    r_idx = lax.broadcasted_iota(jnp.int32, (rows, rows), 0)
    c_idx = lax.broadcasted_iota(jnp.int32, (rows, rows), 1)
    earlier = _mask_bf16(c_idx < r_idx)
    rank = _dot(earlier, sel.astype(BF16))
    rank_a = jnp.sum(jnp.where(oh1, rank, 0.0), axis=-1, keepdims=True)
    rank_b = jnp.sum(jnp.where(oh2, rank, 0.0), axis=-1, keepdims=True)

    route = jnp.zeros(logits.shape, F32)
    for k, val in enumerate((i1, i2, rank_a, rank_b, gate_a, gate_b)):
        route = jnp.where(lane == float(k), val, route)
    route_ref[...] = route
    cnt_ref[...] = jnp.sum(sel, axis=0, keepdims=True)


def _moe_router(x, mods, norm_g, w_router, b_router, *, n_prompt, tok_per_cond):
    t, d = x.shape
    n_exp = w_router.shape[1]
    ch = ROUTE_CHUNK
    wr = jnp.zeros((d, LANES), F32).at[:, :n_exp].set(w_router.astype(F32))
    br = jnp.full((1, LANES), -jnp.inf, F32).at[0, :n_exp].set(b_router.astype(F32))
    row = _cond_row_joint(ch, n_prompt, tok_per_cond)
    return pl.pallas_call(
        _router_kernel,
        grid=(t // ch,),
        in_specs=[
            pl.BlockSpec((ch, d), lambda i: (i, 0)),
            mods.spec(_Mods.SHIFT2, row),
            mods.spec(_Mods.SCALE2, row),
            pl.BlockSpec((1, d), lambda i: (0, 0)),
            pl.BlockSpec((d, LANES), lambda i: (0, 0)),
            pl.BlockSpec((1, LANES), lambda i: (0, 0)),
        ],
        out_specs=[
            pl.BlockSpec((ch, d), lambda i: (i, 0)),
            pl.BlockSpec((ch, LANES), lambda i: (i, 0)),
            pl.BlockSpec((None, 1, LANES), lambda i: (i, 0, 0)),
        ],
        out_shape=[
            jax.ShapeDtypeStruct((t, d), BF16),
            jax.ShapeDtypeStruct((t, LANES), F32),
            jax.ShapeDtypeStruct((t // ch, 1, LANES), F32),
        ],
        compiler_params=_params(("arbitrary",)),
        name="moe_router",
    )(x, mods.table, mods.table, norm_g, wr, br)


def _dispatch_plan(route, counts, n_exp):
    ch, rt, gr = ROUTE_CHUNK, ROW_TILE, GEMM_ROWS
    t = route.shape[0]
    n_chunks = t // ch
    i32 = jnp.int32
    cnt = counts[:, 0, :n_exp].astype(i32)
    total = cnt.sum(0)
    padded = ((total + gr - 1) // gr) * gr
    off = jnp.cumsum(padded) - padded
    start = off[None, :] + jnp.cumsum(cnt, axis=0) - cnt

    e_a, e_b = route[:, 0].astype(i32), route[:, 1].astype(i32)
    chunk_of = jnp.arange(t, dtype=i32) // ch
    dest_a = start[chunk_of, e_a] + route[:, 2].astype(i32)
    dest_b = start[chunk_of, e_b] + route[:, 3].astype(i32)

    max_tiles = (t * TOP_K) // rt + n_exp * (gr // rt)
    n_gemm_tiles = (t * TOP_K) // gr + n_exp
    big = jnp.iinfo(i32).max

    seg_first = start // rt
    seg_last = (start + cnt - 1) // rt
    nonempty = cnt > 0
    c_ids = jnp.broadcast_to(jnp.arange(n_chunks, dtype=i32)[:, None], cnt.shape)
    cand_r = jnp.stack([seg_first, seg_last], axis=-1)
    cand_c = jnp.stack([c_ids, c_ids], axis=-1)
    cand_ok = jnp.stack([nonempty, nonempty & (seg_last > seg_first)], axis=-1)
    k = jnp.arange(gr // rt - 1, dtype=i32)[None, :] + 1
    pad_r = (off + padded)[:, None] // rt - k
    pad_ok = (padded[:, None] - k * rt >= total[:, None]) & (padded[:, None] > 0)

    def worklist(r, c, ok, by_chunk, size):
        r, c, ok = r.reshape(-1), c.reshape(-1), ok.reshape(-1)
        major, minor = (c, r) if by_chunk else (r, c)
        key = jnp.where(ok, major * (max_tiles + n_chunks + 1) + minor, big)
        order = jnp.argsort(key)[:size]
        n_ok = jnp.minimum(ok.sum(), size)
        last = order[jnp.maximum(n_ok - 1, 0)]
        idx = jnp.where(jnp.arange(size) < n_ok, order, last)
        valid = (jnp.arange(size) < n_ok).astype(i32)
        r, c = r[idx], c[idx]
        major = c if by_chunk else r
        first = jnp.concatenate([jnp.ones((1,), bool), major[1:] != major[:-1]]).astype(i32) * valid
        final = jnp.concatenate([major[1:] != major[:-1], jnp.ones((1,), bool)])
        final = (final | (jnp.arange(size) == n_ok - 1)).astype(i32) * valid
        return r.astype(i32), c.astype(i32), first, final, valid

    n_items = n_chunks * n_exp + max_tiles
    gather_r = jnp.concatenate([cand_r.reshape(-1), pad_r.reshape(-1)])
    gather_c = jnp.concatenate([cand_c.reshape(-1), jnp.zeros(pad_r.size, i32)])
    gather_ok = jnp.concatenate([cand_ok.reshape(-1), pad_ok.reshape(-1)])
    gather_list = worklist(gather_r, gather_c, gather_ok, False, n_items)
    combine_list = worklist(cand_r, cand_c, cand_ok, True, n_items)

    tile_lo = jnp.arange(n_gemm_tiles, dtype=i32) * gr
    used = tile_lo < (off + padded)[-1]
    ends = off + padded
    expert_of = jnp.minimum(jnp.sum(tile_lo[:, None] >= ends[None, :], axis=1), n_exp - 1).astype(i32)
    n_used = jnp.maximum(used.sum(), 1)
    clamp = jnp.minimum(jnp.arange(n_gemm_tiles, dtype=i32), n_used - 1).astype(i32)
    gemm_list = (clamp, expert_of[clamp], used.astype(i32))

    dest_rows = jnp.stack([dest_a.reshape(n_chunks, ch), dest_b.reshape(n_chunks, ch)], axis=1)
    tok_info = jnp.stack([dest_a.astype(F32), dest_b.astype(F32), route[:, 4], route[:, 5]], axis=1)
    tok_info = jnp.concatenate([tok_info, jnp.zeros((t, 4), F32)], axis=1)
    return dest_rows, tok_info, gather_list, combine_list, gemm_list, max_tiles * rt


def _gather_kernel(r_ref, c_ref, first_ref, final_ref, valid_ref, h_ref, dest_ref, o_ref):
    w = pl.program_id(0)

    @pl.when(valid_ref[w] == 1)
    def _():
        rt, ch = o_ref.shape[0], h_ref.shape[0]
        row = r_ref[w] * rt + lax.broadcasted_iota(jnp.int32, (rt, ch), 0)
        dest = dest_ref[...]
        onehot = _mask_bf16((dest[0:1, :] == row) | (dest[1:2, :] == row))
        part = _dot(onehot, h_ref[...]).astype(BF16)

        @pl.when(first_ref[w] == 1)
        def _():
            o_ref[...] = part

        @pl.when(first_ref[w] == 0)
        def _():
            o_ref[...] += part


def _moe_gather(h, dest_rows, worklist, n_rows):
    t, d = h.shape
    ch, rt = ROUTE_CHUNK, ROW_TILE
    n_items = worklist[0].shape[0]
    grid_spec = pltpu.PrefetchScalarGridSpec(
        num_scalar_prefetch=5,
        grid=(n_items,),
        in_specs=[
            pl.BlockSpec((ch, d), lambda w, r, c, *_: (c[w], 0)),
            pl.BlockSpec((None, 2, ch), lambda w, r, c, *_: (c[w], 0, 0)),
        ],
        out_specs=pl.BlockSpec((rt, d), lambda w, r, c, *_: (r[w], 0)),
    )
    return pl.pallas_call(
        _gather_kernel,
        grid_spec=grid_spec,
        out_shape=jax.ShapeDtypeStruct((n_rows, d), BF16),
        compiler_params=_params(("arbitrary",)),
        name="moe_gather",
    )(*worklist, h, dest_rows)


def _expert_kernel(blk_ref, exp_ref, used_ref, x_ref, wg_ref, wu_ref, wd_ref, o_ref, acc_scr):
    r, f = pl.program_id(0), pl.program_id(1)

    @pl.when(used_ref[r] == 1)
    def _():
        x = x_ref[...]
        gte = _dot(x, wg_ref[...])
        up = _dot(x, wu_ref[...])
        act = (gte * jax.nn.sigmoid(gte) * up).astype(BF16)
        part = _dot(act, wd_ref[...])

        @pl.when(f == 0)
        def _():
            acc_scr[...] = part

        @pl.when(f > 0)
        def _():
            acc_scr[...] += part

        @pl.when(f == pl.num_programs(1) - 1)
        def _():
            o_ref[...] = acc_scr[...].astype(o_ref.dtype)


def _moe_experts(xs, w_gu, w_down, gemm_list):
    n_rows, d = xs.shape
    ff = w_down.shape[1]
    gr = GEMM_ROWS
    tf = _tile(ff, 256, LANES)
    nf = ff // tf
    n_tiles = gemm_list[0].shape[0]

    def fcol(f, used, r):
        return jnp.where(used[r] == 1, f, nf - 1)

    grid_spec = pltpu.PrefetchScalarGridSpec(
        num_scalar_prefetch=3,
        grid=(n_tiles, nf),
        in_specs=[
            pl.BlockSpec((gr, d), lambda r, f, blk, ex, used: (blk[r], 0)),
            pl.BlockSpec((None, d, tf), lambda r, f, blk, ex, used: (ex[r], 0, fcol(f, used, r))),
            pl.BlockSpec((None, d, tf), lambda r, f, blk, ex, used: (ex[r], 0, nf + fcol(f, used, r))),
            pl.BlockSpec((None, tf, d), lambda r, f, blk, ex, used: (ex[r], fcol(f, used, r), 0)),
        ],
        out_specs=pl.BlockSpec((gr, d), lambda r, f, blk, ex, used: (blk[r], 0)),
        scratch_shapes=[pltpu.VMEM((gr, d), F32)],
    )
    return pl.pallas_call(
        _expert_kernel,
        grid_spec=grid_spec,
        out_shape=jax.ShapeDtypeStruct((n_rows, d), BF16),
        compiler_params=_params(("arbitrary", "arbitrary")),
        name="moe_experts",
    )(*gemm_list, xs, w_gu, w_gu, w_down)


def _combine_kernel(r_ref, c_ref, first_ref, final_ref, valid_ref, y_ref, info_ref, x_ref, gate_ref,
                    o_ref, acc_scr):
    w = pl.program_id(0)

    @pl.when(valid_ref[w] == 1)
    def _():
        rt, ch = y_ref.shape[0], x_ref.shape[0]
        info = info_ref[...]
        row = (r_ref[w] * rt + lax.broadcasted_iota(jnp.int32, (ch, rt), 1)).astype(F32)
        weights = (jnp.where(info[:, 0:1] == row, info[:, 2:3], 0.0)
                   + jnp.where(info[:, 1:2] == row, info[:, 3:4], 0.0)).astype(BF16)
        part = _dot(weights, y_ref[...])

        @pl.when(first_ref[w] == 1)
        def _():
            acc_scr[...] = part

        @pl.when(first_ref[w] == 0)
        def _():
            acc_scr[...] += part

        @pl.when(final_ref[w] == 1)
        def _():
            o_ref[...] = x_ref[...] + gate_ref[...] * acc_scr[...]


def _moe_combine(y, tok_info, x, mods, worklist, *, n_prompt, tok_per_cond):
    t, d = x.shape
    ch, rt = ROUTE_CHUNK, ROW_TILE
    n_items = worklist[0].shape[0]
    row_fn = _cond_row_joint(ch, n_prompt, tok_per_cond)
    gate = mods.spec(_Mods.GATE2, lambda w, r, c, *_: row_fn(c[w]))
    grid_spec = pltpu.PrefetchScalarGridSpec(
        num_scalar_prefetch=5,
        grid=(n_items,),
        in_specs=[
            pl.BlockSpec((rt, d), lambda w, r, c, *_: (r[w], 0)),
            pl.BlockSpec((ch, tok_info.shape[1]), lambda w, r, c, *_: (c[w], 0)),
            pl.BlockSpec((ch, d), lambda w, r, c, *_: (c[w], 0)),
            gate,
        ],
        out_specs=pl.BlockSpec((ch, d), lambda w, r, c, *_: (c[w], 0)),
        scratch_shapes=[pltpu.VMEM((ch, d), F32)],
    )
    return pl.pallas_call(
        _combine_kernel,
        grid_spec=grid_spec,
        out_shape=jax.ShapeDtypeStruct((t, d), F32),
        compiler_params=_params(("arbitrary",)),
        name="moe_combine",
    )(*worklist, y, tok_info, x, mods.table)


def kernel(x_prompt, x_sample, c, cache_k, cache_v, c_ctx, ada_w, ada_b, norm1_g, norm2_g,
           attn_w_qkv, attn_w_o, attn_q_norm, attn_k_norm, attn_lambda, attn_subln_g,
           pool_w, pool_scale, ffn_w_gu, ffn_w_down,
           moe_w_router, moe_b_router, moe_w_gu, moe_w_down):
    b_ctx, l_ctx, d = x_prompt.shape
    b_dec, l_dec, _ = x_sample.shape
    depth = ada_w.shape[0]
    n_even, l_past, heads, v_dim = cache_k.shape[1:]
    head_dim = v_dim // 2
    assert head_dim == LANES and depth == 2 and l_dec % GRID_W == 0
    n_exp = moe_w_router.shape[-1]
    tp_, ts_ = b_ctx * l_ctx, b_dec * l_dec
    t_all = tp_ + ts_

    cond_rows = 8 * ((1 + b_dec + 7) // 8)
    cond = jnp.zeros((cond_rows, d), F32).at[0].set(c_ctx).at[1:1 + b_dec].set(c)
    mod_table = _adaln_mods(cond, ada_w, ada_b).reshape(depth * cond_rows * 6, 1, d)

    xp = x_prompt.reshape(tp_, d)
    xs = x_sample.reshape(ts_, d)

    layer, j = 0, 0
    mods = _Mods(mod_table, cond_rows, layer)
    g1 = norm1_g[layer].reshape(1, d)
    w_qkv = attn_w_qkv[j].astype(BF16)
    q_gain = (attn_q_norm[j] * (head_dim ** -0.5 * LOG2E)).reshape(1, head_dim)
    k_gain = attn_k_norm[j].reshape(1, head_dim)
    tables = _rope_tables(l_dec)
    prompt = dict(cond_base=0, tok_per_cond=tp_)
    sample = dict(cond_base=1, tok_per_cond=l_dec)

    q_p = _qkv_section(xp, mods, g1, w_qkv, 0, q_gain, None, out_dtype=BF16, **prompt)
    k_p = _qkv_section(xp, mods, g1, w_qkv, 1, k_gain, None, out_dtype=F32, **prompt)
    v_p = _qkv_section(xp, mods, g1, w_qkv, 2, None, None, out_dtype=F32, **prompt)
    q_s = _qkv_section(xs, mods, g1, w_qkv, 0, q_gain, tables, out_dtype=BF16, **sample)
    k_s = _qkv_section(xs, mods, g1, w_qkv, 1, k_gain, tables, out_dtype=BF16, **sample)
    v_s = _qkv_section(xs, mods, g1, w_qkv, 2, None, None, out_dtype=BF16, **sample)

    lam_params = attn_lambda[j].astype(F32)
    subln = attn_subln_g[j].reshape(1, v_dim)
    o_p = _diff_attention(q_p, k_p, v_p, None, None, lam_params, subln,
                          batch=b_ctx, seq=l_ctx, layer=layer)
    ck = cache_k[:, j].reshape(b_dec * l_past, heads * v_dim)
    cv = cache_v[:, j].reshape(b_dec * l_past, heads * v_dim)
    o_s = _diff_attention(q_s, k_s, v_s, ck, cv, lam_params, subln,
                          batch=b_dec, seq=l_dec, layer=layer)

    w_o = attn_w_o[j].astype(BF16)
    x1 = _out_proj_residual(o_p, w_o, xp, mods, None, total_rows=t_all, row_off=0, **prompt)
    x1 = _out_proj_residual(o_s, w_o, xs, mods, x1, total_rows=t_all, row_off=tp_, **sample)

    x2 = _dense_ffn(x1, mods, norm2_g[layer].reshape(1, d), ffn_w_gu[j].astype(BF16),
                    ffn_w_down[j].astype(BF16), n_prompt=tp_, tok_per_cond=l_dec)

    layer, j = 1, 0
    mods = _Mods(mod_table, cond_rows, layer)
    x3 = _pool_mixer(x2, mods, norm1_g[layer].reshape(1, d), pool_w[j].astype(BF16),
                     pool_scale[j].reshape(1, d), n_prompt=tp_, prompt_seq=l_ctx, sample_seq=l_dec)

    h2, route, counts = _moe_router(x3, mods, norm2_g[layer].reshape(1, d), moe_w_router[j],
                                    moe_b_router[j], n_prompt=tp_, tok_per_cond=l_dec)
    dest_rows, tok_info, gather_list, combine_list, gemm_list, n_rows = _dispatch_plan(
        route, counts, n_exp)
    rows_sorted = _moe_gather(h2, dest_rows, gather_list, n_rows)
    y_sorted = _moe_experts(rows_sorted, moe_w_gu[j].astype(BF16), moe_w_down[j].astype(BF16),
                            gemm_list)
    x4 = _moe_combine(y_sorted, tok_info, x3, mods, combine_list, n_prompt=tp_, tok_per_cond=l_dec)

    y_prompt = x4[:tp_].reshape(b_ctx, l_ctx, d)
    y_sample = x4[tp_:].reshape(b_dec, l_dec, d)
    state_k = k_p.reshape(b_ctx, 1, l_ctx, heads, v_dim)
    state_v = v_p.reshape(b_ctx, 1, l_ctx, heads, v_dim)
    return (y_prompt, y_sample, state_k, state_v)
```

```python
import functools
import math

import jax
import jax.numpy as jnp
from jax import lax
from jax.experimental import pallas as pl
from jax.experimental.pallas import tpu as pltpu

F32 = jnp.float32
BF16 = jnp.bfloat16

GRID_W = 64
ROPE_BASE = 10000.0
POOL_WINDOWS = (2, 4, 8, 16)
TOP_K = 2
EPS = 1e-6
LOG2E = 1.4426950408889634

LANES = 128
VMEM_LIMIT = 52 * 2**20

ROUTE_CHUNK = 256
ROW_TILE = 256
GEMM_ROWS = 512


def _lambda_init(layer):
    return 0.8 - 0.6 * math.exp(-0.3 * layer)


def _tile(n, pref, mult):
    best = None
    t = mult
    while t <= min(n, pref):
        if n % t == 0:
            best = t
        t += mult
    if best is None:
        raise ValueError(f"no tile for {n} (multiple of {mult}, <= {pref})")
    return best


def _params(semantics):
    return pltpu.CompilerParams(dimension_semantics=semantics, vmem_limit_bytes=VMEM_LIMIT)


def _dot(a, b):
    return jnp.dot(a, b, preferred_element_type=F32)


def _mask_bf16(m):
    return jnp.where(m, 1.0, 0.0).astype(BF16)


def _split_bf16(x):
    hi = x.astype(BF16)
    lo = (x - hi.astype(F32)).astype(BF16)
    return hi, lo


def _norm_modulate(x, g, scale, shift):
    ms = jnp.mean(x * x, axis=-1, keepdims=True)
    return x * lax.rsqrt(ms + EPS) * (g * (1.0 + scale)) + shift


def _adaln_kernel(cond_ref, w_ref, b_ref, o_ref):
    c = cond_ref[...]
    s = c * jax.nn.sigmoid(c)
    s_hi, s_lo = _split_bf16(s)
    w_hi, w_lo = _split_bf16(w_ref[...])
    o_ref[...] = _dot(s_hi, w_hi) + _dot(s_lo, w_hi) + _dot(s_hi, w_lo) + b_ref[...]


def _adaln_mods(cond, ada_w, ada_b):
    depth, d, n = ada_w.shape
    rows = cond.shape[0]
    tn = _tile(n, 512, LANES)
    return pl.pallas_call(
        _adaln_kernel,
        grid=(depth, n // tn),
        in_specs=[
            pl.BlockSpec((rows, d), lambda l, j: (0, 0)),
            pl.BlockSpec((None, d, tn), lambda l, j: (l, 0, j)),
            pl.BlockSpec((None, 1, tn), lambda l, j: (l, 0, j)),
        ],
        out_specs=pl.BlockSpec((None, rows, tn), lambda l, j: (l, 0, j)),
        out_shape=jax.ShapeDtypeStruct((depth, rows, n), F32),
        compiler_params=_params(("arbitrary", "arbitrary")),
        name="adaln_mods",
    )(cond, ada_w, ada_b.reshape(depth, 1, n))


class _Mods:
    SHIFT1, SCALE1, GATE1, SHIFT2, SCALE2, GATE2 = range(6)

    def __init__(self, table, rows, layer):
        self.table = table
        self.rows = rows
        self.layer = layer

    def spec(self, which, row_fn, width=None, col_fn=None):
        d = self.table.shape[-1]
        width = d if width is None else width
        base = self.layer * self.rows

        def index(*ids):
            col = 0 if col_fn is None else col_fn(*ids)
            return ((base + row_fn(*ids)) * 6 + which, 0, col)

        return pl.BlockSpec((None, 1, width), index)


def _qkv_kernel(*refs, rope, tn, n_sec):
    x_ref, sh_ref, sc_ref, g_ref, w_ref, qg_ref, kg_ref = refs[:7]
    rest = list(refs[7:])
    cos_ref, sin_ref = (rest.pop(0), rest.pop(0)) if rope else (None, None)
    q_ref, k_ref, v_ref, h_scr = rest
    j = pl.program_id(1)

    @pl.when(j == 0)
    def _():
        h = _norm_modulate(x_ref[...], g_ref[...], sc_ref[...], sh_ref[...])
        h_scr[...] = h.astype(BF16)

    acc = _dot(h_scr[...], w_ref[...])

    def qk_epilogue(gain_ref, out_ref):
        r_idx = lax.broadcasted_iota(jnp.int32, (2 * LANES, 2 * LANES), 0)
        c_idx = lax.broadcasted_iota(jnp.int32, (2 * LANES, 2 * LANES), 1)
        if rope:
            swap = (r_idx >= LANES) & (c_idx >= LANES) & (r_idx - LANES == jnp.bitwise_xor(c_idx - LANES, LANES // 4))
            rhs = _mask_bf16(((r_idx < LANES) & (c_idx < LANES)) | swap)
        else:
            rhs = jnp.ones((2 * LANES, LANES), BF16)
        for g in range(tn // LANES):
            y = acc[:, g * LANES:(g + 1) * LANES]
            yg = y * gain_ref[...]
            if rope:
                res = _dot(jnp.concatenate([(y * y).astype(BF16), yg.astype(BF16)], axis=1), rhs)
                inv = lax.rsqrt(res[:, :LANES] * (1.0 / LANES) + EPS)
                out = (yg * cos_ref[...] + res[:, LANES:] * sin_ref[...]) * inv
            else:
                ss = _dot(jnp.concatenate(_split_bf16(y * y), axis=1), rhs)
                out = yg * lax.rsqrt(ss * (1.0 / LANES) + EPS)
            out_ref[:, g * LANES:(g + 1) * LANES] = out.astype(out_ref.dtype)

    @pl.when(j < n_sec)
    def _():
        qk_epilogue(qg_ref, q_ref)

    @pl.when((j >= n_sec) & (j < 2 * n_sec))
    def _():
        qk_epilogue(kg_ref, k_ref)

    @pl.when(j >= 2 * n_sec)
    def _():
        v_ref[...] = acc.astype(v_ref.dtype)


def _qkv_proj(x, mods, norm_g, w, q_gain, k_gain, tables, *, cond_base, tok_per_cond, kv_dtype):
    t, d = x.shape
    tm = _tile(math.gcd(t, tok_per_cond), 1024, 8)
    tn = _tile(d, 512, LANES)
    n_sec = d // tn
    rope = tables is not None
    row = lambda i, j: cond_base + (i * tm) // tok_per_cond
    in_specs = [
        pl.BlockSpec((tm, d), lambda i, j: (i, 0)),
        mods.spec(_Mods.SHIFT1, row),
        mods.spec(_Mods.SCALE1, row),
        pl.BlockSpec((1, d), lambda i, j: (0, 0)),
        pl.BlockSpec((d, tn), lambda i, j: (0, j)),
        pl.BlockSpec((1, LANES), lambda i, j: (0, 0)),
        pl.BlockSpec((1, LANES), lambda i, j: (0, 0)),
    ]
    args = [x, mods.table, mods.table, norm_g, w, q_gain, k_gain]
    if rope:
        nblk = tables[0].shape[0] // tm
        for tab in tables:
            in_specs.append(pl.BlockSpec((tm, LANES), lambda i, j: (i % nblk, 0)))
            args.append(tab)
    out_specs = [
        pl.BlockSpec((tm, tn), lambda i, j, s=s: (i, jnp.clip(j - s * n_sec, 0, n_sec - 1)))
        for s in range(3)
    ]
    return pl.pallas_call(
        functools.partial(_qkv_kernel, rope=rope, tn=tn, n_sec=n_sec),
        grid=(t // tm, 3 * n_sec),
        in_specs=in_specs,
        out_specs=out_specs,
        out_shape=[jax.ShapeDtypeStruct((t, d), BF16), jax.ShapeDtypeStruct((t, d), kv_dtype),
                   jax.ShapeDtypeStruct((t, d), kv_dtype)],
        scratch_shapes=[pltpu.VMEM((tm, d), BF16)],
        compiler_params=_params(("arbitrary", "arbitrary")),
        name="qkv_proj_rope" if rope else "qkv_proj",
    )(*args)


def _rope_tables(seq):
    n_freq = LANES // 4
    pos = jnp.arange(seq, dtype=jnp.int32)
    r = (pos // GRID_W).astype(F32)
    col = (pos % GRID_W).astype(F32)
    freqs = ROPE_BASE ** (-jnp.arange(n_freq, dtype=F32) / n_freq)
    cos_t, sin_t = [], []
    for p in (r, col):
        ang = p[:, None] * freqs
        cos, sin = jnp.cos(ang), jnp.sin(ang)
        cos_t += [cos, cos]
        sin_t += [-sin, sin]
    return jnp.concatenate(cos_t, axis=1), jnp.concatenate(sin_t, axis=1)


def _attn_kernel(*refs, n_cache, n_chunks, tk, lambda_init):
    lam_ref, g_ref, q_ref = refs[:3]
    if n_cache:
        kc_ref, vtc_ref, k_ref, vt_ref, o_ref, acc_ref = refs[3:]
    else:
        k_ref, vt_ref, o_ref, acc_ref = refs[3:]
    q = q_ref[...]
    qs = (q[:, :LANES], q[:, LANES:])

    def scores(kblk):
        return [lax.dot_general(kblk[:, c * LANES:(c + 1) * LANES], qs[c],
                                (((1,), (1,)), ((), ())), preferred_element_type=F32)
                for c in range(2)]

    def accumulate(sts, vt, stats):
        out = []
        for c in range(2):
            st = sts[c]
            m_cur = jnp.max(st, axis=0, keepdims=True)
            if stats is None:
                m_new = m_cur
                p = jnp.exp2(st - m_new)
                l_new = jnp.sum(p, axis=0, keepdims=True)
                acc_ref[c] = _dot(vt, p.astype(BF16))
            else:
                m_prev, l_prev = stats[c]
                m_new = jnp.maximum(m_prev, m_cur)
                alpha = jnp.exp2(m_prev - m_new)
                p = jnp.exp2(st - m_new)
                l_new = alpha * l_prev + jnp.sum(p, axis=0, keepdims=True)
                acc_ref[c] = alpha * acc_ref[c] + _dot(vt, p.astype(BF16))
            out.append((m_new, l_new))
        return out

    chunks = []
    if n_cache:
        chunks.append((lambda: kc_ref[...].astype(BF16), lambda: vtc_ref[...]))
    for i in range(n_chunks):
        chunks.append((lambda i=i: k_ref[i * tk:(i + 1) * tk, :].astype(BF16), lambda i=i: vt_ref[i]))
    stats = None
    sts = scores(chunks[0][0]())
    for n, (_, values) in enumerate(chunks):
        nxt = scores(chunks[n + 1][0]()) if n + 1 < len(chunks) else None
        stats = accumulate(sts, values(), stats)
        sts = nxt

    lp = lam_ref[...]
    lam = (jnp.exp(jnp.sum(lp[0:1] * lp[1:2], axis=-1, keepdims=True))
           - jnp.exp(jnp.sum(lp[2:3] * lp[3:4], axis=-1, keepdims=True)) + lambda_init)
    o_t = acc_ref[0] / stats[0][1] - lam * (acc_ref[1] / stats[1][1])
    o = o_t.T
    o = o * lax.rsqrt(jnp.mean(o * o, axis=-1, keepdims=True) + EPS) * g_ref[...]
    o_ref[...] = (o * (1.0 - lambda_init)).astype(o_ref.dtype)


def _chunked_transpose(v, batch, seq, heads, vd, tk):
    v = v.astype(BF16).reshape(batch, seq // tk, tk, heads, vd)
    return v.transpose(0, 3, 1, 4, 2).reshape(batch * heads, seq // tk, vd, tk)


def _diff_attention(q, k, v, cache_k, cache_v, lam_params, subln_g, *, batch, seq, layer):
    t, d = q.shape
    vd = subln_g.shape[-1]
    heads = d // vd
    n_cache = 0 if cache_k is None else cache_k.shape[0] // batch
    tq = _tile(seq, 512, LANES)
    tk = _tile(seq, 512, LANES)
    nq = seq // tq
    n_chunks = seq // tk
    in_specs = [
        pl.BlockSpec(lam_params.shape, lambda b, h, i: (0, 0)),
        pl.BlockSpec((1, vd), lambda b, h, i: (0, 0)),
        pl.BlockSpec((tq, vd), lambda b, h, i: (b * nq + i, h)),
    ]
    args = [lam_params, subln_g, q]
    if n_cache:
        in_specs += [pl.BlockSpec((n_cache, vd), lambda b, h, i: (b, h)),
                     pl.BlockSpec((None, None, vd, n_cache), lambda b, h, i: (b * heads + h, 0, 0, 0))]
        args += [cache_k, _chunked_transpose(cache_v, batch, n_cache, heads, vd, n_cache)]
    in_specs += [pl.BlockSpec((seq, vd), lambda b, h, i: (b, h)),
                 pl.BlockSpec((None, n_chunks, vd, tk), lambda b, h, i: (b * heads + h, 0, 0, 0))]
    args += [k, _chunked_transpose(v, batch, seq, heads, vd, tk)]
    return pl.pallas_call(
        functools.partial(_attn_kernel, n_cache=n_cache, n_chunks=n_chunks, tk=tk,
                          lambda_init=_lambda_init(layer)),
        grid=(batch, heads, nq),
        in_specs=in_specs,
        out_specs=pl.BlockSpec((tq, vd), lambda b, h, i: (b * nq + i, h)),
        out_shape=jax.ShapeDtypeStruct((t, d), BF16),
        scratch_shapes=[pltpu.VMEM((2, vd, tq), F32)],
        compiler_params=_params(("arbitrary", "arbitrary", "arbitrary")),
        name=f"diff_attention_{'latent' if n_cache else 'context'}",
    )(*args)


def _out_proj_kernel(*refs, aliased):
    o_ref, w_ref, x_ref, gate_ref = refs[:4]
    out_ref = refs[-1]
    out_ref[...] = x_ref[...] + gate_ref[...] * _dot(o_ref[...], w_ref[...])


def _out_proj_residual(o, w, x, mods, prev, *, total_rows, row_off, cond_base, tok_per_cond):
    t, d = x.shape
    tm = _tile(math.gcd(math.gcd(t, tok_per_cond), row_off or t), 1024, 8)
    tn = _tile(d, 512, LANES)
    blk_off = row_off // tm
    row = lambda i, j: cond_base + (i * tm) // tok_per_cond
    in_specs = [
        pl.BlockSpec((tm, d), lambda i, j: (i, 0)),
        pl.BlockSpec((d, tn), lambda i, j: (0, j)),
        pl.BlockSpec((tm, tn), lambda i, j: (i, j)),
        mods.spec(_Mods.GATE1, row, width=tn, col_fn=lambda i, j: j),
    ]
    args = [o, w, x, mods.table]
    aliases = {}
    if prev is not None:
        in_specs.append(pl.BlockSpec(memory_space=pl.ANY))
        args.append(prev)
        aliases = {4: 0}
    return pl.pallas_call(
        functools.partial(_out_proj_kernel, aliased=prev is not None),
        grid=(t // tm, d // tn),
        in_specs=in_specs,
        out_specs=pl.BlockSpec((tm, tn), lambda i, j: (blk_off + i, j)),
        out_shape=jax.ShapeDtypeStruct((total_rows, d), F32),
        input_output_aliases=aliases,
        compiler_params=_params(("arbitrary", "arbitrary")),
        name="attn_out_proj",
    )(*args)


def _ffn_kernel(x_ref, sh_ref, sc_ref, gate_ref, g_ref, wg_ref, wu_ref, wd_ref, o_ref, h_scr, acc_scr):
    f = pl.program_id(1)

    @pl.when(f == 0)
    def _():
        h = _norm_modulate(x_ref[...], g_ref[...], sc_ref[...], sh_ref[...])
        h_scr[...] = h.astype(BF16)

    h = h_scr[...]
    gte = _dot(h, wg_ref[...])
    up = _dot(h, wu_ref[...])
    act = (gte * jax.nn.sigmoid(gte) * up).astype(BF16)
    part = _dot(act, wd_ref[...])

    @pl.when(f == 0)
    def _():
        acc_scr[...] = part

    @pl.when(f > 0)
    def _():
        acc_scr[...] += part

    @pl.when(f == pl.num_programs(1) - 1)
    def _():
        o_ref[...] = x_ref[...] + gate_ref[...] * acc_scr[...]


def _cond_row_joint(tm, n_prompt, tok_per_cond):
    def row(i, *_):
        tok = i * tm
        return jnp.where(tok < n_prompt, 0, 1 + (tok - n_prompt) // tok_per_cond)
    return row


def _dense_ffn(x, mods, norm_g, w_gu, w_down, *, n_prompt, tok_per_cond):
    t, d = x.shape
    ff = w_down.shape[0]
    tm = _tile(math.gcd(n_prompt, tok_per_cond), 512, 8)
    tf = _tile(ff, 512, LANES)
    nf = ff // tf
    row = _cond_row_joint(tm, n_prompt, tok_per_cond)
    return pl.pallas_call(
        _ffn_kernel,
        grid=(t // tm, nf),
        in_specs=[
            pl.BlockSpec((tm, d), lambda i, f: (i, 0)),
            mods.spec(_Mods.SHIFT2, row),
            mods.spec(_Mods.SCALE2, row),
            mods.spec(_Mods.GATE2, row),
            pl.BlockSpec((1, d), lambda i, f: (0, 0)),
            pl.BlockSpec((d, tf), lambda i, f: (0, f)),
            pl.BlockSpec((d, tf), lambda i, f: (0, nf + f)),
            pl.BlockSpec((tf, d), lambda i, f: (f, 0)),
        ],
        out_specs=pl.BlockSpec((tm, d), lambda i, f: (i, 0)),
        out_shape=jax.ShapeDtypeStruct((t, d), F32),
        scratch_shapes=[pltpu.VMEM((tm, d), BF16), pltpu.VMEM((tm, d), F32)],
        compiler_params=_params(("arbitrary", "arbitrary")),
        name="dense_ffn",
    )(x, mods.table, mods.table, mods.table, norm_g, w_gu, w_gu, w_down)


def _pool_kernel(x_ref, xp_ref, xn_ref, sh_ref, sc_ref, gate_ref, g_ref, w_ref, ps_ref, o_ref, *,
                 tp, halo, n_prompt_tiles, prompt_tiles_per_seq, sample_tiles_per_seq, group_dim):
    i = pl.program_id(0)
    is_prompt = i < n_prompt_tiles
    tiles_per_seq = jnp.where(is_prompt, prompt_tiles_per_seq, sample_tiles_per_seq)
    local = lax.rem(jnp.where(is_prompt, i, i - n_prompt_tiles), tiles_per_seq)
    first = local == 0
    last = local == tiles_per_seq - 1
    seq_len = tiles_per_seq * tp

    g, sc, sh = g_ref[...], sc_ref[...], sh_ref[...]
    x = x_ref[...]
    h = _norm_modulate(x, g, sc, sh)
    h_prev = jnp.where(first, 0.0, _norm_modulate(xp_ref[...], g, sc, sh))
    h_next = jnp.where(last, 0.0, _norm_modulate(xn_ref[...], g, sc, sh))
    d = h.shape[-1]
    h_halo = jnp.concatenate([h_prev, h_next, jnp.zeros((LANES - 2 * halo, d), F32)], axis=0)

    t_idx = lax.broadcasted_iota(jnp.int32, (tp, tp), 0)
    s_idx = lax.broadcasted_iota(jnp.int32, (tp, tp), 1)
    t_h = lax.broadcasted_iota(jnp.int32, (tp, LANES), 0)
    u_h = lax.broadcasted_iota(jnp.int32, (tp, LANES), 1)
    off_h = jnp.where(u_h < halo, u_h - halo, tp + u_h - halo)
    pos = local * tp + lax.broadcasted_iota(jnp.int32, (tp, 1), 0)

    for grp, win in enumerate(POOL_WINDOWS):
        lo, hi = win // 2, win // 2 - 1
        cols = slice(grp * group_dim, (grp + 1) * group_dim)
        band = _mask_bf16((s_idx >= t_idx - lo) & (s_idx <= t_idx + hi))
        band_h = _mask_bf16((u_h < 2 * halo) & (off_h >= t_h - lo) & (off_h <= t_h + hi))
        c_hi, c_lo = _split_bf16(h[:, cols])
        e_hi, e_lo = _split_bf16(h_halo[:, cols])
        total = (_dot(band, c_hi) + _dot(band, c_lo)) + (_dot(band_h, e_hi) + _dot(band_h, e_lo))
        cnt = (jnp.minimum(pos + hi, seq_len - 1) - jnp.maximum(pos - lo, 0) + 1).astype(F32)
        pooled = (total / cnt - h[:, cols]).astype(BF16)
        y = _dot(pooled, w_ref[grp]) * ps_ref[:, cols]
        o_ref[:, cols] = x[:, cols] + gate_ref[:, cols] * y


def _pool_mixer(x, mods, norm_g, pool_w, pool_scale, *, n_prompt, prompt_seq, sample_seq):
    t, d = x.shape
    halo = max(POOL_WINDOWS) // 2
    tp = _tile(math.gcd(prompt_seq, sample_seq), 256, 8)
    n_groups, group_dim = pool_w.shape[0], pool_w.shape[1]
    hb = tp // halo
    n_halo_blocks = t // halo
    row = _cond_row_joint(tp, n_prompt, sample_seq)
    return pl.pallas_call(
        functools.partial(_pool_kernel, tp=tp, halo=halo, n_prompt_tiles=n_prompt // tp,
                          prompt_tiles_per_seq=prompt_seq // tp,
                          sample_tiles_per_seq=sample_seq // tp, group_dim=group_dim),
        grid=(t // tp,),
        in_specs=[
            pl.BlockSpec((tp, d), lambda i: (i, 0)),
            pl.BlockSpec((halo, d), lambda i: (jnp.maximum(i * hb - 1, 0), 0)),
            pl.BlockSpec((halo, d), lambda i: (jnp.minimum((i + 1) * hb, n_halo_blocks - 1), 0)),
            mods.spec(_Mods.SHIFT1, row),
            mods.spec(_Mods.SCALE1, row),
            mods.spec(_Mods.GATE1, row),
            pl.BlockSpec((1, d), lambda i: (0, 0)),
            pl.BlockSpec((n_groups, group_dim, group_dim), lambda i: (0, 0, 0)),
            pl.BlockSpec((1, d), lambda i: (0, 0)),
        ],
        out_specs=pl.BlockSpec((tp, d), lambda i: (i, 0)),
        out_shape=jax.ShapeDtypeStruct((t, d), F32),
        compiler_params=_params(("arbitrary",)),
        name="pool_mixer",
    )(x, x, x, mods.table, mods.table, mods.table, norm_g, pool_w, pool_scale)


def _router_kernel(x_ref, sh_ref, sc_ref, g_ref, wr_ref, br_ref, h_ref, route_ref, cnt_ref):
    h = _norm_modulate(x_ref[...], g_ref[...], sc_ref[...], sh_ref[...])
    h_ref[...] = h.astype(BF16)
    h_hi, h_lo = _split_bf16(h)
    w_hi, w_lo = _split_bf16(wr_ref[...])
    logits = _dot(h_hi, w_hi) + _dot(h_lo, w_hi) + _dot(h_hi, w_lo) + br_ref[...]

    rows = logits.shape[0]
    lane = lax.broadcasted_iota(jnp.int32, logits.shape, 1).astype(F32)
    m1 = jnp.max(logits, axis=-1, keepdims=True)
    i1 = jnp.min(jnp.where(logits == m1, lane, float(LANES)), axis=-1, keepdims=True)
    oh1 = lane == i1
    rest = jnp.where(oh1, -jnp.inf, logits)
    m2 = jnp.max(rest, axis=-1, keepdims=True)
    i2 = jnp.min(jnp.where(rest == m2, lane, float(LANES)), axis=-1, keepdims=True)
    oh2 = lane == i2
    e = jnp.exp(m2 - m1)
    gate_a = 1.0 / (1.0 + e)
    gate_b = e / (1.0 + e)

    sel = jnp.where(oh1 | oh2, 1.0, 0.0)
    r_idx = lax.broadcasted_iota(jnp.int32, (rows, rows), 0)
    c_idx = lax.broadcasted_iota(jnp.int32, (rows, rows), 1)
    earlier = _mask_bf16(c_idx < r_idx)
    rank = _dot(earlier, sel.astype(BF16))
    rank_a = jnp.sum(jnp.where(oh1, rank, 0.0), axis=-1, keepdims=True)
    rank_b = jnp.sum(jnp.where(oh2, rank, 0.0), axis=-1, keepdims=True)

    route = jnp.zeros(logits.shape, F32)
    for k, val in enumerate((i1, i2, rank_a, rank_b, gate_a, gate_b)):
        route = jnp.where(lane == float(k), val, route)
    route_ref[...] = route
    cnt_ref[...] = jnp.sum(sel, axis=0, keepdims=True)


def _moe_router(x, mods, norm_g, w_router, b_router, *, n_prompt, tok_per_cond):
    t, d = x.shape
    n_exp = w_router.shape[1]
    ch = ROUTE_CHUNK
    wr = jnp.zeros((d, LANES), F32).at[:, :n_exp].set(w_router.astype(F32))
    br = jnp.full((1, LANES), -jnp.inf, F32).at[0, :n_exp].set(b_router.astype(F32))
    row = _cond_row_joint(ch, n_prompt, tok_per_cond)
    return pl.pallas_call(
        _router_kernel,
        grid=(t // ch,),
        in_specs=[
            pl.BlockSpec((ch, d), lambda i: (i, 0)),
            mods.spec(_Mods.SHIFT2, row),
            mods.spec(_Mods.SCALE2, row),
            pl.BlockSpec((1, d), lambda i: (0, 0)),
            pl.BlockSpec((d, LANES), lambda i: (0, 0)),
            pl.BlockSpec((1, LANES), lambda i: (0, 0)),
        ],
        out_specs=[
            pl.BlockSpec((ch, d), lambda i: (i, 0)),
            pl.BlockSpec((ch, LANES), lambda i: (i, 0)),
            pl.BlockSpec((None, 1, LANES), lambda i: (i, 0, 0)),
        ],
        out_shape=[
            jax.ShapeDtypeStruct((t, d), BF16),
            jax.ShapeDtypeStruct((t, LANES), F32),
            jax.ShapeDtypeStruct((t // ch, 1, LANES), F32),
        ],
        compiler_params=_params(("arbitrary",)),
        name="moe_router",
    )(x, mods.table, mods.table, norm_g, wr, br)


def _dispatch_plan(route, counts, n_exp):
    ch, rt, gr = ROUTE_CHUNK, ROW_TILE, GEMM_ROWS
    t = route.shape[0]
    n_chunks = t // ch
    i32 = jnp.int32
    cnt = counts[:, 0, :n_exp].astype(i32)
    total = cnt.sum(0)
    padded = ((total + gr - 1) // gr) * gr
    off = jnp.cumsum(padded) - padded
    start = off[None, :] + jnp.cumsum(cnt, axis=0) - cnt

    def dest(choice):
        e = route[:, choice].astype(i32).reshape(n_chunks, ch, 1)
        hit = e == jnp.arange(n_exp, dtype=i32)
        base = jnp.sum(jnp.where(hit, start[:, None, :], 0), axis=-1)
        return base.reshape(t) + route[:, 2 + choice].astype(i32)

    dest_a, dest_b = dest(0), dest(1)

    max_tiles = (t * TOP_K) // rt + n_exp * (gr // rt)
    n_gemm_tiles = (t * TOP_K) // gr + n_exp
    big = jnp.iinfo(i32).max

    seg_first = start // rt
    seg_last = (start + cnt - 1) // rt
    nonempty = cnt > 0
    c_ids = jnp.broadcast_to(jnp.arange(n_chunks, dtype=i32)[:, None], cnt.shape)
    cand_r = jnp.stack([seg_first, seg_last], axis=-1)
    cand_c = jnp.stack([c_ids, c_ids], axis=-1)
    cand_ok = jnp.stack([nonempty, nonempty & (seg_last > seg_first)], axis=-1)
    k = jnp.arange(gr // rt - 1, dtype=i32)[None, :] + 1
    pad_r = (off + padded)[:, None] // rt - k
    pad_ok = (padded[:, None] - k * rt >= total[:, None]) & (padded[:, None] > 0)

    def worklist(r, c, ok, by_chunk, size):
        r, c, ok = r.reshape(-1), c.reshape(-1), ok.reshape(-1)
        major, minor = (c, r) if by_chunk else (r, c)
        key = jnp.where(ok, major * (max_tiles + n_chunks + 1) + minor, big)
        order = jnp.argsort(key)[:size]
        n_ok = jnp.minimum(ok.sum(), size)
        last = order[jnp.maximum(n_ok - 1, 0)]
        idx = jnp.where(jnp.arange(size) < n_ok, order, last)
        valid = (jnp.arange(size) < n_ok).astype(i32)
        r, c = r[idx], c[idx]
        major = c if by_chunk else r
        first = jnp.concatenate([jnp.ones((1,), bool), major[1:] != major[:-1]]).astype(i32) * valid
        final = jnp.concatenate([major[1:] != major[:-1], jnp.ones((1,), bool)])
        final = (final | (jnp.arange(size) == n_ok - 1)).astype(i32) * valid
        return r.astype(i32), c.astype(i32), first, final, valid

    n_items = n_chunks * n_exp + max_tiles
    gather_r = jnp.concatenate([cand_r.reshape(-1), pad_r.reshape(-1)])
    gather_c = jnp.concatenate([cand_c.reshape(-1), jnp.zeros(pad_r.size, i32)])
    gather_ok = jnp.concatenate([cand_ok.reshape(-1), pad_ok.reshape(-1)])
    gather_list = worklist(gather_r, gather_c, gather_ok, False, n_items)
    combine_list = worklist(cand_r, cand_c, cand_ok, True, n_items)

    tile_lo = jnp.arange(n_gemm_tiles, dtype=i32) * gr
    used = tile_lo < (off + padded)[-1]
    ends = off + padded
    expert_of = jnp.minimum(jnp.sum(tile_lo[:, None] >= ends[None, :], axis=1), n_exp - 1).astype(i32)
    n_used = jnp.maximum(used.sum(), 1)
    clamp = jnp.minimum(jnp.arange(n_gemm_tiles, dtype=i32), n_used - 1).astype(i32)
    gemm_list = (clamp, expert_of[clamp], used.astype(i32))

    dest_rows = jnp.stack([dest_a.reshape(n_chunks, ch), dest_b.reshape(n_chunks, ch)], axis=1)
    tok_info = jnp.stack([dest_a.astype(F32), dest_b.astype(F32), route[:, 4], route[:, 5]], axis=1)
    tok_info = jnp.concatenate([tok_info, jnp.zeros((t, 4), F32)], axis=1)
    return dest_rows, tok_info, gather_list, combine_list, gemm_list, max_tiles * rt


def _gather_kernel(r_ref, c_ref, first_ref, final_ref, valid_ref, h_ref, dest_ref, o_ref):
    w = pl.program_id(0)

    @pl.when(valid_ref[w] == 1)
    def _():
        rt, ch = o_ref.shape[0], h_ref.shape[0]
        row = r_ref[w] * rt + lax.broadcasted_iota(jnp.int32, (rt, ch), 0)
        dest = dest_ref[...]
        onehot = _mask_bf16((dest[0:1, :] == row) | (dest[1:2, :] == row))
        part = _dot(onehot, h_ref[...]).astype(BF16)

        @pl.when(first_ref[w] == 1)
        def _():
            o_ref[...] = part

        @pl.when(first_ref[w] == 0)
        def _():
            o_ref[...] += part


def _moe_gather(h, dest_rows, worklist, n_rows):
    t, d = h.shape
    ch, rt = ROUTE_CHUNK, ROW_TILE
    n_items = worklist[0].shape[0]
    grid_spec = pltpu.PrefetchScalarGridSpec(
        num_scalar_prefetch=5,
        grid=(n_items,),
        in_specs=[
            pl.BlockSpec((ch, d), lambda w, r, c, *_: (c[w], 0)),
            pl.BlockSpec((None, 2, ch), lambda w, r, c, *_: (c[w], 0, 0)),
        ],
        out_specs=pl.BlockSpec((rt, d), lambda w, r, c, *_: (r[w], 0)),
    )
    return pl.pallas_call(
        _gather_kernel,
        grid_spec=grid_spec,
        out_shape=jax.ShapeDtypeStruct((n_rows, d), BF16),
        compiler_params=_params(("arbitrary",)),
        name="moe_gather",
    )(*worklist, h, dest_rows)


def _expert_kernel(blk_ref, exp_ref, used_ref, x_ref, wg_ref, wu_ref, wd_ref, o_ref, acc_scr):
    r, f = pl.program_id(0), pl.program_id(1)

    @pl.when(used_ref[r] == 1)
    def _():
        x = x_ref[...]
        gte = _dot(x, wg_ref[...])
        up = _dot(x, wu_ref[...])
        act = (gte * jax.nn.sigmoid(gte) * up).astype(BF16)
        part = _dot(act, wd_ref[...])

        @pl.when(f == 0)
        def _():
            acc_scr[...] = part

        @pl.when(f > 0)
        def _():
            acc_scr[...] += part

        @pl.when(f == pl.num_programs(1) - 1)
        def _():
            o_ref[...] = acc_scr[...].astype(o_ref.dtype)


def _moe_experts(xs, w_gu, w_down, gemm_list):
    n_rows, d = xs.shape
    ff = w_down.shape[1]
    gr = GEMM_ROWS
    tf = _tile(ff, 256, LANES)
    nf = ff // tf
    n_tiles = gemm_list[0].shape[0]

    def fcol(f, used, r):
        return jnp.where(used[r] == 1, f, nf - 1)

    grid_spec = pltpu.PrefetchScalarGridSpec(
        num_scalar_prefetch=3,
        grid=(n_tiles, nf),
        in_specs=[
            pl.BlockSpec((gr, d), lambda r, f, blk, ex, used: (blk[r], 0)),
            pl.BlockSpec((None, d, tf), lambda r, f, blk, ex, used: (ex[r], 0, fcol(f, used, r))),
            pl.BlockSpec((None, d, tf), lambda r, f, blk, ex, used: (ex[r], 0, nf + fcol(f, used, r))),
            pl.BlockSpec((None, tf, d), lambda r, f, blk, ex, used: (ex[r], fcol(f, used, r), 0)),
        ],
        out_specs=pl.BlockSpec((gr, d), lambda r, f, blk, ex, used: (blk[r], 0)),
        scratch_shapes=[pltpu.VMEM((gr, d), F32)],
    )
    return pl.pallas_call(
        _expert_kernel,
        grid_spec=grid_spec,
        out_shape=jax.ShapeDtypeStruct((n_rows, d), BF16),
        compiler_params=_params(("arbitrary", "arbitrary")),
        name="moe_experts",
    )(*gemm_list, xs, w_gu, w_gu, w_down)


def _combine_kernel(r_ref, c_ref, first_ref, final_ref, valid_ref, y_ref, info_ref, x_ref, gate_ref,
                    op_ref, os_ref, acc_scr, *, n_prompt_chunks):
    w = pl.program_id(0)

    @pl.when(valid_ref[w] == 1)
    def _():
        rt, ch = y_ref.shape[0], x_ref.shape[0]
        info = info_ref[...]
        row = (r_ref[w] * rt + lax.broadcasted_iota(jnp.int32, (ch, rt), 1)).astype(F32)
        weights = (jnp.where(info[:, 0:1] == row, info[:, 2:3], 0.0)
                   + jnp.where(info[:, 1:2] == row, info[:, 3:4], 0.0)).astype(BF16)
        part = _dot(weights, y_ref[...])

        @pl.when(first_ref[w] == 1)
        def _():
            acc_scr[...] = part

        @pl.when(first_ref[w] == 0)
        def _():
            acc_scr[...] += part

        is_prompt = c_ref[w] < n_prompt_chunks

        @pl.when((final_ref[w] == 1) & is_prompt)
        def _():
            op_ref[...] = x_ref[...] + gate_ref[...] * acc_scr[...]

        @pl.when((final_ref[w] == 1) & jnp.logical_not(is_prompt))
        def _():
            os_ref[...] = x_ref[...] + gate_ref[...] * acc_scr[...]


def _moe_combine(y, tok_info, x, mods, worklist, *, n_prompt, tok_per_cond):
    t, d = x.shape
    ch, rt = ROUTE_CHUNK, ROW_TILE
    n_items = worklist[0].shape[0]
    n_pc = n_prompt // ch
    row_fn = _cond_row_joint(ch, n_prompt, tok_per_cond)
    gate = mods.spec(_Mods.GATE2, lambda w, r, c, *_: row_fn(c[w]))
    grid_spec = pltpu.PrefetchScalarGridSpec(
        num_scalar_prefetch=5,
        grid=(n_items,),
        in_specs=[
            pl.BlockSpec((rt, d), lambda w, r, c, *_: (r[w], 0)),
            pl.BlockSpec((ch, tok_info.shape[1]), lambda w, r, c, *_: (c[w], 0)),
            pl.BlockSpec((ch, d), lambda w, r, c, *_: (c[w], 0)),
            gate,
        ],
        out_specs=[
            pl.BlockSpec((ch, d), lambda w, r, c, *_: (jnp.minimum(c[w], n_pc - 1), 0)),
            pl.BlockSpec((ch, d), lambda w, r, c, *_: (jnp.maximum(c[w] - n_pc, 0), 0)),
        ],
        scratch_shapes=[pltpu.VMEM((ch, d), F32)],
    )
    return pl.pallas_call(
        functools.partial(_combine_kernel, n_prompt_chunks=n_pc),
        grid_spec=grid_spec,
        out_shape=[jax.ShapeDtypeStruct((n_prompt, d), F32), jax.ShapeDtypeStruct((t - n_prompt, d), F32)],
        compiler_params=_params(("arbitrary",)),
        name="moe_combine",
    )(*worklist, y, tok_info, x, mods.table)


def kernel(x_prompt, x_sample, c, cache_k, cache_v, c_ctx, ada_w, ada_b, norm1_g, norm2_g,
           attn_w_qkv, attn_w_o, attn_q_norm, attn_k_norm, attn_lambda, attn_subln_g,
           pool_w, pool_scale, ffn_w_gu, ffn_w_down,
           moe_w_router, moe_b_router, moe_w_gu, moe_w_down):
    b_ctx, l_ctx, d = x_prompt.shape
    b_dec, l_dec, _ = x_sample.shape
    depth = ada_w.shape[0]
    n_even, l_past, heads, v_dim = cache_k.shape[1:]
    head_dim = v_dim // 2
    assert head_dim == LANES and depth == 2 and l_dec % GRID_W == 0
    n_exp = moe_w_router.shape[-1]
    tp_, ts_ = b_ctx * l_ctx, b_dec * l_dec
    t_all = tp_ + ts_

    cond_rows = 8 * ((1 + b_dec + 7) // 8)
    cond = jnp.zeros((cond_rows, d), F32).at[0].set(c_ctx).at[1:1 + b_dec].set(c)
    mod_table = _adaln_mods(cond, ada_w, ada_b).reshape(depth * cond_rows * 6, 1, d)

    xp = x_prompt.reshape(tp_, d)
    xs = x_sample.reshape(ts_, d)

    layer, j = 0, 0
    mods = _Mods(mod_table, cond_rows, layer)
    g1 = norm1_g[layer].reshape(1, d)
    w_qkv = attn_w_qkv[j].astype(BF16)
    q_gain = (attn_q_norm[j] * (head_dim ** -0.5 * LOG2E)).reshape(1, head_dim)
    k_gain = attn_k_norm[j].reshape(1, head_dim)
    tables = _rope_tables(l_dec)
    prompt = dict(cond_base=0, tok_per_cond=tp_)
    sample = dict(cond_base=1, tok_per_cond=l_dec)

    q_p, k_p, v_p = _qkv_proj(xp, mods, g1, w_qkv, q_gain, k_gain, None, kv_dtype=F32, **prompt)
    q_s, k_s, v_s = _qkv_proj(xs, mods, g1, w_qkv, q_gain, k_gain, tables, kv_dtype=BF16, **sample)

    lam_params = attn_lambda[j].astype(F32)
    subln = attn_subln_g[j].reshape(1, v_dim)
    o_p = _diff_attention(q_p, k_p, v_p, None, None, lam_params, subln,
                          batch=b_ctx, seq=l_ctx, layer=layer)
    ck = cache_k[:, j].reshape(b_dec * l_past, heads * v_dim)
    cv = cache_v[:, j].reshape(b_dec * l_past, heads * v_dim)
    o_s = _diff_attention(q_s, k_s, v_s, ck, cv, lam_params, subln,
                          batch=b_dec, seq=l_dec, layer=layer)

    w_o = attn_w_o[j].astype(BF16)
    x1 = _out_proj_residual(o_p, w_o, xp, mods, None, total_rows=t_all, row_off=0, **prompt)
    x1 = _out_proj_residual(o_s, w_o, xs, mods, x1, total_rows=t_all, row_off=tp_, **sample)

    x2 = _dense_ffn(x1, mods, norm2_g[layer].reshape(1, d), ffn_w_gu[j].astype(BF16),
                    ffn_w_down[j].astype(BF16), n_prompt=tp_, tok_per_cond=l_dec)

    layer, j = 1, 0
    mods = _Mods(mod_table, cond_rows, layer)
    x3 = _pool_mixer(x2, mods, norm1_g[layer].reshape(1, d), pool_w[j].astype(BF16),
                     pool_scale[j].reshape(1, d), n_prompt=tp_, prompt_seq=l_ctx, sample_seq=l_dec)

    h2, route, counts = _moe_router(x3, mods, norm2_g[layer].reshape(1, d), moe_w_router[j],
                                    moe_b_router[j], n_prompt=tp_, tok_per_cond=l_dec)
    dest_rows, tok_info, gather_list, combine_list, gemm_list, n_rows = _dispatch_plan(
        route, counts, n_exp)
    rows_sorted = _moe_gather(h2, dest_rows, gather_list, n_rows)
    y_sorted = _moe_experts(rows_sorted, moe_w_gu[j].astype(BF16), moe_w_down[j].astype(BF16),
                            gemm_list)
    y_p, y_s = _moe_combine(y_sorted, tok_info, x3, mods, combine_list, n_prompt=tp_, tok_per_cond=l_dec)

    y_prompt = y_p.reshape(b_ctx, l_ctx, d)
    y_sample = y_s.reshape(b_dec, l_dec, d)
    state_k = k_p.reshape(b_ctx, 1, l_ctx, heads, v_dim)
    state_v = v_p.reshape(b_ctx, 1, l_ctx, heads, v_dim)
    return (y_prompt, y_sample, state_k, state_v)
```

```python
import functools
import math

import jax
import jax.numpy as jnp
from jax import lax
from jax.experimental import pallas as pl
from jax.experimental.pallas import tpu as pltpu

F32 = jnp.float32
BF16 = jnp.bfloat16

GRID_W = 64
ROPE_BASE = 10000.0
POOL_WINDOWS = (2, 4, 8, 16)
TOP_K = 2
EPS = 1e-6
LOG2E = 1.4426950408889634

LANES = 128
VMEM_LIMIT = 52 * 2**20

ROUTE_CHUNK = 256
ROW_TILE = 256
GEMM_ROWS = 512


def _lambda_init(layer):
    return 0.8 - 0.6 * math.exp(-0.3 * layer)


def _tile(n, pref, mult):
    best = None
    t = mult
    while t <= min(n, pref):
        if n % t == 0:
            best = t
        t += mult
    if best is None:
        raise ValueError(f"no tile for {n} (multiple of {mult}, <= {pref})")
    return best


def _params(semantics):
    return pltpu.CompilerParams(dimension_semantics=semantics, vmem_limit_bytes=VMEM_LIMIT)


def _dot(a, b):
    return jnp.dot(a, b, preferred_element_type=F32)


def _mask_bf16(m):
    return jnp.where(m, 1.0, 0.0).astype(BF16)


def _split_bf16(x):
    hi = x.astype(BF16)
    lo = (x - hi.astype(F32)).astype(BF16)
    return hi, lo


def _norm_modulate(x, g, scale, shift):
    ms = jnp.mean(x * x, axis=-1, keepdims=True)
    return x * lax.rsqrt(ms + EPS) * (g * (1.0 + scale)) + shift


def _adaln_kernel(cond_ref, w_ref, b_ref, o_ref):
    c = cond_ref[...]
    s = c * jax.nn.sigmoid(c)
    s_hi, s_lo = _split_bf16(s)
    w_hi, w_lo = _split_bf16(w_ref[...])
    o_ref[...] = _dot(s_hi, w_hi) + _dot(s_lo, w_hi) + _dot(s_hi, w_lo) + b_ref[...]


def _adaln_mods(cond, ada_w, ada_b):
    depth, d, n = ada_w.shape
    rows = cond.shape[0]
    tn = _tile(n, 512, LANES)
    return pl.pallas_call(
        _adaln_kernel,
        grid=(depth, n // tn),
        in_specs=[
            pl.BlockSpec((rows, d), lambda l, j: (0, 0)),
            pl.BlockSpec((None, d, tn), lambda l, j: (l, 0, j)),
            pl.BlockSpec((None, 1, tn), lambda l, j: (l, 0, j)),
        ],
        out_specs=pl.BlockSpec((None, rows, tn), lambda l, j: (l, 0, j)),
        out_shape=jax.ShapeDtypeStruct((depth, rows, n), F32),
        compiler_params=_params(("arbitrary", "arbitrary")),
        name="adaln_mods",
    )(cond, ada_w, ada_b.reshape(depth, 1, n))


class _Mods:
    SHIFT1, SCALE1, GATE1, SHIFT2, SCALE2, GATE2 = range(6)

    def __init__(self, table, rows, layer):
        self.table = table
        self.rows = rows
        self.layer = layer

    def spec(self, which, row_fn, width=None, col_fn=None):
        d = self.table.shape[-1]
        width = d if width is None else width
        base = self.layer * self.rows

        def index(*ids):
            col = 0 if col_fn is None else col_fn(*ids)
            return ((base + row_fn(*ids)) * 6 + which, 0, col)

        return pl.BlockSpec((None, 1, width), index)


def _qkv_kernel(*refs, rope, tn, n_sec):
    x_ref, sh_ref, sc_ref, g_ref, w_ref, qg_ref, kg_ref = refs[:7]
    rest = list(refs[7:])
    cos_ref, sin_ref = (rest.pop(0), rest.pop(0)) if rope else (None, None)
    q_ref, k_ref, v_ref, h_scr = rest
    j = pl.program_id(1)

    @pl.when(j == 0)
    def _():
        h = _norm_modulate(x_ref[...], g_ref[...], sc_ref[...], sh_ref[...])
        h_scr[...] = h.astype(BF16)

    acc = _dot(h_scr[...], w_ref[...])

    def qk_epilogue(gain_ref, out_ref):
        r_idx = lax.broadcasted_iota(jnp.int32, (2 * LANES, 2 * LANES), 0)
        c_idx = lax.broadcasted_iota(jnp.int32, (2 * LANES, 2 * LANES), 1)
        if rope:
            swap = (r_idx >= LANES) & (c_idx >= LANES) & (r_idx - LANES == jnp.bitwise_xor(c_idx - LANES, LANES // 4))
            rhs = _mask_bf16(((r_idx < LANES) & (c_idx < LANES)) | swap)
        else:
            rhs = jnp.ones((2 * LANES, LANES), BF16)
        for g in range(tn // LANES):
            y = acc[:, g * LANES:(g + 1) * LANES]
            yg = y * gain_ref[...]
            if rope:
                res = _dot(jnp.concatenate([(y * y).astype(BF16), yg.astype(BF16)], axis=1), rhs)
                inv = lax.rsqrt(res[:, :LANES] * (1.0 / LANES) + EPS)
                out = (yg * cos_ref[...] + res[:, LANES:] * sin_ref[...]) * inv
            else:
                ss = _dot(jnp.concatenate(_split_bf16(y * y), axis=1), rhs)
                out = yg * lax.rsqrt(ss * (1.0 / LANES) + EPS)
            out_ref[:, g * LANES:(g + 1) * LANES] = out.astype(out_ref.dtype)

    @pl.when(j < n_sec)
    def _():
        qk_epilogue(qg_ref, q_ref)

    @pl.when((j >= n_sec) & (j < 2 * n_sec))
    def _():
        qk_epilogue(kg_ref, k_ref)

    @pl.when(j >= 2 * n_sec)
    def _():
        v_ref[...] = acc.astype(v_ref.dtype)


def _qkv_proj(x, mods, norm_g, w, q_gain, k_gain, tables, *, cond_base, tok_per_cond, kv_dtype):
    t, d = x.shape
    tm = _tile(math.gcd(t, tok_per_cond), 1024, 8)
    tn = _tile(d, 512, LANES)
    n_sec = d // tn
    rope = tables is not None
    row = lambda i, j: cond_base + (i * tm) // tok_per_cond
    in_specs = [
        pl.BlockSpec((tm, d), lambda i, j: (i, 0)),
        mods.spec(_Mods.SHIFT1, row),
        mods.spec(_Mods.SCALE1, row),
        pl.BlockSpec((1, d), lambda i, j: (0, 0)),
        pl.BlockSpec((d, tn), lambda i, j: (0, j)),
        pl.BlockSpec((1, LANES), lambda i, j: (0, 0)),
        pl.BlockSpec((1, LANES), lambda i, j: (0, 0)),
    ]
    args = [x, mods.table, mods.table, norm_g, w, q_gain, k_gain]
    if rope:
        nblk = tables[0].shape[0] // tm
        for tab in tables:
            in_specs.append(pl.BlockSpec((tm, LANES), lambda i, j: (i % nblk, 0)))
            args.append(tab)
    out_specs = [
        pl.BlockSpec((tm, tn), lambda i, j, s=s: (i, jnp.clip(j - s * n_sec, 0, n_sec - 1)))
        for s in range(3)
    ]
    return pl.pallas_call(
        functools.partial(_qkv_kernel, rope=rope, tn=tn, n_sec=n_sec),
        grid=(t // tm, 3 * n_sec),
        in_specs=in_specs,
        out_specs=out_specs,
        out_shape=[jax.ShapeDtypeStruct((t, d), BF16), jax.ShapeDtypeStruct((t, d), kv_dtype),
                   jax.ShapeDtypeStruct((t, d), kv_dtype)],
        scratch_shapes=[pltpu.VMEM((tm, d), BF16)],
        compiler_params=_params(("arbitrary", "arbitrary")),
        name="qkv_proj_rope" if rope else "qkv_proj",
    )(*args)


def _rope_tables(seq):
    n_freq = LANES // 4
    pos = jnp.arange(seq, dtype=jnp.int32)
    r = (pos // GRID_W).astype(F32)
    col = (pos % GRID_W).astype(F32)
    freqs = ROPE_BASE ** (-jnp.arange(n_freq, dtype=F32) / n_freq)
    cos_t, sin_t = [], []
    for p in (r, col):
        ang = p[:, None] * freqs
        cos, sin = jnp.cos(ang), jnp.sin(ang)
        cos_t += [cos, cos]
        sin_t += [-sin, sin]
    return jnp.concatenate(cos_t, axis=1), jnp.concatenate(sin_t, axis=1)


def _attn_kernel(*refs, n_cache, n_chunks, tk, lambda_init):
    lam_ref, g_ref, q_ref = refs[:3]
    if n_cache:
        kc_ref, vtc_ref, k_ref, vt_ref, o_ref, acc_ref = refs[3:]
    else:
        k_ref, vt_ref, o_ref, acc_ref = refs[3:]
    q = q_ref[...]
    qs = (q[:, :LANES], q[:, LANES:])

    def scores(kblk):
        return [lax.dot_general(kblk[:, c * LANES:(c + 1) * LANES], qs[c],
                                (((1,), (1,)), ((), ())), preferred_element_type=F32)
                for c in range(2)]

    def accumulate(sts, vt, stats):
        out = []
        for c in range(2):
            st = sts[c]
            m_cur = jnp.max(st, axis=0, keepdims=True)
            if stats is None:
                m_new = m_cur
                p = jnp.exp2(st - m_new)
                l_new = jnp.sum(p, axis=0, keepdims=True)
                acc_ref[c] = _dot(vt, p.astype(BF16))
            else:
                m_prev, l_prev = stats[c]
                m_new = jnp.maximum(m_prev, m_cur)
                alpha = jnp.exp2(m_prev - m_new)
                p = jnp.exp2(st - m_new)
                l_new = alpha * l_prev + jnp.sum(p, axis=0, keepdims=True)
                acc_ref[c] = alpha * acc_ref[c] + _dot(vt, p.astype(BF16))
            out.append((m_new, l_new))
        return out

    chunks = []
    if n_cache:
        chunks.append((lambda: kc_ref[...].astype(BF16), lambda: vtc_ref[...]))
    for i in range(n_chunks):
        chunks.append((lambda i=i: k_ref[i * tk:(i + 1) * tk, :].astype(BF16), lambda i=i: vt_ref[i]))
    stats = None
    sts = scores(chunks[0][0]())
    for n, (_, values) in enumerate(chunks):
        nxt = scores(chunks[n + 1][0]()) if n + 1 < len(chunks) else None
        stats = accumulate(sts, values(), stats)
        sts = nxt

    lp = lam_ref[...]
    lam = (jnp.exp(jnp.sum(lp[0:1] * lp[1:2], axis=-1, keepdims=True))
           - jnp.exp(jnp.sum(lp[2:3] * lp[3:4], axis=-1, keepdims=True)) + lambda_init)
    o_t = acc_ref[0] / stats[0][1] - lam * (acc_ref[1] / stats[1][1])
    o = o_t.T
    o = o * lax.rsqrt(jnp.mean(o * o, axis=-1, keepdims=True) + EPS) * g_ref[...]
    o_ref[...] = (o * (1.0 - lambda_init)).astype(o_ref.dtype)


def _chunked_transpose(v, batch, seq, heads, vd, tk):
    v = v.astype(BF16).reshape(batch, seq // tk, tk, heads, vd)
    return v.transpose(0, 3, 1, 4, 2).reshape(batch * heads, seq // tk, vd, tk)


def _diff_attention(q, k, v, cache_k, cache_v, lam_params, subln_g, *, batch, seq, layer):
    t, d = q.shape
    vd = subln_g.shape[-1]
    heads = d // vd
    n_cache = 0 if cache_k is None else cache_k.shape[0] // batch
    tq = _tile(seq, 512, LANES)
    tk = _tile(seq, 512, LANES)
    nq = seq // tq
    n_chunks = seq // tk
    in_specs = [
        pl.BlockSpec(lam_params.shape, lambda b, h, i: (0, 0)),
        pl.BlockSpec((1, vd), lambda b, h, i: (0, 0)),
        pl.BlockSpec((tq, vd), lambda b, h, i: (b * nq + i, h)),
    ]
    args = [lam_params, subln_g, q]
    if n_cache:
        in_specs += [pl.BlockSpec((n_cache, vd), lambda b, h, i: (b, h)),
                     pl.BlockSpec((None, None, vd, n_cache), lambda b, h, i: (b * heads + h, 0, 0, 0))]
        args += [cache_k, _chunked_transpose(cache_v, batch, n_cache, heads, vd, n_cache)]
    in_specs += [pl.BlockSpec((seq, vd), lambda b, h, i: (b, h)),
                 pl.BlockSpec((None, n_chunks, vd, tk), lambda b, h, i: (b * heads + h, 0, 0, 0))]
    args += [k, _chunked_transpose(v, batch, seq, heads, vd, tk)]
    return pl.pallas_call(
        functools.partial(_attn_kernel, n_cache=n_cache, n_chunks=n_chunks, tk=tk,
                          lambda_init=_lambda_init(layer)),
        grid=(batch, heads, nq),
        in_specs=in_specs,
        out_specs=pl.BlockSpec((tq, vd), lambda b, h, i: (b * nq + i, h)),
        out_shape=jax.ShapeDtypeStruct((t, d), BF16),
        scratch_shapes=[pltpu.VMEM((2, vd, tq), F32)],
        compiler_params=_params(("arbitrary", "arbitrary", "arbitrary")),
        name=f"diff_attention_{'latent' if n_cache else 'context'}",
    )(*args)


def _out_proj_kernel(*refs, aliased):
    o_ref, w_ref, x_ref, gate_ref = refs[:4]
    out_ref = refs[-1]
    out_ref[...] = x_ref[...] + gate_ref[...] * _dot(o_ref[...], w_ref[...])


def _out_proj_residual(o, w, x, mods, prev, *, total_rows, row_off, cond_base, tok_per_cond):
    t, d = x.shape
    tm = _tile(math.gcd(math.gcd(t, tok_per_cond), row_off or t), 1024, 8)
    tn = _tile(d, 512, LANES)
    blk_off = row_off // tm
    row = lambda i, j: cond_base + (i * tm) // tok_per_cond
    in_specs = [
        pl.BlockSpec((tm, d), lambda i, j: (i, 0)),
        pl.BlockSpec((d, tn), lambda i, j: (0, j)),
        pl.BlockSpec((tm, tn), lambda i, j: (i, j)),
        mods.spec(_Mods.GATE1, row, width=tn, col_fn=lambda i, j: j),
    ]
    args = [o, w, x, mods.table]
    aliases = {}
    if prev is not None:
        in_specs.append(pl.BlockSpec(memory_space=pl.ANY))
        args.append(prev)
        aliases = {4: 0}
    return pl.pallas_call(
        functools.partial(_out_proj_kernel, aliased=prev is not None),
        grid=(t // tm, d // tn),
        in_specs=in_specs,
        out_specs=pl.BlockSpec((tm, tn), lambda i, j: (blk_off + i, j)),
        out_shape=jax.ShapeDtypeStruct((total_rows, d), F32),
        input_output_aliases=aliases,
        compiler_params=_params(("arbitrary", "arbitrary")),
        name="attn_out_proj",
    )(*args)


def _ffn_kernel(x_ref, sh_ref, sc_ref, gate_ref, g_ref, wg_ref, wu_ref, wd_ref, o_ref, h_scr, acc_scr):
    f = pl.program_id(1)

    @pl.when(f == 0)
    def _():
        h = _norm_modulate(x_ref[...], g_ref[...], sc_ref[...], sh_ref[...])
        h_scr[...] = h.astype(BF16)
        acc_scr[...] = jnp.zeros(acc_scr.shape, F32)

    h = h_scr[...]
    gte = _dot(h, wg_ref[...])
    up = _dot(h, wu_ref[...])
    act = (gte * jax.nn.sigmoid(gte) * up).astype(BF16)
    acc_scr[...] += _dot(act, wd_ref[...])

    @pl.when(f == pl.num_programs(1) - 1)
    def _():
        o_ref[...] = x_ref[...] + gate_ref[...] * acc_scr[...]


def _cond_row_joint(tm, n_prompt, tok_per_cond):
    def row(i, *_):
        tok = i * tm
        return jnp.where(tok < n_prompt, 0, 1 + (tok - n_prompt) // tok_per_cond)
    return row


def _dense_ffn(x, mods, norm_g, w_gu, w_down, *, n_prompt, tok_per_cond):
    t, d = x.shape
    ff = w_down.shape[0]
    tm = _tile(math.gcd(n_prompt, tok_per_cond), 512, 8)
    tf = _tile(ff, 512, LANES)
    nf = ff // tf
    row = _cond_row_joint(tm, n_prompt, tok_per_cond)
    return pl.pallas_call(
        _ffn_kernel,
        grid=(t // tm, nf),
        in_specs=[
            pl.BlockSpec((tm, d), lambda i, f: (i, 0)),
            mods.spec(_Mods.SHIFT2, row),
            mods.spec(_Mods.SCALE2, row),
            mods.spec(_Mods.GATE2, row),
            pl.BlockSpec((1, d), lambda i, f: (0, 0)),
            pl.BlockSpec((d, tf), lambda i, f: (0, f)),
            pl.BlockSpec((d, tf), lambda i, f: (0, nf + f)),
            pl.BlockSpec((tf, d), lambda i, f: (f, 0)),
        ],
        out_specs=pl.BlockSpec((tm, d), lambda i, f: (i, 0)),
        out_shape=jax.ShapeDtypeStruct((t, d), F32),
        scratch_shapes=[pltpu.VMEM((tm, d), BF16), pltpu.VMEM((tm, d), F32)],
        compiler_params=_params(("arbitrary", "arbitrary")),
        name="dense_ffn",
    )(x, mods.table, mods.table, mods.table, norm_g, w_gu, w_gu, w_down)


def _pool_kernel(x_ref, xp_ref, xn_ref, sh_ref, sc_ref, gate_ref, g_ref, w_ref, ps_ref, o_ref, *,
                 tp, halo, n_prompt_tiles, prompt_tiles_per_seq, sample_tiles_per_seq, group_dim):
    i = pl.program_id(0)
    is_prompt = i < n_prompt_tiles
    tiles_per_seq = jnp.where(is_prompt, prompt_tiles_per_seq, sample_tiles_per_seq)
    local = lax.rem(jnp.where(is_prompt, i, i - n_prompt_tiles), tiles_per_seq)
    first = local == 0
    last = local == tiles_per_seq - 1
    seq_len = tiles_per_seq * tp

    g, sc, sh = g_ref[...], sc_ref[...], sh_ref[...]
    x = x_ref[...]
    h = _norm_modulate(x, g, sc, sh)
    h_prev = jnp.where(first, 0.0, _norm_modulate(xp_ref[...], g, sc, sh))
    h_next = jnp.where(last, 0.0, _norm_modulate(xn_ref[...], g, sc, sh))
    d = h.shape[-1]
    h_halo = jnp.concatenate([h_prev, h_next, jnp.zeros((LANES - 2 * halo, d), F32)], axis=0)

    t_idx = lax.broadcasted_iota(jnp.int32, (tp, tp), 0)
    s_idx = lax.broadcasted_iota(jnp.int32, (tp, tp), 1)
    t_h = lax.broadcasted_iota(jnp.int32, (tp, LANES), 0)
    u_h = lax.broadcasted_iota(jnp.int32, (tp, LANES), 1)
    off_h = jnp.where(u_h < halo, u_h - halo, tp + u_h - halo)
    pos = local * tp + lax.broadcasted_iota(jnp.int32, (tp, 1), 0)

    for grp, win in enumerate(POOL_WINDOWS):
        lo, hi = win // 2, win // 2 - 1
        cols = slice(grp * group_dim, (grp + 1) * group_dim)
        band = _mask_bf16((s_idx >= t_idx - lo) & (s_idx <= t_idx + hi))
        band_h = _mask_bf16((u_h < 2 * halo) & (off_h >= t_h - lo) & (off_h <= t_h + hi))
        c_hi, c_lo = _split_bf16(h[:, cols])
        e_hi, e_lo = _split_bf16(h_halo[:, cols])
        total = (_dot(band, c_hi) + _dot(band, c_lo)) + (_dot(band_h, e_hi) + _dot(band_h, e_lo))
        cnt = (jnp.minimum(pos + hi, seq_len - 1) - jnp.maximum(pos - lo, 0) + 1).astype(F32)
        pooled = (total / cnt - h[:, cols]).astype(BF16)
        y = _dot(pooled, w_ref[grp]) * ps_ref[:, cols]
        o_ref[:, cols] = x[:, cols] + gate_ref[:, cols] * y


def _pool_mixer(x, mods, norm_g, pool_w, pool_scale, *, n_prompt, prompt_seq, sample_seq):
    t, d = x.shape
    halo = max(POOL_WINDOWS) // 2
    tp = _tile(math.gcd(prompt_seq, sample_seq), 256, 8)
    n_groups, group_dim = pool_w.shape[0], pool_w.shape[1]
    hb = tp // halo
    n_halo_blocks = t // halo
    row = _cond_row_joint(tp, n_prompt, sample_seq)
    return pl.pallas_call(
        functools.partial(_pool_kernel, tp=tp, halo=halo, n_prompt_tiles=n_prompt // tp,
                          prompt_tiles_per_seq=prompt_seq // tp,
                          sample_tiles_per_seq=sample_seq // tp, group_dim=group_dim),
        grid=(t // tp,),
        in_specs=[
            pl.BlockSpec((tp, d), lambda i: (i, 0)),
            pl.BlockSpec((halo, d), lambda i: (jnp.maximum(i * hb - 1, 0), 0)),
            pl.BlockSpec((halo, d), lambda i: (jnp.minimum((i + 1) * hb, n_halo_blocks - 1), 0)),
            mods.spec(_Mods.SHIFT1, row),
            mods.spec(_Mods.SCALE1, row),
            mods.spec(_Mods.GATE1, row),
            pl.BlockSpec((1, d), lambda i: (0, 0)),
            pl.BlockSpec((n_groups, group_dim, group_dim), lambda i: (0, 0, 0)),
            pl.BlockSpec((1, d), lambda i: (0, 0)),
        ],
        out_specs=pl.BlockSpec((tp, d), lambda i: (i, 0)),
        out_shape=jax.ShapeDtypeStruct((t, d), F32),
        compiler_params=_params(("arbitrary",)),
        name="pool_mixer",
    )(x, x, x, mods.table, mods.table, mods.table, norm_g, pool_w, pool_scale)


def _router_kernel(x_ref, sh_ref, sc_ref, g_ref, wr_ref, br_ref, h_ref, route_ref, cnt_ref):
    h = _norm_modulate(x_ref[...], g_ref[...], sc_ref[...], sh_ref[...])
    h_ref[...] = h.astype(BF16)
    h_hi, h_lo = _split_bf16(h)
    w_hi, w_lo = _split_bf16(wr_ref[...])
    logits = _dot(h_hi, w_hi) + _dot(h_lo, w_hi) + _dot(h_hi, w_lo) + br_ref[...]

    rows = logits.shape[0]
    lane = lax.broadcasted_iota(jnp.int32, logits.shape, 1).astype(F32)
    m1 = jnp.max(logits, axis=-1, keepdims=True)
    i1 = jnp.min(jnp.where(logits == m1, lane, float(LANES)), axis=-1, keepdims=True)
    oh1 = lane == i1
    rest = jnp.where(oh1, -jnp.inf, logits)
    m2 = jnp.max(rest, axis=-1, keepdims=True)
    i2 = jnp.min(jnp.where(rest == m2, lane, float(LANES)), axis=-1, keepdims=True)
    oh2 = lane == i2
    e = jnp.exp(m2 - m1)
    gate_a = 1.0 / (1.0 + e)
    gate_b = e / (1.0 + e)

    sel = jnp.where(oh1 | oh2, 1.0, 0.0)
    r_idx = lax.broadcasted_iota(jnp.int32, (rows, rows), 0)
    c_idx = lax.broadcasted_iota(jnp.int32, (rows, rows), 1)
    earlier = _mask_bf16(c_idx < r_idx)
    rank = _dot(earlier, sel.astype(BF16))
    rank_a = jnp.sum(jnp.where(oh1, rank, 0.0), axis=-1, keepdims=True)
    rank_b = jnp.sum(jnp.where(oh2, rank, 0.0), axis=-1, keepdims=True)

    route = jnp.zeros(logits.shape, F32)
    for k, val in enumerate((i1, i2, rank_a, rank_b, gate_a, gate_b)):
        route = jnp.where(lane == float(k), val, route)
    route_ref[...] = route
    cnt_ref[...] = jnp.sum(sel, axis=0, keepdims=True)


def _moe_router(x, mods, norm_g, w_router, b_router, *, n_prompt, tok_per_cond):
    t, d = x.shape
    n_exp = w_router.shape[1]
    ch = ROUTE_CHUNK
    wr = jnp.zeros((d, LANES), F32).at[:, :n_exp].set(w_router.astype(F32))
    br = jnp.full((1, LANES), -jnp.inf, F32).at[0, :n_exp].set(b_router.astype(F32))
    row = _cond_row_joint(ch, n_prompt, tok_per_cond)
    return pl.pallas_call(
        _router_kernel,
        grid=(t // ch,),
        in_specs=[
            pl.BlockSpec((ch, d), lambda i: (i, 0)),
            mods.spec(_Mods.SHIFT2, row),
            mods.spec(_Mods.SCALE2, row),
            pl.BlockSpec((1, d), lambda i: (0, 0)),
            pl.BlockSpec((d, LANES), lambda i: (0, 0)),
            pl.BlockSpec((1, LANES), lambda i: (0, 0)),
        ],
        out_specs=[
            pl.BlockSpec((ch, d), lambda i: (i, 0)),
            pl.BlockSpec((ch, LANES), lambda i: (i, 0)),
            pl.BlockSpec((None, 1, LANES), lambda i: (i, 0, 0)),
        ],
        out_shape=[
            jax.ShapeDtypeStruct((t, d), BF16),
            jax.ShapeDtypeStruct((t, LANES), F32),
            jax.ShapeDtypeStruct((t // ch, 1, LANES), F32),
        ],
        compiler_params=_params(("arbitrary",)),
        name="moe_router",
    )(x, mods.table, mods.table, norm_g, wr, br)


def _dispatch_plan(route, counts, n_exp):
    ch, rt, gr = ROUTE_CHUNK, ROW_TILE, GEMM_ROWS
    t = route.shape[0]
    n_chunks = t // ch
    i32 = jnp.int32
    cnt = counts[:, 0, :n_exp].astype(i32)
    total = cnt.sum(0)
    padded = ((total + gr - 1) // gr) * gr
    off = jnp.cumsum(padded) - padded
    start = off[None, :] + jnp.cumsum(cnt, axis=0) - cnt

    def dest(choice):
        e = route[:, choice].astype(i32).reshape(n_chunks, ch, 1)
        hit = e == jnp.arange(n_exp, dtype=i32)
        base = jnp.sum(jnp.where(hit, start[:, None, :], 0), axis=-1)
        return base.reshape(t) + route[:, 2 + choice].astype(i32)

    dest_a, dest_b = dest(0), dest(1)

    max_tiles = (t * TOP_K) // rt + n_exp * (gr // rt)
    n_gemm_tiles = (t * TOP_K) // gr + n_exp
    big = jnp.iinfo(i32).max

    seg_first = start // rt
    seg_last = (start + cnt - 1) // rt
    nonempty = cnt > 0
    c_ids = jnp.broadcast_to(jnp.arange(n_chunks, dtype=i32)[:, None], cnt.shape)
    cand_r = jnp.stack([seg_first, seg_last], axis=-1)
    cand_c = jnp.stack([c_ids, c_ids], axis=-1)
    cand_ok = jnp.stack([nonempty, nonempty & (seg_last > seg_first)], axis=-1)
    k = jnp.arange(gr // rt - 1, dtype=i32)[None, :] + 1
    pad_r = (off + padded)[:, None] // rt - k
    pad_ok = (padded[:, None] - k * rt >= total[:, None]) & (padded[:, None] > 0)

    def worklist(r, c, ok, by_chunk, size):
        r, c, ok = r.reshape(-1), c.reshape(-1), ok.reshape(-1)
        major, minor = (c, r) if by_chunk else (r, c)
        key = jnp.where(ok, major * (max_tiles + n_chunks + 1) + minor, big)
        order = jnp.argsort(key)[:size]
        n_ok = jnp.minimum(ok.sum(), size)
        last = order[jnp.maximum(n_ok - 1, 0)]
        idx = jnp.where(jnp.arange(size) < n_ok, order, last)
        valid = (jnp.arange(size) < n_ok).astype(i32)
        r, c = r[idx], c[idx]
        major = c if by_chunk else r
        first = jnp.concatenate([jnp.ones((1,), bool), major[1:] != major[:-1]]).astype(i32) * valid
        final = jnp.concatenate([major[1:] != major[:-1], jnp.ones((1,), bool)])
        final = (final | (jnp.arange(size) == n_ok - 1)).astype(i32) * valid
        return r.astype(i32), c.astype(i32), first, final, valid

    n_items = n_chunks * n_exp + max_tiles
    gather_r = jnp.concatenate([cand_r.reshape(-1), pad_r.reshape(-1)])
    gather_c = jnp.concatenate([cand_c.reshape(-1), jnp.zeros(pad_r.size, i32)])
    gather_ok = jnp.concatenate([cand_ok.reshape(-1), pad_ok.reshape(-1)])
    gather_list = worklist(gather_r, gather_c, gather_ok, False, n_items)
    combine_list = worklist(cand_r, cand_c, cand_ok, True, n_items)

    tile_lo = jnp.arange(n_gemm_tiles, dtype=i32) * gr
    used = tile_lo < (off + padded)[-1]
    ends = off + padded
    expert_of = jnp.minimum(jnp.sum(tile_lo[:, None] >= ends[None, :], axis=1), n_exp - 1).astype(i32)
    n_used = jnp.maximum(used.sum(), 1)
    clamp = jnp.minimum(jnp.arange(n_gemm_tiles, dtype=i32), n_used - 1).astype(i32)
    gemm_list = (clamp, expert_of[clamp], used.astype(i32))

    dest_rows = jnp.stack([dest_a.reshape(n_chunks, ch), dest_b.reshape(n_chunks, ch)], axis=1)
    tok_info = jnp.stack([dest_a.astype(F32), dest_b.astype(F32), route[:, 4], route[:, 5]], axis=1)
    tok_info = jnp.concatenate([tok_info, jnp.zeros((t, 4), F32)], axis=1)
    return dest_rows, tok_info, gather_list, combine_list, gemm_list, max_tiles * rt


def _gather_kernel(r_ref, c_ref, first_ref, final_ref, valid_ref, h_ref, dest_ref, o_ref):
    w = pl.program_id(0)

    @pl.when(valid_ref[w] == 1)
    def _():
        @pl.when(first_ref[w] == 1)
        def _():
            o_ref[...] = jnp.zeros(o_ref.shape, o_ref.dtype)

        rt, ch = o_ref.shape[0], h_ref.shape[0]
        row = r_ref[w] * rt + lax.broadcasted_iota(jnp.int32, (rt, ch), 0)
        dest = dest_ref[...]
        onehot = _mask_bf16((dest[0:1, :] == row) | (dest[1:2, :] == row))
        o_ref[...] += _dot(onehot, h_ref[...]).astype(BF16)


def _moe_gather(h, dest_rows, worklist, n_rows):
    t, d = h.shape
    ch, rt = ROUTE_CHUNK, ROW_TILE
    n_items = worklist[0].shape[0]
    grid_spec = pltpu.PrefetchScalarGridSpec(
        num_scalar_prefetch=5,
        grid=(n_items,),
        in_specs=[
            pl.BlockSpec((ch, d), lambda w, r, c, *_: (c[w], 0)),
            pl.BlockSpec((None, 2, ch), lambda w, r, c, *_: (c[w], 0, 0)),
        ],
        out_specs=pl.BlockSpec((rt, d), lambda w, r, c, *_: (r[w], 0)),
    )
    return pl.pallas_call(
        _gather_kernel,
        grid_spec=grid_spec,
        out_shape=jax.ShapeDtypeStruct((n_rows, d), BF16),
        compiler_params=_params(("arbitrary",)),
        name="moe_gather",
    )(*worklist, h, dest_rows)


def _expert_kernel(blk_ref, exp_ref, used_ref, x_ref, wg_ref, wu_ref, wd_ref, o_ref, acc_scr):
    r, f = pl.program_id(0), pl.program_id(1)

    @pl.when(used_ref[r] == 1)
    def _():
        @pl.when(f == 0)
        def _():
            acc_scr[...] = jnp.zeros(acc_scr.shape, F32)

        x = x_ref[...]
        gte = _dot(x, wg_ref[...])
        up = _dot(x, wu_ref[...])
        act = (gte * jax.nn.sigmoid(gte) * up).astype(BF16)
        acc_scr[...] += _dot(act, wd_ref[...])

        @pl.when(f == pl.num_programs(1) - 1)
        def _():
            o_ref[...] = acc_scr[...].astype(o_ref.dtype)


def _moe_experts(xs, w_gu, w_down, gemm_list):
    n_rows, d = xs.shape
    ff = w_down.shape[1]
    gr = GEMM_ROWS
    tf = _tile(ff, 256, LANES)
    nf = ff // tf
    n_tiles = gemm_list[0].shape[0]

    def fcol(f, used, r):
        return jnp.where(used[r] == 1, f, nf - 1)

    grid_spec = pltpu.PrefetchScalarGridSpec(
        num_scalar_prefetch=3,
        grid=(n_tiles, nf),
        in_specs=[
            pl.BlockSpec((gr, d), lambda r, f, blk, ex, used: (blk[r], 0)),
            pl.BlockSpec((None, d, tf), lambda r, f, blk, ex, used: (ex[r], 0, fcol(f, used, r))),
            pl.BlockSpec((None, d, tf), lambda r, f, blk, ex, used: (ex[r], 0, nf + fcol(f, used, r))),
            pl.BlockSpec((None, tf, d), lambda r, f, blk, ex, used: (ex[r], fcol(f, used, r), 0)),
        ],
        out_specs=pl.BlockSpec((gr, d), lambda r, f, blk, ex, used: (blk[r], 0)),
        scratch_shapes=[pltpu.VMEM((gr, d), F32)],
    )
    return pl.pallas_call(
        _expert_kernel,
        grid_spec=grid_spec,
        out_shape=jax.ShapeDtypeStruct((n_rows, d), BF16),
        compiler_params=_params(("arbitrary", "arbitrary")),
        name="moe_experts",
    )(*gemm_list, xs, w_gu, w_gu, w_down)


def _combine_kernel(r_ref, c_ref, first_ref, final_ref, valid_ref, y_ref, info_ref, x_ref, gate_ref,
                    op_ref, os_ref, acc_scr, *, n_prompt_chunks):
    w = pl.program_id(0)

    @pl.when(valid_ref[w] == 1)
    def _():
        @pl.when(first_ref[w] == 1)
        def _():
            acc_scr[...] = jnp.zeros(acc_scr.shape, F32)

        rt, ch = y_ref.shape[0], x_ref.shape[0]
        info = info_ref[...]
        row = (r_ref[w] * rt + lax.broadcasted_iota(jnp.int32, (ch, rt), 1)).astype(F32)
        weights = (jnp.where(info[:, 0:1] == row, info[:, 2:3], 0.0)
                   + jnp.where(info[:, 1:2] == row, info[:, 3:4], 0.0)).astype(BF16)
        acc_scr[...] += _dot(weights, y_ref[...])

        is_prompt = c_ref[w] < n_prompt_chunks

        @pl.when((final_ref[w] == 1) & is_prompt)
        def _():
            op_ref[...] = x_ref[...] + gate_ref[...] * acc_scr[...]

        @pl.when((final_ref[w] == 1) & jnp.logical_not(is_prompt))
        def _():
            os_ref[...] = x_ref[...] + gate_ref[...] * acc_scr[...]


def _moe_combine(y, tok_info, x, mods, worklist, *, n_prompt, tok_per_cond):
    t, d = x.shape
    ch, rt = ROUTE_CHUNK, ROW_TILE
    n_items = worklist[0].shape[0]
    n_pc = n_prompt // ch
    row_fn = _cond_row_joint(ch, n_prompt, tok_per_cond)
    gate = mods.spec(_Mods.GATE2, lambda w, r, c, *_: row_fn(c[w]))
    grid_spec = pltpu.PrefetchScalarGridSpec(
        num_scalar_prefetch=5,
        grid=(n_items,),
        in_specs=[
            pl.BlockSpec((rt, d), lambda w, r, c, *_: (r[w], 0)),
            pl.BlockSpec((ch, tok_info.shape[1]), lambda w, r, c, *_: (c[w], 0)),
            pl.BlockSpec((ch, d), lambda w, r, c, *_: (c[w], 0)),
            gate,
        ],
        out_specs=[
            pl.BlockSpec((ch, d), lambda w, r, c, *_: (jnp.minimum(c[w], n_pc - 1), 0)),
            pl.BlockSpec((ch, d), lambda w, r, c, *_: (jnp.maximum(c[w] - n_pc, 0), 0)),
        ],
        scratch_shapes=[pltpu.VMEM((ch, d), F32)],
    )
    return pl.pallas_call(
        functools.partial(_combine_kernel, n_prompt_chunks=n_pc),
        grid_spec=grid_spec,
        out_shape=[jax.ShapeDtypeStruct((n_prompt, d), F32), jax.ShapeDtypeStruct((t - n_prompt, d), F32)],
        compiler_params=_params(("arbitrary",)),
        name="moe_combine",
    )(*worklist, y, tok_info, x, mods.table)


def kernel(x_prompt, x_sample, c, cache_k, cache_v, c_ctx, ada_w, ada_b, norm1_g, norm2_g,
           attn_w_qkv, attn_w_o, attn_q_norm, attn_k_norm, attn_lambda, attn_subln_g,
           pool_w, pool_scale, ffn_w_gu, ffn_w_down,
           moe_w_router, moe_b_router, moe_w_gu, moe_w_down):
    b_ctx, l_ctx, d = x_prompt.shape
    b_dec, l_dec, _ = x_sample.shape
    depth = ada_w.shape[0]
    n_even, l_past, heads, v_dim = cache_k.shape[1:]
    head_dim = v_dim // 2
    assert head_dim == LANES and depth == 2 and l_dec % GRID_W == 0
    n_exp = moe_w_router.shape[-1]
    tp_, ts_ = b_ctx * l_ctx, b_dec * l_dec
    t_all = tp_ + ts_

    cond_rows = 8 * ((1 + b_dec + 7) // 8)
    cond = jnp.zeros((cond_rows, d), F32).at[0].set(c_ctx).at[1:1 + b_dec].set(c)
    mod_table = _adaln_mods(cond, ada_w, ada_b).reshape(depth * cond_rows * 6, 1, d)

    xp = x_prompt.reshape(tp_, d)
    xs = x_sample.reshape(ts_, d)

    layer, j = 0, 0
    mods = _Mods(mod_table, cond_rows, layer)
    g1 = norm1_g[layer].reshape(1, d)
    w_qkv = attn_w_qkv[j].astype(BF16)
    q_gain = (attn_q_norm[j] * (head_dim ** -0.5 * LOG2E)).reshape(1, head_dim)
    k_gain = attn_k_norm[j].reshape(1, head_dim)
    tables = _rope_tables(l_dec)
    prompt = dict(cond_base=0, tok_per_cond=tp_)
    sample = dict(cond_base=1, tok_per_cond=l_dec)

    q_p, k_p, v_p = _qkv_proj(xp, mods, g1, w_qkv, q_gain, k_gain, None, kv_dtype=F32, **prompt)
    q_s, k_s, v_s = _qkv_proj(xs, mods, g1, w_qkv, q_gain, k_gain, tables, kv_dtype=BF16, **sample)

    lam_params = attn_lambda[j].astype(F32)
    subln = attn_subln_g[j].reshape(1, v_dim)
    o_p = _diff_attention(q_p, k_p, v_p, None, None, lam_params, subln,
                          batch=b_ctx, seq=l_ctx, layer=layer)
    ck = cache_k[:, j].reshape(b_dec * l_past, heads * v_dim)
    cv = cache_v[:, j].reshape(b_dec * l_past, heads * v_dim)
    o_s = _diff_attention(q_s, k_s, v_s, ck, cv, lam_params, subln,
                          batch=b_dec, seq=l_dec, layer=layer)

    w_o = attn_w_o[j].astype(BF16)
    x1 = _out_proj_residual(o_p, w_o, xp, mods, None, total_rows=t_all, row_off=0, **prompt)
    x1 = _out_proj_residual(o_s, w_o, xs, mods, x1, total_rows=t_all, row_off=tp_, **sample)

    x2 = _dense_ffn(x1, mods, norm2_g[layer].reshape(1, d), ffn_w_gu[j].astype(BF16),
                    ffn_w_down[j].astype(BF16), n_prompt=tp_, tok_per_cond=l_dec)

    layer, j = 1, 0
    mods = _Mods(mod_table, cond_rows, layer)
    x3 = _pool_mixer(x2, mods, norm1_g[layer].reshape(1, d), pool_w[j].astype(BF16),
                     pool_scale[j].reshape(1, d), n_prompt=tp_, prompt_seq=l_ctx, sample_seq=l_dec)

    h2, route, counts = _moe_router(x3, mods, norm2_g[layer].reshape(1, d), moe_w_router[j],
                                    moe_b_router[j], n_prompt=tp_, tok_per_cond=l_dec)
    dest_rows, tok_info, gather_list, combine_list, gemm_list, n_rows = _dispatch_plan(
        route, counts, n_exp)
    rows_sorted = _moe_gather(h2, dest_rows, gather_list, n_rows)
    y_sorted = _moe_experts(rows_sorted, moe_w_gu[j].astype(BF16), moe_w_down[j].astype(BF16),
                            gemm_list)
    y_p, y_s = _moe_combine(y_sorted, tok_info, x3, mods, combine_list, n_prompt=tp_, tok_per_cond=l_dec)

    y_prompt = y_p.reshape(b_ctx, l_ctx, d)
    y_sample = y_s.reshape(b_dec, l_dec, d)
    state_k = k_p.reshape(b_ctx, 1, l_ctx, heads, v_dim)
    state_v = v_p.reshape(b_ctx, 1, l_ctx, heads, v_dim)
    return (y_prompt, y_sample, state_k, state_v)
```

```python
import functools
import math

import jax
import jax.numpy as jnp
from jax import lax
from jax.experimental import pallas as pl
from jax.experimental.pallas import tpu as pltpu

F32 = jnp.float32
BF16 = jnp.bfloat16

GRID_W = 64
ROPE_BASE = 10000.0
POOL_WINDOWS = (2, 4, 8, 16)
TOP_K = 2
EPS = 1e-6
LOG2E = 1.4426950408889634

LANES = 128
VMEM_LIMIT = 52 * 2**20

ROUTE_CHUNK = 256
GEMM_ROWS = 512


def _lambda_init(layer):
    return 0.8 - 0.6 * math.exp(-0.3 * layer)


def _tile(n, pref, mult):
    best = None
    t = mult
    while t <= min(n, pref):
        if n % t == 0:
            best = t
        t += mult
    if best is None:
        raise ValueError(f"no tile for {n} (multiple of {mult}, <= {pref})")
    return best


def _params(semantics):
    return pltpu.CompilerParams(dimension_semantics=semantics, vmem_limit_bytes=VMEM_LIMIT)


def _dot(a, b):
    return jnp.dot(a, b, preferred_element_type=F32)


def _mask_bf16(m):
    return jnp.where(m, 1.0, 0.0).astype(BF16)


def _split_bf16(x):
    hi = x.astype(BF16)
    lo = (x - hi.astype(F32)).astype(BF16)
    return hi, lo


def _norm_modulate(x, g, scale, shift):
    ms = jnp.mean(x * x, axis=-1, keepdims=True)
    return x * lax.rsqrt(ms + EPS) * (g * (1.0 + scale)) + shift


def _adaln_kernel(cond_ref, w_ref, b_ref, o_ref):
    c = cond_ref[...]
    s = c * jax.nn.sigmoid(c)
    s_hi, s_lo = _split_bf16(s)
    w_hi, w_lo = _split_bf16(w_ref[...])
    o_ref[...] = _dot(s_hi, w_hi) + _dot(s_lo, w_hi) + _dot(s_hi, w_lo) + b_ref[...]


def _adaln_mods(cond, ada_w, ada_b):
    depth, d, n = ada_w.shape
    rows = cond.shape[0]
    tn = _tile(n, 512, LANES)
    return pl.pallas_call(
        _adaln_kernel,
        grid=(depth, n // tn),
        in_specs=[
            pl.BlockSpec((rows, d), lambda l, j: (0, 0)),
            pl.BlockSpec((None, d, tn), lambda l, j: (l, 0, j)),
            pl.BlockSpec((None, 1, tn), lambda l, j: (l, 0, j)),
        ],
        out_specs=pl.BlockSpec((None, rows, tn), lambda l, j: (l, 0, j)),
        out_shape=jax.ShapeDtypeStruct((depth, rows, n), F32),
        compiler_params=_params(("arbitrary", "arbitrary")),
        name="adaln_mods",
    )(cond, ada_w, ada_b.reshape(depth, 1, n))


class _Mods:
    SHIFT1, SCALE1, GATE1, SHIFT2, SCALE2, GATE2 = range(6)

    def __init__(self, table, rows, layer):
        self.table = table
        self.rows = rows
        self.layer = layer

    def spec(self, which, row_fn, width=None, col_fn=None):
        d = self.table.shape[-1]
        width = d if width is None else width
        base = self.layer * self.rows

        def index(*ids):
            col = 0 if col_fn is None else col_fn(*ids)
            return ((base + row_fn(*ids)) * 6 + which, 0, col)

        return pl.BlockSpec((None, 1, width), index)


def _qkv_kernel(*refs, rope, tn, n_sec):
    x_ref, sh_ref, sc_ref, g_ref, w_ref, qg_ref, kg_ref = refs[:7]
    rest = list(refs[7:])
    cos_ref, sin_ref = (rest.pop(0), rest.pop(0)) if rope else (None, None)
    q_ref, k_ref, v_ref, h_scr = rest
    j = pl.program_id(1)

    @pl.when(j == 0)
    def _():
        h = _norm_modulate(x_ref[...], g_ref[...], sc_ref[...], sh_ref[...])
        h_scr[...] = h.astype(BF16)

    acc = _dot(h_scr[...], w_ref[...])

    def qk_epilogue(gain_ref, out_ref):
        r_idx = lax.broadcasted_iota(jnp.int32, (2 * LANES, 2 * LANES), 0)
        c_idx = lax.broadcasted_iota(jnp.int32, (2 * LANES, 2 * LANES), 1)
        if rope:
            swap = (r_idx >= LANES) & (c_idx >= LANES) & (r_idx - LANES == jnp.bitwise_xor(c_idx - LANES, LANES // 4))
            rhs = _mask_bf16(((r_idx < LANES) & (c_idx < LANES)) | swap)
        else:
            rhs = jnp.ones((2 * LANES, LANES), BF16)
        for g in range(tn // LANES):
            y = acc[:, g * LANES:(g + 1) * LANES]
            yg = y * gain_ref[...]
            if rope:
                res = _dot(jnp.concatenate([(y * y).astype(BF16), yg.astype(BF16)], axis=1), rhs)
                inv = lax.rsqrt(res[:, :LANES] * (1.0 / LANES) + EPS)
                out = (yg * cos_ref[...] + res[:, LANES:] * sin_ref[...]) * inv
            else:
                ss = _dot(jnp.concatenate(_split_bf16(y * y), axis=1), rhs)
                out = yg * lax.rsqrt(ss * (1.0 / LANES) + EPS)
            out_ref[:, g * LANES:(g + 1) * LANES] = out.astype(out_ref.dtype)

    @pl.when(j < n_sec)
    def _():
        qk_epilogue(qg_ref, q_ref)

    @pl.when((j >= n_sec) & (j < 2 * n_sec))
    def _():
        qk_epilogue(kg_ref, k_ref)

    @pl.when(j >= 2 * n_sec)
    def _():
        v_ref[...] = acc.astype(v_ref.dtype)


def _qkv_proj(x, mods, norm_g, w, q_gain, k_gain, tables, *, cond_base, tok_per_cond, kv_dtype):
    t, d = x.shape
    tm = _tile(math.gcd(t, tok_per_cond), 1024, 8)
    tn = _tile(d, 512, LANES)
    n_sec = d // tn
    rope = tables is not None
    row = lambda i, j: cond_base + (i * tm) // tok_per_cond
    in_specs = [
        pl.BlockSpec((tm, d), lambda i, j: (i, 0)),
        mods.spec(_Mods.SHIFT1, row),
        mods.spec(_Mods.SCALE1, row),
        pl.BlockSpec((1, d), lambda i, j: (0, 0)),
        pl.BlockSpec((d, tn), lambda i, j: (0, j)),
        pl.BlockSpec((1, LANES), lambda i, j: (0, 0)),
        pl.BlockSpec((1, LANES), lambda i, j: (0, 0)),
    ]
    args = [x, mods.table, mods.table, norm_g, w, q_gain, k_gain]
    if rope:
        nblk = tables[0].shape[0] // tm
        for tab in tables:
            in_specs.append(pl.BlockSpec((tm, LANES), lambda i, j: (i % nblk, 0)))
            args.append(tab)
    out_specs = [
        pl.BlockSpec((tm, tn), lambda i, j, s=s: (i, jnp.clip(j - s * n_sec, 0, n_sec - 1)))
        for s in range(3)
    ]
    return pl.pallas_call(
        functools.partial(_qkv_kernel, rope=rope, tn=tn, n_sec=n_sec),
        grid=(t // tm, 3 * n_sec),
        in_specs=in_specs,
        out_specs=out_specs,
        out_shape=[jax.ShapeDtypeStruct((t, d), BF16), jax.ShapeDtypeStruct((t, d), kv_dtype),
                   jax.ShapeDtypeStruct((t, d), kv_dtype)],
        scratch_shapes=[pltpu.VMEM((tm, d), BF16)],
        compiler_params=_params(("arbitrary", "arbitrary")),
        name="qkv_proj_rope" if rope else "qkv_proj",
    )(*args)


def _rope_tables(seq):
    n_freq = LANES // 4
    pos = jnp.arange(seq, dtype=jnp.int32)
    r = (pos // GRID_W).astype(F32)
    col = (pos % GRID_W).astype(F32)
    freqs = ROPE_BASE ** (-jnp.arange(n_freq, dtype=F32) / n_freq)
    cos_t, sin_t = [], []
    for p in (r, col):
        ang = p[:, None] * freqs
        cos, sin = jnp.cos(ang), jnp.sin(ang)
        cos_t += [cos, cos]
        sin_t += [-sin, sin]
    return jnp.concatenate(cos_t, axis=1), jnp.concatenate(sin_t, axis=1)


def _attn_kernel(*refs, n_cache, n_chunks, tk, lambda_init):
    lam_ref, g_ref, q_ref = refs[:3]
    if n_cache:
        kc_ref, vtc_ref, k_ref, vt_ref, o_ref, acc_ref = refs[3:]
    else:
        k_ref, vt_ref, o_ref, acc_ref = refs[3:]
    q = q_ref[...]
    qs = (q[:, :LANES], q[:, LANES:])

    def scores(kblk):
        return [lax.dot_general(kblk[:, c * LANES:(c + 1) * LANES], qs[c],
                                (((1,), (1,)), ((), ())), preferred_element_type=F32)
                for c in range(2)]

    def accumulate(sts, vt, stats):
        out = []
        for c in range(2):
            st = sts[c]
            m_cur = jnp.max(st, axis=0, keepdims=True)
            if stats is None:
                m_new = m_cur
                p = jnp.exp2(st - m_new)
                l_new = jnp.sum(p, axis=0, keepdims=True)
                acc_ref[c] = _dot(vt, p.astype(BF16))
            else:
                m_prev, l_prev = stats[c]
                m_new = jnp.maximum(m_prev, m_cur)
                alpha = jnp.exp2(m_prev - m_new)
                p = jnp.exp2(st - m_new)
                l_new = alpha * l_prev + jnp.sum(p, axis=0, keepdims=True)
                acc_ref[c] = alpha * acc_ref[c] + _dot(vt, p.astype(BF16))
            out.append((m_new, l_new))
        return out

    chunks = []
    if n_cache:
        chunks.append((lambda: kc_ref[...].astype(BF16), lambda: vtc_ref[...]))
    for i in range(n_chunks):
        chunks.append((lambda i=i: k_ref[i * tk:(i + 1) * tk, :].astype(BF16), lambda i=i: vt_ref[i]))
    stats = None
    sts = scores(chunks[0][0]())
    for n, (_, values) in enumerate(chunks):
        nxt = scores(chunks[n + 1][0]()) if n + 1 < len(chunks) else None
        stats = accumulate(sts, values(), stats)
        sts = nxt

    lp = lam_ref[...]
    lam = (jnp.exp(jnp.sum(lp[0:1] * lp[1:2], axis=-1, keepdims=True))
           - jnp.exp(jnp.sum(lp[2:3] * lp[3:4], axis=-1, keepdims=True)) + lambda_init)
    o_t = acc_ref[0] / stats[0][1] - lam * (acc_ref[1] / stats[1][1])
    o = o_t.T
    o = o * lax.rsqrt(jnp.mean(o * o, axis=-1, keepdims=True) + EPS) * g_ref[...]
    o_ref[...] = (o * (1.0 - lambda_init)).astype(o_ref.dtype)


def _chunked_transpose(v, batch, seq, heads, vd, tk):
    v = v.astype(BF16).reshape(batch, seq // tk, tk, heads, vd)
    return v.transpose(0, 3, 1, 4, 2).reshape(batch * heads, seq // tk, vd, tk)


def _diff_attention(q, k, v, cache_k, cache_v, lam_params, subln_g, *, batch, seq, layer):
    t, d = q.shape
    vd = subln_g.shape[-1]
    heads = d // vd
    n_cache = 0 if cache_k is None else cache_k.shape[0] // batch
    tq = _tile(seq, 512, LANES)
    tk = _tile(seq, 512, LANES)
    nq = seq // tq
    n_chunks = seq // tk
    in_specs = [
        pl.BlockSpec(lam_params.shape, lambda b, h, i: (0, 0)),
        pl.BlockSpec((1, vd), lambda b, h, i: (0, 0)),
        pl.BlockSpec((tq, vd), lambda b, h, i: (b * nq + i, h)),
    ]
    args = [lam_params, subln_g, q]
    if n_cache:
        in_specs += [pl.BlockSpec((n_cache, vd), lambda b, h, i: (b, h)),
                     pl.BlockSpec((None, None, vd, n_cache), lambda b, h, i: (b * heads + h, 0, 0, 0))]
        args += [cache_k, _chunked_transpose(cache_v, batch, n_cache, heads, vd, n_cache)]
    in_specs += [pl.BlockSpec((seq, vd), lambda b, h, i: (b, h)),
                 pl.BlockSpec((None, n_chunks, vd, tk), lambda b, h, i: (b * heads + h, 0, 0, 0))]
    args += [k, _chunked_transpose(v, batch, seq, heads, vd, tk)]
    return pl.pallas_call(
        functools.partial(_attn_kernel, n_cache=n_cache, n_chunks=n_chunks, tk=tk,
                          lambda_init=_lambda_init(layer)),
        grid=(batch, heads, nq),
        in_specs=in_specs,
        out_specs=pl.BlockSpec((tq, vd), lambda b, h, i: (b * nq + i, h)),
        out_shape=jax.ShapeDtypeStruct((t, d), BF16),
        scratch_shapes=[pltpu.VMEM((2, vd, tq), F32)],
        compiler_params=_params(("arbitrary", "arbitrary", "arbitrary")),
        name=f"diff_attention_{'latent' if n_cache else 'context'}",
    )(*args)


def _out_proj_kernel(op_ref, os_ref, w_ref, xp_ref, xs_ref, gate_ref, out_ref, *, n_prompt_tiles):
    is_prompt = pl.program_id(0) < n_prompt_tiles

    @pl.when(is_prompt)
    def _():
        out_ref[...] = xp_ref[...] + gate_ref[...] * _dot(op_ref[...], w_ref[...])

    @pl.when(jnp.logical_not(is_prompt))
    def _():
        out_ref[...] = xs_ref[...] + gate_ref[...] * _dot(os_ref[...], w_ref[...])


def _out_proj_residual(o_p, o_s, w, x_p, x_s, mods, *, tok_per_cond):
    n_prompt, d = x_p.shape
    t = n_prompt + x_s.shape[0]
    tm = _tile(math.gcd(n_prompt, tok_per_cond), 1024, 8)
    tn = _tile(d, 512, LANES)
    n_pt = n_prompt // tm
    row = _cond_row_joint(tm, n_prompt, tok_per_cond)
    p_rows = lambda i: jnp.minimum(i, n_pt - 1)
    s_rows = lambda i: jnp.maximum(i - n_pt, 0)
    return pl.pallas_call(
        functools.partial(_out_proj_kernel, n_prompt_tiles=n_pt),
        grid=(t // tm, d // tn),
        in_specs=[
            pl.BlockSpec((tm, d), lambda i, j: (p_rows(i), 0)),
            pl.BlockSpec((tm, d), lambda i, j: (s_rows(i), 0)),
            pl.BlockSpec((d, tn), lambda i, j: (0, j)),
            pl.BlockSpec((tm, tn), lambda i, j: (p_rows(i), jnp.where(i < n_pt, j, d // tn - 1))),
            pl.BlockSpec((tm, tn), lambda i, j: (s_rows(i), jnp.where(i < n_pt, 0, j))),
            mods.spec(_Mods.GATE1, row, width=tn, col_fn=lambda i, j: j),
        ],
        out_specs=pl.BlockSpec((tm, tn), lambda i, j: (i, j)),
        out_shape=jax.ShapeDtypeStruct((t, d), F32),
        compiler_params=_params(("arbitrary", "arbitrary")),
        name="attn_out_proj",
    )(o_p, o_s, w, x_p, x_s, mods.table)


def _ffn_kernel(x_ref, sh_ref, sc_ref, gate_ref, g_ref, wg_ref, wu_ref, wd_ref, o_ref, h_scr, acc_scr):
    f = pl.program_id(1)

    @pl.when(f == 0)
    def _():
        h = _norm_modulate(x_ref[...], g_ref[...], sc_ref[...], sh_ref[...])
        h_scr[...] = h.astype(BF16)
        acc_scr[...] = jnp.zeros(acc_scr.shape, F32)

    h = h_scr[...]
    gte = _dot(h, wg_ref[...])
    up = _dot(h, wu_ref[...])
    act = (gte * jax.nn.sigmoid(gte) * up).astype(BF16)
    acc_scr[...] += _dot(act, wd_ref[...])

    @pl.when(f == pl.num_programs(1) - 1)
    def _():
        o_ref[...] = x_ref[...] + gate_ref[...] * acc_scr[...]


def _cond_row_joint(tm, n_prompt, tok_per_cond):
    def row(i, *_):
        tok = i * tm
        return jnp.where(tok < n_prompt, 0, 1 + (tok - n_prompt) // tok_per_cond)
    return row


def _dense_ffn(x, mods, norm_g, w_gu, w_down, *, n_prompt, tok_per_cond):
    t, d = x.shape
    ff = w_down.shape[0]
    tm = _tile(math.gcd(n_prompt, tok_per_cond), 512, 8)
    tf = _tile(ff, 512, LANES)
    nf = ff // tf
    row = _cond_row_joint(tm, n_prompt, tok_per_cond)
    return pl.pallas_call(
        _ffn_kernel,
        grid=(t // tm, nf),
        in_specs=[
            pl.BlockSpec((tm, d), lambda i, f: (i, 0)),
            mods.spec(_Mods.SHIFT2, row),
            mods.spec(_Mods.SCALE2, row),
            mods.spec(_Mods.GATE2, row),
            pl.BlockSpec((1, d), lambda i, f: (0, 0)),
            pl.BlockSpec((d, tf), lambda i, f: (0, f)),
            pl.BlockSpec((d, tf), lambda i, f: (0, nf + f)),
            pl.BlockSpec((tf, d), lambda i, f: (f, 0)),
        ],
        out_specs=pl.BlockSpec((tm, d), lambda i, f: (i, 0)),
        out_shape=jax.ShapeDtypeStruct((t, d), F32),
        scratch_shapes=[pltpu.VMEM((tm, d), BF16), pltpu.VMEM((tm, d), F32)],
        compiler_params=_params(("arbitrary", "arbitrary")),
        name="dense_ffn",
    )(x, mods.table, mods.table, mods.table, norm_g, w_gu, w_gu, w_down)


def _pool_kernel(x_ref, xp_ref, xn_ref, sh_ref, sc_ref, gate_ref, g_ref, w_ref, ps_ref, o_ref, *,
                 tp, halo, n_prompt_tiles, prompt_tiles_per_seq, sample_tiles_per_seq, group_dim):
    i = pl.program_id(0)
    is_prompt = i < n_prompt_tiles
    tiles_per_seq = jnp.where(is_prompt, prompt_tiles_per_seq, sample_tiles_per_seq)
    local = lax.rem(jnp.where(is_prompt, i, i - n_prompt_tiles), tiles_per_seq)
    first = local == 0
    last = local == tiles_per_seq - 1
    seq_len = tiles_per_seq * tp

    g, sc, sh = g_ref[...], sc_ref[...], sh_ref[...]
    x = x_ref[...]
    h = _norm_modulate(x, g, sc, sh)
    h_prev = jnp.where(first, 0.0, _norm_modulate(xp_ref[...], g, sc, sh))
    h_next = jnp.where(last, 0.0, _norm_modulate(xn_ref[...], g, sc, sh))
    d = h.shape[-1]
    h_halo = jnp.concatenate([h_prev, h_next, jnp.zeros((LANES - 2 * halo, d), F32)], axis=0)

    t_idx = lax.broadcasted_iota(jnp.int32, (tp, tp), 0)
    s_idx = lax.broadcasted_iota(jnp.int32, (tp, tp), 1)
    t_h = lax.broadcasted_iota(jnp.int32, (tp, LANES), 0)
    u_h = lax.broadcasted_iota(jnp.int32, (tp, LANES), 1)
    off_h = jnp.where(u_h < halo, u_h - halo, tp + u_h - halo)
    pos = local * tp + lax.broadcasted_iota(jnp.int32, (tp, 1), 0)

    for grp, win in enumerate(POOL_WINDOWS):
        lo, hi = win // 2, win // 2 - 1
        cols = slice(grp * group_dim, (grp + 1) * group_dim)
        band = _mask_bf16((s_idx >= t_idx - lo) & (s_idx <= t_idx + hi))
        band_h = _mask_bf16((u_h < 2 * halo) & (off_h >= t_h - lo) & (off_h <= t_h + hi))
        c_hi, c_lo = _split_bf16(h[:, cols])
        e_hi, e_lo = _split_bf16(h_halo[:, cols])
        total = (_dot(band, c_hi) + _dot(band, c_lo)) + (_dot(band_h, e_hi) + _dot(band_h, e_lo))
        cnt = (jnp.minimum(pos + hi, seq_len - 1) - jnp.maximum(pos - lo, 0) + 1).astype(F32)
        pooled = (total / cnt - h[:, cols]).astype(BF16)
        y = _dot(pooled, w_ref[grp]) * ps_ref[:, cols]
        o_ref[:, cols] = x[:, cols] + gate_ref[:, cols] * y


def _pool_mixer(x, mods, norm_g, pool_w, pool_scale, *, n_prompt, prompt_seq, sample_seq):
    t, d = x.shape
    halo = max(POOL_WINDOWS) // 2
    tp = _tile(math.gcd(prompt_seq, sample_seq), 256, 8)
    n_groups, group_dim = pool_w.shape[0], pool_w.shape[1]
    hb = tp // halo
    n_halo_blocks = t // halo
    row = _cond_row_joint(tp, n_prompt, sample_seq)
    return pl.pallas_call(
        functools.partial(_pool_kernel, tp=tp, halo=halo, n_prompt_tiles=n_prompt // tp,
                          prompt_tiles_per_seq=prompt_seq // tp,
                          sample_tiles_per_seq=sample_seq // tp, group_dim=group_dim),
        grid=(t // tp,),
        in_specs=[
            pl.BlockSpec((tp, d), lambda i: (i, 0)),
            pl.BlockSpec((halo, d), lambda i: (jnp.maximum(i * hb - 1, 0), 0)),
            pl.BlockSpec((halo, d), lambda i: (jnp.minimum((i + 1) * hb, n_halo_blocks - 1), 0)),
            mods.spec(_Mods.SHIFT1, row),
            mods.spec(_Mods.SCALE1, row),
            mods.spec(_Mods.GATE1, row),
            pl.BlockSpec((1, d), lambda i: (0, 0)),
            pl.BlockSpec((n_groups, group_dim, group_dim), lambda i: (0, 0, 0)),
            pl.BlockSpec((1, d), lambda i: (0, 0)),
        ],
        out_specs=pl.BlockSpec((tp, d), lambda i: (i, 0)),
        out_shape=jax.ShapeDtypeStruct((t, d), F32),
        compiler_params=_params(("arbitrary",)),
        name="pool_mixer",
    )(x, x, x, mods.table, mods.table, mods.table, norm_g, pool_w, pool_scale)


def _router_kernel(x_ref, sh_ref, sc_ref, g_ref, wr_ref, br_ref, route_ref, cnt_ref):
    h = _norm_modulate(x_ref[...], g_ref[...], sc_ref[...], sh_ref[...])
    h_hi, h_lo = _split_bf16(h)
    w_hi, w_lo = _split_bf16(wr_ref[...])
    logits = _dot(h_hi, w_hi) + _dot(h_lo, w_hi) + _dot(h_hi, w_lo) + br_ref[...]

    rows = logits.shape[0]
    lane = lax.broadcasted_iota(jnp.int32, logits.shape, 1).astype(F32)
    m1 = jnp.max(logits, axis=-1, keepdims=True)
    i1 = jnp.min(jnp.where(logits == m1, lane, float(LANES)), axis=-1, keepdims=True)
    oh1 = lane == i1
    rest = jnp.where(oh1, -jnp.inf, logits)
    m2 = jnp.max(rest, axis=-1, keepdims=True)
    i2 = jnp.min(jnp.where(rest == m2, lane, float(LANES)), axis=-1, keepdims=True)
    oh2 = lane == i2
    e = jnp.exp(m2 - m1)
    gate_a = 1.0 / (1.0 + e)
    gate_b = e / (1.0 + e)

    sel = jnp.where(oh1 | oh2, 1.0, 0.0)
    r_idx = lax.broadcasted_iota(jnp.int32, (rows, rows), 0)
    c_idx = lax.broadcasted_iota(jnp.int32, (rows, rows), 1)
    earlier = _mask_bf16(c_idx < r_idx)
    rank = _dot(earlier, sel.astype(BF16))
    rank_a = jnp.sum(jnp.where(oh1, rank, 0.0), axis=-1, keepdims=True)
    rank_b = jnp.sum(jnp.where(oh2, rank, 0.0), axis=-1, keepdims=True)

    route = jnp.zeros(logits.shape, F32)
    for k, val in enumerate((i1, i2, rank_a, rank_b, gate_a, gate_b)):
        route = jnp.where(lane == float(k), val, route)
    route_ref[...] = route
    cnt_ref[...] = jnp.sum(sel, axis=0, keepdims=True)


def _moe_router(x, mods, norm_g, w_router, b_router, *, n_prompt, tok_per_cond):
    t, d = x.shape
    n_exp = w_router.shape[1]
    ch = ROUTE_CHUNK
    wr = jnp.zeros((d, LANES), F32).at[:, :n_exp].set(w_router.astype(F32))
    br = jnp.full((1, LANES), -jnp.inf, F32).at[0, :n_exp].set(b_router.astype(F32))
    row = _cond_row_joint(ch, n_prompt, tok_per_cond)
    return pl.pallas_call(
        _router_kernel,
        grid=(t // ch,),
        in_specs=[
            pl.BlockSpec((ch, d), lambda i: (i, 0)),
            mods.spec(_Mods.SHIFT2, row),
            mods.spec(_Mods.SCALE2, row),
            pl.BlockSpec((1, d), lambda i: (0, 0)),
            pl.BlockSpec((d, LANES), lambda i: (0, 0)),
            pl.BlockSpec((1, LANES), lambda i: (0, 0)),
        ],
        out_specs=[
            pl.BlockSpec((ch, LANES), lambda i: (i, 0)),
            pl.BlockSpec((None, 1, LANES), lambda i: (i, 0, 0)),
        ],
        out_shape=[
            jax.ShapeDtypeStruct((t, LANES), F32),
            jax.ShapeDtypeStruct((t // ch, 1, LANES), F32),
        ],
        compiler_params=_params(("arbitrary",)),
        name="moe_router",
    )(x, mods.table, mods.table, norm_g, wr, br)


def _dispatch_plan(route, counts, n_exp):
    ch, gr = ROUTE_CHUNK, GEMM_ROWS
    t = route.shape[0]
    n_chunks = t // ch
    i32 = jnp.int32
    cnt = counts[:, 0, :n_exp].astype(i32)
    total = cnt.sum(0)
    padded = ((total + gr - 1) // gr) * gr
    off = jnp.cumsum(padded) - padded
    start = off[None, :] + jnp.cumsum(cnt, axis=0) - cnt

    def dest(choice):
        e = route[:, choice].astype(i32).reshape(n_chunks, ch, 1)
        hit = e == jnp.arange(n_exp, dtype=i32)
        base = jnp.sum(jnp.where(hit, start[:, None, :], 0), axis=-1)
        return base.reshape(t) + route[:, 2 + choice].astype(i32)

    dest_a, dest_b = dest(0), dest(1)

    n_gemm_tiles = (t * TOP_K) // gr + n_exp

    tile_lo = jnp.arange(n_gemm_tiles, dtype=i32) * gr
    used = tile_lo < (off + padded)[-1]
    ends = off + padded
    expert_of = jnp.minimum(jnp.sum(tile_lo[:, None] >= ends[None, :], axis=1), n_exp - 1).astype(i32)
    n_used = jnp.maximum(used.sum(), 1)
    clamp = jnp.minimum(jnp.arange(n_gemm_tiles, dtype=i32), n_used - 1).astype(i32)
    gemm_list = (clamp, expert_of[clamp], used.astype(i32))

    dest_rows = jnp.stack([dest_a.reshape(n_chunks, ch), dest_b.reshape(n_chunks, ch)], axis=1)
    gates = jnp.concatenate([route[:, 4:6], jnp.zeros((t, 6), F32)], axis=1)
    return dest_rows, gates, gemm_list, n_gemm_tiles * gr


def _wait_rows(src, dst, sem, n_rows):
    pltpu.make_async_copy(src.at[pl.ds(0, n_rows), :], dst.at[pl.ds(0, n_rows), :], sem).wait()


def _scatter_kernel(dest_ref, x_ref, sh_ref, sc_ref, g_ref, init_hbm, out_hbm, h_scr, sem):
    del init_hbm
    ch = x_ref.shape[0]
    h_scr[...] = _norm_modulate(x_ref[...], g_ref[...], sc_ref[...], sh_ref[...])

    def issue(j, carry):
        for k in range(TOP_K):
            pltpu.make_async_copy(h_scr.at[pl.ds(j, 1), :], out_hbm.at[pl.ds(dest_ref[k, j], 1), :],
                                  sem).start(priority=k)
        return carry

    lax.fori_loop(0, ch, issue, 0, unroll=8)
    for _ in range(TOP_K):
        _wait_rows(h_scr, out_hbm, sem, ch)


def _moe_scatter(x, mods, norm_g, dest_rows, n_rows, *, n_prompt, tok_per_cond):
    t, d = x.shape
    ch = ROUTE_CHUNK
    row = _cond_row_joint(ch, n_prompt, tok_per_cond)
    return pl.pallas_call(
        _scatter_kernel,
        grid=(t // ch,),
        in_specs=[
            pl.BlockSpec((None, TOP_K, ch), lambda i: (i, 0, 0), memory_space=pltpu.SMEM),
            pl.BlockSpec((ch, d), lambda i: (i, 0)),
            mods.spec(_Mods.SHIFT2, row),
            mods.spec(_Mods.SCALE2, row),
            pl.BlockSpec((1, d), lambda i: (0, 0)),
            pl.BlockSpec(memory_space=pl.ANY),
        ],
        out_specs=pl.BlockSpec(memory_space=pl.ANY),
        out_shape=jax.ShapeDtypeStruct((n_rows, d), F32),
        scratch_shapes=[pltpu.VMEM((ch, d), F32), pltpu.SemaphoreType.DMA(())],
        input_output_aliases={5: 0},
        compiler_params=_params(("arbitrary",)),
        name="moe_scatter",
    )(dest_rows, x, mods.table, mods.table, norm_g, jnp.zeros((n_rows, d), F32))


def _expert_kernel(blk_ref, exp_ref, used_ref, x_ref, wg_ref, wu_ref, wd_ref, o_ref, acc_scr, x_scr):
    r, f = pl.program_id(0), pl.program_id(1)

    @pl.when(used_ref[r] == 1)
    def _():
        @pl.when(f == 0)
        def _():
            acc_scr[...] = jnp.zeros(acc_scr.shape, F32)
            x_scr[...] = x_ref[...].astype(BF16)

        x = x_scr[...]
        gte = _dot(x, wg_ref[...])
        up = _dot(x, wu_ref[...])
        act = (gte * jax.nn.sigmoid(gte) * up).astype(BF16)
        acc_scr[...] += _dot(act, wd_ref[...])

        @pl.when(f == pl.num_programs(1) - 1)
        def _():
            o_ref[...] = acc_scr[...].astype(o_ref.dtype)


def _moe_experts(xs, w_gu, w_down, gemm_list):
    n_rows, d = xs.shape
    ff = w_down.shape[1]
    gr = GEMM_ROWS
    tf = _tile(ff, 256, LANES)
    nf = ff // tf
    n_tiles = gemm_list[0].shape[0]

    def fcol(f, used, r):
        return jnp.where(used[r] == 1, f, nf - 1)

    grid_spec = pltpu.PrefetchScalarGridSpec(
        num_scalar_prefetch=3,
        grid=(n_tiles, nf),
        in_specs=[
            pl.BlockSpec((gr, d), lambda r, f, blk, ex, used: (blk[r], 0)),
            pl.BlockSpec((None, d, tf), lambda r, f, blk, ex, used: (ex[r], 0, fcol(f, used, r))),
            pl.BlockSpec((None, d, tf), lambda r, f, blk, ex, used: (ex[r], 0, nf + fcol(f, used, r))),
            pl.BlockSpec((None, tf, d), lambda r, f, blk, ex, used: (ex[r], fcol(f, used, r), 0)),
        ],
        out_specs=pl.BlockSpec((gr, d), lambda r, f, blk, ex, used: (blk[r], 0)),
        scratch_shapes=[pltpu.VMEM((gr, d), F32), pltpu.VMEM((gr, d), BF16)],
    )
    return pl.pallas_call(
        _expert_kernel,
        grid_spec=grid_spec,
        out_shape=jax.ShapeDtypeStruct((n_rows, d), F32),
        input_output_aliases={3: 0},
        compiler_params=_params(("arbitrary", "arbitrary")),
        name="moe_experts",
    )(*gemm_list, xs, w_gu, w_gu, w_down)


def _combine_kernel(dest_ref, gates_ref, x_ref, gate_ref, y_hbm, op_ref, os_ref, y_scr, sem, *,
                    n_prompt_chunks):
    ch = x_ref.shape[0]

    def issue(j, carry):
        for k in range(TOP_K):
            pltpu.make_async_copy(y_hbm.at[pl.ds(dest_ref[k, j], 1), :], y_scr.at[k, pl.ds(j, 1), :],
                                  sem).start(priority=k)
        return carry

    lax.fori_loop(0, ch, issue, 0, unroll=8)
    for k in range(TOP_K):
        _wait_rows(y_hbm, y_scr.at[k], sem, ch)

    gates = gates_ref[...]
    mix = gates[:, 0:1] * y_scr[0] + gates[:, 1:2] * y_scr[1]
    out = x_ref[...] + gate_ref[...] * mix
    is_prompt = pl.program_id(0) < n_prompt_chunks

    @pl.when(is_prompt)
    def _():
        op_ref[...] = out

    @pl.when(jnp.logical_not(is_prompt))
    def _():
        os_ref[...] = out


def _moe_combine(y, dest_rows, gates, x, mods, *, n_prompt, tok_per_cond):
    t, d = x.shape
    ch = ROUTE_CHUNK
    n_pc = n_prompt // ch
    row = _cond_row_joint(ch, n_prompt, tok_per_cond)
    return pl.pallas_call(
        functools.partial(_combine_kernel, n_prompt_chunks=n_pc),
        grid=(t // ch,),
        in_specs=[
            pl.BlockSpec((None, TOP_K, ch), lambda i: (i, 0, 0), memory_space=pltpu.SMEM),
            pl.BlockSpec((ch, gates.shape[1]), lambda i: (i, 0)),
            pl.BlockSpec((ch, d), lambda i: (i, 0)),
            mods.spec(_Mods.GATE2, row),
            pl.BlockSpec(memory_space=pl.ANY),
        ],
        out_specs=[
            pl.BlockSpec((ch, d), lambda i: (jnp.minimum(i, n_pc - 1), 0)),
            pl.BlockSpec((ch, d), lambda i: (jnp.maximum(i - n_pc, 0), 0)),
        ],
        out_shape=[jax.ShapeDtypeStruct((n_prompt, d), F32), jax.ShapeDtypeStruct((t - n_prompt, d), F32)],
        scratch_shapes=[pltpu.VMEM((TOP_K, ch, d), F32), pltpu.SemaphoreType.DMA(())],
        compiler_params=_params(("arbitrary",)),
        name="moe_combine",
    )(dest_rows, gates, x, mods.table, y)


def kernel(x_prompt, x_sample, c, cache_k, cache_v, c_ctx, ada_w, ada_b, norm1_g, norm2_g,
           attn_w_qkv, attn_w_o, attn_q_norm, attn_k_norm, attn_lambda, attn_subln_g,
           pool_w, pool_scale, ffn_w_gu, ffn_w_down,
           moe_w_router, moe_b_router, moe_w_gu, moe_w_down):
    b_ctx, l_ctx, d = x_prompt.shape
    b_dec, l_dec, _ = x_sample.shape
    depth = ada_w.shape[0]
    n_even, l_past, heads, v_dim = cache_k.shape[1:]
    head_dim = v_dim // 2
    assert head_dim == LANES and depth == 2 and l_dec % GRID_W == 0
    n_exp = moe_w_router.shape[-1]
    tp_, ts_ = b_ctx * l_ctx, b_dec * l_dec
    t_all = tp_ + ts_

    cond_rows = 8 * ((1 + b_dec + 7) // 8)
    cond = jnp.zeros((cond_rows, d), F32).at[0].set(c_ctx).at[1:1 + b_dec].set(c)
    mod_table = _adaln_mods(cond, ada_w, ada_b).reshape(depth * cond_rows * 6, 1, d)

    xp = x_prompt.reshape(tp_, d)
    xs = x_sample.reshape(ts_, d)

    layer, j = 0, 0
    mods = _Mods(mod_table, cond_rows, layer)
    g1 = norm1_g[layer].reshape(1, d)
    w_qkv = attn_w_qkv[j].astype(BF16)
    q_gain = (attn_q_norm[j] * (head_dim ** -0.5 * LOG2E)).reshape(1, head_dim)
    k_gain = attn_k_norm[j].reshape(1, head_dim)
    tables = _rope_tables(l_dec)
    prompt = dict(cond_base=0, tok_per_cond=tp_)
    sample = dict(cond_base=1, tok_per_cond=l_dec)

    q_p, k_p, v_p = _qkv_proj(xp, mods, g1, w_qkv, q_gain, k_gain, None, kv_dtype=F32, **prompt)
    q_s, k_s, v_s = _qkv_proj(xs, mods, g1, w_qkv, q_gain, k_gain, tables, kv_dtype=BF16, **sample)

    lam_params = attn_lambda[j].astype(F32)
    subln = attn_subln_g[j].reshape(1, v_dim)
    o_p = _diff_attention(q_p, k_p, v_p, None, None, lam_params, subln,
                          batch=b_ctx, seq=l_ctx, layer=layer)
    ck = cache_k[:, j].reshape(b_dec * l_past, heads * v_dim)
    cv = cache_v[:, j].reshape(b_dec * l_past, heads * v_dim)
    o_s = _diff_attention(q_s, k_s, v_s, ck, cv, lam_params, subln,
                          batch=b_dec, seq=l_dec, layer=layer)

    w_o = attn_w_o[j].astype(BF16)
    x1 = _out_proj_residual(o_p, o_s, w_o, xp, xs, mods, tok_per_cond=l_dec)

    x2 = _dense_ffn(x1, mods, norm2_g[layer].reshape(1, d), ffn_w_gu[j].astype(BF16),
                    ffn_w_down[j].astype(BF16), n_prompt=tp_, tok_per_cond=l_dec)

    layer, j = 1, 0
    mods = _Mods(mod_table, cond_rows, layer)
    x3 = _pool_mixer(x2, mods, norm1_g[layer].reshape(1, d), pool_w[j].astype(BF16),
                     pool_scale[j].reshape(1, d), n_prompt=tp_, prompt_seq=l_ctx, sample_seq=l_dec)

    g2 = norm2_g[layer].reshape(1, d)
    joint = dict(n_prompt=tp_, tok_per_cond=l_dec)
    route, counts = _moe_router(x3, mods, g2, moe_w_router[j], moe_b_router[j], **joint)
    dest_rows, gates, gemm_list, n_rows = _dispatch_plan(route, counts, n_exp)
    rows_sorted = _moe_scatter(x3, mods, g2, dest_rows, n_rows, **joint)
    y_sorted = _moe_experts(rows_sorted, moe_w_gu[j].astype(BF16), moe_w_down[j].astype(BF16),
                            gemm_list)
    y_p, y_s = _moe_combine(y_sorted, dest_rows, gates, x3, mods, **joint)

    y_prompt = y_p.reshape(b_ctx, l_ctx, d)
    y_sample = y_s.reshape(b_dec, l_dec, d)
    state_k = k_p.reshape(b_ctx, 1, l_ctx, heads, v_dim)
    state_v = v_p.reshape(b_ctx, 1, l_ctx, heads, v_dim)
    return (y_prompt, y_sample, state_k, state_v)
```

```python
import functools
import math

import jax
import jax.numpy as jnp
from jax import lax
from jax.experimental import pallas as pl
from jax.experimental.pallas import tpu as pltpu

F32 = jnp.float32
BF16 = jnp.bfloat16

GRID_W = 64
ROPE_BASE = 10000.0
POOL_WINDOWS = (2, 4, 8, 16)
TOP_K = 2
EPS = 1e-6
LOG2E = 1.4426950408889634

LANES = 128
VMEM_LIMIT = 52 * 2**20

ROUTE_CHUNK = 256
GEMM_ROWS = 1024


def _lambda_init(layer):
    return 0.8 - 0.6 * math.exp(-0.3 * layer)


def _tile(n, pref, mult):
    best = None
    t = mult
    while t <= min(n, pref):
        if n % t == 0:
            best = t
        t += mult
    if best is None:
        raise ValueError(f"no tile for {n} (multiple of {mult}, <= {pref})")
    return best


def _params(semantics):
    return pltpu.CompilerParams(dimension_semantics=semantics, vmem_limit_bytes=VMEM_LIMIT)


def _dot(a, b):
    return jnp.dot(a, b, preferred_element_type=F32)


def _mask_bf16(m):
    return jnp.where(m, 1.0, 0.0).astype(BF16)


def _split_bf16(x):
    hi = x.astype(BF16)
    lo = (x - hi.astype(F32)).astype(BF16)
    return hi, lo


def _norm_modulate(x, g, scale, shift):
    ms = jnp.mean(x * x, axis=-1, keepdims=True)
    return x * lax.rsqrt(ms + EPS) * (g * (1.0 + scale)) + shift


def _adaln_kernel(cond_ref, w_ref, b_ref, o_ref):
    c = cond_ref[...]
    s = c * jax.nn.sigmoid(c)
    s_hi, s_lo = _split_bf16(s)
    w_hi, w_lo = _split_bf16(w_ref[...])
    o_ref[...] = _dot(s_hi, w_hi) + _dot(s_lo, w_hi) + _dot(s_hi, w_lo) + b_ref[...]


def _adaln_mods(cond, ada_w, ada_b):
    depth, d, n = ada_w.shape
    rows = cond.shape[0]
    tn = _tile(n, 512, LANES)
    return pl.pallas_call(
        _adaln_kernel,
        grid=(depth, n // tn),
        in_specs=[
            pl.BlockSpec((rows, d), lambda l, j: (0, 0)),
            pl.BlockSpec((None, d, tn), lambda l, j: (l, 0, j)),
            pl.BlockSpec((None, 1, tn), lambda l, j: (l, 0, j)),
        ],
        out_specs=pl.BlockSpec((None, rows, tn), lambda l, j: (l, 0, j)),
        out_shape=jax.ShapeDtypeStruct((depth, rows, n), F32),
        compiler_params=_params(("arbitrary", "arbitrary")),
        name="adaln_mods",
    )(cond, ada_w, ada_b.reshape(depth, 1, n))


class _Mods:
    SHIFT1, SCALE1, GATE1, SHIFT2, SCALE2, GATE2 = range(6)

    def __init__(self, table, rows, layer):
        self.table = table
        self.rows = rows
        self.layer = layer

    def spec(self, which, row_fn, width=None, col_fn=None):
        d = self.table.shape[-1]
        width = d if width is None else width
        base = self.layer * self.rows

        def index(*ids):
            col = 0 if col_fn is None else col_fn(*ids)
            return ((base + row_fn(*ids)) * 6 + which, 0, col)

        return pl.BlockSpec((None, 1, width), index)


def _qkv_kernel(*refs, rope, tn, n_sec):
    x_ref, sh_ref, sc_ref, g_ref, w_ref, qg_ref, kg_ref = refs[:7]
    rest = list(refs[7:])
    cos_ref, sin_ref = (rest.pop(0), rest.pop(0)) if rope else (None, None)
    q_ref, k_ref, v_ref, h_scr = rest
    j = pl.program_id(1)

    @pl.when(j == 0)
    def _():
        h = _norm_modulate(x_ref[...], g_ref[...], sc_ref[...], sh_ref[...])
        h_scr[...] = h.astype(BF16)

    acc = _dot(h_scr[...], w_ref[...])

    def qk_epilogue(gain_ref, out_ref):
        r_idx = lax.broadcasted_iota(jnp.int32, (2 * LANES, 2 * LANES), 0)
        c_idx = lax.broadcasted_iota(jnp.int32, (2 * LANES, 2 * LANES), 1)
        if rope:
            swap = (r_idx >= LANES) & (c_idx >= LANES) & (r_idx - LANES == jnp.bitwise_xor(c_idx - LANES, LANES // 4))
            rhs = _mask_bf16(((r_idx < LANES) & (c_idx < LANES)) | swap)
        else:
            rhs = jnp.ones((2 * LANES, LANES), BF16)
        for g in range(tn // LANES):
            y = acc[:, g * LANES:(g + 1) * LANES]
            yg = y * gain_ref[...]
            if rope:
                res = _dot(jnp.concatenate([(y * y).astype(BF16), yg.astype(BF16)], axis=1), rhs)
                inv = lax.rsqrt(res[:, :LANES] * (1.0 / LANES) + EPS)
                out = (yg * cos_ref[...] + res[:, LANES:] * sin_ref[...]) * inv
            else:
                ss = _dot(jnp.concatenate(_split_bf16(y * y), axis=1), rhs)
                out = yg * lax.rsqrt(ss * (1.0 / LANES) + EPS)
            out_ref[:, g * LANES:(g + 1) * LANES] = out.astype(out_ref.dtype)

    @pl.when(j < n_sec)
    def _():
        qk_epilogue(qg_ref, q_ref)

    @pl.when((j >= n_sec) & (j < 2 * n_sec))
    def _():
        qk_epilogue(kg_ref, k_ref)

    @pl.when(j >= 2 * n_sec)
    def _():
        v_ref[...] = acc.astype(v_ref.dtype)


def _qkv_proj(x, mods, norm_g, w, q_gain, k_gain, tables, *, cond_base, tok_per_cond, kv_dtype):
    t, d = x.shape
    tm = _tile(math.gcd(t, tok_per_cond), 1024, 8)
    tn = _tile(d, 512, LANES)
    n_sec = d // tn
    rope = tables is not None
    row = lambda i, j: cond_base + (i * tm) // tok_per_cond
    in_specs = [
        pl.BlockSpec((tm, d), lambda i, j: (i, 0)),
        mods.spec(_Mods.SHIFT1, row),
        mods.spec(_Mods.SCALE1, row),
        pl.BlockSpec((1, d), lambda i, j: (0, 0)),
        pl.BlockSpec((d, tn), lambda i, j: (0, j)),
        pl.BlockSpec((1, LANES), lambda i, j: (0, 0)),
        pl.BlockSpec((1, LANES), lambda i, j: (0, 0)),
    ]
    args = [x, mods.table, mods.table, norm_g, w, q_gain, k_gain]
    if rope:
        nblk = tables[0].shape[0] // tm
        for tab in tables:
            in_specs.append(pl.BlockSpec((tm, LANES), lambda i, j: (i % nblk, 0)))
            args.append(tab)
    out_specs = [
        pl.BlockSpec((tm, tn), lambda i, j, s=s: (i, jnp.clip(j - s * n_sec, 0, n_sec - 1)))
        for s in range(3)
    ]
    return pl.pallas_call(
        functools.partial(_qkv_kernel, rope=rope, tn=tn, n_sec=n_sec),
        grid=(t // tm, 3 * n_sec),
        in_specs=in_specs,
        out_specs=out_specs,
        out_shape=[jax.ShapeDtypeStruct((t, d), BF16), jax.ShapeDtypeStruct((t, d), kv_dtype),
                   jax.ShapeDtypeStruct((t, d), kv_dtype)],
        scratch_shapes=[pltpu.VMEM((tm, d), BF16)],
        compiler_params=_params(("arbitrary", "arbitrary")),
        name="qkv_proj_rope" if rope else "qkv_proj",
    )(*args)


def _rope_tables(seq):
    n_freq = LANES // 4
    pos = jnp.arange(seq, dtype=jnp.int32)
    r = (pos // GRID_W).astype(F32)
    col = (pos % GRID_W).astype(F32)
    freqs = ROPE_BASE ** (-jnp.arange(n_freq, dtype=F32) / n_freq)
    cos_t, sin_t = [], []
    for p in (r, col):
        ang = p[:, None] * freqs
        cos, sin = jnp.cos(ang), jnp.sin(ang)
        cos_t += [cos, cos]
        sin_t += [-sin, sin]
    return jnp.concatenate(cos_t, axis=1), jnp.concatenate(sin_t, axis=1)


def _attn_kernel(*refs, n_cache, n_chunks, tk, lambda_init):
    lam_ref, g_ref, q_ref = refs[:3]
    if n_cache:
        kc_ref, vtc_ref, k_ref, vt_ref, o_ref, acc_ref = refs[3:]
    else:
        k_ref, vt_ref, o_ref, acc_ref = refs[3:]
    q = q_ref[...]
    qs = (q[:, :LANES], q[:, LANES:])

    def scores(kblk):
        return [lax.dot_general(kblk[:, c * LANES:(c + 1) * LANES], qs[c],
                                (((1,), (1,)), ((), ())), preferred_element_type=F32)
                for c in range(2)]

    def accumulate(sts, vt, stats):
        out = []
        for c in range(2):
            st = sts[c]
            m_cur = jnp.max(st, axis=0, keepdims=True)
            if stats is None:
                m_new = m_cur
                p = jnp.exp2(st - m_new)
                l_new = jnp.sum(p, axis=0, keepdims=True)
                acc_ref[c] = _dot(vt, p.astype(BF16))
            else:
                m_prev, l_prev = stats[c]
                m_new = jnp.maximum(m_prev, m_cur)
                alpha = jnp.exp2(m_prev - m_new)
                p = jnp.exp2(st - m_new)
                l_new = alpha * l_prev + jnp.sum(p, axis=0, keepdims=True)
                acc_ref[c] = alpha * acc_ref[c] + _dot(vt, p.astype(BF16))
            out.append((m_new, l_new))
        return out

    chunks = []
    if n_cache:
        chunks.append((lambda: kc_ref[...].astype(BF16), lambda: vtc_ref[...]))
    for i in range(n_chunks):
        chunks.append((lambda i=i: k_ref[i * tk:(i + 1) * tk, :].astype(BF16), lambda i=i: vt_ref[i]))
    stats = None
    sts = scores(chunks[0][0]())
    for n, (_, values) in enumerate(chunks):
        nxt = scores(chunks[n + 1][0]()) if n + 1 < len(chunks) else None
        stats = accumulate(sts, values(), stats)
        sts = nxt

    lp = lam_ref[...]
    lam = (jnp.exp(jnp.sum(lp[0:1] * lp[1:2], axis=-1, keepdims=True))
           - jnp.exp(jnp.sum(lp[2:3] * lp[3:4], axis=-1, keepdims=True)) + lambda_init)
    o_t = acc_ref[0] / stats[0][1] - lam * (acc_ref[1] / stats[1][1])
    o = o_t.T
    o = o * lax.rsqrt(jnp.mean(o * o, axis=-1, keepdims=True) + EPS) * g_ref[...]
    o_ref[...] = (o * (1.0 - lambda_init)).astype(o_ref.dtype)


def _chunked_transpose(v, batch, seq, heads, vd, tk):
    v = v.astype(BF16).reshape(batch, seq // tk, tk, heads, vd)
    return v.transpose(0, 3, 1, 4, 2).reshape(batch * heads, seq // tk, vd, tk)


def _diff_attention(q, k, v, cache_k, cache_v, lam_params, subln_g, *, batch, seq, layer):
    t, d = q.shape
    vd = subln_g.shape[-1]
    heads = d // vd
    n_cache = 0 if cache_k is None else cache_k.shape[0] // batch
    tq = _tile(seq, 512, LANES)
    tk = _tile(seq, 512, LANES)
    nq = seq // tq
    n_chunks = seq // tk
    in_specs = [
        pl.BlockSpec(lam_params.shape, lambda b, h, i: (0, 0)),
        pl.BlockSpec((1, vd), lambda b, h, i: (0, 0)),
        pl.BlockSpec((tq, vd), lambda b, h, i: (b * nq + i, h)),
    ]
    args = [lam_params, subln_g, q]
    if n_cache:
        in_specs += [pl.BlockSpec((n_cache, vd), lambda b, h, i: (b, h)),
                     pl.BlockSpec((None, None, vd, n_cache), lambda b, h, i: (b * heads + h, 0, 0, 0))]
        args += [cache_k, _chunked_transpose(cache_v, batch, n_cache, heads, vd, n_cache)]
    in_specs += [pl.BlockSpec((seq, vd), lambda b, h, i: (b, h)),
                 pl.BlockSpec((None, n_chunks, vd, tk), lambda b, h, i: (b * heads + h, 0, 0, 0))]
    args += [k, _chunked_transpose(v, batch, seq, heads, vd, tk)]
    return pl.pallas_call(
        functools.partial(_attn_kernel, n_cache=n_cache, n_chunks=n_chunks, tk=tk,
                          lambda_init=_lambda_init(layer)),
        grid=(batch, heads, nq),
        in_specs=in_specs,
        out_specs=pl.BlockSpec((tq, vd), lambda b, h, i: (b * nq + i, h)),
        out_shape=jax.ShapeDtypeStruct((t, d), BF16),
        scratch_shapes=[pltpu.VMEM((2, vd, tq), F32)],
        compiler_params=_params(("arbitrary", "arbitrary", "arbitrary")),
        name=f"diff_attention_{'latent' if n_cache else 'context'}",
    )(*args)


def _out_proj_kernel(op_ref, os_ref, w_ref, xp_ref, xs_ref, gate_ref, out_ref, *, n_prompt_tiles):
    is_prompt = pl.program_id(0) < n_prompt_tiles

    @pl.when(is_prompt)
    def _():
        out_ref[...] = xp_ref[...] + gate_ref[...] * _dot(op_ref[...], w_ref[...])

    @pl.when(jnp.logical_not(is_prompt))
    def _():
        out_ref[...] = xs_ref[...] + gate_ref[...] * _dot(os_ref[...], w_ref[...])


def _out_proj_residual(o_p, o_s, w, x_p, x_s, mods, *, tok_per_cond):
    n_prompt, d = x_p.shape
    t = n_prompt + x_s.shape[0]
    tm = _tile(math.gcd(n_prompt, tok_per_cond), 1024, 8)
    tn = _tile(d, 512, LANES)
    n_pt = n_prompt // tm
    row = _cond_row_joint(tm, n_prompt, tok_per_cond)
    p_rows = lambda i: jnp.minimum(i, n_pt - 1)
    s_rows = lambda i: jnp.maximum(i - n_pt, 0)
    return pl.pallas_call(
        functools.partial(_out_proj_kernel, n_prompt_tiles=n_pt),
        grid=(t // tm, d // tn),
        in_specs=[
            pl.BlockSpec((tm, d), lambda i, j: (p_rows(i), 0)),
            pl.BlockSpec((tm, d), lambda i, j: (s_rows(i), 0)),
            pl.BlockSpec((d, tn), lambda i, j: (0, j)),
            pl.BlockSpec((tm, tn), lambda i, j: (p_rows(i), jnp.where(i < n_pt, j, d // tn - 1))),
            pl.BlockSpec((tm, tn), lambda i, j: (s_rows(i), jnp.where(i < n_pt, 0, j))),
            mods.spec(_Mods.GATE1, row, width=tn, col_fn=lambda i, j: j),
        ],
        out_specs=pl.BlockSpec((tm, tn), lambda i, j: (i, j)),
        out_shape=jax.ShapeDtypeStruct((t, d), F32),
        compiler_params=_params(("arbitrary", "arbitrary")),
        name="attn_out_proj",
    )(o_p, o_s, w, x_p, x_s, mods.table)


def _ffn_kernel(x_ref, sh_ref, sc_ref, gate_ref, g_ref, wg_ref, wu_ref, wd_ref, o_ref, h_scr, acc_scr):
    f = pl.program_id(1)

    @pl.when(f == 0)
    def _():
        h = _norm_modulate(x_ref[...], g_ref[...], sc_ref[...], sh_ref[...])
        h_scr[...] = h.astype(BF16)
        acc_scr[...] = jnp.zeros(acc_scr.shape, F32)

    h = h_scr[...]
    gte = _dot(h, wg_ref[...])
    up = _dot(h, wu_ref[...])
    act = (gte * jax.nn.sigmoid(gte) * up).astype(BF16)
    acc_scr[...] += _dot(act, wd_ref[...])

    @pl.when(f == pl.num_programs(1) - 1)
    def _():
        o_ref[...] = x_ref[...] + gate_ref[...] * acc_scr[...]


def _cond_row_joint(tm, n_prompt, tok_per_cond):
    def row(i, *_):
        tok = i * tm
        return jnp.where(tok < n_prompt, 0, 1 + (tok - n_prompt) // tok_per_cond)
    return row


def _dense_ffn(x, mods, norm_g, w_gu, w_down, *, n_prompt, tok_per_cond):
    t, d = x.shape
    ff = w_down.shape[0]
    tm = _tile(math.gcd(n_prompt, tok_per_cond), 512, 8)
    tf = _tile(ff, 512, LANES)
    nf = ff // tf
    row = _cond_row_joint(tm, n_prompt, tok_per_cond)
    return pl.pallas_call(
        _ffn_kernel,
        grid=(t // tm, nf),
        in_specs=[
            pl.BlockSpec((tm, d), lambda i, f: (i, 0)),
            mods.spec(_Mods.SHIFT2, row),
            mods.spec(_Mods.SCALE2, row),
            mods.spec(_Mods.GATE2, row),
            pl.BlockSpec((1, d), lambda i, f: (0, 0)),
            pl.BlockSpec((d, tf), lambda i, f: (0, f)),
            pl.BlockSpec((d, tf), lambda i, f: (0, nf + f)),
            pl.BlockSpec((tf, d), lambda i, f: (f, 0)),
        ],
        out_specs=pl.BlockSpec((tm, d), lambda i, f: (i, 0)),
        out_shape=jax.ShapeDtypeStruct((t, d), F32),
        scratch_shapes=[pltpu.VMEM((tm, d), BF16), pltpu.VMEM((tm, d), F32)],
        compiler_params=_params(("arbitrary", "arbitrary")),
        name="dense_ffn",
    )(x, mods.table, mods.table, mods.table, norm_g, w_gu, w_gu, w_down)


def _pool_kernel(x_ref, xp_ref, xn_ref, sh_ref, sc_ref, gate_ref, g_ref, w_ref, ps_ref, o_ref, *,
                 tp, halo, n_prompt_tiles, prompt_tiles_per_seq, sample_tiles_per_seq, group_dim):
    i = pl.program_id(0)
    is_prompt = i < n_prompt_tiles
    tiles_per_seq = jnp.where(is_prompt, prompt_tiles_per_seq, sample_tiles_per_seq)
    local = lax.rem(jnp.where(is_prompt, i, i - n_prompt_tiles), tiles_per_seq)
    first = local == 0
    last = local == tiles_per_seq - 1
    seq_len = tiles_per_seq * tp

    g, sc, sh = g_ref[...], sc_ref[...], sh_ref[...]
    x = x_ref[...]
    h = _norm_modulate(x, g, sc, sh)
    h_prev = jnp.where(first, 0.0, _norm_modulate(xp_ref[...], g, sc, sh))
    h_next = jnp.where(last, 0.0, _norm_modulate(xn_ref[...], g, sc, sh))
    d = h.shape[-1]
    h_halo = jnp.concatenate([h_prev, h_next, jnp.zeros((LANES - 2 * halo, d), F32)], axis=0)

    t_idx = lax.broadcasted_iota(jnp.int32, (tp, tp), 0)
    s_idx = lax.broadcasted_iota(jnp.int32, (tp, tp), 1)
    t_h = lax.broadcasted_iota(jnp.int32, (tp, LANES), 0)
    u_h = lax.broadcasted_iota(jnp.int32, (tp, LANES), 1)
    off_h = jnp.where(u_h < halo, u_h - halo, tp + u_h - halo)
    pos = local * tp + lax.broadcasted_iota(jnp.int32, (tp, 1), 0)

    for grp, win in enumerate(POOL_WINDOWS):
        lo, hi = win // 2, win // 2 - 1
        cols = slice(grp * group_dim, (grp + 1) * group_dim)
        band = _mask_bf16((s_idx >= t_idx - lo) & (s_idx <= t_idx + hi))
        band_h = _mask_bf16((u_h < 2 * halo) & (off_h >= t_h - lo) & (off_h <= t_h + hi))
        c_hi, c_lo = _split_bf16(h[:, cols])
        e_hi, e_lo = _split_bf16(h_halo[:, cols])
        total = (_dot(band, c_hi) + _dot(band, c_lo)) + (_dot(band_h, e_hi) + _dot(band_h, e_lo))
        cnt = (jnp.minimum(pos + hi, seq_len - 1) - jnp.maximum(pos - lo, 0) + 1).astype(F32)
        pooled = (total / cnt - h[:, cols]).astype(BF16)
        y = _dot(pooled, w_ref[grp]) * ps_ref[:, cols]
        o_ref[:, cols] = x[:, cols] + gate_ref[:, cols] * y


def _pool_mixer(x, mods, norm_g, pool_w, pool_scale, *, n_prompt, prompt_seq, sample_seq):
    t, d = x.shape
    halo = max(POOL_WINDOWS) // 2
    tp = _tile(math.gcd(prompt_seq, sample_seq), 256, 8)
    n_groups, group_dim = pool_w.shape[0], pool_w.shape[1]
    hb = tp // halo
    n_halo_blocks = t // halo
    row = _cond_row_joint(tp, n_prompt, sample_seq)
    return pl.pallas_call(
        functools.partial(_pool_kernel, tp=tp, halo=halo, n_prompt_tiles=n_prompt // tp,
                          prompt_tiles_per_seq=prompt_seq // tp,
                          sample_tiles_per_seq=sample_seq // tp, group_dim=group_dim),
        grid=(t // tp,),
        in_specs=[
            pl.BlockSpec((tp, d), lambda i: (i, 0)),
            pl.BlockSpec((halo, d), lambda i: (jnp.maximum(i * hb - 1, 0), 0)),
            pl.BlockSpec((halo, d), lambda i: (jnp.minimum((i + 1) * hb, n_halo_blocks - 1), 0)),
            mods.spec(_Mods.SHIFT1, row),
            mods.spec(_Mods.SCALE1, row),
            mods.spec(_Mods.GATE1, row),
            pl.BlockSpec((1, d), lambda i: (0, 0)),
            pl.BlockSpec((n_groups, group_dim, group_dim), lambda i: (0, 0, 0)),
            pl.BlockSpec((1, d), lambda i: (0, 0)),
        ],
        out_specs=pl.BlockSpec((tp, d), lambda i: (i, 0)),
        out_shape=jax.ShapeDtypeStruct((t, d), F32),
        compiler_params=_params(("arbitrary",)),
        name="pool_mixer",
    )(x, x, x, mods.table, mods.table, mods.table, norm_g, pool_w, pool_scale)


def _router_kernel(x_ref, sh_ref, sc_ref, g_ref, wr_ref, br_ref, route_ref, cnt_ref):
    h = _norm_modulate(x_ref[...], g_ref[...], sc_ref[...], sh_ref[...])
    h_hi, h_lo = _split_bf16(h)
    w_hi, w_lo = _split_bf16(wr_ref[...])
    logits = _dot(h_hi, w_hi) + _dot(h_lo, w_hi) + _dot(h_hi, w_lo) + br_ref[...]

    rows = logits.shape[0]
    lane = lax.broadcasted_iota(jnp.int32, logits.shape, 1).astype(F32)
    m1 = jnp.max(logits, axis=-1, keepdims=True)
    i1 = jnp.min(jnp.where(logits == m1, lane, float(LANES)), axis=-1, keepdims=True)
    oh1 = lane == i1
    rest = jnp.where(oh1, -jnp.inf, logits)
    m2 = jnp.max(rest, axis=-1, keepdims=True)
    i2 = jnp.min(jnp.where(rest == m2, lane, float(LANES)), axis=-1, keepdims=True)
    oh2 = lane == i2
    e = jnp.exp(m2 - m1)
    gate_a = 1.0 / (1.0 + e)
    gate_b = e / (1.0 + e)

    sel = jnp.where(oh1 | oh2, 1.0, 0.0)
    r_idx = lax.broadcasted_iota(jnp.int32, (rows, rows), 0)
    c_idx = lax.broadcasted_iota(jnp.int32, (rows, rows), 1)
    earlier = _mask_bf16(c_idx < r_idx)
    rank = _dot(earlier, sel.astype(BF16))
    rank_a = jnp.sum(jnp.where(oh1, rank, 0.0), axis=-1, keepdims=True)
    rank_b = jnp.sum(jnp.where(oh2, rank, 0.0), axis=-1, keepdims=True)

    route = jnp.zeros(logits.shape, F32)
    for k, val in enumerate((i1, i2, rank_a, rank_b, gate_a, gate_b)):
        route = jnp.where(lane == float(k), val, route)
    route_ref[...] = route
    cnt_ref[...] = jnp.sum(sel, axis=0, keepdims=True)


def _moe_router(x, mods, norm_g, w_router, b_router, *, n_prompt, tok_per_cond):
    t, d = x.shape
    n_exp = w_router.shape[1]
    ch = ROUTE_CHUNK
    wr = jnp.zeros((d, LANES), F32).at[:, :n_exp].set(w_router.astype(F32))
    br = jnp.full((1, LANES), -jnp.inf, F32).at[0, :n_exp].set(b_router.astype(F32))
    row = _cond_row_joint(ch, n_prompt, tok_per_cond)
    return pl.pallas_call(
        _router_kernel,
        grid=(t // ch,),
        in_specs=[
            pl.BlockSpec((ch, d), lambda i: (i, 0)),
            mods.spec(_Mods.SHIFT2, row),
            mods.spec(_Mods.SCALE2, row),
            pl.BlockSpec((1, d), lambda i: (0, 0)),
            pl.BlockSpec((d, LANES), lambda i: (0, 0)),
            pl.BlockSpec((1, LANES), lambda i: (0, 0)),
        ],
        out_specs=[
            pl.BlockSpec((ch, LANES), lambda i: (i, 0)),
            pl.BlockSpec((None, 1, LANES), lambda i: (i, 0, 0)),
        ],
        out_shape=[
            jax.ShapeDtypeStruct((t, LANES), F32),
            jax.ShapeDtypeStruct((t // ch, 1, LANES), F32),
        ],
        compiler_params=_params(("arbitrary",)),
        name="moe_router",
    )(x, mods.table, mods.table, norm_g, wr, br)


def _dispatch_plan(route, counts, n_exp):
    ch, gr = ROUTE_CHUNK, GEMM_ROWS
    t = route.shape[0]
    n_chunks = t // ch
    i32 = jnp.int32
    cnt = counts[:, 0, :n_exp].astype(i32)
    total = cnt.sum(0)
    padded = ((total + gr - 1) // gr) * gr
    off = jnp.cumsum(padded) - padded
    start = off[None, :] + jnp.cumsum(cnt, axis=0) - cnt

    def dest(choice):
        e = route[:, choice].astype(i32).reshape(n_chunks, ch, 1)
        hit = e == jnp.arange(n_exp, dtype=i32)
        base = jnp.sum(jnp.where(hit, start[:, None, :], 0), axis=-1)
        return base.reshape(t) + route[:, 2 + choice].astype(i32)

    dest_a, dest_b = dest(0), dest(1)

    n_gemm_tiles = (t * TOP_K) // gr + n_exp

    tile_lo = jnp.arange(n_gemm_tiles, dtype=i32) * gr
    used = tile_lo < (off + padded)[-1]
    ends = off + padded
    expert_of = jnp.minimum(jnp.sum(tile_lo[:, None] >= ends[None, :], axis=1), n_exp - 1).astype(i32)
    n_used = jnp.maximum(used.sum(), 1)
    clamp = jnp.minimum(jnp.arange(n_gemm_tiles, dtype=i32), n_used - 1).astype(i32)
    gemm_list = (clamp, expert_of[clamp], used.astype(i32))

    dest_rows = jnp.stack([dest_a.reshape(n_chunks, ch), dest_b.reshape(n_chunks, ch)], axis=1)
    gates = jnp.concatenate([route[:, 4:6], jnp.zeros((t, 6), F32)], axis=1)
    return dest_rows, gates, gemm_list, n_gemm_tiles * gr


def _wait_rows(src, dst, sem, n_rows):
    pltpu.make_async_copy(src.at[pl.ds(0, n_rows), :], dst.at[pl.ds(0, n_rows), :], sem).wait()


def _scatter_kernel(dest_ref, x_ref, sh_ref, sc_ref, g_ref, init_hbm, out_hbm, h_scr, sem):
    del init_hbm
    ch = x_ref.shape[0]
    h_scr[...] = _norm_modulate(x_ref[...], g_ref[...], sc_ref[...], sh_ref[...])

    def issue(j, carry):
        for k in range(TOP_K):
            pltpu.make_async_copy(h_scr.at[pl.ds(j, 1), :], out_hbm.at[pl.ds(dest_ref[k, j], 1), :],
                                  sem).start(priority=k)
        return carry

    lax.fori_loop(0, ch, issue, 0, unroll=8)
    for _ in range(TOP_K):
        _wait_rows(h_scr, out_hbm, sem, ch)


def _moe_scatter(x, mods, norm_g, dest_rows, n_rows, *, n_prompt, tok_per_cond):
    t, d = x.shape
    ch = ROUTE_CHUNK
    row = _cond_row_joint(ch, n_prompt, tok_per_cond)
    return pl.pallas_call(
        _scatter_kernel,
        grid=(t // ch,),
        in_specs=[
            pl.BlockSpec((None, TOP_K, ch), lambda i: (i, 0, 0), memory_space=pltpu.SMEM),
            pl.BlockSpec((ch, d), lambda i: (i, 0)),
            mods.spec(_Mods.SHIFT2, row),
            mods.spec(_Mods.SCALE2, row),
            pl.BlockSpec((1, d), lambda i: (0, 0)),
            pl.BlockSpec(memory_space=pl.ANY),
        ],
        out_specs=pl.BlockSpec(memory_space=pl.ANY),
        out_shape=jax.ShapeDtypeStruct((n_rows, d), F32),
        scratch_shapes=[pltpu.VMEM((ch, d), F32), pltpu.SemaphoreType.DMA(())],
        input_output_aliases={5: 0},
        compiler_params=_params(("arbitrary",)),
        name="moe_scatter",
    )(dest_rows, x, mods.table, mods.table, norm_g, jnp.zeros((n_rows, d), F32))


def _expert_kernel(blk_ref, exp_ref, used_ref, x_ref, wg_ref, wu_ref, wd_ref, o_ref, x_scr):
    r, f = pl.program_id(0), pl.program_id(1)

    @pl.when(used_ref[r] == 1)
    def _():
        @pl.when(f == 0)
        def _():
            o_ref[...] = jnp.zeros(o_ref.shape, F32)
            x_scr[...] = x_ref[...].astype(BF16)

        x = x_scr[...]
        gte = _dot(x, wg_ref[...])
        up = _dot(x, wu_ref[...])
        act = (gte * jax.nn.sigmoid(gte) * up).astype(BF16)
        o_ref[...] += _dot(act, wd_ref[...])


def _moe_experts(xs, w_gu, w_down, gemm_list):
    n_rows, d = xs.shape
    ff = w_down.shape[1]
    gr = GEMM_ROWS
    tf = _tile(ff, 256, LANES)
    nf = ff // tf
    n_tiles = gemm_list[0].shape[0]

    def fcol(f, used, r):
        return jnp.where(used[r] == 1, f, nf - 1)

    grid_spec = pltpu.PrefetchScalarGridSpec(
        num_scalar_prefetch=3,
        grid=(n_tiles, nf),
        in_specs=[
            pl.BlockSpec((gr, d), lambda r, f, blk, ex, used: (blk[r], 0)),
            pl.BlockSpec((None, d, tf), lambda r, f, blk, ex, used: (ex[r], 0, fcol(f, used, r))),
            pl.BlockSpec((None, d, tf), lambda r, f, blk, ex, used: (ex[r], 0, nf + fcol(f, used, r))),
            pl.BlockSpec((None, tf, d), lambda r, f, blk, ex, used: (ex[r], fcol(f, used, r), 0)),
        ],
        out_specs=pl.BlockSpec((gr, d), lambda r, f, blk, ex, used: (blk[r], 0)),
        scratch_shapes=[pltpu.VMEM((gr, d), BF16)],
    )
    return pl.pallas_call(
        _expert_kernel,
        grid_spec=grid_spec,
        out_shape=jax.ShapeDtypeStruct((n_rows, d), F32),
        input_output_aliases={3: 0},
        compiler_params=_params(("arbitrary", "arbitrary")),
        name="moe_experts",
    )(*gemm_list, xs, w_gu, w_gu, w_down)


def _combine_kernel(dest_ref, gates_ref, x_ref, gate_ref, y_hbm, op_ref, os_ref, y_scr, sem, *,
                    n_prompt_chunks):
    ch = x_ref.shape[0]

    def issue(j, carry):
        for k in range(TOP_K):
            pltpu.make_async_copy(y_hbm.at[pl.ds(dest_ref[k, j], 1), :], y_scr.at[k, pl.ds(j, 1), :],
                                  sem).start(priority=k)
        return carry

    lax.fori_loop(0, ch, issue, 0, unroll=8)
    for k in range(TOP_K):
        _wait_rows(y_hbm, y_scr.at[k], sem, ch)

    gates = gates_ref[...]
    mix = gates[:, 0:1] * y_scr[0] + gates[:, 1:2] * y_scr[1]
    out = x_ref[...] + gate_ref[...] * mix
    is_prompt = pl.program_id(0) < n_prompt_chunks

    @pl.when(is_prompt)
    def _():
        op_ref[...] = out

    @pl.when(jnp.logical_not(is_prompt))
    def _():
        os_ref[...] = out


def _moe_combine(y, dest_rows, gates, x, mods, *, n_prompt, tok_per_cond):
    t, d = x.shape
    ch = ROUTE_CHUNK
    n_pc = n_prompt // ch
    row = _cond_row_joint(ch, n_prompt, tok_per_cond)
    return pl.pallas_call(
        functools.partial(_combine_kernel, n_prompt_chunks=n_pc),
        grid=(t // ch,),
        in_specs=[
            pl.BlockSpec((None, TOP_K, ch), lambda i: (i, 0, 0), memory_space=pltpu.SMEM),
            pl.BlockSpec((ch, gates.shape[1]), lambda i: (i, 0)),
            pl.BlockSpec((ch, d), lambda i: (i, 0)),
            mods.spec(_Mods.GATE2, row),
            pl.BlockSpec(memory_space=pl.ANY),
        ],
        out_specs=[
            pl.BlockSpec((ch, d), lambda i: (jnp.minimum(i, n_pc - 1), 0)),
            pl.BlockSpec((ch, d), lambda i: (jnp.maximum(i - n_pc, 0), 0)),
        ],
        out_shape=[jax.ShapeDtypeStruct((n_prompt, d), F32), jax.ShapeDtypeStruct((t - n_prompt, d), F32)],
        scratch_shapes=[pltpu.VMEM((TOP_K, ch, d), F32), pltpu.SemaphoreType.DMA(())],
        compiler_params=_params(("arbitrary",)),
        name="moe_combine",
    )(dest_rows, gates, x, mods.table, y)


def kernel(x_prompt, x_sample, c, cache_k, cache_v, c_ctx, ada_w, ada_b, norm1_g, norm2_g,
           attn_w_qkv, attn_w_o, attn_q_norm, attn_k_norm, attn_lambda, attn_subln_g,
           pool_w, pool_scale, ffn_w_gu, ffn_w_down,
           moe_w_router, moe_b_router, moe_w_gu, moe_w_down):
    b_ctx, l_ctx, d = x_prompt.shape
    b_dec, l_dec, _ = x_sample.shape
    depth = ada_w.shape[0]
    n_even, l_past, heads, v_dim = cache_k.shape[1:]
    head_dim = v_dim // 2
    assert head_dim == LANES and depth == 2 and l_dec % GRID_W == 0
    n_exp = moe_w_router.shape[-1]
    tp_, ts_ = b_ctx * l_ctx, b_dec * l_dec
    t_all = tp_ + ts_

    cond_rows = 8 * ((1 + b_dec + 7) // 8)
    cond = jnp.zeros((cond_rows, d), F32).at[0].set(c_ctx).at[1:1 + b_dec].set(c)
    mod_table = _adaln_mods(cond, ada_w, ada_b).reshape(depth * cond_rows * 6, 1, d)

    xp = x_prompt.reshape(tp_, d)
    xs = x_sample.reshape(ts_, d)

    layer, j = 0, 0
    mods = _Mods(mod_table, cond_rows, layer)
    g1 = norm1_g[layer].reshape(1, d)
    w_qkv = attn_w_qkv[j].astype(BF16)
    q_gain = (attn_q_norm[j] * (head_dim ** -0.5 * LOG2E)).reshape(1, head_dim)
    k_gain = attn_k_norm[j].reshape(1, head_dim)
    tables = _rope_tables(l_dec)
    prompt = dict(cond_base=0, tok_per_cond=tp_)
    sample = dict(cond_base=1, tok_per_cond=l_dec)

    q_p, k_p, v_p = _qkv_proj(xp, mods, g1, w_qkv, q_gain, k_gain, None, kv_dtype=F32, **prompt)
    q_s, k_s, v_s = _qkv_proj(xs, mods, g1, w_qkv, q_gain, k_gain, tables, kv_dtype=BF16, **sample)

    lam_params = attn_lambda[j].astype(F32)
    subln = attn_subln_g[j].reshape(1, v_dim)
    o_p = _diff_attention(q_p, k_p, v_p, None, None, lam_params, subln,
                          batch=b_ctx, seq=l_ctx, layer=layer)
    ck = cache_k[:, j].reshape(b_dec * l_past, heads * v_dim)
    cv = cache_v[:, j].reshape(b_dec * l_past, heads * v_dim)
    o_s = _diff_attention(q_s, k_s, v_s, ck, cv, lam_params, subln,
                          batch=b_dec, seq=l_dec, layer=layer)

    w_o = attn_w_o[j].astype(BF16)
    x1 = _out_proj_residual(o_p, o_s, w_o, xp, xs, mods, tok_per_cond=l_dec)

    x2 = _dense_ffn(x1, mods, norm2_g[layer].reshape(1, d), ffn_w_gu[j].astype(BF16),
                    ffn_w_down[j].astype(BF16), n_prompt=tp_, tok_per_cond=l_dec)

    layer, j = 1, 0
    mods = _Mods(mod_table, cond_rows, layer)
    x3 = _pool_mixer(x2, mods, norm1_g[layer].reshape(1, d), pool_w[j].astype(BF16),
                     pool_scale[j].reshape(1, d), n_prompt=tp_, prompt_seq=l_ctx, sample_seq=l_dec)

    g2 = norm2_g[layer].reshape(1, d)
    joint = dict(n_prompt=tp_, tok_per_cond=l_dec)
    route, counts = _moe_router(x3, mods, g2, moe_w_router[j], moe_b_router[j], **joint)
    dest_rows, gates, gemm_list, n_rows = _dispatch_plan(route, counts, n_exp)
    rows_sorted = _moe_scatter(x3, mods, g2, dest_rows, n_rows, **joint)
    y_sorted = _moe_experts(rows_sorted, moe_w_gu[j].astype(BF16), moe_w_down[j].astype(BF16),
                            gemm_list)
    y_p, y_s = _moe_combine(y_sorted, dest_rows, gates, x3, mods, **joint)

    y_prompt = y_p.reshape(b_ctx, l_ctx, d)
    y_sample = y_s.reshape(b_dec, l_dec, d)
    state_k = k_p.reshape(b_ctx, 1, l_ctx, heads, v_dim)
    state_v = v_p.reshape(b_ctx, 1, l_ctx, heads, v_dim)
    return (y_prompt, y_sample, state_k, state_v)
```

```python
import functools
import math

import jax
import jax.numpy as jnp
from jax import lax
from jax.experimental import pallas as pl
from jax.experimental.pallas import tpu as pltpu

F32 = jnp.float32
BF16 = jnp.bfloat16

GRID_W = 64
ROPE_BASE = 10000.0
POOL_WINDOWS = (2, 4, 8, 16)
TOP_K = 2
EPS = 1e-6
LOG2E = 1.4426950408889634

LANES = 128
VMEM_LIMIT = 52 * 2**20

ROUTE_CHUNK = 256
GEMM_ROWS = 1024


def _lambda_init(layer):
    return 0.8 - 0.6 * math.exp(-0.3 * layer)


def _tile(n, pref, mult):
    best = None
    t = mult
    while t <= min(n, pref):
        if n % t == 0:
            best = t
        t += mult
    if best is None:
        raise ValueError(f"no tile for {n} (multiple of {mult}, <= {pref})")
    return best


def _params(semantics):
    return pltpu.CompilerParams(dimension_semantics=semantics, vmem_limit_bytes=VMEM_LIMIT)


def _dot(a, b):
    return jnp.dot(a, b, preferred_element_type=F32)


def _mask_bf16(m):
    return jnp.where(m, 1.0, 0.0).astype(BF16)


def _split_bf16(x):
    hi = x.astype(BF16)
    lo = (x - hi.astype(F32)).astype(BF16)
    return hi, lo


def _norm_modulate(x, g, scale, shift):
    ms = jnp.mean(x * x, axis=-1, keepdims=True)
    return x * lax.rsqrt(ms + EPS) * (g * (1.0 + scale)) + shift


def _adaln_kernel(cond_ref, w_ref, b_ref, o_ref):
    c = cond_ref[...]
    s = c * jax.nn.sigmoid(c)
    s_hi, s_lo = _split_bf16(s)
    w_hi, w_lo = _split_bf16(w_ref[...])
    o_ref[...] = _dot(s_hi, w_hi) + _dot(s_lo, w_hi) + _dot(s_hi, w_lo) + b_ref[...]


def _adaln_mods(cond, ada_w, ada_b):
    depth, d, n = ada_w.shape
    rows = cond.shape[0]
    tn = _tile(n, 512, LANES)
    return pl.pallas_call(
        _adaln_kernel,
        grid=(depth, n // tn),
        in_specs=[
            pl.BlockSpec((rows, d), lambda l, j: (0, 0)),
            pl.BlockSpec((None, d, tn), lambda l, j: (l, 0, j)),
            pl.BlockSpec((None, 1, tn), lambda l, j: (l, 0, j)),
        ],
        out_specs=pl.BlockSpec((None, rows, tn), lambda l, j: (l, 0, j)),
        out_shape=jax.ShapeDtypeStruct((depth, rows, n), F32),
        compiler_params=_params(("arbitrary", "arbitrary")),
        name="adaln_mods",
    )(cond, ada_w, ada_b.reshape(depth, 1, n))


class _Mods:
    SHIFT1, SCALE1, GATE1, SHIFT2, SCALE2, GATE2 = range(6)

    def __init__(self, table, rows, layer):
        self.table = table
        self.rows = rows
        self.layer = layer

    def spec(self, which, row_fn, width=None, col_fn=None):
        d = self.table.shape[-1]
        width = d if width is None else width
        base = self.layer * self.rows

        def index(*ids):
            col = 0 if col_fn is None else col_fn(*ids)
            return ((base + row_fn(*ids)) * 6 + which, 0, col)

        return pl.BlockSpec((None, 1, width), index)


def _qkv_kernel(*refs, rope, tn, n_sec):
    x_ref, sh_ref, sc_ref, g_ref, w_ref, qg_ref, kg_ref = refs[:7]
    rest = list(refs[7:])
    cos_ref, sin_ref = (rest.pop(0), rest.pop(0)) if rope else (None, None)
    q_ref, k_ref, v_ref, h_scr = rest
    j = pl.program_id(1)

    @pl.when(j == 0)
    def _():
        h = _norm_modulate(x_ref[...], g_ref[...], sc_ref[...], sh_ref[...])
        h_scr[...] = h.astype(BF16)

    acc = _dot(h_scr[...], w_ref[...])

    def qk_epilogue(gain_ref, out_ref):
        r_idx = lax.broadcasted_iota(jnp.int32, (2 * LANES, 2 * LANES), 0)
        c_idx = lax.broadcasted_iota(jnp.int32, (2 * LANES, 2 * LANES), 1)
        if rope:
            swap = (r_idx >= LANES) & (c_idx >= LANES) & (r_idx - LANES == jnp.bitwise_xor(c_idx - LANES, LANES // 4))
            rhs = _mask_bf16(((r_idx < LANES) & (c_idx < LANES)) | swap)
        else:
            rhs = jnp.ones((2 * LANES, LANES), BF16)
        for g in range(tn // LANES):
            y = acc[:, g * LANES:(g + 1) * LANES]
            yg = y * gain_ref[...]
            if rope:
                res = _dot(jnp.concatenate([(y * y).astype(BF16), yg.astype(BF16)], axis=1), rhs)
                inv = lax.rsqrt(res[:, :LANES] * (1.0 / LANES) + EPS)
                out = (yg * cos_ref[...] + res[:, LANES:] * sin_ref[...]) * inv
            else:
                ss = _dot(jnp.concatenate(_split_bf16(y * y), axis=1), rhs)
                out = yg * lax.rsqrt(ss * (1.0 / LANES) + EPS)
            out_ref[:, g * LANES:(g + 1) * LANES] = out.astype(out_ref.dtype)

    @pl.when(j < n_sec)
    def _():
        qk_epilogue(qg_ref, q_ref)

    @pl.when((j >= n_sec) & (j < 2 * n_sec))
    def _():
        qk_epilogue(kg_ref, k_ref)

    @pl.when(j >= 2 * n_sec)
    def _():
        v_ref[...] = acc.astype(v_ref.dtype)


def _qkv_proj(x, mods, norm_g, w, q_gain, k_gain, tables, *, cond_base, tok_per_cond, kv_dtype):
    t, d = x.shape
    tm = _tile(math.gcd(t, tok_per_cond), 1024, 8)
    tn = _tile(d, 512, LANES)
    n_sec = d // tn
    rope = tables is not None
    row = lambda i, j: cond_base + (i * tm) // tok_per_cond
    in_specs = [
        pl.BlockSpec((tm, d), lambda i, j: (i, 0)),
        mods.spec(_Mods.SHIFT1, row),
        mods.spec(_Mods.SCALE1, row),
        pl.BlockSpec((1, d), lambda i, j: (0, 0)),
        pl.BlockSpec((d, tn), lambda i, j: (0, j)),
        pl.BlockSpec((1, LANES), lambda i, j: (0, 0)),
        pl.BlockSpec((1, LANES), lambda i, j: (0, 0)),
    ]
    args = [x, mods.table, mods.table, norm_g, w, q_gain, k_gain]
    if rope:
        nblk = tables[0].shape[0] // tm
        for tab in tables:
            in_specs.append(pl.BlockSpec((tm, LANES), lambda i, j: (i % nblk, 0)))
            args.append(tab)
    out_specs = [
        pl.BlockSpec((tm, tn), lambda i, j, s=s: (i, jnp.clip(j - s * n_sec, 0, n_sec - 1)))
        for s in range(3)
    ]
    return pl.pallas_call(
        functools.partial(_qkv_kernel, rope=rope, tn=tn, n_sec=n_sec),
        grid=(t // tm, 3 * n_sec),
        in_specs=in_specs,
        out_specs=out_specs,
        out_shape=[jax.ShapeDtypeStruct((t, d), BF16), jax.ShapeDtypeStruct((t, d), kv_dtype),
                   jax.ShapeDtypeStruct((t, d), kv_dtype)],
        scratch_shapes=[pltpu.VMEM((tm, d), BF16)],
        compiler_params=_params(("arbitrary", "arbitrary")),
        name="qkv_proj_rope" if rope else "qkv_proj",
    )(*args)


def _rope_tables(seq):
    n_freq = LANES // 4
    pos = jnp.arange(seq, dtype=jnp.int32)
    r = (pos // GRID_W).astype(F32)
    col = (pos % GRID_W).astype(F32)
    freqs = ROPE_BASE ** (-jnp.arange(n_freq, dtype=F32) / n_freq)
    cos_t, sin_t = [], []
    for p in (r, col):
        ang = p[:, None] * freqs
        cos, sin = jnp.cos(ang), jnp.sin(ang)
        cos_t += [cos, cos]
        sin_t += [-sin, sin]
    return jnp.concatenate(cos_t, axis=1), jnp.concatenate(sin_t, axis=1)


def _attn_kernel(*refs, n_cache, n_chunks, tk, lambda_init, cast_blocks):
    n_cast = len(cast_blocks)
    lam_ref, g_ref, q_ref = refs[:3]
    n_in = 3 + (4 if n_cache else 2)
    if n_cache:
        kc_ref, vtc_ref, k_ref, vt_ref = refs[3:n_in]
    else:
        k_ref, vt_ref = refs[3:n_in]
    cast_in = refs[n_in:n_in + n_cast]
    o_ref = refs[n_in + n_cast]
    cast_out = refs[n_in + n_cast + 1:n_in + 2 * n_cast + 1]
    acc_ref = refs[-1]

    step = (pl.program_id(0) * pl.num_programs(1) + pl.program_id(1)) * pl.num_programs(2) + pl.program_id(2)
    for src, dst, n_blocks in zip(cast_in, cast_out, cast_blocks):
        @pl.when(step < n_blocks)
        def _(src=src, dst=dst):
            dst[...] = src[...].astype(dst.dtype)

    q = q_ref[...]
    qs = (q[:, :LANES], q[:, LANES:])

    def scores(kblk):
        return [lax.dot_general(kblk[:, c * LANES:(c + 1) * LANES], qs[c],
                                (((1,), (1,)), ((), ())), preferred_element_type=F32)
                for c in range(2)]

    def accumulate(sts, vt, stats):
        out = []
        for c in range(2):
            st = sts[c]
            m_cur = jnp.max(st, axis=0, keepdims=True)
            if stats is None:
                m_new = m_cur
                p = jnp.exp2(st - m_new)
                l_new = jnp.sum(p, axis=0, keepdims=True)
                acc_ref[c] = _dot(vt, p.astype(BF16))
            else:
                m_prev, l_prev = stats[c]
                m_new = jnp.maximum(m_prev, m_cur)
                alpha = jnp.exp2(m_prev - m_new)
                p = jnp.exp2(st - m_new)
                l_new = alpha * l_prev + jnp.sum(p, axis=0, keepdims=True)
                acc_ref[c] = alpha * acc_ref[c] + _dot(vt, p.astype(BF16))
            out.append((m_new, l_new))
        return out

    chunks = []
    if n_cache:
        chunks.append((lambda: kc_ref[...].astype(BF16), lambda: vtc_ref[...]))
    for i in range(n_chunks):
        chunks.append((lambda i=i: k_ref[i * tk:(i + 1) * tk, :].astype(BF16), lambda i=i: vt_ref[i]))
    stats = None
    sts = scores(chunks[0][0]())
    for n, (_, values) in enumerate(chunks):
        nxt = scores(chunks[n + 1][0]()) if n + 1 < len(chunks) else None
        stats = accumulate(sts, values(), stats)
        sts = nxt

    lp = lam_ref[...]
    lam = (jnp.exp(jnp.sum(lp[0:1] * lp[1:2], axis=-1, keepdims=True))
           - jnp.exp(jnp.sum(lp[2:3] * lp[3:4], axis=-1, keepdims=True)) + lambda_init)
    o_t = acc_ref[0] / stats[0][1] - lam * (acc_ref[1] / stats[1][1])
    o = o_t.T
    o = o * lax.rsqrt(jnp.mean(o * o, axis=-1, keepdims=True) + EPS) * g_ref[...]
    o_ref[...] = (o * (1.0 - lambda_init)).astype(o_ref.dtype)


def _chunked_transpose(v, batch, seq, heads, vd, tk):
    v = v.astype(BF16).reshape(batch, seq // tk, tk, heads, vd)
    return v.transpose(0, 3, 1, 4, 2).reshape(batch * heads, seq // tk, vd, tk)


def _cast_block_rows(rows, n_steps):
    blk = 16
    while rows % blk or rows // blk > n_steps:
        blk += 16
        if blk > rows:
            raise ValueError(f"no cast block for {rows} rows in {n_steps} steps")
    return blk


def _diff_attention(q, k, v, cache_k, cache_v, lam_params, subln_g, *, batch, seq, layer, casts=()):
    t, d = q.shape
    vd = subln_g.shape[-1]
    heads = d // vd
    n_cache = 0 if cache_k is None else cache_k.shape[0] // batch
    tq = _tile(seq, 512, LANES)
    tk = _tile(seq, 512, LANES)
    nq = seq // tq
    n_chunks = seq // tk
    in_specs = [
        pl.BlockSpec(lam_params.shape, lambda b, h, i: (0, 0)),
        pl.BlockSpec((1, vd), lambda b, h, i: (0, 0)),
        pl.BlockSpec((tq, vd), lambda b, h, i: (b * nq + i, h)),
    ]
    args = [lam_params, subln_g, q]
    if n_cache:
        in_specs += [pl.BlockSpec((n_cache, vd), lambda b, h, i: (b, h)),
                     pl.BlockSpec((None, None, vd, n_cache), lambda b, h, i: (b * heads + h, 0, 0, 0))]
        args += [cache_k, _chunked_transpose(cache_v, batch, n_cache, heads, vd, n_cache)]
    in_specs += [pl.BlockSpec((seq, vd), lambda b, h, i: (b, h)),
                 pl.BlockSpec((None, n_chunks, vd, tk), lambda b, h, i: (b * heads + h, 0, 0, 0))]
    args += [k, _chunked_transpose(v, batch, seq, heads, vd, tk)]
    out_specs = [pl.BlockSpec((tq, vd), lambda b, h, i: (b * nq + i, h))]
    out_shape = [jax.ShapeDtypeStruct((t, d), BF16)]
    n_steps = batch * heads * nq
    cast_blocks = []
    for w in casts:
        rows, cols = w.shape
        blk = _cast_block_rows(rows, n_steps)
        n_blocks = rows // blk
        cast_blocks.append(n_blocks)
        spec = pl.BlockSpec((blk, cols), lambda b, h, i, n=n_blocks: (
            jnp.minimum((b * heads + h) * nq + i, n - 1), 0))
        in_specs.append(spec)
        args.append(w)
        out_specs.append(spec)
        out_shape.append(jax.ShapeDtypeStruct((rows, cols), BF16))
    res = pl.pallas_call(
        functools.partial(_attn_kernel, n_cache=n_cache, n_chunks=n_chunks, tk=tk,
                          lambda_init=_lambda_init(layer), cast_blocks=tuple(cast_blocks)),
        grid=(batch, heads, nq),
        in_specs=in_specs,
        out_specs=out_specs,
        out_shape=out_shape,
        scratch_shapes=[pltpu.VMEM((2, vd, tq), F32)],
        compiler_params=_params(("arbitrary", "arbitrary", "arbitrary")),
        name=f"diff_attention_{'latent' if n_cache else 'context'}",
    )(*args)
    return res[0], list(res[1:])


def _out_proj_kernel(op_ref, os_ref, w_ref, xp_ref, xs_ref, gate_ref, out_ref, *, n_prompt_tiles):
    is_prompt = pl.program_id(0) < n_prompt_tiles

    @pl.when(is_prompt)
    def _():
        out_ref[...] = xp_ref[...] + gate_ref[...] * _dot(op_ref[...], w_ref[...])

    @pl.when(jnp.logical_not(is_prompt))
    def _():
        out_ref[...] = xs_ref[...] + gate_ref[...] * _dot(os_ref[...], w_ref[...])


def _out_proj_residual(o_p, o_s, w, x_p, x_s, mods, *, tok_per_cond):
    n_prompt, d = x_p.shape
    t = n_prompt + x_s.shape[0]
    tm = _tile(math.gcd(n_prompt, tok_per_cond), 1024, 8)
    tn = _tile(d, 512, LANES)
    n_pt = n_prompt // tm
    row = _cond_row_joint(tm, n_prompt, tok_per_cond)
    p_rows = lambda i: jnp.minimum(i, n_pt - 1)
    s_rows = lambda i: jnp.maximum(i - n_pt, 0)
    return pl.pallas_call(
        functools.partial(_out_proj_kernel, n_prompt_tiles=n_pt),
        grid=(t // tm, d // tn),
        in_specs=[
            pl.BlockSpec((tm, d), lambda i, j: (p_rows(i), 0)),
            pl.BlockSpec((tm, d), lambda i, j: (s_rows(i), 0)),
            pl.BlockSpec((d, tn), lambda i, j: (0, j)),
            pl.BlockSpec((tm, tn), lambda i, j: (p_rows(i), jnp.where(i < n_pt, j, d // tn - 1))),
            pl.BlockSpec((tm, tn), lambda i, j: (s_rows(i), jnp.where(i < n_pt, 0, j))),
            mods.spec(_Mods.GATE1, row, width=tn, col_fn=lambda i, j: j),
        ],
        out_specs=pl.BlockSpec((tm, tn), lambda i, j: (i, j)),
        out_shape=jax.ShapeDtypeStruct((t, d), F32),
        compiler_params=_params(("arbitrary", "arbitrary")),
        name="attn_out_proj",
    )(o_p, o_s, w, x_p, x_s, mods.table)


def _ffn_kernel(x_ref, sh_ref, sc_ref, gate_ref, g_ref, wg_ref, wu_ref, wd_ref, o_ref, h_scr):
    f = pl.program_id(1)

    @pl.when(f == 0)
    def _():
        h = _norm_modulate(x_ref[...], g_ref[...], sc_ref[...], sh_ref[...])
        h_scr[...] = h.astype(BF16)
        o_ref[...] = jnp.zeros(o_ref.shape, F32)

    h = h_scr[...]
    gte = _dot(h, wg_ref[...])
    up = _dot(h, wu_ref[...])
    act = (gte * jax.nn.sigmoid(gte) * up).astype(BF16)
    o_ref[...] += _dot(act, wd_ref[...])

    @pl.when(f == pl.num_programs(1) - 1)
    def _():
        o_ref[...] = x_ref[...] + gate_ref[...] * o_ref[...]


def _cond_row_joint(tm, n_prompt, tok_per_cond):
    def row(i, *_):
        tok = i * tm
        return jnp.where(tok < n_prompt, 0, 1 + (tok - n_prompt) // tok_per_cond)
    return row


def _dense_ffn(x, mods, norm_g, w_gu, w_down, *, n_prompt, tok_per_cond):
    t, d = x.shape
    ff = w_down.shape[0]
    tm = _tile(math.gcd(n_prompt, tok_per_cond), 1024, 8)
    tf = _tile(ff, 256, LANES)
    nf = ff // tf
    row = _cond_row_joint(tm, n_prompt, tok_per_cond)
    return pl.pallas_call(
        _ffn_kernel,
        grid=(t // tm, nf),
        in_specs=[
            pl.BlockSpec((tm, d), lambda i, f: (i, 0)),
            mods.spec(_Mods.SHIFT2, row),
            mods.spec(_Mods.SCALE2, row),
            mods.spec(_Mods.GATE2, row),
            pl.BlockSpec((1, d), lambda i, f: (0, 0)),
            pl.BlockSpec((d, tf), lambda i, f: (0, f)),
            pl.BlockSpec((d, tf), lambda i, f: (0, nf + f)),
            pl.BlockSpec((tf, d), lambda i, f: (f, 0)),
        ],
        out_specs=pl.BlockSpec((tm, d), lambda i, f: (i, 0)),
        out_shape=jax.ShapeDtypeStruct((t, d), F32),
        scratch_shapes=[pltpu.VMEM((tm, d), BF16)],
        compiler_params=_params(("arbitrary", "arbitrary")),
        name="dense_ffn",
    )(x, mods.table, mods.table, mods.table, norm_g, w_gu, w_gu, w_down)


def _pool_kernel(x_ref, xp_ref, xn_ref, sh_ref, sc_ref, gate_ref, g_ref, w_ref, ps_ref, o_ref, *,
                 tp, halo, n_prompt_tiles, prompt_tiles_per_seq, sample_tiles_per_seq, group_dim):
    i = pl.program_id(0)
    is_prompt = i < n_prompt_tiles
    tiles_per_seq = jnp.where(is_prompt, prompt_tiles_per_seq, sample_tiles_per_seq)
    local = lax.rem(jnp.where(is_prompt, i, i - n_prompt_tiles), tiles_per_seq)
    first = local == 0
    last = local == tiles_per_seq - 1
    seq_len = tiles_per_seq * tp

    g, sc, sh = g_ref[...], sc_ref[...], sh_ref[...]
    x = x_ref[...]
    h = _norm_modulate(x, g, sc, sh)
    h_prev = jnp.where(first, 0.0, _norm_modulate(xp_ref[...], g, sc, sh))
    h_next = jnp.where(last, 0.0, _norm_modulate(xn_ref[...], g, sc, sh))
    d = h.shape[-1]
    h_halo = jnp.concatenate([h_prev, h_next, jnp.zeros((LANES - 2 * halo, d), F32)], axis=0)

    t_idx = lax.broadcasted_iota(jnp.int32, (tp, tp), 0)
    s_idx = lax.broadcasted_iota(jnp.int32, (tp, tp), 1)
    t_h = lax.broadcasted_iota(jnp.int32, (tp, LANES), 0)
    u_h = lax.broadcasted_iota(jnp.int32, (tp, LANES), 1)
    off_h = jnp.where(u_h < halo, u_h - halo, tp + u_h - halo)
    pos = local * tp + lax.broadcasted_iota(jnp.int32, (tp, 1), 0)

    for grp, win in enumerate(POOL_WINDOWS):
        lo, hi = win // 2, win // 2 - 1
        cols = slice(grp * group_dim, (grp + 1) * group_dim)
        band = _mask_bf16((s_idx >= t_idx - lo) & (s_idx <= t_idx + hi))
        band_h = _mask_bf16((u_h < 2 * halo) & (off_h >= t_h - lo) & (off_h <= t_h + hi))
        c_hi, c_lo = _split_bf16(h[:, cols])
        e_hi, e_lo = _split_bf16(h_halo[:, cols])
        total = (_dot(band, c_hi) + _dot(band, c_lo)) + (_dot(band_h, e_hi) + _dot(band_h, e_lo))
        cnt = (jnp.minimum(pos + hi, seq_len - 1) - jnp.maximum(pos - lo, 0) + 1).astype(F32)
        pooled = (total / cnt - h[:, cols]).astype(BF16)
        y = _dot(pooled, w_ref[grp]) * ps_ref[:, cols]
        o_ref[:, cols] = x[:, cols] + gate_ref[:, cols] * y


def _pool_mixer(x, mods, norm_g, pool_w, pool_scale, *, n_prompt, prompt_seq, sample_seq):
    t, d = x.shape
    halo = max(POOL_WINDOWS) // 2
    tp = _tile(math.gcd(prompt_seq, sample_seq), 256, 8)
    n_groups, group_dim = pool_w.shape[0], pool_w.shape[1]
    hb = tp // halo
    n_halo_blocks = t // halo
    row = _cond_row_joint(tp, n_prompt, sample_seq)
    return pl.pallas_call(
        functools.partial(_pool_kernel, tp=tp, halo=halo, n_prompt_tiles=n_prompt // tp,
                          prompt_tiles_per_seq=prompt_seq // tp,
                          sample_tiles_per_seq=sample_seq // tp, group_dim=group_dim),
        grid=(t // tp,),
        in_specs=[
            pl.BlockSpec((tp, d), lambda i: (i, 0)),
            pl.BlockSpec((halo, d), lambda i: (jnp.maximum(i * hb - 1, 0), 0)),
            pl.BlockSpec((halo, d), lambda i: (jnp.minimum((i + 1) * hb, n_halo_blocks - 1), 0)),
            mods.spec(_Mods.SHIFT1, row),
            mods.spec(_Mods.SCALE1, row),
            mods.spec(_Mods.GATE1, row),
            pl.BlockSpec((1, d), lambda i: (0, 0)),
            pl.BlockSpec((n_groups, group_dim, group_dim), lambda i: (0, 0, 0)),
            pl.BlockSpec((1, d), lambda i: (0, 0)),
        ],
        out_specs=pl.BlockSpec((tp, d), lambda i: (i, 0)),
        out_shape=jax.ShapeDtypeStruct((t, d), F32),
        compiler_params=_params(("arbitrary",)),
        name="pool_mixer",
    )(x, x, x, mods.table, mods.table, mods.table, norm_g, pool_w, pool_scale)


def _router_kernel(x_ref, sh_ref, sc_ref, g_ref, wr_ref, br_ref, route_ref, cnt_ref):
    h = _norm_modulate(x_ref[...], g_ref[...], sc_ref[...], sh_ref[...])
    h_hi, h_lo = _split_bf16(h)
    w_hi, w_lo = _split_bf16(wr_ref[...])
    logits = _dot(h_hi, w_hi) + _dot(h_lo, w_hi) + _dot(h_hi, w_lo) + br_ref[...]

    rows = logits.shape[0]
    lane = lax.broadcasted_iota(jnp.int32, logits.shape, 1).astype(F32)
    m1 = jnp.max(logits, axis=-1, keepdims=True)
    i1 = jnp.min(jnp.where(logits == m1, lane, float(LANES)), axis=-1, keepdims=True)
    oh1 = lane == i1
    rest = jnp.where(oh1, -jnp.inf, logits)
    m2 = jnp.max(rest, axis=-1, keepdims=True)
    i2 = jnp.min(jnp.where(rest == m2, lane, float(LANES)), axis=-1, keepdims=True)
    oh2 = lane == i2
    e = jnp.exp(m2 - m1)
    gate_a = 1.0 / (1.0 + e)
    gate_b = e / (1.0 + e)

    sel = jnp.where(oh1 | oh2, 1.0, 0.0)
    r_idx = lax.broadcasted_iota(jnp.int32, (rows, rows), 0)
    c_idx = lax.broadcasted_iota(jnp.int32, (rows, rows), 1)
    earlier = _mask_bf16(c_idx < r_idx)
    rank = _dot(earlier, sel.astype(BF16))
    rank_a = jnp.sum(jnp.where(oh1, rank, 0.0), axis=-1, keepdims=True)
    rank_b = jnp.sum(jnp.where(oh2, rank, 0.0), axis=-1, keepdims=True)

    route = jnp.zeros(logits.shape, F32)
    for k, val in enumerate((i1, i2, rank_a, rank_b, gate_a, gate_b)):
        route = jnp.where(lane == float(k), val, route)
    route_ref[...] = route
    cnt_ref[...] = jnp.sum(sel, axis=0, keepdims=True)


def _moe_router(x, mods, norm_g, w_router, b_router, *, n_prompt, tok_per_cond):
    t, d = x.shape
    n_exp = w_router.shape[1]
    ch = ROUTE_CHUNK
    wr = jnp.zeros((d, LANES), F32).at[:, :n_exp].set(w_router.astype(F32))
    br = jnp.full((1, LANES), -jnp.inf, F32).at[0, :n_exp].set(b_router.astype(F32))
    row = _cond_row_joint(ch, n_prompt, tok_per_cond)
    return pl.pallas_call(
        _router_kernel,
        grid=(t // ch,),
        in_specs=[
            pl.BlockSpec((ch, d), lambda i: (i, 0)),
            mods.spec(_Mods.SHIFT2, row),
            mods.spec(_Mods.SCALE2, row),
            pl.BlockSpec((1, d), lambda i: (0, 0)),
            pl.BlockSpec((d, LANES), lambda i: (0, 0)),
            pl.BlockSpec((1, LANES), lambda i: (0, 0)),
        ],
        out_specs=[
            pl.BlockSpec((ch, LANES), lambda i: (i, 0)),
            pl.BlockSpec((None, 1, LANES), lambda i: (i, 0, 0)),
        ],
        out_shape=[
            jax.ShapeDtypeStruct((t, LANES), F32),
            jax.ShapeDtypeStruct((t // ch, 1, LANES), F32),
        ],
        compiler_params=_params(("arbitrary",)),
        name="moe_router",
    )(x, mods.table, mods.table, norm_g, wr, br)


def _dispatch_plan(route, counts, n_exp):
    ch, gr = ROUTE_CHUNK, GEMM_ROWS
    t = route.shape[0]
    n_chunks = t // ch
    i32 = jnp.int32
    cnt = counts[:, 0, :n_exp].astype(i32)
    total = cnt.sum(0)
    padded = ((total + gr - 1) // gr) * gr
    off = jnp.cumsum(padded) - padded
    start = off[None, :] + jnp.cumsum(cnt, axis=0) - cnt

    def dest(choice):
        e = route[:, choice].astype(i32).reshape(n_chunks, ch, 1)
        hit = e == jnp.arange(n_exp, dtype=i32)
        base = jnp.sum(jnp.where(hit, start[:, None, :], 0), axis=-1)
        return base.reshape(t) + route[:, 2 + choice].astype(i32)

    dest_a, dest_b = dest(0), dest(1)

    n_gemm_tiles = (t * TOP_K) // gr + n_exp

    tile_lo = jnp.arange(n_gemm_tiles, dtype=i32) * gr
    used = tile_lo < (off + padded)[-1]
    ends = off + padded
    expert_of = jnp.minimum(jnp.sum(tile_lo[:, None] >= ends[None, :], axis=1), n_exp - 1).astype(i32)
    n_used = jnp.maximum(used.sum(), 1)
    clamp = jnp.minimum(jnp.arange(n_gemm_tiles, dtype=i32), n_used - 1).astype(i32)
    gemm_list = (clamp, expert_of[clamp], used.astype(i32))

    dest_rows = jnp.stack([dest_a.reshape(n_chunks, ch), dest_b.reshape(n_chunks, ch)], axis=1)
    gates = jnp.concatenate([route[:, 4:6], jnp.zeros((t, 6), F32)], axis=1)
    return dest_rows, gates, gemm_list, n_gemm_tiles * gr


def _wait_rows(src, dst, sem, n_rows):
    pltpu.make_async_copy(src.at[pl.ds(0, n_rows), :], dst.at[pl.ds(0, n_rows), :], sem).wait()


def _scatter_kernel(dest_ref, x_ref, sh_ref, sc_ref, g_ref, init_hbm, out_hbm, h_scr, sem):
    del init_hbm
    ch = x_ref.shape[0]
    h_scr[...] = _norm_modulate(x_ref[...], g_ref[...], sc_ref[...], sh_ref[...])

    def issue(j, carry):
        for k in range(TOP_K):
            pltpu.make_async_copy(h_scr.at[pl.ds(j, 1), :], out_hbm.at[pl.ds(dest_ref[k, j], 1), :],
                                  sem).start(priority=k)
        return carry

    lax.fori_loop(0, ch, issue, 0, unroll=8)
    for _ in range(TOP_K):
        _wait_rows(h_scr, out_hbm, sem, ch)


def _moe_scatter(x, mods, norm_g, dest_rows, n_rows, *, n_prompt, tok_per_cond):
    t, d = x.shape
    ch = ROUTE_CHUNK
    row = _cond_row_joint(ch, n_prompt, tok_per_cond)
    return pl.pallas_call(
        _scatter_kernel,
        grid=(t // ch,),
        in_specs=[
            pl.BlockSpec((None, TOP_K, ch), lambda i: (i, 0, 0), memory_space=pltpu.SMEM),
            pl.BlockSpec((ch, d), lambda i: (i, 0)),
            mods.spec(_Mods.SHIFT2, row),
            mods.spec(_Mods.SCALE2, row),
            pl.BlockSpec((1, d), lambda i: (0, 0)),
            pl.BlockSpec(memory_space=pl.ANY),
        ],
        out_specs=pl.BlockSpec(memory_space=pl.ANY),
        out_shape=jax.ShapeDtypeStruct((n_rows, d), F32),
        scratch_shapes=[pltpu.VMEM((ch, d), F32), pltpu.SemaphoreType.DMA(())],
        input_output_aliases={5: 0},
        compiler_params=_params(("arbitrary",)),
        name="moe_scatter",
    )(dest_rows, x, mods.table, mods.table, norm_g, jnp.zeros((n_rows, d), F32))


def _expert_kernel(blk_ref, exp_ref, used_ref, x_ref, wg_ref, wu_ref, wd_ref, o_ref, x_scr):
    r, f = pl.program_id(0), pl.program_id(1)

    @pl.when(used_ref[r] == 1)
    def _():
        @pl.when(f == 0)
        def _():
            o_ref[...] = jnp.zeros(o_ref.shape, F32)
            x_scr[...] = x_ref[...].astype(BF16)

        x = x_scr[...]
        gte = _dot(x, wg_ref[...])
        up = _dot(x, wu_ref[...])
        act = (gte * jax.nn.sigmoid(gte) * up).astype(BF16)
        o_ref[...] += _dot(act, wd_ref[...])


def _moe_experts(xs, w_gu, w_down, gemm_list):
    n_rows, d = xs.shape
    ff = w_down.shape[1]
    gr = GEMM_ROWS
    tf = _tile(ff, 256, LANES)
    nf = ff // tf
    n_tiles = gemm_list[0].shape[0]

    def fcol(f, used, r):
        return jnp.where(used[r] == 1, f, nf - 1)

    grid_spec = pltpu.PrefetchScalarGridSpec(
        num_scalar_prefetch=3,
        grid=(n_tiles, nf),
        in_specs=[
            pl.BlockSpec((gr, d), lambda r, f, blk, ex, used: (blk[r], 0)),
            pl.BlockSpec((None, d, tf), lambda r, f, blk, ex, used: (ex[r], 0, fcol(f, used, r))),
            pl.BlockSpec((None, d, tf), lambda r, f, blk, ex, used: (ex[r], 0, nf + fcol(f, used, r))),
            pl.BlockSpec((None, tf, d), lambda r, f, blk, ex, used: (ex[r], fcol(f, used, r), 0)),
        ],
        out_specs=pl.BlockSpec((gr, d), lambda r, f, blk, ex, used: (blk[r], 0)),
        scratch_shapes=[pltpu.VMEM((gr, d), BF16)],
    )
    return pl.pallas_call(
        _expert_kernel,
        grid_spec=grid_spec,
        out_shape=jax.ShapeDtypeStruct((n_rows, d), F32),
        input_output_aliases={3: 0},
        compiler_params=_params(("arbitrary", "arbitrary")),
        name="moe_experts",
    )(*gemm_list, xs, w_gu, w_gu, w_down)


def _combine_kernel(dest_ref, gates_ref, x_ref, gate_ref, y_hbm, op_ref, os_ref, y_scr, sem, *,
                    n_prompt_chunks):
    ch = x_ref.shape[0]

    def issue(j, carry):
        for k in range(TOP_K):
            pltpu.make_async_copy(y_hbm.at[pl.ds(dest_ref[k, j], 1), :], y_scr.at[k, pl.ds(j, 1), :],
                                  sem).start(priority=k)
        return carry

    lax.fori_loop(0, ch, issue, 0, unroll=8)
    for k in range(TOP_K):
        _wait_rows(y_hbm, y_scr.at[k], sem, ch)

    gates = gates_ref[...]
    mix = gates[:, 0:1] * y_scr[0] + gates[:, 1:2] * y_scr[1]
    out = x_ref[...] + gate_ref[...] * mix
    is_prompt = pl.program_id(0) < n_prompt_chunks

    @pl.when(is_prompt)
    def _():
        op_ref[...] = out

    @pl.when(jnp.logical_not(is_prompt))
    def _():
        os_ref[...] = out


def _moe_combine(y, dest_rows, gates, x, mods, *, n_prompt, tok_per_cond):
    t, d = x.shape
    ch = ROUTE_CHUNK
    n_pc = n_prompt // ch
    row = _cond_row_joint(ch, n_prompt, tok_per_cond)
    return pl.pallas_call(
        functools.partial(_combine_kernel, n_prompt_chunks=n_pc),
        grid=(t // ch,),
        in_specs=[
            pl.BlockSpec((None, TOP_K, ch), lambda i: (i, 0, 0), memory_space=pltpu.SMEM),
            pl.BlockSpec((ch, gates.shape[1]), lambda i: (i, 0)),
            pl.BlockSpec((ch, d), lambda i: (i, 0)),
            mods.spec(_Mods.GATE2, row),
            pl.BlockSpec(memory_space=pl.ANY),
        ],
        out_specs=[
            pl.BlockSpec((ch, d), lambda i: (jnp.minimum(i, n_pc - 1), 0)),
            pl.BlockSpec((ch, d), lambda i: (jnp.maximum(i - n_pc, 0), 0)),
        ],
        out_shape=[jax.ShapeDtypeStruct((n_prompt, d), F32), jax.ShapeDtypeStruct((t - n_prompt, d), F32)],
        scratch_shapes=[pltpu.VMEM((TOP_K, ch, d), F32), pltpu.SemaphoreType.DMA(())],
        compiler_params=_params(("arbitrary",)),
        name="moe_combine",
    )(dest_rows, gates, x, mods.table, y)


def kernel(x_prompt, x_sample, c, cache_k, cache_v, c_ctx, ada_w, ada_b, norm1_g, norm2_g,
           attn_w_qkv, attn_w_o, attn_q_norm, attn_k_norm, attn_lambda, attn_subln_g,
           pool_w, pool_scale, ffn_w_gu, ffn_w_down,
           moe_w_router, moe_b_router, moe_w_gu, moe_w_down):
    b_ctx, l_ctx, d = x_prompt.shape
    b_dec, l_dec, _ = x_sample.shape
    depth = ada_w.shape[0]
    n_even, l_past, heads, v_dim = cache_k.shape[1:]
    head_dim = v_dim // 2
    assert head_dim == LANES and depth == 2 and l_dec % GRID_W == 0
    n_exp = moe_w_router.shape[-1]
    tp_, ts_ = b_ctx * l_ctx, b_dec * l_dec
    t_all = tp_ + ts_

    cond_rows = 8 * ((1 + b_dec + 7) // 8)
    cond = jnp.zeros((cond_rows, d), F32).at[0].set(c_ctx).at[1:1 + b_dec].set(c)
    mod_table = _adaln_mods(cond, ada_w, ada_b).reshape(depth * cond_rows * 6, 1, d)

    xp = x_prompt.reshape(tp_, d)
    xs = x_sample.reshape(ts_, d)

    layer, j = 0, 0
    mods = _Mods(mod_table, cond_rows, layer)
    g1 = norm1_g[layer].reshape(1, d)
    w_qkv = attn_w_qkv[j].astype(BF16)
    q_gain = (attn_q_norm[j] * (head_dim ** -0.5 * LOG2E)).reshape(1, head_dim)
    k_gain = attn_k_norm[j].reshape(1, head_dim)
    tables = _rope_tables(l_dec)
    prompt = dict(cond_base=0, tok_per_cond=tp_)
    sample = dict(cond_base=1, tok_per_cond=l_dec)

    q_p, k_p, v_p = _qkv_proj(xp, mods, g1, w_qkv, q_gain, k_gain, None, kv_dtype=F32, **prompt)
    q_s, k_s, v_s = _qkv_proj(xs, mods, g1, w_qkv, q_gain, k_gain, tables, kv_dtype=BF16, **sample)

    lam_params = attn_lambda[j].astype(F32)
    subln = attn_subln_g[j].reshape(1, v_dim)
    o_p, _ = _diff_attention(q_p, k_p, v_p, None, None, lam_params, subln,
                             batch=b_ctx, seq=l_ctx, layer=layer)
    ck = cache_k[:, j].reshape(b_dec * l_past, heads * v_dim)
    cv = cache_v[:, j].reshape(b_dec * l_past, heads * v_dim)
    later_weights = [attn_w_o[j], ffn_w_gu[j], ffn_w_down[j], pool_w[0], moe_w_gu[0], moe_w_down[0]]
    o_s, later_bf16 = _diff_attention(q_s, k_s, v_s, ck, cv, lam_params, subln,
                                      batch=b_dec, seq=l_dec, layer=layer,
                                      casts=[w.reshape(-1, w.shape[-1]) for w in later_weights])
    w_o, w_ffn_gu, w_ffn_down, w_pool, w_moe_gu, w_moe_down = [
        b.reshape(w.shape) for b, w in zip(later_bf16, later_weights)]

    x1 = _out_proj_residual(o_p, o_s, w_o, xp, xs, mods, tok_per_cond=l_dec)

    x2 = _dense_ffn(x1, mods, norm2_g[layer].reshape(1, d), w_ffn_gu, w_ffn_down,
                    n_prompt=tp_, tok_per_cond=l_dec)

    layer, j = 1, 0
    mods = _Mods(mod_table, cond_rows, layer)
    x3 = _pool_mixer(x2, mods, norm1_g[layer].reshape(1, d), w_pool,
                     pool_scale[j].reshape(1, d), n_prompt=tp_, prompt_seq=l_ctx, sample_seq=l_dec)

    g2 = norm2_g[layer].reshape(1, d)
    joint = dict(n_prompt=tp_, tok_per_cond=l_dec)
    route, counts = _moe_router(x3, mods, g2, moe_w_router[j], moe_b_router[j], **joint)
    dest_rows, gates, gemm_list, n_rows = _dispatch_plan(route, counts, n_exp)
    rows_sorted = _moe_scatter(x3, mods, g2, dest_rows, n_rows, **joint)
    y_sorted = _moe_experts(rows_sorted, w_moe_gu, w_moe_down, gemm_list)
    y_p, y_s = _moe_combine(y_sorted, dest_rows, gates, x3, mods, **joint)

    y_prompt = y_p.reshape(b_ctx, l_ctx, d)
    y_sample = y_s.reshape(b_dec, l_dec, d)
    state_k = k_p.reshape(b_ctx, 1, l_ctx, heads, v_dim)
    state_v = v_p.reshape(b_ctx, 1, l_ctx, heads, v_dim)
    return (y_prompt, y_sample, state_k, state_v)
```

```python
import functools
import math

import jax
import jax.numpy as jnp
from jax import lax
from jax.experimental import pallas as pl
from jax.experimental.pallas import tpu as pltpu

F32 = jnp.float32
BF16 = jnp.bfloat16

GRID_W = 64
ROPE_BASE = 10000.0
POOL_WINDOWS = (2, 4, 8, 16)
TOP_K = 2
EPS = 1e-6
LOG2E = 1.4426950408889634

LANES = 128
VMEM_LIMIT = 52 * 2**20

ROUTE_CHUNK = 256
GEMM_ROWS = 1024


def _lambda_init(layer):
    return 0.8 - 0.6 * math.exp(-0.3 * layer)


def _tile(n, pref, mult):
    best = None
    t = mult
    while t <= min(n, pref):
        if n % t == 0:
            best = t
        t += mult
    if best is None:
        raise ValueError(f"no tile for {n} (multiple of {mult}, <= {pref})")
    return best


def _params(semantics):
    return pltpu.CompilerParams(dimension_semantics=semantics, vmem_limit_bytes=VMEM_LIMIT)


def _dot(a, b):
    return jnp.dot(a, b, preferred_element_type=F32)


def _mask_bf16(m):
    return jnp.where(m, 1.0, 0.0).astype(BF16)


def _split_bf16(x):
    hi = x.astype(BF16)
    lo = (x - hi.astype(F32)).astype(BF16)
    return hi, lo


def _norm_modulate(x, g, scale, shift):
    ms = jnp.mean(x * x, axis=-1, keepdims=True)
    return x * lax.rsqrt(ms + EPS) * (g * (1.0 + scale)) + shift


def _adaln_kernel(cond_ref, w_ref, b_ref, o_ref):
    c = cond_ref[...]
    s = c * jax.nn.sigmoid(c)
    s_hi, s_lo = _split_bf16(s)
    w_hi, w_lo = _split_bf16(w_ref[...])
    o_ref[...] = _dot(s_hi, w_hi) + _dot(s_lo, w_hi) + _dot(s_hi, w_lo) + b_ref[...]


def _adaln_mods(cond, ada_w, ada_b):
    depth, d, n = ada_w.shape
    rows = cond.shape[0]
    tn = _tile(n, 512, LANES)
    return pl.pallas_call(
        _adaln_kernel,
        grid=(depth, n // tn),
        in_specs=[
            pl.BlockSpec((rows, d), lambda l, j: (0, 0)),
            pl.BlockSpec((None, d, tn), lambda l, j: (l, 0, j)),
            pl.BlockSpec((None, 1, tn), lambda l, j: (l, 0, j)),
        ],
        out_specs=pl.BlockSpec((None, rows, tn), lambda l, j: (l, 0, j)),
        out_shape=jax.ShapeDtypeStruct((depth, rows, n), F32),
        compiler_params=_params(("arbitrary", "arbitrary")),
        name="adaln_mods",
    )(cond, ada_w, ada_b.reshape(depth, 1, n))


class _Mods:
    SHIFT1, SCALE1, GATE1, SHIFT2, SCALE2, GATE2 = range(6)

    def __init__(self, table, rows, layer):
        self.table = table
        self.rows = rows
        self.layer = layer

    def spec(self, which, row_fn, width=None, col_fn=None):
        d = self.table.shape[-1]
        width = d if width is None else width
        base = self.layer * self.rows

        def index(*ids):
            col = 0 if col_fn is None else col_fn(*ids)
            return ((base + row_fn(*ids)) * 6 + which, 0, col)

        return pl.BlockSpec((None, 1, width), index)


def _qkv_kernel(*refs, rope, tn, n_sec):
    x_ref, sh_ref, sc_ref, g_ref, w_ref, qg_ref, kg_ref = refs[:7]
    rest = list(refs[7:])
    cos_ref, sin_ref = (rest.pop(0), rest.pop(0)) if rope else (None, None)
    q_ref, k_ref, v_ref, h_scr = rest
    j = pl.program_id(1)

    @pl.when(j == 0)
    def _():
        h = _norm_modulate(x_ref[...], g_ref[...], sc_ref[...], sh_ref[...])
        h_scr[...] = h.astype(BF16)

    def qk_epilogue(gain_ref, out_ref):
        acc = _dot(h_scr[...], w_ref[...])
        r_idx = lax.broadcasted_iota(jnp.int32, (2 * LANES, 2 * LANES), 0)
        c_idx = lax.broadcasted_iota(jnp.int32, (2 * LANES, 2 * LANES), 1)
        if rope:
            swap = (r_idx >= LANES) & (c_idx >= LANES) & (r_idx - LANES == jnp.bitwise_xor(c_idx - LANES, LANES // 4))
            rhs = _mask_bf16(((r_idx < LANES) & (c_idx < LANES)) | swap)
        else:
            rhs = jnp.ones((2 * LANES, LANES), BF16)
        for g in range(tn // LANES):
            y = acc[:, g * LANES:(g + 1) * LANES]
            yg = y * gain_ref[...]
            if rope:
                res = _dot(jnp.concatenate([(y * y).astype(BF16), yg.astype(BF16)], axis=1), rhs)
                inv = lax.rsqrt(res[:, :LANES] * (1.0 / LANES) + EPS)
                out = (yg * cos_ref[...] + res[:, LANES:] * sin_ref[...]) * inv
            else:
                ss = _dot(jnp.concatenate(_split_bf16(y * y), axis=1), rhs)
                out = yg * lax.rsqrt(ss * (1.0 / LANES) + EPS)
            out_ref[:, g * LANES:(g + 1) * LANES] = out.astype(out_ref.dtype)

    @pl.when(j < n_sec)
    def _():
        qk_epilogue(qg_ref, q_ref)

    @pl.when((j >= n_sec) & (j < 2 * n_sec))
    def _():
        qk_epilogue(kg_ref, k_ref)

    @pl.when(j >= 2 * n_sec)
    def _():
        v_ref[...] = _dot(h_scr[...], w_ref[...]).astype(v_ref.dtype)


def _qkv_proj(x, mods, norm_g, w, q_gain, k_gain, tables, *, cond_base, tok_per_cond, kv_dtype):
    t, d = x.shape
    tm = _tile(math.gcd(t, tok_per_cond), 1024, 8)
    tn = _tile(d, 512, LANES)
    n_sec = d // tn
    rope = tables is not None
    row = lambda i, j: cond_base + (i * tm) // tok_per_cond
    in_specs = [
        pl.BlockSpec((tm, d), lambda i, j: (i, 0)),
        mods.spec(_Mods.SHIFT1, row),
        mods.spec(_Mods.SCALE1, row),
        pl.BlockSpec((1, d), lambda i, j: (0, 0)),
        pl.BlockSpec((d, tn), lambda i, j: (0, j)),
        pl.BlockSpec((1, LANES), lambda i, j: (0, 0)),
        pl.BlockSpec((1, LANES), lambda i, j: (0, 0)),
    ]
    args = [x, mods.table, mods.table, norm_g, w, q_gain, k_gain]
    if rope:
        nblk = tables[0].shape[0] // tm
        for tab in tables:
            in_specs.append(pl.BlockSpec((tm, LANES), lambda i, j: (i % nblk, 0)))
            args.append(tab)
    out_specs = [
        pl.BlockSpec((tm, tn), lambda i, j, s=s: (i, jnp.clip(j - s * n_sec, 0, n_sec - 1)))
        for s in range(3)
    ]
    return pl.pallas_call(
        functools.partial(_qkv_kernel, rope=rope, tn=tn, n_sec=n_sec),
        grid=(t // tm, 3 * n_sec),
        in_specs=in_specs,
        out_specs=out_specs,
        out_shape=[jax.ShapeDtypeStruct((t, d), BF16), jax.ShapeDtypeStruct((t, d), kv_dtype),
                   jax.ShapeDtypeStruct((t, d), kv_dtype)],
        scratch_shapes=[pltpu.VMEM((tm, d), BF16)],
        compiler_params=_params(("arbitrary", "arbitrary")),
        name="qkv_proj_rope" if rope else "qkv_proj",
    )(*args)


def _rope_tables(seq):
    n_freq = LANES // 4
    pos = jnp.arange(seq, dtype=jnp.int32)
    r = (pos // GRID_W).astype(F32)
    col = (pos % GRID_W).astype(F32)
    freqs = ROPE_BASE ** (-jnp.arange(n_freq, dtype=F32) / n_freq)
    cos_t, sin_t = [], []
    for p in (r, col):
        ang = p[:, None] * freqs
        cos, sin = jnp.cos(ang), jnp.sin(ang)
        cos_t += [cos, cos]
        sin_t += [-sin, sin]
    return jnp.concatenate(cos_t, axis=1), jnp.concatenate(sin_t, axis=1)


def _attn_kernel(*refs, n_cache, n_chunks, tk, lambda_init, cast_blocks, heads_per_step):
    n_cast = len(cast_blocks)
    lam_ref, g_ref, q_ref = refs[:3]
    n_in = 3 + (4 if n_cache else 2)
    if n_cache:
        kc_ref, vtc_ref, k_ref, vt_ref = refs[3:n_in]
    else:
        k_ref, vt_ref = refs[3:n_in]
    cast_in = refs[n_in:n_in + n_cast]
    o_ref = refs[n_in + n_cast]
    cast_out = refs[n_in + n_cast + 1:n_in + 2 * n_cast + 1]
    acc_ref = refs[-1]

    step = (pl.program_id(0) * pl.num_programs(1) + pl.program_id(1)) * pl.num_programs(2) + pl.program_id(2)
    for src, dst, n_blocks in zip(cast_in, cast_out, cast_blocks):
        @pl.when(step < n_blocks)
        def _(src=src, dst=dst):
            dst[...] = src[...].astype(dst.dtype)

    vd = 2 * LANES
    for hq in range(heads_per_step):
        cols = slice(hq * vd, (hq + 1) * vd)
        _attn_head(lam_ref, g_ref, q_ref.at[:, cols],
                   kc_ref.at[:, cols] if n_cache else None, vtc_ref.at[hq] if n_cache else None,
                   k_ref.at[:, cols], vt_ref.at[hq], o_ref.at[:, cols], acc_ref,
                   n_chunks=n_chunks, tk=tk, lambda_init=lambda_init)


def _attn_head(lam_ref, g_ref, q_ref, kc_ref, vtc_ref, k_ref, vt_ref, o_ref, acc_ref, *,
               n_chunks, tk, lambda_init):
    q = q_ref[...]
    qs = (q[:, :LANES], q[:, LANES:])

    def scores(kblk):
        return [lax.dot_general(kblk[:, c * LANES:(c + 1) * LANES], qs[c],
                                (((1,), (1,)), ((), ())), preferred_element_type=F32)
                for c in range(2)]

    def accumulate(sts, vt, stats):
        out = []
        for c in range(2):
            st = sts[c]
            m_cur = jnp.max(st, axis=0, keepdims=True)
            if stats is None:
                m_new = m_cur
                p = jnp.exp2(st - m_new)
                l_new = jnp.sum(p, axis=0, keepdims=True)
                acc_ref[c] = _dot(vt, p.astype(BF16))
            else:
                m_prev, l_prev = stats[c]
                m_new = jnp.maximum(m_prev, m_cur)
                alpha = jnp.exp2(m_prev - m_new)
                p = jnp.exp2(st - m_new)
                l_new = alpha * l_prev + jnp.sum(p, axis=0, keepdims=True)
                acc_ref[c] = alpha * acc_ref[c] + _dot(vt, p.astype(BF16))
            out.append((m_new, l_new))
        return out

    chunks = []
    if kc_ref is not None:
        chunks.append((lambda: kc_ref[...].astype(BF16), lambda: vtc_ref[...]))
    for i in range(n_chunks):
        chunks.append((lambda i=i: k_ref[i * tk:(i + 1) * tk, :].astype(BF16), lambda i=i: vt_ref[i]))
    stats = None
    sts = scores(chunks[0][0]())
    for n, (_, values) in enumerate(chunks):
        nxt = scores(chunks[n + 1][0]()) if n + 1 < len(chunks) else None
        stats = accumulate(sts, values(), stats)
        sts = nxt

    lp = lam_ref[...]
    lam = (jnp.exp(jnp.sum(lp[0:1] * lp[1:2], axis=-1, keepdims=True))
           - jnp.exp(jnp.sum(lp[2:3] * lp[3:4], axis=-1, keepdims=True)) + lambda_init)
    o_t = acc_ref[0] / stats[0][1] - lam * (acc_ref[1] / stats[1][1])
    o = o_t.T
    o = o * lax.rsqrt(jnp.mean(o * o, axis=-1, keepdims=True) + EPS) * g_ref[...]
    o_ref[...] = (o * (1.0 - lambda_init)).astype(o_ref.dtype)


def _chunked_transpose(v, batch, seq, heads, vd, tk):
    v = v.astype(BF16).reshape(batch, seq // tk, tk, heads, vd)
    return v.transpose(0, 3, 1, 4, 2).reshape(batch * heads, seq // tk, vd, tk)


def _cast_block_rows(rows, n_steps):
    blk = 16
    while rows % blk or rows // blk > n_steps:
        blk += 16
        if blk > rows:
            raise ValueError(f"no cast block for {rows} rows in {n_steps} steps")
    return blk


def _diff_attention(q, k, v, cache_k, cache_v, lam_params, subln_g, *, batch, seq, layer, casts=()):
    t, d = q.shape
    vd = subln_g.shape[-1]
    heads = d // vd
    n_cache = 0 if cache_k is None else cache_k.shape[0] // batch
    tq = _tile(seq, 512, LANES)
    tk = _tile(seq, 512, LANES)
    nq = seq // tq
    n_chunks = seq // tk
    hps = heads if seq <= tq else 1
    groups = heads // hps
    wide = hps * vd
    in_specs = [
        pl.BlockSpec(lam_params.shape, lambda b, h, i: (0, 0)),
        pl.BlockSpec((1, vd), lambda b, h, i: (0, 0)),
        pl.BlockSpec((tq, wide), lambda b, h, i: (b * nq + i, h)),
    ]
    args = [lam_params, subln_g, q]
    if n_cache:
        in_specs += [pl.BlockSpec((n_cache, wide), lambda b, h, i: (b, h)),
                     pl.BlockSpec((hps, None, vd, n_cache), lambda b, h, i: (b * groups + h, 0, 0, 0))]
        args += [cache_k, _chunked_transpose(cache_v, batch, n_cache, heads, vd, n_cache)]
    in_specs += [pl.BlockSpec((seq, wide), lambda b, h, i: (b, h)),
                 pl.BlockSpec((hps, n_chunks, vd, tk), lambda b, h, i: (b * groups + h, 0, 0, 0))]
    args += [k, _chunked_transpose(v, batch, seq, heads, vd, tk)]
    out_specs = [pl.BlockSpec((tq, wide), lambda b, h, i: (b * nq + i, h))]
    out_shape = [jax.ShapeDtypeStruct((t, d), BF16)]
    n_steps = batch * groups * nq
    cast_blocks = []
    for w in casts:
        rows, cols = w.shape
        blk = _cast_block_rows(rows, n_steps)
        n_blocks = rows // blk
        cast_blocks.append(n_blocks)
        spec = pl.BlockSpec((blk, cols), lambda b, h, i, n=n_blocks: (
            jnp.minimum((b * groups + h) * nq + i, n - 1), 0))
        in_specs.append(spec)
        args.append(w)
        out_specs.append(spec)
        out_shape.append(jax.ShapeDtypeStruct((rows, cols), BF16))
    res = pl.pallas_call(
        functools.partial(_attn_kernel, n_cache=n_cache, n_chunks=n_chunks, tk=tk,
                          lambda_init=_lambda_init(layer), cast_blocks=tuple(cast_blocks),
                          heads_per_step=hps),
        grid=(batch, groups, nq),
        in_specs=in_specs,
        out_specs=out_specs,
        out_shape=out_shape,
        scratch_shapes=[pltpu.VMEM((2, vd, tq), F32)],
        compiler_params=_params(("arbitrary", "arbitrary", "arbitrary")),
        name=f"diff_attention_{'latent' if n_cache else 'context'}",
    )(*args)
    return res[0], list(res[1:])


def _out_proj_kernel(op_ref, os_ref, w_ref, xp_ref, xs_ref, gate_ref, out_ref, *, n_prompt_tiles):
    is_prompt = pl.program_id(0) < n_prompt_tiles

    @pl.when(is_prompt)
    def _():
        out_ref[...] = xp_ref[...] + gate_ref[...] * _dot(op_ref[...], w_ref[...])

    @pl.when(jnp.logical_not(is_prompt))
    def _():
        out_ref[...] = xs_ref[...] + gate_ref[...] * _dot(os_ref[...], w_ref[...])


def _out_proj_residual(o_p, o_s, w, x_p, x_s, mods, *, tok_per_cond):
    n_prompt, d = x_p.shape
    t = n_prompt + x_s.shape[0]
    tm = _tile(math.gcd(n_prompt, tok_per_cond), 1024, 8)
    tn = _tile(d, 512, LANES)
    n_pt = n_prompt // tm
    row = _cond_row_joint(tm, n_prompt, tok_per_cond)
    p_rows = lambda i: jnp.minimum(i, n_pt - 1)
    s_rows = lambda i: jnp.maximum(i - n_pt, 0)
    return pl.pallas_call(
        functools.partial(_out_proj_kernel, n_prompt_tiles=n_pt),
        grid=(t // tm, d // tn),
        in_specs=[
            pl.BlockSpec((tm, d), lambda i, j: (p_rows(i), 0)),
            pl.BlockSpec((tm, d), lambda i, j: (s_rows(i), 0)),
            pl.BlockSpec((d, tn), lambda i, j: (0, j)),
            pl.BlockSpec((tm, tn), lambda i, j: (p_rows(i), jnp.where(i < n_pt, j, d // tn - 1))),
            pl.BlockSpec((tm, tn), lambda i, j: (s_rows(i), jnp.where(i < n_pt, 0, j))),
            mods.spec(_Mods.GATE1, row, width=tn, col_fn=lambda i, j: j),
        ],
        out_specs=pl.BlockSpec((tm, tn), lambda i, j: (i, j)),
        out_shape=jax.ShapeDtypeStruct((t, d), F32),
        compiler_params=_params(("arbitrary", "arbitrary")),
        name="attn_out_proj",
    )(o_p, o_s, w, x_p, x_s, mods.table)


def _ffn_kernel(x_ref, sh_ref, sc_ref, gate_ref, g_ref, wg_ref, wu_ref, wd_ref, o_ref, h_scr):
    f = pl.program_id(1)

    @pl.when(f == 0)
    def _():
        h = _norm_modulate(x_ref[...], g_ref[...], sc_ref[...], sh_ref[...])
        h_scr[...] = h.astype(BF16)
        o_ref[...] = jnp.zeros(o_ref.shape, F32)

    h = h_scr[...]
    gte = _dot(h, wg_ref[...])
    up = _dot(h, wu_ref[...])
    act = (gte * jax.nn.sigmoid(gte) * up).astype(BF16)
    o_ref[...] += _dot(act, wd_ref[...])

    @pl.when(f == pl.num_programs(1) - 1)
    def _():
        o_ref[...] = x_ref[...] + gate_ref[...] * o_ref[...]


def _cond_row_joint(tm, n_prompt, tok_per_cond):
    def row(i, *_):
        tok = i * tm
        return jnp.where(tok < n_prompt, 0, 1 + (tok - n_prompt) // tok_per_cond)
    return row


def _dense_ffn(x, mods, norm_g, w_gu, w_down, *, n_prompt, tok_per_cond):
    t, d = x.shape
    ff = w_down.shape[0]
    tm = _tile(math.gcd(n_prompt, tok_per_cond), 1024, 8)
    tf = _tile(ff, 256, LANES)
    nf = ff // tf
    row = _cond_row_joint(tm, n_prompt, tok_per_cond)
    return pl.pallas_call(
        _ffn_kernel,
        grid=(t // tm, nf),
        in_specs=[
            pl.BlockSpec((tm, d), lambda i, f: (i, 0)),
            mods.spec(_Mods.SHIFT2, row),
            mods.spec(_Mods.SCALE2, row),
            mods.spec(_Mods.GATE2, row),
            pl.BlockSpec((1, d), lambda i, f: (0, 0)),
            pl.BlockSpec((d, tf), lambda i, f: (0, f)),
            pl.BlockSpec((d, tf), lambda i, f: (0, nf + f)),
            pl.BlockSpec((tf, d), lambda i, f: (f, 0)),
        ],
        out_specs=pl.BlockSpec((tm, d), lambda i, f: (i, 0)),
        out_shape=jax.ShapeDtypeStruct((t, d), F32),
        scratch_shapes=[pltpu.VMEM((tm, d), BF16)],
        compiler_params=_params(("arbitrary", "arbitrary")),
        name="dense_ffn",
    )(x, mods.table, mods.table, mods.table, norm_g, w_gu, w_gu, w_down)


def _pool_kernel(x_ref, xp_ref, xn_ref, sh_ref, sc_ref, gate_ref, g_ref, w_ref, ps_ref, o_ref, *,
                 tp, halo, n_prompt_tiles, prompt_tiles_per_seq, sample_tiles_per_seq, group_dim):
    i = pl.program_id(0)
    is_prompt = i < n_prompt_tiles
    tiles_per_seq = jnp.where(is_prompt, prompt_tiles_per_seq, sample_tiles_per_seq)
    local = lax.rem(jnp.where(is_prompt, i, i - n_prompt_tiles), tiles_per_seq)
    first = local == 0
    last = local == tiles_per_seq - 1
    seq_len = tiles_per_seq * tp

    g, sc, sh = g_ref[...], sc_ref[...], sh_ref[...]
    x = x_ref[...]
    h = _norm_modulate(x, g, sc, sh)
    h_prev = jnp.where(first, 0.0, _norm_modulate(xp_ref[...], g, sc, sh))
    h_next = jnp.where(last, 0.0, _norm_modulate(xn_ref[...], g, sc, sh))
    d = h.shape[-1]
    h_halo = jnp.concatenate([h_prev, h_next, jnp.zeros((LANES - 2 * halo, d), F32)], axis=0)

    t_idx = lax.broadcasted_iota(jnp.int32, (tp, tp), 0)
    s_idx = lax.broadcasted_iota(jnp.int32, (tp, tp), 1)
    t_h = lax.broadcasted_iota(jnp.int32, (tp, LANES), 0)
    u_h = lax.broadcasted_iota(jnp.int32, (tp, LANES), 1)
    off_h = jnp.where(u_h < halo, u_h - halo, tp + u_h - halo)
    pos = local * tp + lax.broadcasted_iota(jnp.int32, (tp, 1), 0)

    for grp, win in enumerate(POOL_WINDOWS):
        lo, hi = win // 2, win // 2 - 1
        cols = slice(grp * group_dim, (grp + 1) * group_dim)
        band = _mask_bf16((s_idx >= t_idx - lo) & (s_idx <= t_idx + hi))
        band_h = _mask_bf16((u_h < 2 * halo) & (off_h >= t_h - lo) & (off_h <= t_h + hi))
        total = _dot(band, h[:, cols].astype(BF16)) + _dot(band_h, h_halo[:, cols].astype(BF16))
        cnt = (jnp.minimum(pos + hi, seq_len - 1) - jnp.maximum(pos - lo, 0) + 1).astype(F32)
        pooled = (total / cnt - h[:, cols]).astype(BF16)
        y = _dot(pooled, w_ref[grp]) * ps_ref[:, cols]
        o_ref[:, cols] = x[:, cols] + gate_ref[:, cols] * y


def _pool_mixer(x, mods, norm_g, pool_w, pool_scale, *, n_prompt, prompt_seq, sample_seq):
    t, d = x.shape
    halo = max(POOL_WINDOWS) // 2
    tp = _tile(math.gcd(prompt_seq, sample_seq), 256, 8)
    n_groups, group_dim = pool_w.shape[0], pool_w.shape[1]
    hb = tp // halo
    n_halo_blocks = t // halo
    row = _cond_row_joint(tp, n_prompt, sample_seq)
    return pl.pallas_call(
        functools.partial(_pool_kernel, tp=tp, halo=halo, n_prompt_tiles=n_prompt // tp,
                          prompt_tiles_per_seq=prompt_seq // tp,
                          sample_tiles_per_seq=sample_seq // tp, group_dim=group_dim),
        grid=(t // tp,),
        in_specs=[
            pl.BlockSpec((tp, d), lambda i: (i, 0)),
            pl.BlockSpec((halo, d), lambda i: (jnp.maximum(i * hb - 1, 0), 0)),
            pl.BlockSpec((halo, d), lambda i: (jnp.minimum((i + 1) * hb, n_halo_blocks - 1), 0)),
            mods.spec(_Mods.SHIFT1, row),
            mods.spec(_Mods.SCALE1, row),
            mods.spec(_Mods.GATE1, row),
            pl.BlockSpec((1, d), lambda i: (0, 0)),
            pl.BlockSpec((n_groups, group_dim, group_dim), lambda i: (0, 0, 0)),
            pl.BlockSpec((1, d), lambda i: (0, 0)),
        ],
        out_specs=pl.BlockSpec((tp, d), lambda i: (i, 0)),
        out_shape=jax.ShapeDtypeStruct((t, d), F32),
        compiler_params=_params(("arbitrary",)),
        name="pool_mixer",
    )(x, x, x, mods.table, mods.table, mods.table, norm_g, pool_w, pool_scale)


def _router_kernel(x_ref, sh_ref, sc_ref, g_ref, wr_ref, br_ref, route_ref, cnt_ref):
    h = _norm_modulate(x_ref[...], g_ref[...], sc_ref[...], sh_ref[...])
    h_hi, h_lo = _split_bf16(h)
    w_hi, w_lo = _split_bf16(wr_ref[...])
    logits = _dot(h_hi, w_hi) + _dot(h_lo, w_hi) + _dot(h_hi, w_lo) + br_ref[...]

    rows = logits.shape[0]
    lane = lax.broadcasted_iota(jnp.int32, logits.shape, 1).astype(F32)
    m1 = jnp.max(logits, axis=-1, keepdims=True)
    i1 = jnp.min(jnp.where(logits == m1, lane, float(LANES)), axis=-1, keepdims=True)
    oh1 = lane == i1
    rest = jnp.where(oh1, -jnp.inf, logits)
    m2 = jnp.max(rest, axis=-1, keepdims=True)
    i2 = jnp.min(jnp.where(rest == m2, lane, float(LANES)), axis=-1, keepdims=True)
    oh2 = lane == i2
    e = jnp.exp(m2 - m1)
    gate_a = 1.0 / (1.0 + e)
    gate_b = e / (1.0 + e)

    sel = jnp.where(oh1 | oh2, 1.0, 0.0)
    r_idx = lax.broadcasted_iota(jnp.int32, (rows, rows), 0)
    c_idx = lax.broadcasted_iota(jnp.int32, (rows, rows), 1)
    earlier = _mask_bf16(c_idx < r_idx)
    rank = _dot(earlier, sel.astype(BF16))
    rank_a = jnp.sum(jnp.where(oh1, rank, 0.0), axis=-1, keepdims=True)
    rank_b = jnp.sum(jnp.where(oh2, rank, 0.0), axis=-1, keepdims=True)

    route = jnp.zeros(logits.shape, F32)
    for k, val in enumerate((i1, i2, rank_a, rank_b, gate_a, gate_b)):
        route = jnp.where(lane == float(k), val, route)
    route_ref[...] = route
    cnt_ref[...] = jnp.sum(sel, axis=0, keepdims=True)


def _moe_router(x, mods, norm_g, w_router, b_router, *, n_prompt, tok_per_cond):
    t, d = x.shape
    n_exp = w_router.shape[1]
    ch = ROUTE_CHUNK
    wr = jnp.zeros((d, LANES), F32).at[:, :n_exp].set(w_router.astype(F32))
    br = jnp.full((1, LANES), -jnp.inf, F32).at[0, :n_exp].set(b_router.astype(F32))
    row = _cond_row_joint(ch, n_prompt, tok_per_cond)
    return pl.pallas_call(
        _router_kernel,
        grid=(t // ch,),
        in_specs=[
            pl.BlockSpec((ch, d), lambda i: (i, 0)),
            mods.spec(_Mods.SHIFT2, row),
            mods.spec(_Mods.SCALE2, row),
            pl.BlockSpec((1, d), lambda i: (0, 0)),
            pl.BlockSpec((d, LANES), lambda i: (0, 0)),
            pl.BlockSpec((1, LANES), lambda i: (0, 0)),
        ],
        out_specs=[
            pl.BlockSpec((ch, LANES), lambda i: (i, 0)),
            pl.BlockSpec((None, 1, LANES), lambda i: (i, 0, 0)),
        ],
        out_shape=[
            jax.ShapeDtypeStruct((t, LANES), F32),
            jax.ShapeDtypeStruct((t // ch, 1, LANES), F32),
        ],
        compiler_params=_params(("arbitrary",)),
        name="moe_router",
    )(x, mods.table, mods.table, norm_g, wr, br)


def _dispatch_plan(route, counts, n_exp):
    ch, gr = ROUTE_CHUNK, GEMM_ROWS
    t = route.shape[0]
    n_chunks = t // ch
    i32 = jnp.int32
    cnt = counts[:, 0, :n_exp].astype(i32)
    total = cnt.sum(0)
    padded = ((total + gr - 1) // gr) * gr
    off = jnp.cumsum(padded) - padded
    start = off[None, :] + jnp.cumsum(cnt, axis=0) - cnt

    def dest(choice):
        e = route[:, choice].astype(i32).reshape(n_chunks, ch, 1)
        hit = e == jnp.arange(n_exp, dtype=i32)
        base = jnp.sum(jnp.where(hit, start[:, None, :], 0), axis=-1)
        return base.reshape(t) + route[:, 2 + choice].astype(i32)

    dest_a, dest_b = dest(0), dest(1)

    n_gemm_tiles = (t * TOP_K) // gr + n_exp

    tile_lo = jnp.arange(n_gemm_tiles, dtype=i32) * gr
    used = tile_lo < (off + padded)[-1]
    ends = off + padded
    expert_of = jnp.minimum(jnp.sum(tile_lo[:, None] >= ends[None, :], axis=1), n_exp - 1).astype(i32)
    n_used = jnp.maximum(used.sum(), 1)
    clamp = jnp.minimum(jnp.arange(n_gemm_tiles, dtype=i32), n_used - 1).astype(i32)
    gemm_list = (clamp, expert_of[clamp], used.astype(i32))

    dest_rows = jnp.stack([dest_a.reshape(n_chunks, ch), dest_b.reshape(n_chunks, ch)], axis=1)
    gates = jnp.concatenate([route[:, 4:6], jnp.zeros((t, 6), F32)], axis=1)
    return dest_rows, gates, gemm_list, n_gemm_tiles * gr


def _wait_rows(src, dst, sem, n_rows):
    pltpu.make_async_copy(src.at[pl.ds(0, n_rows), :], dst.at[pl.ds(0, n_rows), :], sem).wait()


def _scatter_kernel(dest_ref, x_ref, sh_ref, sc_ref, g_ref, init_hbm, out_hbm, h_scr, sem):
    del init_hbm
    ch = x_ref.shape[0]
    step, n_steps = pl.program_id(0), pl.num_programs(0)
    slot = lax.rem(step, 2)
    h_scr[slot] = _norm_modulate(x_ref[...], g_ref[...], sc_ref[...], sh_ref[...])

    def issue(j, carry):
        for k in range(TOP_K):
            pltpu.make_async_copy(h_scr.at[slot, pl.ds(j, 1), :],
                                  out_hbm.at[pl.ds(dest_ref[k, j], 1), :],
                                  sem.at[slot]).start(priority=k)
        return carry

    lax.fori_loop(0, ch, issue, 0, unroll=8)

    def wait_slot(s):
        for _ in range(TOP_K):
            _wait_rows(h_scr.at[s], out_hbm, sem.at[s], ch)

    @pl.when(step > 0)
    def _():
        wait_slot(1 - slot)

    @pl.when(step == n_steps - 1)
    def _():
        wait_slot(slot)


def _moe_scatter(x, mods, norm_g, dest_rows, n_rows, *, n_prompt, tok_per_cond):
    t, d = x.shape
    ch = ROUTE_CHUNK
    row = _cond_row_joint(ch, n_prompt, tok_per_cond)
    return pl.pallas_call(
        _scatter_kernel,
        grid=(t // ch,),
        in_specs=[
            pl.BlockSpec((None, TOP_K, ch), lambda i: (i, 0, 0), memory_space=pltpu.SMEM),
            pl.BlockSpec((ch, d), lambda i: (i, 0)),
            mods.spec(_Mods.SHIFT2, row),
            mods.spec(_Mods.SCALE2, row),
            pl.BlockSpec((1, d), lambda i: (0, 0)),
            pl.BlockSpec(memory_space=pl.ANY),
        ],
        out_specs=pl.BlockSpec(memory_space=pl.ANY),
        out_shape=jax.ShapeDtypeStruct((n_rows, d), F32),
        scratch_shapes=[pltpu.VMEM((2, ch, d), F32), pltpu.SemaphoreType.DMA((2,))],
        input_output_aliases={5: 0},
        compiler_params=_params(("arbitrary",)),
        name="moe_scatter",
    )(dest_rows, x, mods.table, mods.table, norm_g, jnp.zeros((n_rows, d), F32))


def _expert_kernel(blk_ref, exp_ref, used_ref, x_ref, wg_ref, wu_ref, wd_ref, o_ref, x_scr):
    r, f = pl.program_id(0), pl.program_id(1)

    @pl.when(used_ref[r] == 1)
    def _():
        @pl.when(f == 0)
        def _():
            o_ref[...] = jnp.zeros(o_ref.shape, F32)
            x_scr[...] = x_ref[...].astype(BF16)

        x = x_scr[...]
        gte = _dot(x, wg_ref[...])
        up = _dot(x, wu_ref[...])
        act = (gte * jax.nn.sigmoid(gte) * up).astype(BF16)
        o_ref[...] += _dot(act, wd_ref[...])


def _moe_experts(xs, w_gu, w_down, gemm_list):
    n_rows, d = xs.shape
    ff = w_down.shape[1]
    gr = GEMM_ROWS
    tf = _tile(ff, 256, LANES)
    nf = ff // tf
    n_tiles = gemm_list[0].shape[0]

    def fcol(f, used, r):
        return jnp.where(used[r] == 1, f, nf - 1)

    grid_spec = pltpu.PrefetchScalarGridSpec(
        num_scalar_prefetch=3,
        grid=(n_tiles, nf),
        in_specs=[
            pl.BlockSpec((gr, d), lambda r, f, blk, ex, used: (blk[r], 0)),
            pl.BlockSpec((None, d, tf), lambda r, f, blk, ex, used: (ex[r], 0, fcol(f, used, r))),
            pl.BlockSpec((None, d, tf), lambda r, f, blk, ex, used: (ex[r], 0, nf + fcol(f, used, r))),
            pl.BlockSpec((None, tf, d), lambda r, f, blk, ex, used: (ex[r], fcol(f, used, r), 0)),
        ],
        out_specs=pl.BlockSpec((gr, d), lambda r, f, blk, ex, used: (blk[r], 0)),
        scratch_shapes=[pltpu.VMEM((gr, d), BF16)],
    )
    return pl.pallas_call(
        _expert_kernel,
        grid_spec=grid_spec,
        out_shape=jax.ShapeDtypeStruct((n_rows, d), F32),
        input_output_aliases={3: 0},
        compiler_params=_params(("arbitrary", "arbitrary")),
        name="moe_experts",
    )(*gemm_list, xs, w_gu, w_gu, w_down)


def _combine_kernel(dest_ref, dest_next_ref, gates_ref, x_ref, gate_ref, y_hbm, op_ref, os_ref,
                    y_scr, sem, *, n_prompt_chunks):
    ch = x_ref.shape[0]
    step, n_steps = pl.program_id(0), pl.num_programs(0)
    slot = lax.rem(step, 2)

    def gather(idx_ref, s):
        def issue(j, carry):
            for k in range(TOP_K):
                pltpu.make_async_copy(y_hbm.at[pl.ds(idx_ref[k, j], 1), :],
                                      y_scr.at[s, k, pl.ds(j, 1), :], sem.at[s]).start(priority=k)
            return carry
        lax.fori_loop(0, ch, issue, 0, unroll=8)

    @pl.when(step == 0)
    def _():
        gather(dest_ref, slot)

    @pl.when(step + 1 < n_steps)
    def _():
        gather(dest_next_ref, 1 - slot)

    for k in range(TOP_K):
        _wait_rows(y_hbm, y_scr.at[slot, k], sem.at[slot], ch)

    gates = gates_ref[...]
    mix = gates[:, 0:1] * y_scr[slot, 0] + gates[:, 1:2] * y_scr[slot, 1]
    out = x_ref[...] + gate_ref[...] * mix
    is_prompt = step < n_prompt_chunks

    @pl.when(is_prompt)
    def _():
        op_ref[...] = out

    @pl.when(jnp.logical_not(is_prompt))
    def _():
        os_ref[...] = out


def _moe_combine(y, dest_rows, gates, x, mods, *, n_prompt, tok_per_cond):
    t, d = x.shape
    ch = ROUTE_CHUNK
    n_pc = n_prompt // ch
    row = _cond_row_joint(ch, n_prompt, tok_per_cond)
    return pl.pallas_call(
        functools.partial(_combine_kernel, n_prompt_chunks=n_pc),
        grid=(t // ch,),
        in_specs=[
            pl.BlockSpec((None, TOP_K, ch), lambda i: (i, 0, 0), memory_space=pltpu.SMEM),
            pl.BlockSpec((None, TOP_K, ch), lambda i: (jnp.minimum(i + 1, t // ch - 1), 0, 0),
                         memory_space=pltpu.SMEM),
            pl.BlockSpec((ch, gates.shape[1]), lambda i: (i, 0)),
            pl.BlockSpec((ch, d), lambda i: (i, 0)),
            mods.spec(_Mods.GATE2, row),
            pl.BlockSpec(memory_space=pl.ANY),
        ],
        out_specs=[
            pl.BlockSpec((ch, d), lambda i: (jnp.minimum(i, n_pc - 1), 0)),
            pl.BlockSpec((ch, d), lambda i: (jnp.maximum(i - n_pc, 0), 0)),
        ],
        out_shape=[jax.ShapeDtypeStruct((n_prompt, d), F32), jax.ShapeDtypeStruct((t - n_prompt, d), F32)],
        scratch_shapes=[pltpu.VMEM((2, TOP_K, ch, d), F32), pltpu.SemaphoreType.DMA((2,))],
        compiler_params=_params(("arbitrary",)),
        name="moe_combine",
    )(dest_rows, dest_rows, gates, x, mods.table, y)


def kernel(x_prompt, x_sample, c, cache_k, cache_v, c_ctx, ada_w, ada_b, norm1_g, norm2_g,
           attn_w_qkv, attn_w_o, attn_q_norm, attn_k_norm, attn_lambda, attn_subln_g,
           pool_w, pool_scale, ffn_w_gu, ffn_w_down,
           moe_w_router, moe_b_router, moe_w_gu, moe_w_down):
    b_ctx, l_ctx, d = x_prompt.shape
    b_dec, l_dec, _ = x_sample.shape
    depth = ada_w.shape[0]
    n_even, l_past, heads, v_dim = cache_k.shape[1:]
    head_dim = v_dim // 2
    assert head_dim == LANES and depth == 2 and l_dec % GRID_W == 0
    n_exp = moe_w_router.shape[-1]
    tp_, ts_ = b_ctx * l_ctx, b_dec * l_dec
    t_all = tp_ + ts_

    cond_rows = 8 * ((1 + b_dec + 7) // 8)
    cond = jnp.zeros((cond_rows, d), F32).at[0].set(c_ctx).at[1:1 + b_dec].set(c)
    mod_table = _adaln_mods(cond, ada_w, ada_b).reshape(depth * cond_rows * 6, 1, d)

    xp = x_prompt.reshape(tp_, d)
    xs = x_sample.reshape(ts_, d)

    layer, j = 0, 0
    mods = _Mods(mod_table, cond_rows, layer)
    g1 = norm1_g[layer].reshape(1, d)
    w_qkv = attn_w_qkv[j].astype(BF16)
    q_gain = (attn_q_norm[j] * (head_dim ** -0.5 * LOG2E)).reshape(1, head_dim)
    k_gain = attn_k_norm[j].reshape(1, head_dim)
    tables = _rope_tables(l_dec)
    prompt = dict(cond_base=0, tok_per_cond=tp_)
    sample = dict(cond_base=1, tok_per_cond=l_dec)

    q_p, k_p, v_p = _qkv_proj(xp, mods, g1, w_qkv, q_gain, k_gain, None, kv_dtype=F32, **prompt)
    q_s, k_s, v_s = _qkv_proj(xs, mods, g1, w_qkv, q_gain, k_gain, tables, kv_dtype=BF16, **sample)

    lam_params = attn_lambda[j].astype(F32)
    subln = attn_subln_g[j].reshape(1, v_dim)
    o_p, _ = _diff_attention(q_p, k_p, v_p, None, None, lam_params, subln,
                             batch=b_ctx, seq=l_ctx, layer=layer)
    ck = cache_k[:, j].reshape(b_dec * l_past, heads * v_dim)
    cv = cache_v[:, j].reshape(b_dec * l_past, heads * v_dim)
    later_weights = [attn_w_o[j], ffn_w_gu[j], ffn_w_down[j], pool_w[0], moe_w_gu[0], moe_w_down[0]]
    o_s, later_bf16 = _diff_attention(q_s, k_s, v_s, ck, cv, lam_params, subln,
                                      batch=b_dec, seq=l_dec, layer=layer,
                                      casts=[w.reshape(-1, w.shape[-1]) for w in later_weights])
    w_o, w_ffn_gu, w_ffn_down, w_pool, w_moe_gu, w_moe_down = [
        b.reshape(w.shape) for b, w in zip(later_bf16, later_weights)]

    x1 = _out_proj_residual(o_p, o_s, w_o, xp, xs, mods, tok_per_cond=l_dec)

    x2 = _dense_ffn(x1, mods, norm2_g[layer].reshape(1, d), w_ffn_gu, w_ffn_down,
                    n_prompt=tp_, tok_per_cond=l_dec)

    layer, j = 1, 0
    mods = _Mods(mod_table, cond_rows, layer)
    x3 = _pool_mixer(x2, mods, norm1_g[layer].reshape(1, d), w_pool,
                     pool_scale[j].reshape(1, d), n_prompt=tp_, prompt_seq=l_ctx, sample_seq=l_dec)

    g2 = norm2_g[layer].reshape(1, d)
    joint = dict(n_prompt=tp_, tok_per_cond=l_dec)
    route, counts = _moe_router(x3, mods, g2, moe_w_router[j], moe_b_router[j], **joint)
    dest_rows, gates, gemm_list, n_rows = _dispatch_plan(route, counts, n_exp)
    rows_sorted = _moe_scatter(x3, mods, g2, dest_rows, n_rows, **joint)
    y_sorted = _moe_experts(rows_sorted, w_moe_gu, w_moe_down, gemm_list)
    y_p, y_s = _moe_combine(y_sorted, dest_rows, gates, x3, mods, **joint)

    y_prompt = y_p.reshape(b_ctx, l_ctx, d)
    y_sample = y_s.reshape(b_dec, l_dec, d)
    state_k = k_p.reshape(b_ctx, 1, l_ctx, heads, v_dim)
    state_v = v_p.reshape(b_ctx, 1, l_ctx, heads, v_dim)
    return (y_prompt, y_sample, state_k, state_v)
```

```python
import functools
import math

import jax
import jax.numpy as jnp
from jax import lax
from jax.experimental import pallas as pl
from jax.experimental.pallas import tpu as pltpu

F32 = jnp.float32
BF16 = jnp.bfloat16

GRID_W = 64
ROPE_BASE = 10000.0
POOL_WINDOWS = (2, 4, 8, 16)
TOP_K = 2
EPS = 1e-6
LOG2E = 1.4426950408889634

LANES = 128
VMEM_LIMIT = 52 * 2**20

ROUTE_CHUNK = 256
GEMM_ROWS = 1024


def _lambda_init(layer):
    return 0.8 - 0.6 * math.exp(-0.3 * layer)


def _tile(n, pref, mult):
    best = None
    t = mult
    while t <= min(n, pref):
        if n % t == 0:
            best = t
        t += mult
    if best is None:
        raise ValueError(f"no tile for {n} (multiple of {mult}, <= {pref})")
    return best


def _params(semantics):
    return pltpu.CompilerParams(dimension_semantics=semantics, vmem_limit_bytes=VMEM_LIMIT)


def _dot(a, b):
    return jnp.dot(a, b, preferred_element_type=F32)


def _mask_bf16(m):
    return jnp.where(m, 1.0, 0.0).astype(BF16)


def _split_bf16(x):
    hi = x.astype(BF16)
    lo = (x - hi.astype(F32)).astype(BF16)
    return hi, lo


def _norm_modulate(x, g, scale, shift):
    ms = jnp.mean(x * x, axis=-1, keepdims=True)
    return x * lax.rsqrt(ms + EPS) * (g * (1.0 + scale)) + shift


def _adaln_kernel(cond_ref, w_ref, b_ref, o_ref):
    c = cond_ref[...]
    s = c * jax.nn.sigmoid(c)
    s_hi, s_lo = _split_bf16(s)
    w_hi, w_lo = _split_bf16(w_ref[...])
    o_ref[...] = _dot(s_hi, w_hi) + _dot(s_lo, w_hi) + _dot(s_hi, w_lo) + b_ref[...]


def _adaln_mods(cond, ada_w, ada_b):
    depth, d, n = ada_w.shape
    rows = cond.shape[0]
    tn = _tile(n, 512, LANES)
    return pl.pallas_call(
        _adaln_kernel,
        grid=(depth, n // tn),
        in_specs=[
            pl.BlockSpec((rows, d), lambda l, j: (0, 0)),
            pl.BlockSpec((None, d, tn), lambda l, j: (l, 0, j)),
            pl.BlockSpec((None, 1, tn), lambda l, j: (l, 0, j)),
        ],
        out_specs=pl.BlockSpec((None, rows, tn), lambda l, j: (l, 0, j)),
        out_shape=jax.ShapeDtypeStruct((depth, rows, n), F32),
        compiler_params=_params(("arbitrary", "arbitrary")),
        name="adaln_mods",
    )(cond, ada_w, ada_b.reshape(depth, 1, n))


class _Mods:
    SHIFT1, SCALE1, GATE1, SHIFT2, SCALE2, GATE2 = range(6)

    def __init__(self, table, rows, layer):
        self.table = table
        self.rows = rows
        self.layer = layer

    def spec(self, which, row_fn, width=None, col_fn=None):
        d = self.table.shape[-1]
        width = d if width is None else width
        base = self.layer * self.rows

        def index(*ids):
            col = 0 if col_fn is None else col_fn(*ids)
            return ((base + row_fn(*ids)) * 6 + which, 0, col)

        return pl.BlockSpec((None, 1, width), index)


def _qkv_kernel(*refs, rope, tn, n_sec):
    x_ref, sh_ref, sc_ref, g_ref, w_ref, qg_ref, kg_ref = refs[:7]
    rest = list(refs[7:])
    cos_ref, sin_ref = (rest.pop(0), rest.pop(0)) if rope else (None, None)
    q_ref, k_ref, v_ref, h_scr = rest
    j = pl.program_id(1)

    @pl.when(j == 0)
    def _():
        h = _norm_modulate(x_ref[...], g_ref[...], sc_ref[...], sh_ref[...])
        h_scr[...] = h.astype(BF16)

    def qk_epilogue(gain_ref, out_ref):
        acc = _dot(h_scr[...], w_ref[...])
        r_idx = lax.broadcasted_iota(jnp.int32, (2 * LANES, 2 * LANES), 0)
        c_idx = lax.broadcasted_iota(jnp.int32, (2 * LANES, 2 * LANES), 1)
        if rope:
            swap = (r_idx >= LANES) & (c_idx >= LANES) & (r_idx - LANES == jnp.bitwise_xor(c_idx - LANES, LANES // 4))
            rhs = _mask_bf16(((r_idx < LANES) & (c_idx < LANES)) | swap)
        else:
            rhs = jnp.ones((2 * LANES, LANES), BF16)
        for g in range(tn // LANES):
            y = acc[:, g * LANES:(g + 1) * LANES]
            yg = y * gain_ref[...]
            if rope:
                res = _dot(jnp.concatenate([(y * y).astype(BF16), yg.astype(BF16)], axis=1), rhs)
                inv = lax.rsqrt(res[:, :LANES] * (1.0 / LANES) + EPS)
                out = (yg * cos_ref[...] + res[:, LANES:] * sin_ref[...]) * inv
            else:
                ss = _dot(jnp.concatenate(_split_bf16(y * y), axis=1), rhs)
                out = yg * lax.rsqrt(ss * (1.0 / LANES) + EPS)
            out_ref[:, g * LANES:(g + 1) * LANES] = out.astype(out_ref.dtype)

    @pl.when(j < n_sec)
    def _():
        qk_epilogue(qg_ref, q_ref)

    @pl.when((j >= n_sec) & (j < 2 * n_sec))
    def _():
        qk_epilogue(kg_ref, k_ref)

    @pl.when(j >= 2 * n_sec)
    def _():
        v_ref[...] = _dot(h_scr[...], w_ref[...]).astype(v_ref.dtype)


def _qkv_proj(x, mods, norm_g, w, q_gain, k_gain, tables, *, cond_base, tok_per_cond, kv_dtype):
    t, d = x.shape
    tm = _tile(math.gcd(t, tok_per_cond), 1024, 8)
    tn = _tile(d, 512, LANES)
    n_sec = d // tn
    rope = tables is not None
    row = lambda i, j: cond_base + (i * tm) // tok_per_cond
    in_specs = [
        pl.BlockSpec((tm, d), lambda i, j: (i, 0)),
        mods.spec(_Mods.SHIFT1, row),
        mods.spec(_Mods.SCALE1, row),
        pl.BlockSpec((1, d), lambda i, j: (0, 0)),
        pl.BlockSpec((d, tn), lambda i, j: (0, j)),
        pl.BlockSpec((1, LANES), lambda i, j: (0, 0)),
        pl.BlockSpec((1, LANES), lambda i, j: (0, 0)),
    ]
    args = [x, mods.table, mods.table, norm_g, w, q_gain, k_gain]
    if rope:
        nblk = tables[0].shape[0] // tm
        for tab in tables:
            in_specs.append(pl.BlockSpec((tm, LANES), lambda i, j: (i % nblk, 0)))
            args.append(tab)
    out_specs = [
        pl.BlockSpec((tm, tn), lambda i, j, s=s: (i, jnp.clip(j - s * n_sec, 0, n_sec - 1)))
        for s in range(3)
    ]
    return pl.pallas_call(
        functools.partial(_qkv_kernel, rope=rope, tn=tn, n_sec=n_sec),
        grid=(t // tm, 3 * n_sec),
        in_specs=in_specs,
        out_specs=out_specs,
        out_shape=[jax.ShapeDtypeStruct((t, d), BF16), jax.ShapeDtypeStruct((t, d), kv_dtype),
                   jax.ShapeDtypeStruct((t, d), kv_dtype)],
        scratch_shapes=[pltpu.VMEM((tm, d), BF16)],
        compiler_params=_params(("arbitrary", "arbitrary")),
        name="qkv_proj_rope" if rope else "qkv_proj",
    )(*args)


def _rope_tables(seq):
    n_freq = LANES // 4
    pos = jnp.arange(seq, dtype=jnp.int32)
    r = (pos // GRID_W).astype(F32)
    col = (pos % GRID_W).astype(F32)
    freqs = ROPE_BASE ** (-jnp.arange(n_freq, dtype=F32) / n_freq)
    cos_t, sin_t = [], []
    for p in (r, col):
        ang = p[:, None] * freqs
        cos, sin = jnp.cos(ang), jnp.sin(ang)
        cos_t += [cos, cos]
        sin_t += [-sin, sin]
    return jnp.concatenate(cos_t, axis=1), jnp.concatenate(sin_t, axis=1)


def _attn_kernel(*refs, n_cache, n_chunks, tk, lambda_init, cast_blocks, heads_per_step):
    n_cast = len(cast_blocks)
    lam_ref, g_ref, q_ref = refs[:3]
    n_in = 3 + (4 if n_cache else 2)
    if n_cache:
        kc_ref, vtc_ref, k_ref, vt_ref = refs[3:n_in]
    else:
        k_ref, vt_ref = refs[3:n_in]
    cast_in = refs[n_in:n_in + n_cast]
    o_ref = refs[n_in + n_cast]
    cast_out = refs[n_in + n_cast + 1:n_in + 2 * n_cast + 1]
    acc_ref = refs[-1]

    step = (pl.program_id(0) * pl.num_programs(1) + pl.program_id(1)) * pl.num_programs(2) + pl.program_id(2)
    for src, dst, n_blocks in zip(cast_in, cast_out, cast_blocks):
        @pl.when(step < n_blocks)
        def _(src=src, dst=dst):
            dst[...] = src[...].astype(dst.dtype)

    vd = 2 * LANES
    for hq in range(heads_per_step):
        cols = slice(hq * vd, (hq + 1) * vd)
        _attn_head(lam_ref, g_ref, q_ref.at[:, cols],
                   kc_ref.at[:, cols] if n_cache else None, vtc_ref.at[hq] if n_cache else None,
                   k_ref.at[:, cols], vt_ref.at[hq], o_ref.at[:, cols], acc_ref,
                   n_chunks=n_chunks, tk=tk, lambda_init=lambda_init)


def _attn_head(lam_ref, g_ref, q_ref, kc_ref, vtc_ref, k_ref, vt_ref, o_ref, acc_ref, *,
               n_chunks, tk, lambda_init):
    q = q_ref[...]
    qs = (q[:, :LANES], q[:, LANES:])

    def scores(kblk):
        return [lax.dot_general(kblk[:, c * LANES:(c + 1) * LANES], qs[c],
                                (((1,), (1,)), ((), ())), preferred_element_type=F32)
                for c in range(2)]

    def accumulate(sts, vt, stats):
        out = []
        for c in range(2):
            st = sts[c]
            m_cur = jnp.max(st, axis=0, keepdims=True)
            if stats is None:
                m_new = m_cur
                p = jnp.exp2(st - m_new)
                l_new = jnp.sum(p, axis=0, keepdims=True)
                acc_ref[c] = _dot(vt, p.astype(BF16))
            else:
                m_prev, l_prev = stats[c]
                m_new = jnp.maximum(m_prev, m_cur)
                alpha = jnp.exp2(m_prev - m_new)
                p = jnp.exp2(st - m_new)
                l_new = alpha * l_prev + jnp.sum(p, axis=0, keepdims=True)
                acc_ref[c] = alpha * acc_ref[c] + _dot(vt, p.astype(BF16))
            out.append((m_new, l_new))
        return out

    chunks = []
    if kc_ref is not None:
        chunks.append((lambda: kc_ref[...].astype(BF16), lambda: vtc_ref[...]))
    for i in range(n_chunks):
        chunks.append((lambda i=i: k_ref[i * tk:(i + 1) * tk, :].astype(BF16), lambda i=i: vt_ref[i]))
    stats = None
    sts = scores(chunks[0][0]())
    for n, (_, values) in enumerate(chunks):
        nxt = scores(chunks[n + 1][0]()) if n + 1 < len(chunks) else None
        stats = accumulate(sts, values(), stats)
        sts = nxt

    lp = lam_ref[...]
    lam = (jnp.exp(jnp.sum(lp[0:1] * lp[1:2], axis=-1, keepdims=True))
           - jnp.exp(jnp.sum(lp[2:3] * lp[3:4], axis=-1, keepdims=True)) + lambda_init)
    o_t = acc_ref[0] / stats[0][1] - lam * (acc_ref[1] / stats[1][1])
    o = o_t.T
    o = o * lax.rsqrt(jnp.mean(o * o, axis=-1, keepdims=True) + EPS) * g_ref[...]
    o_ref[...] = (o * (1.0 - lambda_init)).astype(o_ref.dtype)


def _chunked_transpose(v, batch, seq, heads, vd, tk):
    v = v.astype(BF16).reshape(batch, seq // tk, tk, heads, vd)
    return v.transpose(0, 3, 1, 4, 2).reshape(batch * heads, seq // tk, vd, tk)


def _cast_block_rows(rows, n_steps):
    blk = 16
    while rows % blk or rows // blk > n_steps:
        blk += 16
        if blk > rows:
            raise ValueError(f"no cast block for {rows} rows in {n_steps} steps")
    return blk


def _diff_attention(q, k, v, cache_k, cache_v, lam_params, subln_g, *, batch, seq, layer, casts=()):
    t, d = q.shape
    vd = subln_g.shape[-1]
    heads = d // vd
    n_cache = 0 if cache_k is None else cache_k.shape[0] // batch
    tq = _tile(seq, 512, LANES)
    tk = _tile(seq, 512, LANES)
    nq = seq // tq
    n_chunks = seq // tk
    hps = heads if seq <= tq else 1
    groups = heads // hps
    wide = hps * vd
    in_specs = [
        pl.BlockSpec(lam_params.shape, lambda b, h, i: (0, 0)),
        pl.BlockSpec((1, vd), lambda b, h, i: (0, 0)),
        pl.BlockSpec((tq, wide), lambda b, h, i: (b * nq + i, h)),
    ]
    args = [lam_params, subln_g, q]
    if n_cache:
        in_specs += [pl.BlockSpec((n_cache, wide), lambda b, h, i: (b, h)),
                     pl.BlockSpec((hps, None, vd, n_cache), lambda b, h, i: (b * groups + h, 0, 0, 0))]
        args += [cache_k, _chunked_transpose(cache_v, batch, n_cache, heads, vd, n_cache)]
    in_specs += [pl.BlockSpec((seq, wide), lambda b, h, i: (b, h)),
                 pl.BlockSpec((hps, n_chunks, vd, tk), lambda b, h, i: (b * groups + h, 0, 0, 0))]
    args += [k, _chunked_transpose(v, batch, seq, heads, vd, tk)]
    out_specs = [pl.BlockSpec((tq, wide), lambda b, h, i: (b * nq + i, h))]
    out_shape = [jax.ShapeDtypeStruct((t, d), BF16)]
    n_steps = batch * groups * nq
    cast_blocks = []
    for w in casts:
        rows, cols = w.shape
        blk = _cast_block_rows(rows, n_steps)
        n_blocks = rows // blk
        cast_blocks.append(n_blocks)
        spec = pl.BlockSpec((blk, cols), lambda b, h, i, n=n_blocks: (
            jnp.minimum((b * groups + h) * nq + i, n - 1), 0))
        in_specs.append(spec)
        args.append(w)
        out_specs.append(spec)
        out_shape.append(jax.ShapeDtypeStruct((rows, cols), BF16))
    res = pl.pallas_call(
        functools.partial(_attn_kernel, n_cache=n_cache, n_chunks=n_chunks, tk=tk,
                          lambda_init=_lambda_init(layer), cast_blocks=tuple(cast_blocks),
                          heads_per_step=hps),
        grid=(batch, groups, nq),
        in_specs=in_specs,
        out_specs=out_specs,
        out_shape=out_shape,
        scratch_shapes=[pltpu.VMEM((2, vd, tq), F32)],
        compiler_params=_params(("arbitrary", "arbitrary", "arbitrary")),
        name=f"diff_attention_{'latent' if n_cache else 'context'}",
    )(*args)
    return res[0], list(res[1:])


def _out_proj_kernel(op_ref, os_ref, w_ref, xp_ref, xs_ref, gate_ref, out_ref, *, n_prompt_tiles):
    is_prompt = pl.program_id(0) < n_prompt_tiles

    @pl.when(is_prompt)
    def _():
        out_ref[...] = xp_ref[...] + gate_ref[...] * _dot(op_ref[...], w_ref[...])

    @pl.when(jnp.logical_not(is_prompt))
    def _():
        out_ref[...] = xs_ref[...] + gate_ref[...] * _dot(os_ref[...], w_ref[...])


def _out_proj_residual(o_p, o_s, w, x_p, x_s, mods, *, tok_per_cond):
    n_prompt, d = x_p.shape
    t = n_prompt + x_s.shape[0]
    tm = _tile(math.gcd(n_prompt, tok_per_cond), 1024, 8)
    tn = _tile(d, 512, LANES)
    n_pt = n_prompt // tm
    row = _cond_row_joint(tm, n_prompt, tok_per_cond)
    p_rows = lambda i: jnp.minimum(i, n_pt - 1)
    s_rows = lambda i: jnp.maximum(i - n_pt, 0)
    return pl.pallas_call(
        functools.partial(_out_proj_kernel, n_prompt_tiles=n_pt),
        grid=(t // tm, d // tn),
        in_specs=[
            pl.BlockSpec((tm, d), lambda i, j: (p_rows(i), 0)),
            pl.BlockSpec((tm, d), lambda i, j: (s_rows(i), 0)),
            pl.BlockSpec((d, tn), lambda i, j: (0, j)),
            pl.BlockSpec((tm, tn), lambda i, j: (p_rows(i), jnp.where(i < n_pt, j, d // tn - 1))),
            pl.BlockSpec((tm, tn), lambda i, j: (s_rows(i), jnp.where(i < n_pt, 0, j))),
            mods.spec(_Mods.GATE1, row, width=tn, col_fn=lambda i, j: j),
        ],
        out_specs=pl.BlockSpec((tm, tn), lambda i, j: (i, j)),
        out_shape=jax.ShapeDtypeStruct((t, d), F32),
        compiler_params=_params(("arbitrary", "arbitrary")),
        name="attn_out_proj",
    )(o_p, o_s, w, x_p, x_s, mods.table)


def _ffn_kernel(x_ref, sh_ref, sc_ref, gate_ref, g_ref, wg_ref, wu_ref, wd_ref, o_ref, h_scr):
    f = pl.program_id(1)

    @pl.when(f == 0)
    def _():
        h = _norm_modulate(x_ref[...], g_ref[...], sc_ref[...], sh_ref[...])
        h_scr[...] = h.astype(BF16)
        o_ref[...] = jnp.zeros(o_ref.shape, F32)

    h = h_scr[...]
    gte = _dot(h, wg_ref[...])
    up = _dot(h, wu_ref[...])
    act = (gte * jax.nn.sigmoid(gte) * up).astype(BF16)
    o_ref[...] += _dot(act, wd_ref[...])

    @pl.when(f == pl.num_programs(1) - 1)
    def _():
        o_ref[...] = x_ref[...] + gate_ref[...] * o_ref[...]


def _cond_row_joint(tm, n_prompt, tok_per_cond):
    def row(i, *_):
        tok = i * tm
        return jnp.where(tok < n_prompt, 0, 1 + (tok - n_prompt) // tok_per_cond)
    return row


def _dense_ffn(x, mods, norm_g, w_gu, w_down, *, n_prompt, tok_per_cond):
    t, d = x.shape
    ff = w_down.shape[0]
    tm = _tile(math.gcd(n_prompt, tok_per_cond), 1024, 8)
    tf = _tile(ff, 256, LANES)
    nf = ff // tf
    row = _cond_row_joint(tm, n_prompt, tok_per_cond)
    return pl.pallas_call(
        _ffn_kernel,
        grid=(t // tm, nf),
        in_specs=[
            pl.BlockSpec((tm, d), lambda i, f: (i, 0)),
            mods.spec(_Mods.SHIFT2, row),
            mods.spec(_Mods.SCALE2, row),
            mods.spec(_Mods.GATE2, row),
            pl.BlockSpec((1, d), lambda i, f: (0, 0)),
            pl.BlockSpec((d, tf), lambda i, f: (0, f)),
            pl.BlockSpec((d, tf), lambda i, f: (0, nf + f)),
            pl.BlockSpec((tf, d), lambda i, f: (f, 0)),
        ],
        out_specs=pl.BlockSpec((tm, d), lambda i, f: (i, 0)),
        out_shape=jax.ShapeDtypeStruct((t, d), F32),
        scratch_shapes=[pltpu.VMEM((tm, d), BF16)],
        compiler_params=_params(("arbitrary", "arbitrary")),
        name="dense_ffn",
    )(x, mods.table, mods.table, mods.table, norm_g, w_gu, w_gu, w_down)


def _pool_kernel(x_ref, xp_ref, xn_ref, sh_ref, sc_ref, gate_ref, g_ref, w_ref, ps_ref, o_ref, *,
                 tp, halo, n_prompt_tiles, prompt_tiles_per_seq, sample_tiles_per_seq, group_dim):
    i = pl.program_id(0)
    is_prompt = i < n_prompt_tiles
    tiles_per_seq = jnp.where(is_prompt, prompt_tiles_per_seq, sample_tiles_per_seq)
    local = lax.rem(jnp.where(is_prompt, i, i - n_prompt_tiles), tiles_per_seq)
    first = local == 0
    last = local == tiles_per_seq - 1
    seq_len = tiles_per_seq * tp

    g, sc, sh = g_ref[...], sc_ref[...], sh_ref[...]
    x = x_ref[...]
    h = _norm_modulate(x, g, sc, sh)
    h_prev = jnp.where(first, 0.0, _norm_modulate(xp_ref[...], g, sc, sh))
    h_next = jnp.where(last, 0.0, _norm_modulate(xn_ref[...], g, sc, sh))
    d = h.shape[-1]
    h_halo = jnp.concatenate([h_prev, h_next, jnp.zeros((LANES - 2 * halo, d), F32)], axis=0)

    t_idx = lax.broadcasted_iota(jnp.int32, (tp, tp), 0)
    s_idx = lax.broadcasted_iota(jnp.int32, (tp, tp), 1)
    t_h = lax.broadcasted_iota(jnp.int32, (tp, LANES), 0)
    u_h = lax.broadcasted_iota(jnp.int32, (tp, LANES), 1)
    off_h = jnp.where(u_h < halo, u_h - halo, tp + u_h - halo)
    pos = local * tp + lax.broadcasted_iota(jnp.int32, (tp, 1), 0)

    for grp, win in enumerate(POOL_WINDOWS):
        lo, hi = win // 2, win // 2 - 1
        cols = slice(grp * group_dim, (grp + 1) * group_dim)
        band = _mask_bf16((s_idx >= t_idx - lo) & (s_idx <= t_idx + hi))
        band_h = _mask_bf16((u_h < 2 * halo) & (off_h >= t_h - lo) & (off_h <= t_h + hi))
        total = _dot(band, h[:, cols].astype(BF16)) + _dot(band_h, h_halo[:, cols].astype(BF16))
        cnt = (jnp.minimum(pos + hi, seq_len - 1) - jnp.maximum(pos - lo, 0) + 1).astype(F32)
        pooled = (total / cnt - h[:, cols]).astype(BF16)
        y = _dot(pooled, w_ref[grp]) * ps_ref[:, cols]
        o_ref[:, cols] = x[:, cols] + gate_ref[:, cols] * y


def _pool_mixer(x, mods, norm_g, pool_w, pool_scale, *, n_prompt, prompt_seq, sample_seq):
    t, d = x.shape
    halo = max(POOL_WINDOWS) // 2
    tp = _tile(math.gcd(prompt_seq, sample_seq), 256, 8)
    n_groups, group_dim = pool_w.shape[0], pool_w.shape[1]
    hb = tp // halo
    n_halo_blocks = t // halo
    row = _cond_row_joint(tp, n_prompt, sample_seq)
    return pl.pallas_call(
        functools.partial(_pool_kernel, tp=tp, halo=halo, n_prompt_tiles=n_prompt // tp,
                          prompt_tiles_per_seq=prompt_seq // tp,
                          sample_tiles_per_seq=sample_seq // tp, group_dim=group_dim),
        grid=(t // tp,),
        in_specs=[
            pl.BlockSpec((tp, d), lambda i: (i, 0)),
            pl.BlockSpec((halo, d), lambda i: (jnp.maximum(i * hb - 1, 0), 0)),
            pl.BlockSpec((halo, d), lambda i: (jnp.minimum((i + 1) * hb, n_halo_blocks - 1), 0)),
            mods.spec(_Mods.SHIFT1, row),
            mods.spec(_Mods.SCALE1, row),
            mods.spec(_Mods.GATE1, row),
            pl.BlockSpec((1, d), lambda i: (0, 0)),
            pl.BlockSpec((n_groups, group_dim, group_dim), lambda i: (0, 0, 0)),
            pl.BlockSpec((1, d), lambda i: (0, 0)),
        ],
        out_specs=pl.BlockSpec((tp, d), lambda i: (i, 0)),
        out_shape=jax.ShapeDtypeStruct((t, d), F32),
        compiler_params=_params(("arbitrary",)),
        name="pool_mixer",
    )(x, x, x, mods.table, mods.table, mods.table, norm_g, pool_w, pool_scale)


def _router_kernel(x_ref, sh_ref, sc_ref, g_ref, wr_ref, br_ref, route_ref, cnt_ref):
    h = _norm_modulate(x_ref[...], g_ref[...], sc_ref[...], sh_ref[...])
    h_hi, h_lo = _split_bf16(h)
    w_hi, w_lo = _split_bf16(wr_ref[...])
    logits = _dot(h_hi, w_hi) + _dot(h_lo, w_hi) + _dot(h_hi, w_lo) + br_ref[...]

    rows = logits.shape[0]
    lane = lax.broadcasted_iota(jnp.int32, logits.shape, 1).astype(F32)
    m1 = jnp.max(logits, axis=-1, keepdims=True)
    i1 = jnp.min(jnp.where(logits == m1, lane, float(LANES)), axis=-1, keepdims=True)
    oh1 = lane == i1
    rest = jnp.where(oh1, -jnp.inf, logits)
    m2 = jnp.max(rest, axis=-1, keepdims=True)
    i2 = jnp.min(jnp.where(rest == m2, lane, float(LANES)), axis=-1, keepdims=True)
    oh2 = lane == i2
    e = jnp.exp(m2 - m1)
    gate_a = 1.0 / (1.0 + e)
    gate_b = e / (1.0 + e)

    sel = jnp.where(oh1 | oh2, 1.0, 0.0)
    r_idx = lax.broadcasted_iota(jnp.int32, (rows, rows), 0)
    c_idx = lax.broadcasted_iota(jnp.int32, (rows, rows), 1)
    earlier = _mask_bf16(c_idx < r_idx)
    rank = _dot(earlier, sel.astype(BF16))
    rank_a = jnp.sum(jnp.where(oh1, rank, 0.0), axis=-1, keepdims=True)
    rank_b = jnp.sum(jnp.where(oh2, rank, 0.0), axis=-1, keepdims=True)

    route = jnp.zeros(logits.shape, F32)
    for k, val in enumerate((i1, i2, rank_a, rank_b, gate_a, gate_b)):
        route = jnp.where(lane == float(k), val, route)
    route_ref[...] = route
    cnt_ref[...] = jnp.sum(sel, axis=0, keepdims=True)


def _moe_router(x, mods, norm_g, w_router, b_router, *, n_prompt, tok_per_cond):
    t, d = x.shape
    n_exp = w_router.shape[1]
    ch = ROUTE_CHUNK
    wr = jnp.zeros((d, LANES), F32).at[:, :n_exp].set(w_router.astype(F32))
    br = jnp.full((1, LANES), -jnp.inf, F32).at[0, :n_exp].set(b_router.astype(F32))
    row = _cond_row_joint(ch, n_prompt, tok_per_cond)
    return pl.pallas_call(
        _router_kernel,
        grid=(t // ch,),
        in_specs=[
            pl.BlockSpec((ch, d), lambda i: (i, 0)),
            mods.spec(_Mods.SHIFT2, row),
            mods.spec(_Mods.SCALE2, row),
            pl.BlockSpec((1, d), lambda i: (0, 0)),
            pl.BlockSpec((d, LANES), lambda i: (0, 0)),
            pl.BlockSpec((1, LANES), lambda i: (0, 0)),
        ],
        out_specs=[
            pl.BlockSpec((ch, LANES), lambda i: (i, 0)),
            pl.BlockSpec((None, 1, LANES), lambda i: (i, 0, 0)),
        ],
        out_shape=[
            jax.ShapeDtypeStruct((t, LANES), F32),
            jax.ShapeDtypeStruct((t // ch, 1, LANES), F32),
        ],
        compiler_params=_params(("arbitrary",)),
        name="moe_router",
    )(x, mods.table, mods.table, norm_g, wr, br)


def _dispatch_plan(route, counts, n_exp):
    ch, gr = ROUTE_CHUNK, GEMM_ROWS
    t = route.shape[0]
    n_chunks = t // ch
    i32 = jnp.int32
    cnt = counts[:, 0, :n_exp].astype(i32)
    total = cnt.sum(0)
    padded = ((total + gr - 1) // gr) * gr
    off = jnp.cumsum(padded) - padded
    start = off[None, :] + jnp.cumsum(cnt, axis=0) - cnt

    def dest(choice):
        e = route[:, choice].astype(i32).reshape(n_chunks, ch, 1)
        hit = e == jnp.arange(n_exp, dtype=i32)
        base = jnp.sum(jnp.where(hit, start[:, None, :], 0), axis=-1)
        return base.reshape(t) + route[:, 2 + choice].astype(i32)

    dest_a, dest_b = dest(0), dest(1)

    n_gemm_tiles = (t * TOP_K) // gr + n_exp

    tile_lo = jnp.arange(n_gemm_tiles, dtype=i32) * gr
    used = tile_lo < (off + padded)[-1]
    ends = off + padded
    expert_of = jnp.minimum(jnp.sum(tile_lo[:, None] >= ends[None, :], axis=1), n_exp - 1).astype(i32)
    n_used = jnp.maximum(used.sum(), 1)
    clamp = jnp.minimum(jnp.arange(n_gemm_tiles, dtype=i32), n_used - 1).astype(i32)
    gemm_list = (clamp, expert_of[clamp], used.astype(i32))

    tail = jnp.arange(n_gemm_tiles - n_exp, n_gemm_tiles, dtype=i32)
    zero_rows = jnp.concatenate([jnp.maximum(ends - gr, 0), tail * gr]).astype(i32)
    zero_ok = jnp.concatenate([padded > 0, tail >= used.sum()]).astype(i32)

    dest_rows = jnp.stack([dest_a.reshape(n_chunks, ch), dest_b.reshape(n_chunks, ch)], axis=1)
    gates = jnp.concatenate([route[:, 4:6], jnp.zeros((t, 6), F32)], axis=1)
    return dest_rows, gates, gemm_list, (zero_rows, zero_ok), n_gemm_tiles * gr


def _wait_rows(src, dst, sem, n_rows):
    pltpu.make_async_copy(src.at[pl.ds(0, n_rows), :], dst.at[pl.ds(0, n_rows), :], sem).wait()


def _scatter_kernel(zrow_ref, zok_ref, dest_ref, x_ref, sh_ref, sc_ref, g_ref, out_hbm,
                    h_scr, z_scr, sem, zsem, *, tile_rows):
    ch = x_ref.shape[0]
    step, n_steps = pl.program_id(0), pl.num_programs(0)
    slot = lax.rem(step, 2)

    @pl.when(step == 0)
    def _():
        z_scr[...] = jnp.zeros(z_scr.shape, F32)
        zr = z_scr.shape[0]

        def zero_copy(n, part):
            row0 = pl.multiple_of(zrow_ref[n] + part * zr, zr)
            return pltpu.make_async_copy(z_scr, out_hbm.at[pl.ds(row0, zr), :], zsem)

        for n in range(zrow_ref.shape[0]):
            @pl.when(zok_ref[n] == 1)
            def _(n=n):
                for part in range(tile_rows // zr):
                    zero_copy(n, part).start()
        for n in range(zrow_ref.shape[0]):
            @pl.when(zok_ref[n] == 1)
            def _(n=n):
                for part in range(tile_rows // zr):
                    zero_copy(n, part).wait()

    h_scr[slot] = _norm_modulate(x_ref[...], g_ref[...], sc_ref[...], sh_ref[...])

    src_rows = h_scr.at[slot]
    for j in range(ch):
        for k in range(TOP_K):
            pltpu.make_async_copy(src_rows.at[pl.ds(j, 1), :],
                                  out_hbm.at[pl.ds(dest_ref[k, j], 1), :],
                                  sem.at[slot]).start(priority=k)

    def wait_slot(s):
        for _ in range(TOP_K):
            _wait_rows(h_scr.at[s], out_hbm, sem.at[s], ch)

    @pl.when(step > 0)
    def _():
        wait_slot(1 - slot)

    @pl.when(step == n_steps - 1)
    def _():
        wait_slot(slot)


def _moe_scatter(x, mods, norm_g, dest_rows, zero_tiles, n_rows, *, n_prompt, tok_per_cond):
    t, d = x.shape
    ch = ROUTE_CHUNK
    row = _cond_row_joint(ch, n_prompt, tok_per_cond)
    grid_spec = pltpu.PrefetchScalarGridSpec(
        num_scalar_prefetch=2,
        grid=(t // ch,),
        in_specs=[
            pl.BlockSpec((None, TOP_K, ch), lambda i, *_: (i, 0, 0), memory_space=pltpu.SMEM),
            pl.BlockSpec((ch, d), lambda i, *_: (i, 0)),
            mods.spec(_Mods.SHIFT2, row),
            mods.spec(_Mods.SCALE2, row),
            pl.BlockSpec((1, d), lambda i, *_: (0, 0)),
        ],
        out_specs=pl.BlockSpec(memory_space=pl.ANY),
        scratch_shapes=[pltpu.VMEM((2, ch, d), F32), pltpu.VMEM((ch, d), F32),
                        pltpu.SemaphoreType.DMA((2,)), pltpu.SemaphoreType.DMA(())],
    )
    return pl.pallas_call(
        functools.partial(_scatter_kernel, tile_rows=GEMM_ROWS),
        grid_spec=grid_spec,
        out_shape=jax.ShapeDtypeStruct((n_rows, d), F32),
        compiler_params=_params(("arbitrary",)),
        name="moe_scatter",
    )(*zero_tiles, dest_rows, x, mods.table, mods.table, norm_g)


def _expert_kernel(blk_ref, exp_ref, used_ref, x_ref, wg_ref, wu_ref, wd_ref, o_ref, x_scr):
    r, f = pl.program_id(0), pl.program_id(1)

    @pl.when(used_ref[r] == 1)
    def _():
        @pl.when(f == 0)
        def _():
            o_ref[...] = jnp.zeros(o_ref.shape, F32)
            x_scr[...] = x_ref[...].astype(BF16)

        x = x_scr[...]
        gte = _dot(x, wg_ref[...])
        up = _dot(x, wu_ref[...])
        act = (gte * jax.nn.sigmoid(gte) * up).astype(BF16)
        o_ref[...] += _dot(act, wd_ref[...])


def _moe_experts(xs, w_gu, w_down, gemm_list):
    n_rows, d = xs.shape
    ff = w_down.shape[1]
    gr = GEMM_ROWS
    tf = _tile(ff, 256, LANES)
    nf = ff // tf
    n_tiles = gemm_list[0].shape[0]

    def fcol(f, used, r):
        return jnp.where(used[r] == 1, f, nf - 1)

    grid_spec = pltpu.PrefetchScalarGridSpec(
        num_scalar_prefetch=3,
        grid=(n_tiles, nf),
        in_specs=[
            pl.BlockSpec((gr, d), lambda r, f, blk, ex, used: (blk[r], 0)),
            pl.BlockSpec((None, d, tf), lambda r, f, blk, ex, used: (ex[r], 0, fcol(f, used, r))),
            pl.BlockSpec((None, d, tf), lambda r, f, blk, ex, used: (ex[r], 0, nf + fcol(f, used, r))),
            pl.BlockSpec((None, tf, d), lambda r, f, blk, ex, used: (ex[r], fcol(f, used, r), 0)),
        ],
        out_specs=pl.BlockSpec((gr, d), lambda r, f, blk, ex, used: (blk[r], 0)),
        scratch_shapes=[pltpu.VMEM((gr, d), BF16)],
    )
    return pl.pallas_call(
        _expert_kernel,
        grid_spec=grid_spec,
        out_shape=jax.ShapeDtypeStruct((n_rows, d), F32),
        input_output_aliases={3: 0},
        compiler_params=_params(("arbitrary", "arbitrary")),
        name="moe_experts",
    )(*gemm_list, xs, w_gu, w_gu, w_down)


def _combine_kernel(dest_ref, dest_next_ref, gates_ref, x_ref, gate_ref, y_hbm, op_ref, os_ref,
                    y_scr, sem, *, n_prompt_chunks):
    ch = x_ref.shape[0]
    step, n_steps = pl.program_id(0), pl.num_programs(0)
    slot = lax.rem(step, 2)

    def gather(idx_ref, s):
        for j in range(ch):
            for k in range(TOP_K):
                pltpu.make_async_copy(y_hbm.at[pl.ds(idx_ref[k, j], 1), :],
                                      y_scr.at[s, k, pl.ds(j, 1), :], sem.at[s]).start(priority=k)

    @pl.when(step == 0)
    def _():
        gather(dest_ref, slot)

    @pl.when(step + 1 < n_steps)
    def _():
        gather(dest_next_ref, 1 - slot)

    for k in range(TOP_K):
        _wait_rows(y_hbm, y_scr.at[slot, k], sem.at[slot], ch)

    gates = gates_ref[...]
    mix = gates[:, 0:1] * y_scr[slot, 0] + gates[:, 1:2] * y_scr[slot, 1]
    out = x_ref[...] + gate_ref[...] * mix
    is_prompt = step < n_prompt_chunks

    @pl.when(is_prompt)
    def _():
        op_ref[...] = out

    @pl.when(jnp.logical_not(is_prompt))
    def _():
        os_ref[...] = out


def _moe_combine(y, dest_rows, gates, x, mods, *, n_prompt, tok_per_cond):
    t, d = x.shape
    ch = ROUTE_CHUNK
    n_pc = n_prompt // ch
    row = _cond_row_joint(ch, n_prompt, tok_per_cond)
    return pl.pallas_call(
        functools.partial(_combine_kernel, n_prompt_chunks=n_pc),
        grid=(t // ch,),
        in_specs=[
            pl.BlockSpec((None, TOP_K, ch), lambda i: (i, 0, 0), memory_space=pltpu.SMEM),
            pl.BlockSpec((None, TOP_K, ch), lambda i: (jnp.minimum(i + 1, t // ch - 1), 0, 0),
                         memory_space=pltpu.SMEM),
            pl.BlockSpec((ch, gates.shape[1]), lambda i: (i, 0)),
            pl.BlockSpec((ch, d), lambda i: (i, 0)),
            mods.spec(_Mods.GATE2, row),
            pl.BlockSpec(memory_space=pl.ANY),
        ],
        out_specs=[
            pl.BlockSpec((ch, d), lambda i: (jnp.minimum(i, n_pc - 1), 0)),
            pl.BlockSpec((ch, d), lambda i: (jnp.maximum(i - n_pc, 0), 0)),
        ],
        out_shape=[jax.ShapeDtypeStruct((n_prompt, d), F32), jax.ShapeDtypeStruct((t - n_prompt, d), F32)],
        scratch_shapes=[pltpu.VMEM((2, TOP_K, ch, d), F32), pltpu.SemaphoreType.DMA((2,))],
        compiler_params=_params(("arbitrary",)),
        name="moe_combine",
    )(dest_rows, dest_rows, gates, x, mods.table, y)


def kernel(x_prompt, x_sample, c, cache_k, cache_v, c_ctx, ada_w, ada_b, norm1_g, norm2_g,
           attn_w_qkv, attn_w_o, attn_q_norm, attn_k_norm, attn_lambda, attn_subln_g,
           pool_w, pool_scale, ffn_w_gu, ffn_w_down,
           moe_w_router, moe_b_router, moe_w_gu, moe_w_down):
    b_ctx, l_ctx, d = x_prompt.shape
    b_dec, l_dec, _ = x_sample.shape
    depth = ada_w.shape[0]
    n_even, l_past, heads, v_dim = cache_k.shape[1:]
    head_dim = v_dim // 2
    assert head_dim == LANES and depth == 2 and l_dec % GRID_W == 0
    n_exp = moe_w_router.shape[-1]
    tp_, ts_ = b_ctx * l_ctx, b_dec * l_dec
    t_all = tp_ + ts_

    cond_rows = 8 * ((1 + b_dec + 7) // 8)
    cond = jnp.zeros((cond_rows, d), F32).at[0].set(c_ctx).at[1:1 + b_dec].set(c)
    mod_table = _adaln_mods(cond, ada_w, ada_b).reshape(depth * cond_rows * 6, 1, d)

    xp = x_prompt.reshape(tp_, d)
    xs = x_sample.reshape(ts_, d)

    layer, j = 0, 0
    mods = _Mods(mod_table, cond_rows, layer)
    g1 = norm1_g[layer].reshape(1, d)
    w_qkv = attn_w_qkv[j].astype(BF16)
    q_gain = (attn_q_norm[j] * (head_dim ** -0.5 * LOG2E)).reshape(1, head_dim)
    k_gain = attn_k_norm[j].reshape(1, head_dim)
    tables = _rope_tables(l_dec)
    prompt = dict(cond_base=0, tok_per_cond=tp_)
    sample = dict(cond_base=1, tok_per_cond=l_dec)

    q_p, k_p, v_p = _qkv_proj(xp, mods, g1, w_qkv, q_gain, k_gain, None, kv_dtype=F32, **prompt)
    q_s, k_s, v_s = _qkv_proj(xs, mods, g1, w_qkv, q_gain, k_gain, tables, kv_dtype=BF16, **sample)

    lam_params = attn_lambda[j].astype(F32)
    subln = attn_subln_g[j].reshape(1, v_dim)
    o_p, _ = _diff_attention(q_p, k_p, v_p, None, None, lam_params, subln,
                             batch=b_ctx, seq=l_ctx, layer=layer)
    ck = cache_k[:, j].reshape(b_dec * l_past, heads * v_dim)
    cv = cache_v[:, j].reshape(b_dec * l_past, heads * v_dim)
    later_weights = [attn_w_o[j], ffn_w_gu[j], ffn_w_down[j], pool_w[0], moe_w_gu[0], moe_w_down[0]]
    o_s, later_bf16 = _diff_attention(q_s, k_s, v_s, ck, cv, lam_params, subln,
                                      batch=b_dec, seq=l_dec, layer=layer,
                                      casts=[w.reshape(-1, w.shape[-1]) for w in later_weights])
    w_o, w_ffn_gu, w_ffn_down, w_pool, w_moe_gu, w_moe_down = [
        b.reshape(w.shape) for b, w in zip(later_bf16, later_weights)]

    x1 = _out_proj_residual(o_p, o_s, w_o, xp, xs, mods, tok_per_cond=l_dec)

    x2 = _dense_ffn(x1, mods, norm2_g[layer].reshape(1, d), w_ffn_gu, w_ffn_down,
                    n_prompt=tp_, tok_per_cond=l_dec)

    layer, j = 1, 0
    mods = _Mods(mod_table, cond_rows, layer)
    x3 = _pool_mixer(x2, mods, norm1_g[layer].reshape(1, d), w_pool,
                     pool_scale[j].reshape(1, d), n_prompt=tp_, prompt_seq=l_ctx, sample_seq=l_dec)

    g2 = norm2_g[layer].reshape(1, d)
    joint = dict(n_prompt=tp_, tok_per_cond=l_dec)
    route, counts = _moe_router(x3, mods, g2, moe_w_router[j], moe_b_router[j], **joint)
    dest_rows, gates, gemm_list, zero_tiles, n_rows = _dispatch_plan(route, counts, n_exp)
    rows_sorted = _moe_scatter(x3, mods, g2, dest_rows, zero_tiles, n_rows, **joint)
    y_sorted = _moe_experts(rows_sorted, w_moe_gu, w_moe_down, gemm_list)
    y_p, y_s = _moe_combine(y_sorted, dest_rows, gates, x3, mods, **joint)

    y_prompt = y_p.reshape(b_ctx, l_ctx, d)
    y_sample = y_s.reshape(b_dec, l_dec, d)
    state_k = k_p.reshape(b_ctx, 1, l_ctx, heads, v_dim)
    state_v = v_p.reshape(b_ctx, 1, l_ctx, heads, v_dim)
    return (y_prompt, y_sample, state_k, state_v)
```

```python
import functools
import math

import jax
import jax.numpy as jnp
from jax import lax
from jax.experimental import pallas as pl
from jax.experimental.pallas import tpu as pltpu

F32 = jnp.float32
BF16 = jnp.bfloat16

GRID_W = 64
ROPE_BASE = 10000.0
POOL_WINDOWS = (2, 4, 8, 16)
TOP_K = 2
EPS = 1e-6
LOG2E = 1.4426950408889634

LANES = 128
VMEM_LIMIT = 52 * 2**20

ROUTE_CHUNK = 256
GEMM_ROWS = 1024


def _lambda_init(layer):
    return 0.8 - 0.6 * math.exp(-0.3 * layer)


def _tile(n, pref, mult):
    best = None
    t = mult
    while t <= min(n, pref):
        if n % t == 0:
            best = t
        t += mult
    if best is None:
        raise ValueError(f"no tile for {n} (multiple of {mult}, <= {pref})")
    return best


def _params(semantics):
    return pltpu.CompilerParams(dimension_semantics=semantics, vmem_limit_bytes=VMEM_LIMIT)


def _dot(a, b):
    return jnp.dot(a, b, preferred_element_type=F32)


def _mask_bf16(m):
    return jnp.where(m, 1.0, 0.0).astype(BF16)


def _split_bf16(x):
    hi = x.astype(BF16)
    lo = (x - hi.astype(F32)).astype(BF16)
    return hi, lo


def _norm_modulate(x, g, scale, shift):
    ms = jnp.mean(x * x, axis=-1, keepdims=True)
    return x * lax.rsqrt(ms + EPS) * (g * (1.0 + scale)) + shift


def _adaln_kernel(cond_ref, w_ref, b_ref, o_ref):
    c = cond_ref[...]
    s = c * jax.nn.sigmoid(c)
    s_hi, s_lo = _split_bf16(s)
    w_hi, w_lo = _split_bf16(w_ref[...])
    o_ref[...] = _dot(s_hi, w_hi) + _dot(s_lo, w_hi) + _dot(s_hi, w_lo) + b_ref[...]


def _adaln_mods(cond, ada_w, ada_b):
    depth, d, n = ada_w.shape
    rows = cond.shape[0]
    tn = _tile(n, 512, LANES)
    return pl.pallas_call(
        _adaln_kernel,
        grid=(depth, n // tn),
        in_specs=[
            pl.BlockSpec((rows, d), lambda l, j: (0, 0)),
            pl.BlockSpec((None, d, tn), lambda l, j: (l, 0, j)),
            pl.BlockSpec((None, 1, tn), lambda l, j: (l, 0, j)),
        ],
        out_specs=pl.BlockSpec((None, rows, tn), lambda l, j: (l, 0, j)),
        out_shape=jax.ShapeDtypeStruct((depth, rows, n), F32),
        compiler_params=_params(("arbitrary", "arbitrary")),
        name="adaln_mods",
    )(cond, ada_w, ada_b.reshape(depth, 1, n))


class _Mods:
    SHIFT1, SCALE1, GATE1, SHIFT2, SCALE2, GATE2 = range(6)

    def __init__(self, table, rows, layer):
        self.table = table
        self.rows = rows
        self.layer = layer

    def spec(self, which, row_fn, width=None, col_fn=None):
        d = self.table.shape[-1]
        width = d if width is None else width
        base = self.layer * self.rows

        def index(*ids):
            col = 0 if col_fn is None else col_fn(*ids)
            return ((base + row_fn(*ids)) * 6 + which, 0, col)

        return pl.BlockSpec((None, 1, width), index)


def _qkv_kernel(*refs, rope, tn, n_sec):
    x_ref, sh_ref, sc_ref, g_ref, w_ref, qg_ref, kg_ref = refs[:7]
    rest = list(refs[7:])
    cos_ref, sin_ref = (rest.pop(0), rest.pop(0)) if rope else (None, None)
    q_ref, k_ref, v_ref, h_scr = rest
    j = pl.program_id(1)

    @pl.when(j == 0)
    def _():
        h = _norm_modulate(x_ref[...], g_ref[...], sc_ref[...], sh_ref[...])
        h_scr[...] = h.astype(BF16)

    def qk_epilogue(gain_ref, out_ref):
        acc = _dot(h_scr[...], w_ref[...])
        r_idx = lax.broadcasted_iota(jnp.int32, (2 * LANES, 2 * LANES), 0)
        c_idx = lax.broadcasted_iota(jnp.int32, (2 * LANES, 2 * LANES), 1)
        if rope:
            swap = (r_idx >= LANES) & (c_idx >= LANES) & (r_idx - LANES == jnp.bitwise_xor(c_idx - LANES, LANES // 4))
            rhs = _mask_bf16(((r_idx < LANES) & (c_idx < LANES)) | swap)
        else:
            rhs = jnp.ones((2 * LANES, LANES), BF16)
        for g in range(tn // LANES):
            y = acc[:, g * LANES:(g + 1) * LANES]
            yg = y * gain_ref[...]
            if rope:
                res = _dot(jnp.concatenate([(y * y).astype(BF16), yg.astype(BF16)], axis=1), rhs)
                inv = lax.rsqrt(res[:, :LANES] * (1.0 / LANES) + EPS)
                out = (yg * cos_ref[...] + res[:, LANES:] * sin_ref[...]) * inv
            else:
                ss = _dot(jnp.concatenate(_split_bf16(y * y), axis=1), rhs)
                out = yg * lax.rsqrt(ss * (1.0 / LANES) + EPS)
            out_ref[:, g * LANES:(g + 1) * LANES] = out.astype(out_ref.dtype)

    @pl.when(j < n_sec)
    def _():
        qk_epilogue(qg_ref, q_ref)

    @pl.when((j >= n_sec) & (j < 2 * n_sec))
    def _():
        qk_epilogue(kg_ref, k_ref)

    @pl.when(j >= 2 * n_sec)
    def _():
        v_ref[...] = _dot(h_scr[...], w_ref[...]).astype(v_ref.dtype)


def _qkv_proj(x, mods, norm_g, w, q_gain, k_gain, tables, *, cond_base, tok_per_cond, kv_dtype):
    t, d = x.shape
    tm = _tile(math.gcd(t, tok_per_cond), 1024, 8)
    tn = _tile(d, 512, LANES)
    n_sec = d // tn
    rope = tables is not None
    row = lambda i, j: cond_base + (i * tm) // tok_per_cond
    in_specs = [
        pl.BlockSpec((tm, d), lambda i, j: (i, 0)),
        mods.spec(_Mods.SHIFT1, row),
        mods.spec(_Mods.SCALE1, row),
        pl.BlockSpec((1, d), lambda i, j: (0, 0)),
        pl.BlockSpec((d, tn), lambda i, j: (0, j)),
        pl.BlockSpec((1, LANES), lambda i, j: (0, 0)),
        pl.BlockSpec((1, LANES), lambda i, j: (0, 0)),
    ]
    args = [x, mods.table, mods.table, norm_g, w, q_gain, k_gain]
    if rope:
        nblk = tables[0].shape[0] // tm
        for tab in tables:
            in_specs.append(pl.BlockSpec((tm, LANES), lambda i, j: (i % nblk, 0)))
            args.append(tab)
    out_specs = [
        pl.BlockSpec((tm, tn), lambda i, j, s=s: (i, jnp.clip(j - s * n_sec, 0, n_sec - 1)))
        for s in range(3)
    ]
    return pl.pallas_call(
        functools.partial(_qkv_kernel, rope=rope, tn=tn, n_sec=n_sec),
        grid=(t // tm, 3 * n_sec),
        in_specs=in_specs,
        out_specs=out_specs,
        out_shape=[jax.ShapeDtypeStruct((t, d), BF16), jax.ShapeDtypeStruct((t, d), kv_dtype),
                   jax.ShapeDtypeStruct((t, d), kv_dtype)],
        scratch_shapes=[pltpu.VMEM((tm, d), BF16)],
        compiler_params=_params(("arbitrary", "arbitrary")),
        name="qkv_proj_rope" if rope else "qkv_proj",
    )(*args)


def _rope_tables(seq):
    n_freq = LANES // 4
    pos = jnp.arange(seq, dtype=jnp.int32)
    r = (pos // GRID_W).astype(F32)
    col = (pos % GRID_W).astype(F32)
    freqs = ROPE_BASE ** (-jnp.arange(n_freq, dtype=F32) / n_freq)
    cos_t, sin_t = [], []
    for p in (r, col):
        ang = p[:, None] * freqs
        cos, sin = jnp.cos(ang), jnp.sin(ang)
        cos_t += [cos, cos]
        sin_t += [-sin, sin]
    return jnp.concatenate(cos_t, axis=1), jnp.concatenate(sin_t, axis=1)


def _attn_kernel(*refs, n_cache, n_chunks, tk, lambda_init, cast_blocks, heads_per_step):
    n_cast = len(cast_blocks)
    lam_ref, g_ref, q_ref = refs[:3]
    n_in = 3 + (4 if n_cache else 2)
    if n_cache:
        kc_ref, vtc_ref, k_ref, vt_ref = refs[3:n_in]
    else:
        k_ref, vt_ref = refs[3:n_in]
    cast_in = refs[n_in:n_in + n_cast]
    o_ref = refs[n_in + n_cast]
    cast_out = refs[n_in + n_cast + 1:n_in + 2 * n_cast + 1]
    acc_ref = refs[-1]

    step = (pl.program_id(0) * pl.num_programs(1) + pl.program_id(1)) * pl.num_programs(2) + pl.program_id(2)
    for src, dst, n_blocks in zip(cast_in, cast_out, cast_blocks):
        @pl.when(step < n_blocks)
        def _(src=src, dst=dst):
            dst[...] = src[...].astype(dst.dtype)

    vd = 2 * LANES
    for hq in range(heads_per_step):
        cols = slice(hq * vd, (hq + 1) * vd)
        _attn_head(lam_ref, g_ref, q_ref.at[:, cols],
                   kc_ref.at[:, cols] if n_cache else None, vtc_ref.at[hq] if n_cache else None,
                   k_ref.at[:, cols], vt_ref.at[hq], o_ref.at[:, cols], acc_ref,
                   n_chunks=n_chunks, tk=tk, lambda_init=lambda_init)


def _attn_head(lam_ref, g_ref, q_ref, kc_ref, vtc_ref, k_ref, vt_ref, o_ref, acc_ref, *,
               n_chunks, tk, lambda_init):
    q = q_ref[...]
    qs = (q[:, :LANES], q[:, LANES:])

    def scores(kblk):
        return [lax.dot_general(kblk[:, c * LANES:(c + 1) * LANES], qs[c],
                                (((1,), (1,)), ((), ())), preferred_element_type=F32)
                for c in range(2)]

    def accumulate(sts, vt, stats):
        out = []
        for c in range(2):
            st = sts[c]
            m_cur = jnp.max(st, axis=0, keepdims=True)
            if stats is None:
                m_new = m_cur
                p = jnp.exp2(st - m_new)
                l_new = jnp.sum(p, axis=0, keepdims=True)
                acc_ref[c] = _dot(vt, p.astype(BF16))
            else:
                m_prev, l_prev = stats[c]
                m_new = jnp.maximum(m_prev, m_cur)
                alpha = jnp.exp2(m_prev - m_new)
                p = jnp.exp2(st - m_new)
                l_new = alpha * l_prev + jnp.sum(p, axis=0, keepdims=True)
                acc_ref[c] = alpha * acc_ref[c] + _dot(vt, p.astype(BF16))
            out.append((m_new, l_new))
        return out

    chunks = []
    if kc_ref is not None:
        chunks.append((lambda: kc_ref[...].astype(BF16), lambda: vtc_ref[...]))
    for i in range(n_chunks):
        chunks.append((lambda i=i: k_ref[i * tk:(i + 1) * tk, :].astype(BF16), lambda i=i: vt_ref[i]))
    stats = None
    sts = scores(chunks[0][0]())
    for n, (_, values) in enumerate(chunks):
        nxt = scores(chunks[n + 1][0]()) if n + 1 < len(chunks) else None
        stats = accumulate(sts, values(), stats)
        sts = nxt

    lp = lam_ref[...]
    lam = (jnp.exp(jnp.sum(lp[0:1] * lp[1:2], axis=-1, keepdims=True))
           - jnp.exp(jnp.sum(lp[2:3] * lp[3:4], axis=-1, keepdims=True)) + lambda_init)
    o_t = acc_ref[0] / stats[0][1] - lam * (acc_ref[1] / stats[1][1])
    o = o_t.T
    o = o * lax.rsqrt(jnp.mean(o * o, axis=-1, keepdims=True) + EPS) * g_ref[...]
    o_ref[...] = (o * (1.0 - lambda_init)).astype(o_ref.dtype)


def _chunked_transpose(v, batch, seq, heads, vd, tk):
    v = v.astype(BF16).reshape(batch, seq // tk, tk, heads, vd)
    return v.transpose(0, 3, 1, 4, 2).reshape(batch * heads, seq // tk, vd, tk)


def _cast_block_rows(rows, n_steps):
    blk = 16
    while rows % blk or rows // blk > n_steps:
        blk += 16
        if blk > rows:
            raise ValueError(f"no cast block for {rows} rows in {n_steps} steps")
    return blk


def _diff_attention(q, k, v, cache_k, cache_v, lam_params, subln_g, *, batch, seq, layer, casts=()):
    t, d = q.shape
    vd = subln_g.shape[-1]
    heads = d // vd
    n_cache = 0 if cache_k is None else cache_k.shape[0] // batch
    tq = _tile(seq, 512, LANES)
    tk = _tile(seq, 512, LANES)
    nq = seq // tq
    n_chunks = seq // tk
    hps = heads if seq <= tq else 1
    groups = heads // hps
    wide = hps * vd
    in_specs = [
        pl.BlockSpec(lam_params.shape, lambda b, h, i: (0, 0)),
        pl.BlockSpec((1, vd), lambda b, h, i: (0, 0)),
        pl.BlockSpec((tq, wide), lambda b, h, i: (b * nq + i, h)),
    ]
    args = [lam_params, subln_g, q]
    if n_cache:
        in_specs += [pl.BlockSpec((n_cache, wide), lambda b, h, i: (b, h)),
                     pl.BlockSpec((hps, None, vd, n_cache), lambda b, h, i: (b * groups + h, 0, 0, 0))]
        args += [cache_k, _chunked_transpose(cache_v, batch, n_cache, heads, vd, n_cache)]
    in_specs += [pl.BlockSpec((seq, wide), lambda b, h, i: (b, h)),
                 pl.BlockSpec((hps, n_chunks, vd, tk), lambda b, h, i: (b * groups + h, 0, 0, 0))]
    args += [k, _chunked_transpose(v, batch, seq, heads, vd, tk)]
    out_specs = [pl.BlockSpec((tq, wide), lambda b, h, i: (b * nq + i, h))]
    out_shape = [jax.ShapeDtypeStruct((t, d), BF16)]
    n_steps = batch * groups * nq
    cast_blocks = []
    for w in casts:
        rows, cols = w.shape
        blk = _cast_block_rows(rows, n_steps)
        n_blocks = rows // blk
        cast_blocks.append(n_blocks)
        spec = pl.BlockSpec((blk, cols), lambda b, h, i, n=n_blocks: (
            jnp.minimum((b * groups + h) * nq + i, n - 1), 0))
        in_specs.append(spec)
        args.append(w)
        out_specs.append(spec)
        out_shape.append(jax.ShapeDtypeStruct((rows, cols), BF16))
    res = pl.pallas_call(
        functools.partial(_attn_kernel, n_cache=n_cache, n_chunks=n_chunks, tk=tk,
                          lambda_init=_lambda_init(layer), cast_blocks=tuple(cast_blocks),
                          heads_per_step=hps),
        grid=(batch, groups, nq),
        in_specs=in_specs,
        out_specs=out_specs,
        out_shape=out_shape,
        scratch_shapes=[pltpu.VMEM((2, vd, tq), F32)],
        compiler_params=_params(("arbitrary", "arbitrary", "arbitrary")),
        name=f"diff_attention_{'latent' if n_cache else 'context'}",
    )(*args)
    return res[0], list(res[1:])


def _out_proj_kernel(op_ref, os_ref, w_ref, xp_ref, xs_ref, gate_ref, out_ref, *, n_prompt_tiles):
    is_prompt = pl.program_id(0) < n_prompt_tiles

    @pl.when(is_prompt)
    def _():
        out_ref[...] = xp_ref[...] + gate_ref[...] * _dot(op_ref[...], w_ref[...])

    @pl.when(jnp.logical_not(is_prompt))
    def _():
        out_ref[...] = xs_ref[...] + gate_ref[...] * _dot(os_ref[...], w_ref[...])


def _out_proj_residual(o_p, o_s, w, x_p, x_s, mods, *, tok_per_cond):
    n_prompt, d = x_p.shape
    t = n_prompt + x_s.shape[0]
    tm = _tile(math.gcd(n_prompt, tok_per_cond), 1024, 8)
    tn = _tile(d, 512, LANES)
    n_pt = n_prompt // tm
    row = _cond_row_joint(tm, n_prompt, tok_per_cond)
    p_rows = lambda i: jnp.minimum(i, n_pt - 1)
    s_rows = lambda i: jnp.maximum(i - n_pt, 0)
    return pl.pallas_call(
        functools.partial(_out_proj_kernel, n_prompt_tiles=n_pt),
        grid=(t // tm, d // tn),
        in_specs=[
            pl.BlockSpec((tm, d), lambda i, j: (p_rows(i), 0)),
            pl.BlockSpec((tm, d), lambda i, j: (s_rows(i), 0)),
            pl.BlockSpec((d, tn), lambda i, j: (0, j)),
            pl.BlockSpec((tm, tn), lambda i, j: (p_rows(i), jnp.where(i < n_pt, j, d // tn - 1))),
            pl.BlockSpec((tm, tn), lambda i, j: (s_rows(i), jnp.where(i < n_pt, 0, j))),
            mods.spec(_Mods.GATE1, row, width=tn, col_fn=lambda i, j: j),
        ],
        out_specs=pl.BlockSpec((tm, tn), lambda i, j: (i, j)),
        out_shape=jax.ShapeDtypeStruct((t, d), F32),
        compiler_params=_params(("arbitrary", "arbitrary")),
        name="attn_out_proj",
    )(o_p, o_s, w, x_p, x_s, mods.table)


def _ffn_kernel(x_ref, sh_ref, sc_ref, gate_ref, g_ref, wg_ref, wu_ref, wd_ref, o_ref, h_scr):
    f = pl.program_id(1)

    @pl.when(f == 0)
    def _():
        h = _norm_modulate(x_ref[...], g_ref[...], sc_ref[...], sh_ref[...])
        h_scr[...] = h.astype(BF16)
        o_ref[...] = jnp.zeros(o_ref.shape, F32)

    h = h_scr[...]
    gte = _dot(h, wg_ref[...])
    up = _dot(h, wu_ref[...])
    act = (gte * jax.nn.sigmoid(gte) * up).astype(BF16)
    o_ref[...] += _dot(act, wd_ref[...])

    @pl.when(f == pl.num_programs(1) - 1)
    def _():
        o_ref[...] = x_ref[...] + gate_ref[...] * o_ref[...]


def _cond_row_joint(tm, n_prompt, tok_per_cond):
    def row(i, *_):
        tok = i * tm
        return jnp.where(tok < n_prompt, 0, 1 + (tok - n_prompt) // tok_per_cond)
    return row


def _dense_ffn(x, mods, norm_g, w_gu, w_down, *, n_prompt, tok_per_cond):
    t, d = x.shape
    ff = w_down.shape[0]
    tm = _tile(math.gcd(n_prompt, tok_per_cond), 1024, 8)
    tf = _tile(ff, 256, LANES)
    nf = ff // tf
    row = _cond_row_joint(tm, n_prompt, tok_per_cond)
    return pl.pallas_call(
        _ffn_kernel,
        grid=(t // tm, nf),
        in_specs=[
            pl.BlockSpec((tm, d), lambda i, f: (i, 0)),
            mods.spec(_Mods.SHIFT2, row),
            mods.spec(_Mods.SCALE2, row),
            mods.spec(_Mods.GATE2, row),
            pl.BlockSpec((1, d), lambda i, f: (0, 0)),
            pl.BlockSpec((d, tf), lambda i, f: (0, f)),
            pl.BlockSpec((d, tf), lambda i, f: (0, nf + f)),
            pl.BlockSpec((tf, d), lambda i, f: (f, 0)),
        ],
        out_specs=pl.BlockSpec((tm, d), lambda i, f: (i, 0)),
        out_shape=jax.ShapeDtypeStruct((t, d), F32),
        scratch_shapes=[pltpu.VMEM((tm, d), BF16)],
        compiler_params=_params(("arbitrary", "arbitrary")),
        name="dense_ffn",
    )(x, mods.table, mods.table, mods.table, norm_g, w_gu, w_gu, w_down)


def _pool_kernel(x_ref, xp_ref, xn_ref, sh_ref, sc_ref, gate_ref, g_ref, w_ref, ps_ref, o_ref, *,
                 tp, halo, n_prompt_tiles, prompt_tiles_per_seq, sample_tiles_per_seq, group_dim):
    i = pl.program_id(0)
    is_prompt = i < n_prompt_tiles
    tiles_per_seq = jnp.where(is_prompt, prompt_tiles_per_seq, sample_tiles_per_seq)
    local = lax.rem(jnp.where(is_prompt, i, i - n_prompt_tiles), tiles_per_seq)
    first = local == 0
    last = local == tiles_per_seq - 1
    seq_len = tiles_per_seq * tp

    g, sc, sh = g_ref[...], sc_ref[...], sh_ref[...]
    x = x_ref[...]
    h = _norm_modulate(x, g, sc, sh)
    h_prev = jnp.where(first, 0.0, _norm_modulate(xp_ref[...], g, sc, sh))
    h_next = jnp.where(last, 0.0, _norm_modulate(xn_ref[...], g, sc, sh))
    d = h.shape[-1]
    h_halo = jnp.concatenate([h_prev, h_next, jnp.zeros((LANES - 2 * halo, d), F32)], axis=0)

    t_idx = lax.broadcasted_iota(jnp.int32, (tp, tp), 0)
    s_idx = lax.broadcasted_iota(jnp.int32, (tp, tp), 1)
    t_h = lax.broadcasted_iota(jnp.int32, (tp, LANES), 0)
    u_h = lax.broadcasted_iota(jnp.int32, (tp, LANES), 1)
    off_h = jnp.where(u_h < halo, u_h - halo, tp + u_h - halo)
    pos = local * tp + lax.broadcasted_iota(jnp.int32, (tp, 1), 0)

    for grp, win in enumerate(POOL_WINDOWS):
        lo, hi = win // 2, win // 2 - 1
        cols = slice(grp * group_dim, (grp + 1) * group_dim)
        band = _mask_bf16((s_idx >= t_idx - lo) & (s_idx <= t_idx + hi))
        band_h = _mask_bf16((u_h < 2 * halo) & (off_h >= t_h - lo) & (off_h <= t_h + hi))
        total = _dot(band, h[:, cols].astype(BF16)) + _dot(band_h, h_halo[:, cols].astype(BF16))
        cnt = (jnp.minimum(pos + hi, seq_len - 1) - jnp.maximum(pos - lo, 0) + 1).astype(F32)
        pooled = (total / cnt - h[:, cols]).astype(BF16)
        y = _dot(pooled, w_ref[grp]) * ps_ref[:, cols]
        o_ref[:, cols] = x[:, cols] + gate_ref[:, cols] * y


def _pool_mixer(x, mods, norm_g, pool_w, pool_scale, *, n_prompt, prompt_seq, sample_seq):
    t, d = x.shape
    halo = max(POOL_WINDOWS) // 2
    tp = _tile(math.gcd(prompt_seq, sample_seq), 256, 8)
    n_groups, group_dim = pool_w.shape[0], pool_w.shape[1]
    hb = tp // halo
    n_halo_blocks = t // halo
    row = _cond_row_joint(tp, n_prompt, sample_seq)
    return pl.pallas_call(
        functools.partial(_pool_kernel, tp=tp, halo=halo, n_prompt_tiles=n_prompt // tp,
                          prompt_tiles_per_seq=prompt_seq // tp,
                          sample_tiles_per_seq=sample_seq // tp, group_dim=group_dim),
        grid=(t // tp,),
        in_specs=[
            pl.BlockSpec((tp, d), lambda i: (i, 0)),
            pl.BlockSpec((halo, d), lambda i: (jnp.maximum(i * hb - 1, 0), 0)),
            pl.BlockSpec((halo, d), lambda i: (jnp.minimum((i + 1) * hb, n_halo_blocks - 1), 0)),
            mods.spec(_Mods.SHIFT1, row),
            mods.spec(_Mods.SCALE1, row),
            mods.spec(_Mods.GATE1, row),
            pl.BlockSpec((1, d), lambda i: (0, 0)),
            pl.BlockSpec((n_groups, group_dim, group_dim), lambda i: (0, 0, 0)),
            pl.BlockSpec((1, d), lambda i: (0, 0)),
        ],
        out_specs=pl.BlockSpec((tp, d), lambda i: (i, 0)),
        out_shape=jax.ShapeDtypeStruct((t, d), F32),
        compiler_params=_params(("arbitrary",)),
        name="pool_mixer",
    )(x, x, x, mods.table, mods.table, mods.table, norm_g, pool_w, pool_scale)


def _router_kernel(x_ref, sh_ref, sc_ref, g_ref, wr_ref, br_ref, route_ref, cnt_ref):
    h = _norm_modulate(x_ref[...], g_ref[...], sc_ref[...], sh_ref[...])
    h_hi, h_lo = _split_bf16(h)
    w_hi, w_lo = _split_bf16(wr_ref[...])
    logits = _dot(h_hi, w_hi) + _dot(h_lo, w_hi) + _dot(h_hi, w_lo) + br_ref[...]

    rows = logits.shape[0]
    lane = lax.broadcasted_iota(jnp.int32, logits.shape, 1).astype(F32)
    m1 = jnp.max(logits, axis=-1, keepdims=True)
    i1 = jnp.min(jnp.where(logits == m1, lane, float(LANES)), axis=-1, keepdims=True)
    oh1 = lane == i1
    rest = jnp.where(oh1, -jnp.inf, logits)
    m2 = jnp.max(rest, axis=-1, keepdims=True)
    i2 = jnp.min(jnp.where(rest == m2, lane, float(LANES)), axis=-1, keepdims=True)
    oh2 = lane == i2
    e = jnp.exp(m2 - m1)
    gate_a = 1.0 / (1.0 + e)
    gate_b = e / (1.0 + e)

    sel = jnp.where(oh1 | oh2, 1.0, 0.0)
    r_idx = lax.broadcasted_iota(jnp.int32, (rows, rows), 0)
    c_idx = lax.broadcasted_iota(jnp.int32, (rows, rows), 1)
    earlier = _mask_bf16(c_idx < r_idx)
    rank = _dot(earlier, sel.astype(BF16))
    rank_a = jnp.sum(jnp.where(oh1, rank, 0.0), axis=-1, keepdims=True)
    rank_b = jnp.sum(jnp.where(oh2, rank, 0.0), axis=-1, keepdims=True)

    route = jnp.zeros(logits.shape, F32)
    for k, val in enumerate((i1, i2, rank_a, rank_b, gate_a, gate_b)):
        route = jnp.where(lane == float(k), val, route)
    route_ref[...] = route
    cnt_ref[...] = jnp.sum(sel, axis=0, keepdims=True)


def _moe_router(x, mods, norm_g, w_router, b_router, *, n_prompt, tok_per_cond):
    t, d = x.shape
    n_exp = w_router.shape[1]
    ch = ROUTE_CHUNK
    wr = jnp.zeros((d, LANES), F32).at[:, :n_exp].set(w_router.astype(F32))
    br = jnp.full((1, LANES), -jnp.inf, F32).at[0, :n_exp].set(b_router.astype(F32))
    row = _cond_row_joint(ch, n_prompt, tok_per_cond)
    return pl.pallas_call(
        _router_kernel,
        grid=(t // ch,),
        in_specs=[
            pl.BlockSpec((ch, d), lambda i: (i, 0)),
            mods.spec(_Mods.SHIFT2, row),
            mods.spec(_Mods.SCALE2, row),
            pl.BlockSpec((1, d), lambda i: (0, 0)),
            pl.BlockSpec((d, LANES), lambda i: (0, 0)),
            pl.BlockSpec((1, LANES), lambda i: (0, 0)),
        ],
        out_specs=[
            pl.BlockSpec((ch, LANES), lambda i: (i, 0)),
            pl.BlockSpec((None, 1, LANES), lambda i: (i, 0, 0)),
        ],
        out_shape=[
            jax.ShapeDtypeStruct((t, LANES), F32),
            jax.ShapeDtypeStruct((t // ch, 1, LANES), F32),
        ],
        compiler_params=_params(("arbitrary",)),
        name="moe_router",
    )(x, mods.table, mods.table, norm_g, wr, br)


def _dispatch_plan(route, counts, n_exp):
    ch, gr = ROUTE_CHUNK, GEMM_ROWS
    t = route.shape[0]
    n_chunks = t // ch
    i32 = jnp.int32
    cnt = counts[:, 0, :n_exp].astype(i32)
    total = cnt.sum(0)
    padded = ((total + gr - 1) // gr) * gr
    off = jnp.cumsum(padded) - padded
    start = off[None, :] + jnp.cumsum(cnt, axis=0) - cnt

    def dest(choice):
        e = route[:, choice].astype(i32).reshape(n_chunks, ch, 1)
        hit = e == jnp.arange(n_exp, dtype=i32)
        base = jnp.sum(jnp.where(hit, start[:, None, :], 0), axis=-1)
        return base.reshape(t) + route[:, 2 + choice].astype(i32)

    dest_a, dest_b = dest(0), dest(1)

    n_gemm_tiles = (t * TOP_K) // gr + n_exp

    tile_lo = jnp.arange(n_gemm_tiles, dtype=i32) * gr
    used = tile_lo < (off + padded)[-1]
    ends = off + padded
    expert_of = jnp.minimum(jnp.sum(tile_lo[:, None] >= ends[None, :], axis=1), n_exp - 1).astype(i32)
    n_used = jnp.maximum(used.sum(), 1)
    clamp = jnp.minimum(jnp.arange(n_gemm_tiles, dtype=i32), n_used - 1).astype(i32)
    token_rows = jnp.clip((off + total)[expert_of] - tile_lo, 0, gr).astype(i32)
    gemm_list = (clamp, expert_of[clamp], used.astype(i32), token_rows[clamp])

    tail = jnp.arange(n_gemm_tiles - n_exp, n_gemm_tiles, dtype=i32)
    zero_rows = jnp.concatenate([jnp.maximum(ends - gr, 0), tail * gr]).astype(i32)
    zero_ok = jnp.concatenate([padded > 0, tail >= used.sum()]).astype(i32)

    dest_rows = jnp.stack([dest_a.reshape(n_chunks, ch), dest_b.reshape(n_chunks, ch)], axis=1)
    gates = jnp.concatenate([route[:, 4:6], jnp.zeros((t, 6), F32)], axis=1)
    return dest_rows, gates, gemm_list, (zero_rows, zero_ok), n_gemm_tiles * gr


def _wait_rows(src, dst, sem, n_rows):
    pltpu.make_async_copy(src.at[pl.ds(0, n_rows), :], dst.at[pl.ds(0, n_rows), :], sem).wait()


def _scatter_kernel(zrow_ref, zok_ref, dest_ref, x_ref, sh_ref, sc_ref, g_ref, out_hbm,
                    h_scr, z_scr, sem, zsem, *, tile_rows):
    ch = x_ref.shape[0]
    step, n_steps = pl.program_id(0), pl.num_programs(0)
    slot = lax.rem(step, 2)

    @pl.when(step == 0)
    def _():
        z_scr[...] = jnp.zeros(z_scr.shape, F32)
        zr = z_scr.shape[0]

        def zero_copy(n, part):
            row0 = pl.multiple_of(zrow_ref[n] + part * zr, zr)
            return pltpu.make_async_copy(z_scr, out_hbm.at[pl.ds(row0, zr), :], zsem)

        for n in range(zrow_ref.shape[0]):
            @pl.when(zok_ref[n] == 1)
            def _(n=n):
                for part in range(tile_rows // zr):
                    zero_copy(n, part).start()
        for n in range(zrow_ref.shape[0]):
            @pl.when(zok_ref[n] == 1)
            def _(n=n):
                for part in range(tile_rows // zr):
                    zero_copy(n, part).wait()

    h_scr[slot] = _norm_modulate(x_ref[...], g_ref[...], sc_ref[...], sh_ref[...])

    src_rows = h_scr.at[slot]
    for j in range(ch):
        for k in range(TOP_K):
            pltpu.make_async_copy(src_rows.at[pl.ds(j, 1), :],
                                  out_hbm.at[pl.ds(dest_ref[k, j], 1), :],
                                  sem.at[slot]).start(priority=k)

    def wait_slot(s):
        for _ in range(TOP_K):
            _wait_rows(h_scr.at[s], out_hbm, sem.at[s], ch)

    @pl.when(step > 0)
    def _():
        wait_slot(1 - slot)

    @pl.when(step == n_steps - 1)
    def _():
        wait_slot(slot)


def _moe_scatter(x, mods, norm_g, dest_rows, zero_tiles, n_rows, *, n_prompt, tok_per_cond):
    t, d = x.shape
    ch = ROUTE_CHUNK
    row = _cond_row_joint(ch, n_prompt, tok_per_cond)
    grid_spec = pltpu.PrefetchScalarGridSpec(
        num_scalar_prefetch=2,
        grid=(t // ch,),
        in_specs=[
            pl.BlockSpec((None, TOP_K, ch), lambda i, *_: (i, 0, 0), memory_space=pltpu.SMEM),
            pl.BlockSpec((ch, d), lambda i, *_: (i, 0)),
            mods.spec(_Mods.SHIFT2, row),
            mods.spec(_Mods.SCALE2, row),
            pl.BlockSpec((1, d), lambda i, *_: (0, 0)),
        ],
        out_specs=pl.BlockSpec(memory_space=pl.ANY),
        scratch_shapes=[pltpu.VMEM((2, ch, d), F32), pltpu.VMEM((ch, d), F32),
                        pltpu.SemaphoreType.DMA((2,)), pltpu.SemaphoreType.DMA(())],
    )
    return pl.pallas_call(
        functools.partial(_scatter_kernel, tile_rows=GEMM_ROWS),
        grid_spec=grid_spec,
        out_shape=jax.ShapeDtypeStruct((n_rows, d), F32),
        compiler_params=_params(("arbitrary",)),
        name="moe_scatter",
    )(*zero_tiles, dest_rows, x, mods.table, mods.table, norm_g)


def _expert_kernel(blk_ref, exp_ref, used_ref, rows_ref, x_ref, wg_ref, wu_ref, wd_ref, o_ref, x_scr, *,
                   sub_rows):
    r, f = pl.program_id(0), pl.program_id(1)
    tile_rows = o_ref.shape[0]

    def swiglu(rows):
        x = x_scr[rows, :]
        gte = _dot(x, wg_ref[...])
        up = _dot(x, wu_ref[...])
        act = (gte * jax.nn.sigmoid(gte) * up).astype(BF16)
        o_ref[rows, :] += _dot(act, wd_ref[...])

    @pl.when(used_ref[r] == 1)
    def _():
        @pl.when(f == 0)
        def _():
            o_ref[...] = jnp.zeros(o_ref.shape, F32)
            x_scr[...] = x_ref[...].astype(BF16)

        valid = rows_ref[r]

        @pl.when(valid > tile_rows - sub_rows)
        def _():
            swiglu(slice(None))

        @pl.when(valid <= tile_rows - sub_rows)
        def _():
            for sb in range(tile_rows // sub_rows - 1):
                @pl.when(valid > sb * sub_rows)
                def _(sb=sb):
                    swiglu(slice(sb * sub_rows, (sb + 1) * sub_rows))


def _moe_experts(xs, w_gu, w_down, gemm_list):
    n_rows, d = xs.shape
    ff = w_down.shape[1]
    gr = GEMM_ROWS
    tf = _tile(ff, 256, LANES)
    nf = ff // tf
    n_tiles = gemm_list[0].shape[0]

    def fcol(f, used, r):
        return jnp.where(used[r] == 1, f, nf - 1)

    grid_spec = pltpu.PrefetchScalarGridSpec(
        num_scalar_prefetch=4,
        grid=(n_tiles, nf),
        in_specs=[
            pl.BlockSpec((gr, d), lambda r, f, blk, ex, used, rows: (blk[r], 0)),
            pl.BlockSpec((None, d, tf), lambda r, f, blk, ex, used, rows: (ex[r], 0, fcol(f, used, r))),
            pl.BlockSpec((None, d, tf), lambda r, f, blk, ex, used, rows: (ex[r], 0, nf + fcol(f, used, r))),
            pl.BlockSpec((None, tf, d), lambda r, f, blk, ex, used, rows: (ex[r], fcol(f, used, r), 0)),
        ],
        out_specs=pl.BlockSpec((gr, d), lambda r, f, blk, ex, used, rows: (blk[r], 0)),
        scratch_shapes=[pltpu.VMEM((gr, d), BF16)],
    )
    return pl.pallas_call(
        functools.partial(_expert_kernel, sub_rows=ROUTE_CHUNK),
        grid_spec=grid_spec,
        out_shape=jax.ShapeDtypeStruct((n_rows, d), F32),
        input_output_aliases={4: 0},
        compiler_params=_params(("arbitrary", "arbitrary")),
        name="moe_experts",
    )(*gemm_list, xs, w_gu, w_gu, w_down)


def _combine_kernel(dest_ref, dest_next_ref, gates_ref, x_ref, gate_ref, y_hbm, op_ref, os_ref,
                    y_scr, sem, *, n_prompt_chunks):
    ch = x_ref.shape[0]
    step, n_steps = pl.program_id(0), pl.num_programs(0)
    slot = lax.rem(step, 2)

    def gather(idx_ref, s):
        for j in range(ch):
            for k in range(TOP_K):
                pltpu.make_async_copy(y_hbm.at[pl.ds(idx_ref[k, j], 1), :],
                                      y_scr.at[s, k, pl.ds(j, 1), :], sem.at[s]).start(priority=k)

    @pl.when(step == 0)
    def _():
        gather(dest_ref, slot)

    @pl.when(step + 1 < n_steps)
    def _():
        gather(dest_next_ref, 1 - slot)

    for k in range(TOP_K):
        _wait_rows(y_hbm, y_scr.at[slot, k], sem.at[slot], ch)

    gates = gates_ref[...]
    mix = gates[:, 0:1] * y_scr[slot, 0] + gates[:, 1:2] * y_scr[slot, 1]
    out = x_ref[...] + gate_ref[...] * mix
    is_prompt = step < n_prompt_chunks

    @pl.when(is_prompt)
    def _():
        op_ref[...] = out

    @pl.when(jnp.logical_not(is_prompt))
    def _():
        os_ref[...] = out


def _moe_combine(y, dest_rows, gates, x, mods, *, n_prompt, tok_per_cond):
    t, d = x.shape
    ch = ROUTE_CHUNK
    n_pc = n_prompt // ch
    row = _cond_row_joint(ch, n_prompt, tok_per_cond)
    return pl.pallas_call(
        functools.partial(_combine_kernel, n_prompt_chunks=n_pc),
        grid=(t // ch,),
        in_specs=[
            pl.BlockSpec((None, TOP_K, ch), lambda i: (i, 0, 0), memory_space=pltpu.SMEM),
            pl.BlockSpec((None, TOP_K, ch), lambda i: (jnp.minimum(i + 1, t // ch - 1), 0, 0),
                         memory_space=pltpu.SMEM),
            pl.BlockSpec((ch, gates.shape[1]), lambda i: (i, 0)),
            pl.BlockSpec((ch, d), lambda i: (i, 0)),
            mods.spec(_Mods.GATE2, row),
            pl.BlockSpec(memory_space=pl.ANY),
        ],
        out_specs=[
            pl.BlockSpec((ch, d), lambda i: (jnp.minimum(i, n_pc - 1), 0)),
            pl.BlockSpec((ch, d), lambda i: (jnp.maximum(i - n_pc, 0), 0)),
        ],
        out_shape=[jax.ShapeDtypeStruct((n_prompt, d), F32), jax.ShapeDtypeStruct((t - n_prompt, d), F32)],
        scratch_shapes=[pltpu.VMEM((2, TOP_K, ch, d), F32), pltpu.SemaphoreType.DMA((2,))],
        compiler_params=_params(("arbitrary",)),
        name="moe_combine",
    )(dest_rows, dest_rows, gates, x, mods.table, y)


def kernel(x_prompt, x_sample, c, cache_k, cache_v, c_ctx, ada_w, ada_b, norm1_g, norm2_g,
           attn_w_qkv, attn_w_o, attn_q_norm, attn_k_norm, attn_lambda, attn_subln_g,
           pool_w, pool_scale, ffn_w_gu, ffn_w_down,
           moe_w_router, moe_b_router, moe_w_gu, moe_w_down):
    b_ctx, l_ctx, d = x_prompt.shape
    b_dec, l_dec, _ = x_sample.shape
    depth = ada_w.shape[0]
    n_even, l_past, heads, v_dim = cache_k.shape[1:]
    head_dim = v_dim // 2
    assert head_dim == LANES and depth == 2 and l_dec % GRID_W == 0
    n_exp = moe_w_router.shape[-1]
    tp_, ts_ = b_ctx * l_ctx, b_dec * l_dec
    t_all = tp_ + ts_

    cond_rows = 8 * ((1 + b_dec + 7) // 8)
    cond = jnp.zeros((cond_rows, d), F32).at[0].set(c_ctx).at[1:1 + b_dec].set(c)
    mod_table = _adaln_mods(cond, ada_w, ada_b).reshape(depth * cond_rows * 6, 1, d)

    xp = x_prompt.reshape(tp_, d)
    xs = x_sample.reshape(ts_, d)

    layer, j = 0, 0
    mods = _Mods(mod_table, cond_rows, layer)
    g1 = norm1_g[layer].reshape(1, d)
    w_qkv = attn_w_qkv[j].astype(BF16)
    q_gain = (attn_q_norm[j] * (head_dim ** -0.5 * LOG2E)).reshape(1, head_dim)
    k_gain = attn_k_norm[j].reshape(1, head_dim)
    tables = _rope_tables(l_dec)
    prompt = dict(cond_base=0, tok_per_cond=tp_)
    sample = dict(cond_base=1, tok_per_cond=l_dec)

    q_p, k_p, v_p = _qkv_proj(xp, mods, g1, w_qkv, q_gain, k_gain, None, kv_dtype=F32, **prompt)
    q_s, k_s, v_s = _qkv_proj(xs, mods, g1, w_qkv, q_gain, k_gain, tables, kv_dtype=BF16, **sample)

    lam_params = attn_lambda[j].astype(F32)
    subln = attn_subln_g[j].reshape(1, v_dim)
    o_p, _ = _diff_attention(q_p, k_p, v_p, None, None, lam_params, subln,
                             batch=b_ctx, seq=l_ctx, layer=layer)
    ck = cache_k[:, j].reshape(b_dec * l_past, heads * v_dim)
    cv = cache_v[:, j].reshape(b_dec * l_past, heads * v_dim)
    later_weights = [attn_w_o[j], ffn_w_gu[j], ffn_w_down[j], pool_w[0], moe_w_gu[0], moe_w_down[0]]
    o_s, later_bf16 = _diff_attention(q_s, k_s, v_s, ck, cv, lam_params, subln,
                                      batch=b_dec, seq=l_dec, layer=layer,
                                      casts=[w.reshape(-1, w.shape[-1]) for w in later_weights])
    w_o, w_ffn_gu, w_ffn_down, w_pool, w_moe_gu, w_moe_down = [
        b.reshape(w.shape) for b, w in zip(later_bf16, later_weights)]

    x1 = _out_proj_residual(o_p, o_s, w_o, xp, xs, mods, tok_per_cond=l_dec)

    x2 = _dense_ffn(x1, mods, norm2_g[layer].reshape(1, d), w_ffn_gu, w_ffn_down,
                    n_prompt=tp_, tok_per_cond=l_dec)

    layer, j = 1, 0
    mods = _Mods(mod_table, cond_rows, layer)
    x3 = _pool_mixer(x2, mods, norm1_g[layer].reshape(1, d), w_pool,
                     pool_scale[j].reshape(1, d), n_prompt=tp_, prompt_seq=l_ctx, sample_seq=l_dec)

    g2 = norm2_g[layer].reshape(1, d)
    joint = dict(n_prompt=tp_, tok_per_cond=l_dec)
    route, counts = _moe_router(x3, mods, g2, moe_w_router[j], moe_b_router[j], **joint)
    dest_rows, gates, gemm_list, zero_tiles, n_rows = _dispatch_plan(route, counts, n_exp)
    rows_sorted = _moe_scatter(x3, mods, g2, dest_rows, zero_tiles, n_rows, **joint)
    y_sorted = _moe_experts(rows_sorted, w_moe_gu, w_moe_down, gemm_list)
    y_p, y_s = _moe_combine(y_sorted, dest_rows, gates, x3, mods, **joint)

    y_prompt = y_p.reshape(b_ctx, l_ctx, d)
    y_sample = y_s.reshape(b_dec, l_dec, d)
    state_k = k_p.reshape(b_ctx, 1, l_ctx, heads, v_dim)
    state_v = v_p.reshape(b_ctx, 1, l_ctx, heads, v_dim)
    return (y_prompt, y_sample, state_k, state_v)
```

```python
import functools
import math

import jax
import jax.numpy as jnp
from jax import lax
from jax.experimental import pallas as pl
from jax.experimental.pallas import tpu as pltpu

F32 = jnp.float32
BF16 = jnp.bfloat16

GRID_W = 64
ROPE_BASE = 10000.0
POOL_WINDOWS = (2, 4, 8, 16)
TOP_K = 2
EPS = 1e-6
LOG2E = 1.4426950408889634

LANES = 128
VMEM_LIMIT = 52 * 2**20

ROUTE_CHUNK = 256
GEMM_ROWS = 1024


def _lambda_init(layer):
    return 0.8 - 0.6 * math.exp(-0.3 * layer)


def _tile(n, pref, mult):
    best = None
    t = mult
    while t <= min(n, pref):
        if n % t == 0:
            best = t
        t += mult
    if best is None:
        raise ValueError(f"no tile for {n} (multiple of {mult}, <= {pref})")
    return best


def _params(semantics):
    return pltpu.CompilerParams(dimension_semantics=semantics, vmem_limit_bytes=VMEM_LIMIT)


def _dot(a, b):
    return jnp.dot(a, b, preferred_element_type=F32)


def _mask_bf16(m):
    return jnp.where(m, 1.0, 0.0).astype(BF16)


def _split_bf16(x):
    hi = x.astype(BF16)
    lo = (x - hi.astype(F32)).astype(BF16)
    return hi, lo


def _norm_modulate(x, g, scale, shift):
    ms = jnp.mean(x * x, axis=-1, keepdims=True)
    return x * lax.rsqrt(ms + EPS) * (g * (1.0 + scale)) + shift


def _adaln_kernel(cond_ref, w_ref, b_ref, o_ref):
    c = cond_ref[...]
    s = c * jax.nn.sigmoid(c)
    s_hi, s_lo = _split_bf16(s)
    w_hi, w_lo = _split_bf16(w_ref[...])
    o_ref[...] = _dot(s_hi, w_hi) + _dot(s_lo, w_hi) + _dot(s_hi, w_lo) + b_ref[...]


def _adaln_mods(cond, ada_w, ada_b):
    depth, d, n = ada_w.shape
    rows = cond.shape[0]
    tn = _tile(n, 512, LANES)
    return pl.pallas_call(
        _adaln_kernel,
        grid=(depth, n // tn),
        in_specs=[
            pl.BlockSpec((rows, d), lambda l, j: (0, 0)),
            pl.BlockSpec((None, d, tn), lambda l, j: (l, 0, j)),
            pl.BlockSpec((None, 1, tn), lambda l, j: (l, 0, j)),
        ],
        out_specs=pl.BlockSpec((None, rows, tn), lambda l, j: (l, 0, j)),
        out_shape=jax.ShapeDtypeStruct((depth, rows, n), F32),
        compiler_params=_params(("arbitrary", "arbitrary")),
        name="adaln_mods",
    )(cond, ada_w, ada_b.reshape(depth, 1, n))


class _Mods:
    SHIFT1, SCALE1, GATE1, SHIFT2, SCALE2, GATE2 = range(6)

    def __init__(self, table, rows, layer):
        self.table = table
        self.rows = rows
        self.layer = layer

    def spec(self, which, row_fn, width=None, col_fn=None):
        d = self.table.shape[-1]
        width = d if width is None else width
        base = self.layer * self.rows

        def index(*ids):
            col = 0 if col_fn is None else col_fn(*ids)
            return ((base + row_fn(*ids)) * 6 + which, 0, col)

        return pl.BlockSpec((None, 1, width), index)


def _qkv_kernel(*refs, rope, tn, n_sec):
    x_ref, sh_ref, sc_ref, g_ref, w_ref, qg_ref, kg_ref = refs[:7]
    rest = list(refs[7:])
    cos_ref, sin_ref = (rest.pop(0), rest.pop(0)) if rope else (None, None)
    q_ref, k_ref, v_ref, h_scr = rest
    j = pl.program_id(1)

    @pl.when(j == 0)
    def _():
        h = _norm_modulate(x_ref[...], g_ref[...], sc_ref[...], sh_ref[...])
        h_scr[...] = h.astype(BF16)

    def qk_epilogue(gain_ref, out_ref):
        acc = _dot(h_scr[...], w_ref[...])
        r_idx = lax.broadcasted_iota(jnp.int32, (2 * LANES, 2 * LANES), 0)
        c_idx = lax.broadcasted_iota(jnp.int32, (2 * LANES, 2 * LANES), 1)
        if rope:
            swap = (r_idx >= LANES) & (c_idx >= LANES) & (r_idx - LANES == jnp.bitwise_xor(c_idx - LANES, LANES // 4))
            rhs = _mask_bf16(((r_idx < LANES) & (c_idx < LANES)) | swap)
        else:
            rhs = jnp.ones((2 * LANES, LANES), BF16)
        for g in range(tn // LANES):
            y = acc[:, g * LANES:(g + 1) * LANES]
            yg = y * gain_ref[...]
            if rope:
                res = _dot(jnp.concatenate([(y * y).astype(BF16), yg.astype(BF16)], axis=1), rhs)
                inv = lax.rsqrt(res[:, :LANES] * (1.0 / LANES) + EPS)
                out = (yg * cos_ref[...] + res[:, LANES:] * sin_ref[...]) * inv
            else:
                ss = _dot(jnp.concatenate(_split_bf16(y * y), axis=1), rhs)
                out = yg * lax.rsqrt(ss * (1.0 / LANES) + EPS)
            out_ref[:, g * LANES:(g + 1) * LANES] = out.astype(out_ref.dtype)

    @pl.when(j < n_sec)
    def _():
        qk_epilogue(qg_ref, q_ref)

    @pl.when((j >= n_sec) & (j < 2 * n_sec))
    def _():
        qk_epilogue(kg_ref, k_ref)

    @pl.when(j >= 2 * n_sec)
    def _():
        v_ref[...] = _dot(h_scr[...], w_ref[...]).astype(v_ref.dtype)


def _qkv_proj(x, mods, norm_g, w, q_gain, k_gain, tables, *, cond_base, tok_per_cond, kv_dtype):
    t, d = x.shape
    tm = _tile(math.gcd(t, tok_per_cond), 1024, 8)
    tn = _tile(d, 512, LANES)
    n_sec = d // tn
    rope = tables is not None
    row = lambda i, j: cond_base + (i * tm) // tok_per_cond
    in_specs = [
        pl.BlockSpec((tm, d), lambda i, j: (i, 0)),
        mods.spec(_Mods.SHIFT1, row),
        mods.spec(_Mods.SCALE1, row),
        pl.BlockSpec((1, d), lambda i, j: (0, 0)),
        pl.BlockSpec((d, tn), lambda i, j: (0, j)),
        pl.BlockSpec((1, LANES), lambda i, j: (0, 0)),
        pl.BlockSpec((1, LANES), lambda i, j: (0, 0)),
    ]
    args = [x, mods.table, mods.table, norm_g, w, q_gain, k_gain]
    if rope:
        nblk = tables[0].shape[0] // tm
        for tab in tables:
            in_specs.append(pl.BlockSpec((tm, LANES), lambda i, j: (i % nblk, 0)))
            args.append(tab)
    out_specs = [
        pl.BlockSpec((tm, tn), lambda i, j, s=s: (i, jnp.clip(j - s * n_sec, 0, n_sec - 1)))
        for s in range(3)
    ]
    return pl.pallas_call(
        functools.partial(_qkv_kernel, rope=rope, tn=tn, n_sec=n_sec),
        grid=(t // tm, 3 * n_sec),
        in_specs=in_specs,
        out_specs=out_specs,
        out_shape=[jax.ShapeDtypeStruct((t, d), BF16), jax.ShapeDtypeStruct((t, d), kv_dtype),
                   jax.ShapeDtypeStruct((t, d), kv_dtype)],
        scratch_shapes=[pltpu.VMEM((tm, d), BF16)],
        compiler_params=_params(("arbitrary", "arbitrary")),
        name="qkv_proj_rope" if rope else "qkv_proj",
    )(*args)


def _rope_tables(seq):
    n_freq = LANES // 4
    pos = jnp.arange(seq, dtype=jnp.int32)
    r = (pos // GRID_W).astype(F32)
    col = (pos % GRID_W).astype(F32)
    freqs = ROPE_BASE ** (-jnp.arange(n_freq, dtype=F32) / n_freq)
    cos_t, sin_t = [], []
    for p in (r, col):
        ang = p[:, None] * freqs
        cos, sin = jnp.cos(ang), jnp.sin(ang)
        cos_t += [cos, cos]
        sin_t += [-sin, sin]
    return jnp.concatenate(cos_t, axis=1), jnp.concatenate(sin_t, axis=1)


def _attn_kernel(*refs, n_cache, n_chunks, tk, lambda_init, cast_blocks, heads_per_step):
    n_cast = len(cast_blocks)
    lam_ref, g_ref, q_ref = refs[:3]
    n_in = 3 + (4 if n_cache else 2)
    if n_cache:
        kc_ref, vtc_ref, k_ref, vt_ref = refs[3:n_in]
    else:
        k_ref, vt_ref = refs[3:n_in]
    cast_in = refs[n_in:n_in + n_cast]
    o_ref = refs[n_in + n_cast]
    cast_out = refs[n_in + n_cast + 1:n_in + 2 * n_cast + 1]
    acc_ref = refs[-1]

    step = (pl.program_id(0) * pl.num_programs(1) + pl.program_id(1)) * pl.num_programs(2) + pl.program_id(2)
    for src, dst, n_blocks in zip(cast_in, cast_out, cast_blocks):
        @pl.when(step < n_blocks)
        def _(src=src, dst=dst):
            dst[...] = src[...].astype(dst.dtype)

    vd = 2 * LANES
    for hq in range(heads_per_step):
        cols = slice(hq * vd, (hq + 1) * vd)
        _attn_head(lam_ref, g_ref, q_ref.at[:, cols],
                   kc_ref.at[:, cols] if n_cache else None, vtc_ref.at[hq] if n_cache else None,
                   k_ref.at[:, cols], vt_ref.at[hq], o_ref.at[:, cols], acc_ref,
                   n_chunks=n_chunks, tk=tk, lambda_init=lambda_init)


def _attn_head(lam_ref, g_ref, q_ref, kc_ref, vtc_ref, k_ref, vt_ref, o_ref, acc_ref, *,
               n_chunks, tk, lambda_init):
    q = q_ref[...]
    qs = (q[:, :LANES], q[:, LANES:])

    def scores(kblk):
        return [lax.dot_general(kblk[:, c * LANES:(c + 1) * LANES], qs[c],
                                (((1,), (1,)), ((), ())), preferred_element_type=F32)
                for c in range(2)]

    def accumulate(sts, vt, stats):
        out = []
        for c in range(2):
            st = sts[c]
            m_cur = jnp.max(st, axis=0, keepdims=True)
            if stats is None:
                m_new = m_cur
                p = jnp.exp2(st - m_new)
                l_new = jnp.sum(p, axis=0, keepdims=True)
                acc_ref[c] = _dot(vt, p.astype(BF16))
            else:
                m_prev, l_prev = stats[c]
                m_new = jnp.maximum(m_prev, m_cur)
                alpha = jnp.exp2(m_prev - m_new)
                p = jnp.exp2(st - m_new)
                l_new = alpha * l_prev + jnp.sum(p, axis=0, keepdims=True)
                acc_ref[c] = alpha * acc_ref[c] + _dot(vt, p.astype(BF16))
            out.append((m_new, l_new))
        return out

    chunks = []
    if kc_ref is not None:
        chunks.append((lambda: kc_ref[...].astype(BF16), lambda: vtc_ref[...]))
    for i in range(n_chunks):
        chunks.append((lambda i=i: k_ref[i * tk:(i + 1) * tk, :].astype(BF16), lambda i=i: vt_ref[i]))
    stats = None
    sts = scores(chunks[0][0]())
    for n, (_, values) in enumerate(chunks):
        nxt = scores(chunks[n + 1][0]()) if n + 1 < len(chunks) else None
        stats = accumulate(sts, values(), stats)
        sts = nxt

    lp = lam_ref[...]
    lam = (jnp.exp(jnp.sum(lp[0:1] * lp[1:2], axis=-1, keepdims=True))
           - jnp.exp(jnp.sum(lp[2:3] * lp[3:4], axis=-1, keepdims=True)) + lambda_init)
    o_t = acc_ref[0] / stats[0][1] - lam * (acc_ref[1] / stats[1][1])
    o = o_t.T
    o = o * lax.rsqrt(jnp.mean(o * o, axis=-1, keepdims=True) + EPS) * g_ref[...]
    o_ref[...] = (o * (1.0 - lambda_init)).astype(o_ref.dtype)


def _chunked_transpose(v, batch, seq, heads, vd, tk):
    v = v.astype(BF16).reshape(batch, seq // tk, tk, heads, vd)
    return v.transpose(0, 3, 1, 4, 2).reshape(batch * heads, seq // tk, vd, tk)


def _cast_block_rows(rows, n_steps):
    blk = 16
    while rows % blk or rows // blk > n_steps:
        blk += 16
        if blk > rows:
            raise ValueError(f"no cast block for {rows} rows in {n_steps} steps")
    return blk


def _diff_attention(q, k, v, cache_k, cache_v, lam_params, subln_g, *, batch, seq, layer, casts=()):
    t, d = q.shape
    vd = subln_g.shape[-1]
    heads = d // vd
    n_cache = 0 if cache_k is None else cache_k.shape[0] // batch
    tq = _tile(seq, 512, LANES)
    tk = _tile(seq, 512, LANES)
    nq = seq // tq
    n_chunks = seq // tk
    hps = heads if seq <= tq else 2
    groups = heads // hps
    wide = hps * vd
    in_specs = [
        pl.BlockSpec(lam_params.shape, lambda b, h, i: (0, 0)),
        pl.BlockSpec((1, vd), lambda b, h, i: (0, 0)),
        pl.BlockSpec((tq, wide), lambda b, h, i: (b * nq + i, h)),
    ]
    args = [lam_params, subln_g, q]
    if n_cache:
        in_specs += [pl.BlockSpec((n_cache, wide), lambda b, h, i: (b, h)),
                     pl.BlockSpec((hps, None, vd, n_cache), lambda b, h, i: (b * groups + h, 0, 0, 0))]
        args += [cache_k, _chunked_transpose(cache_v, batch, n_cache, heads, vd, n_cache)]
    in_specs += [pl.BlockSpec((seq, wide), lambda b, h, i: (b, h)),
                 pl.BlockSpec((hps, n_chunks, vd, tk), lambda b, h, i: (b * groups + h, 0, 0, 0))]
    args += [k, _chunked_transpose(v, batch, seq, heads, vd, tk)]
    out_specs = [pl.BlockSpec((tq, wide), lambda b, h, i: (b * nq + i, h))]
    out_shape = [jax.ShapeDtypeStruct((t, d), BF16)]
    n_steps = batch * groups * nq
    cast_blocks = []
    for w in casts:
        rows, cols = w.shape
        blk = _cast_block_rows(rows, n_steps)
        n_blocks = rows // blk
        cast_blocks.append(n_blocks)
        spec = pl.BlockSpec((blk, cols), lambda b, h, i, n=n_blocks: (
            jnp.minimum((b * groups + h) * nq + i, n - 1), 0))
        in_specs.append(spec)
        args.append(w)
        out_specs.append(spec)
        out_shape.append(jax.ShapeDtypeStruct((rows, cols), BF16))
    res = pl.pallas_call(
        functools.partial(_attn_kernel, n_cache=n_cache, n_chunks=n_chunks, tk=tk,
                          lambda_init=_lambda_init(layer), cast_blocks=tuple(cast_blocks),
                          heads_per_step=hps),
        grid=(batch, groups, nq),
        in_specs=in_specs,
        out_specs=out_specs,
        out_shape=out_shape,
        scratch_shapes=[pltpu.VMEM((2, vd, tq), F32)],
        compiler_params=_params(("arbitrary", "arbitrary", "arbitrary")),
        name=f"diff_attention_{'latent' if n_cache else 'context'}",
    )(*args)
    return res[0], list(res[1:])


def _out_proj_kernel(op_ref, os_ref, w_ref, xp_ref, xs_ref, gate_ref, out_ref, *, n_prompt_tiles):
    is_prompt = pl.program_id(0) < n_prompt_tiles

    @pl.when(is_prompt)
    def _():
        out_ref[...] = xp_ref[...] + gate_ref[...] * _dot(op_ref[...], w_ref[...])

    @pl.when(jnp.logical_not(is_prompt))
    def _():
        out_ref[...] = xs_ref[...] + gate_ref[...] * _dot(os_ref[...], w_ref[...])


def _out_proj_residual(o_p, o_s, w, x_p, x_s, mods, *, tok_per_cond):
    n_prompt, d = x_p.shape
    t = n_prompt + x_s.shape[0]
    tm = _tile(math.gcd(n_prompt, tok_per_cond), 1024, 8)
    tn = _tile(d, 512, LANES)
    n_pt = n_prompt // tm
    row = _cond_row_joint(tm, n_prompt, tok_per_cond)
    p_rows = lambda i: jnp.minimum(i, n_pt - 1)
    s_rows = lambda i: jnp.maximum(i - n_pt, 0)
    return pl.pallas_call(
        functools.partial(_out_proj_kernel, n_prompt_tiles=n_pt),
        grid=(t // tm, d // tn),
        in_specs=[
            pl.BlockSpec((tm, d), lambda i, j: (p_rows(i), 0)),
            pl.BlockSpec((tm, d), lambda i, j: (s_rows(i), 0)),
            pl.BlockSpec((d, tn), lambda i, j: (0, j)),
            pl.BlockSpec((tm, tn), lambda i, j: (p_rows(i), jnp.where(i < n_pt, j, d // tn - 1))),
            pl.BlockSpec((tm, tn), lambda i, j: (s_rows(i), jnp.where(i < n_pt, 0, j))),
            mods.spec(_Mods.GATE1, row, width=tn, col_fn=lambda i, j: j),
        ],
        out_specs=pl.BlockSpec((tm, tn), lambda i, j: (i, j)),
        out_shape=jax.ShapeDtypeStruct((t, d), F32),
        compiler_params=_params(("arbitrary", "arbitrary")),
        name="attn_out_proj",
    )(o_p, o_s, w, x_p, x_s, mods.table)


def _ffn_kernel(x_ref, sh_ref, sc_ref, gate_ref, g_ref, wg_ref, wu_ref, wd_ref, o_ref, h_scr):
    f = pl.program_id(1)

    @pl.when(f == 0)
    def _():
        h = _norm_modulate(x_ref[...], g_ref[...], sc_ref[...], sh_ref[...])
        h_scr[...] = h.astype(BF16)
        o_ref[...] = jnp.zeros(o_ref.shape, F32)

    h = h_scr[...]
    gte = _dot(h, wg_ref[...])
    up = _dot(h, wu_ref[...])
    act = (gte * jax.nn.sigmoid(gte) * up).astype(BF16)
    o_ref[...] += _dot(act, wd_ref[...])

    @pl.when(f == pl.num_programs(1) - 1)
    def _():
        o_ref[...] = x_ref[...] + gate_ref[...] * o_ref[...]


def _cond_row_joint(tm, n_prompt, tok_per_cond):
    def row(i, *_):
        tok = i * tm
        return jnp.where(tok < n_prompt, 0, 1 + (tok - n_prompt) // tok_per_cond)
    return row


def _dense_ffn(x, mods, norm_g, w_gu, w_down, *, n_prompt, tok_per_cond):
    t, d = x.shape
    ff = w_down.shape[0]
    tm = _tile(math.gcd(n_prompt, tok_per_cond), 1024, 8)
    tf = _tile(ff, 256, LANES)
    nf = ff // tf
    row = _cond_row_joint(tm, n_prompt, tok_per_cond)
    return pl.pallas_call(
        _ffn_kernel,
        grid=(t // tm, nf),
        in_specs=[
            pl.BlockSpec((tm, d), lambda i, f: (i, 0)),
            mods.spec(_Mods.SHIFT2, row),
            mods.spec(_Mods.SCALE2, row),
            mods.spec(_Mods.GATE2, row),
            pl.BlockSpec((1, d), lambda i, f: (0, 0)),
            pl.BlockSpec((d, tf), lambda i, f: (0, f)),
            pl.BlockSpec((d, tf), lambda i, f: (0, nf + f)),
            pl.BlockSpec((tf, d), lambda i, f: (f, 0)),
        ],
        out_specs=pl.BlockSpec((tm, d), lambda i, f: (i, 0)),
        out_shape=jax.ShapeDtypeStruct((t, d), F32),
        scratch_shapes=[pltpu.VMEM((tm, d), BF16)],
        compiler_params=_params(("arbitrary", "arbitrary")),
        name="dense_ffn",
    )(x, mods.table, mods.table, mods.table, norm_g, w_gu, w_gu, w_down)


def _pool_kernel(x_ref, xp_ref, xn_ref, sh_ref, sc_ref, gate_ref, g_ref, w_ref, ps_ref, o_ref, *,
                 tp, halo, n_prompt_tiles, prompt_tiles_per_seq, sample_tiles_per_seq, group_dim):
    i = pl.program_id(0)
    is_prompt = i < n_prompt_tiles
    tiles_per_seq = jnp.where(is_prompt, prompt_tiles_per_seq, sample_tiles_per_seq)
    local = lax.rem(jnp.where(is_prompt, i, i - n_prompt_tiles), tiles_per_seq)
    first = local == 0
    last = local == tiles_per_seq - 1
    seq_len = tiles_per_seq * tp

    g, sc, sh = g_ref[...], sc_ref[...], sh_ref[...]
    x = x_ref[...]
    h = _norm_modulate(x, g, sc, sh)
    h_prev = jnp.where(first, 0.0, _norm_modulate(xp_ref[...], g, sc, sh))
    h_next = jnp.where(last, 0.0, _norm_modulate(xn_ref[...], g, sc, sh))
    d = h.shape[-1]
    h_halo = jnp.concatenate([h_prev, h_next, jnp.zeros((LANES - 2 * halo, d), F32)], axis=0)

    t_idx = lax.broadcasted_iota(jnp.int32, (tp, tp), 0)
    s_idx = lax.broadcasted_iota(jnp.int32, (tp, tp), 1)
    t_h = lax.broadcasted_iota(jnp.int32, (tp, LANES), 0)
    u_h = lax.broadcasted_iota(jnp.int32, (tp, LANES), 1)
    off_h = jnp.where(u_h < halo, u_h - halo, tp + u_h - halo)
    pos = local * tp + lax.broadcasted_iota(jnp.int32, (tp, 1), 0)

    for grp, win in enumerate(POOL_WINDOWS):
        lo, hi = win // 2, win // 2 - 1
        cols = slice(grp * group_dim, (grp + 1) * group_dim)
        band = _mask_bf16((s_idx >= t_idx - lo) & (s_idx <= t_idx + hi))
        band_h = _mask_bf16((u_h < 2 * halo) & (off_h >= t_h - lo) & (off_h <= t_h + hi))
        total = _dot(band, h[:, cols].astype(BF16)) + _dot(band_h, h_halo[:, cols].astype(BF16))
        cnt = (jnp.minimum(pos + hi, seq_len - 1) - jnp.maximum(pos - lo, 0) + 1).astype(F32)
        pooled = (total / cnt - h[:, cols]).astype(BF16)
        y = _dot(pooled, w_ref[grp]) * ps_ref[:, cols]
        o_ref[:, cols] = x[:, cols] + gate_ref[:, cols] * y


def _pool_mixer(x, mods, norm_g, pool_w, pool_scale, *, n_prompt, prompt_seq, sample_seq):
    t, d = x.shape
    halo = max(POOL_WINDOWS) // 2
    tp = _tile(math.gcd(prompt_seq, sample_seq), 256, 8)
    n_groups, group_dim = pool_w.shape[0], pool_w.shape[1]
    hb = tp // halo
    n_halo_blocks = t // halo
    row = _cond_row_joint(tp, n_prompt, sample_seq)
    return pl.pallas_call(
        functools.partial(_pool_kernel, tp=tp, halo=halo, n_prompt_tiles=n_prompt // tp,
                          prompt_tiles_per_seq=prompt_seq // tp,
                          sample_tiles_per_seq=sample_seq // tp, group_dim=group_dim),
        grid=(t // tp,),
        in_specs=[
            pl.BlockSpec((tp, d), lambda i: (i, 0)),
            pl.BlockSpec((halo, d), lambda i: (jnp.maximum(i * hb - 1, 0), 0)),
            pl.BlockSpec((halo, d), lambda i: (jnp.minimum((i + 1) * hb, n_halo_blocks - 1), 0)),
            mods.spec(_Mods.SHIFT1, row),
            mods.spec(_Mods.SCALE1, row),
            mods.spec(_Mods.GATE1, row),
            pl.BlockSpec((1, d), lambda i: (0, 0)),
            pl.BlockSpec((n_groups, group_dim, group_dim), lambda i: (0, 0, 0)),
            pl.BlockSpec((1, d), lambda i: (0, 0)),
        ],
        out_specs=pl.BlockSpec((tp, d), lambda i: (i, 0)),
        out_shape=jax.ShapeDtypeStruct((t, d), F32),
        compiler_params=_params(("arbitrary",)),
        name="pool_mixer",
    )(x, x, x, mods.table, mods.table, mods.table, norm_g, pool_w, pool_scale)


def _router_kernel(x_ref, sh_ref, sc_ref, g_ref, wr_ref, br_ref, route_ref, cnt_ref):
    h = _norm_modulate(x_ref[...], g_ref[...], sc_ref[...], sh_ref[...])
    h_hi, h_lo = _split_bf16(h)
    w_hi, w_lo = _split_bf16(wr_ref[...])
    logits = _dot(h_hi, w_hi) + _dot(h_lo, w_hi) + _dot(h_hi, w_lo) + br_ref[...]

    rows = logits.shape[0]
    lane = lax.broadcasted_iota(jnp.int32, logits.shape, 1).astype(F32)
    m1 = jnp.max(logits, axis=-1, keepdims=True)
    i1 = jnp.min(jnp.where(logits == m1, lane, float(LANES)), axis=-1, keepdims=True)
    oh1 = lane == i1
    rest = jnp.where(oh1, -jnp.inf, logits)
    m2 = jnp.max(rest, axis=-1, keepdims=True)
    i2 = jnp.min(jnp.where(rest == m2, lane, float(LANES)), axis=-1, keepdims=True)
    oh2 = lane == i2
    e = jnp.exp(m2 - m1)
    gate_a = 1.0 / (1.0 + e)
    gate_b = e / (1.0 + e)

    sel = jnp.where(oh1 | oh2, 1.0, 0.0)
    r_idx = lax.broadcasted_iota(jnp.int32, (rows, rows), 0)
    c_idx = lax.broadcasted_iota(jnp.int32, (rows, rows), 1)
    earlier = _mask_bf16(c_idx < r_idx)
    rank = _dot(earlier, sel.astype(BF16))
    rank_a = jnp.sum(jnp.where(oh1, rank, 0.0), axis=-1, keepdims=True)
    rank_b = jnp.sum(jnp.where(oh2, rank, 0.0), axis=-1, keepdims=True)

    route = jnp.zeros(logits.shape, F32)
    for k, val in enumerate((i1, i2, rank_a, rank_b, gate_a, gate_b)):
        route = jnp.where(lane == float(k), val, route)
    route_ref[...] = route
    cnt_ref[...] = jnp.sum(sel, axis=0, keepdims=True)


def _moe_router(x, mods, norm_g, w_router, b_router, *, n_prompt, tok_per_cond):
    t, d = x.shape
    n_exp = w_router.shape[1]
    ch = ROUTE_CHUNK
    wr = jnp.zeros((d, LANES), F32).at[:, :n_exp].set(w_router.astype(F32))
    br = jnp.full((1, LANES), -jnp.inf, F32).at[0, :n_exp].set(b_router.astype(F32))
    row = _cond_row_joint(ch, n_prompt, tok_per_cond)
    return pl.pallas_call(
        _router_kernel,
        grid=(t // ch,),
        in_specs=[
            pl.BlockSpec((ch, d), lambda i: (i, 0)),
            mods.spec(_Mods.SHIFT2, row),
            mods.spec(_Mods.SCALE2, row),
            pl.BlockSpec((1, d), lambda i: (0, 0)),
            pl.BlockSpec((d, LANES), lambda i: (0, 0)),
            pl.BlockSpec((1, LANES), lambda i: (0, 0)),
        ],
        out_specs=[
            pl.BlockSpec((ch, LANES), lambda i: (i, 0)),
            pl.BlockSpec((None, 1, LANES), lambda i: (i, 0, 0)),
        ],
        out_shape=[
            jax.ShapeDtypeStruct((t, LANES), F32),
            jax.ShapeDtypeStruct((t // ch, 1, LANES), F32),
        ],
        compiler_params=_params(("arbitrary",)),
        name="moe_router",
    )(x, mods.table, mods.table, norm_g, wr, br)


def _dispatch_plan(route, counts, n_exp):
    ch, gr = ROUTE_CHUNK, GEMM_ROWS
    t = route.shape[0]
    n_chunks = t // ch
    i32 = jnp.int32
    cnt = counts[:, 0, :n_exp].astype(i32)
    total = cnt.sum(0)
    padded = ((total + gr - 1) // gr) * gr
    off = jnp.cumsum(padded) - padded
    start = off[None, :] + jnp.cumsum(cnt, axis=0) - cnt

    def dest(choice):
        e = route[:, choice].astype(i32).reshape(n_chunks, ch, 1)
        hit = e == jnp.arange(n_exp, dtype=i32)
        base = jnp.sum(jnp.where(hit, start[:, None, :], 0), axis=-1)
        return base.reshape(t) + route[:, 2 + choice].astype(i32)

    dest_a, dest_b = dest(0), dest(1)

    n_gemm_tiles = (t * TOP_K) // gr + n_exp

    tile_lo = jnp.arange(n_gemm_tiles, dtype=i32) * gr
    used = tile_lo < (off + padded)[-1]
    ends = off + padded
    expert_of = jnp.minimum(jnp.sum(tile_lo[:, None] >= ends[None, :], axis=1), n_exp - 1).astype(i32)
    n_used = jnp.maximum(used.sum(), 1)
    clamp = jnp.minimum(jnp.arange(n_gemm_tiles, dtype=i32), n_used - 1).astype(i32)
    token_rows = jnp.clip((off + total)[expert_of] - tile_lo, 0, gr).astype(i32)
    gemm_list = (clamp, expert_of[clamp], used.astype(i32), token_rows[clamp])

    tail = jnp.arange(n_gemm_tiles - n_exp, n_gemm_tiles, dtype=i32)
    zero_rows = jnp.concatenate([jnp.maximum(ends - gr, 0), tail * gr]).astype(i32)
    zero_ok = jnp.concatenate([padded > 0, tail >= used.sum()]).astype(i32)

    dest_rows = jnp.stack([dest_a.reshape(n_chunks, ch), dest_b.reshape(n_chunks, ch)], axis=1)
    gates = jnp.concatenate([route[:, 4:6], jnp.zeros((t, 6), F32)], axis=1)
    return dest_rows, gates, gemm_list, (zero_rows, zero_ok), n_gemm_tiles * gr


def _wait_rows(src, dst, sem, n_rows):
    pltpu.make_async_copy(src.at[pl.ds(0, n_rows), :], dst.at[pl.ds(0, n_rows), :], sem).wait()


def _scatter_kernel(zrow_ref, zok_ref, dest_ref, x_ref, sh_ref, sc_ref, g_ref, out_hbm,
                    h_scr, z_scr, sem, zsem, *, tile_rows):
    ch = x_ref.shape[0]
    step, n_steps = pl.program_id(0), pl.num_programs(0)
    slot = lax.rem(step, 2)

    @pl.when(step == 0)
    def _():
        z_scr[...] = jnp.zeros(z_scr.shape, F32)
        zr = z_scr.shape[0]

        def zero_copy(n, part):
            row0 = pl.multiple_of(zrow_ref[n] + part * zr, zr)
            return pltpu.make_async_copy(z_scr, out_hbm.at[pl.ds(row0, zr), :], zsem)

        for n in range(zrow_ref.shape[0]):
            @pl.when(zok_ref[n] == 1)
            def _(n=n):
                for part in range(tile_rows // zr):
                    zero_copy(n, part).start()
        for n in range(zrow_ref.shape[0]):
            @pl.when(zok_ref[n] == 1)
            def _(n=n):
                for part in range(tile_rows // zr):
                    zero_copy(n, part).wait()

    h_scr[slot] = _norm_modulate(x_ref[...], g_ref[...], sc_ref[...], sh_ref[...])

    src_rows = h_scr.at[slot]
    for j in range(ch):
        for k in range(TOP_K):
            pltpu.make_async_copy(src_rows.at[pl.ds(j, 1), :],
                                  out_hbm.at[pl.ds(dest_ref[k, j], 1), :],
                                  sem.at[slot]).start(priority=k)

    def wait_slot(s):
        for _ in range(TOP_K):
            _wait_rows(h_scr.at[s], out_hbm, sem.at[s], ch)

    @pl.when(step > 0)
    def _():
        wait_slot(1 - slot)

    @pl.when(step == n_steps - 1)
    def _():
        wait_slot(slot)


def _moe_scatter(x, mods, norm_g, dest_rows, zero_tiles, n_rows, *, n_prompt, tok_per_cond):
    t, d = x.shape
    ch = ROUTE_CHUNK
    row = _cond_row_joint(ch, n_prompt, tok_per_cond)
    grid_spec = pltpu.PrefetchScalarGridSpec(
        num_scalar_prefetch=2,
        grid=(t // ch,),
        in_specs=[
            pl.BlockSpec((None, TOP_K, ch), lambda i, *_: (i, 0, 0), memory_space=pltpu.SMEM),
            pl.BlockSpec((ch, d), lambda i, *_: (i, 0)),
            mods.spec(_Mods.SHIFT2, row),
            mods.spec(_Mods.SCALE2, row),
            pl.BlockSpec((1, d), lambda i, *_: (0, 0)),
        ],
        out_specs=pl.BlockSpec(memory_space=pl.ANY),
        scratch_shapes=[pltpu.VMEM((2, ch, d), F32), pltpu.VMEM((ch, d), F32),
                        pltpu.SemaphoreType.DMA((2,)), pltpu.SemaphoreType.DMA(())],
    )
    return pl.pallas_call(
        functools.partial(_scatter_kernel, tile_rows=GEMM_ROWS),
        grid_spec=grid_spec,
        out_shape=jax.ShapeDtypeStruct((n_rows, d), F32),
        compiler_params=_params(("arbitrary",)),
        name="moe_scatter",
    )(*zero_tiles, dest_rows, x, mods.table, mods.table, norm_g)


def _expert_kernel(blk_ref, exp_ref, used_ref, rows_ref, x_ref, wg_ref, wu_ref, wd_ref, o_ref, x_scr, *,
                   sub_rows):
    r, f = pl.program_id(0), pl.program_id(1)
    tile_rows = o_ref.shape[0]

    def swiglu(rows):
        x = x_scr[rows, :]
        gte = _dot(x, wg_ref[...])
        up = _dot(x, wu_ref[...])
        act = (gte * jax.nn.sigmoid(gte) * up).astype(BF16)
        o_ref[rows, :] += _dot(act, wd_ref[...])

    @pl.when(used_ref[r] == 1)
    def _():
        @pl.when(f == 0)
        def _():
            o_ref[...] = jnp.zeros(o_ref.shape, F32)
            x_scr[...] = x_ref[...].astype(BF16)

        valid = rows_ref[r]

        @pl.when(valid > tile_rows - sub_rows)
        def _():
            swiglu(slice(None))

        @pl.when(valid <= tile_rows - sub_rows)
        def _():
            for sb in range(tile_rows // sub_rows - 1):
                @pl.when(valid > sb * sub_rows)
                def _(sb=sb):
                    swiglu(slice(sb * sub_rows, (sb + 1) * sub_rows))


def _moe_experts(xs, w_gu, w_down, gemm_list):
    n_rows, d = xs.shape
    ff = w_down.shape[1]
    gr = GEMM_ROWS
    tf = _tile(ff, 256, LANES)
    nf = ff // tf
    n_tiles = gemm_list[0].shape[0]

    def fcol(f, used, r):
        return jnp.where(used[r] == 1, f, nf - 1)

    grid_spec = pltpu.PrefetchScalarGridSpec(
        num_scalar_prefetch=4,
        grid=(n_tiles, nf),
        in_specs=[
            pl.BlockSpec((gr, d), lambda r, f, blk, ex, used, rows: (blk[r], 0)),
            pl.BlockSpec((None, d, tf), lambda r, f, blk, ex, used, rows: (ex[r], 0, fcol(f, used, r))),
            pl.BlockSpec((None, d, tf), lambda r, f, blk, ex, used, rows: (ex[r], 0, nf + fcol(f, used, r))),
            pl.BlockSpec((None, tf, d), lambda r, f, blk, ex, used, rows: (ex[r], fcol(f, used, r), 0)),
        ],
        out_specs=pl.BlockSpec((gr, d), lambda r, f, blk, ex, used, rows: (blk[r], 0)),
        scratch_shapes=[pltpu.VMEM((gr, d), BF16)],
    )
    return pl.pallas_call(
        functools.partial(_expert_kernel, sub_rows=ROUTE_CHUNK),
        grid_spec=grid_spec,
        out_shape=jax.ShapeDtypeStruct((n_rows, d), F32),
        input_output_aliases={4: 0},
        compiler_params=_params(("arbitrary", "arbitrary")),
        name="moe_experts",
    )(*gemm_list, xs, w_gu, w_gu, w_down)


def _combine_kernel(dest_ref, dest_next_ref, gates_ref, x_ref, gate_ref, y_hbm, op_ref, os_ref,
                    y_scr, sem, *, n_prompt_chunks):
    ch = x_ref.shape[0]
    step, n_steps = pl.program_id(0), pl.num_programs(0)
    slot = lax.rem(step, 2)

    def gather(idx_ref, s):
        for j in range(ch):
            for k in range(TOP_K):
                pltpu.make_async_copy(y_hbm.at[pl.ds(idx_ref[k, j], 1), :],
                                      y_scr.at[s, k, pl.ds(j, 1), :], sem.at[s]).start(priority=k)

    @pl.when(step == 0)
    def _():
        gather(dest_ref, slot)

    @pl.when(step + 1 < n_steps)
    def _():
        gather(dest_next_ref, 1 - slot)

    for k in range(TOP_K):
        _wait_rows(y_hbm, y_scr.at[slot, k], sem.at[slot], ch)

    gates = gates_ref[...]
    mix = gates[:, 0:1] * y_scr[slot, 0] + gates[:, 1:2] * y_scr[slot, 1]
    out = x_ref[...] + gate_ref[...] * mix
    is_prompt = step < n_prompt_chunks

    @pl.when(is_prompt)
    def _():
        op_ref[...] = out

    @pl.when(jnp.logical_not(is_prompt))
    def _():
        os_ref[...] = out


def _moe_combine(y, dest_rows, gates, x, mods, *, n_prompt, tok_per_cond):
    t, d = x.shape
    ch = ROUTE_CHUNK
    n_pc = n_prompt // ch
    row = _cond_row_joint(ch, n_prompt, tok_per_cond)
    return pl.pallas_call(
        functools.partial(_combine_kernel, n_prompt_chunks=n_pc),
        grid=(t // ch,),
        in_specs=[
            pl.BlockSpec((None, TOP_K, ch), lambda i: (i, 0, 0), memory_space=pltpu.SMEM),
            pl.BlockSpec((None, TOP_K, ch), lambda i: (jnp.minimum(i + 1, t // ch - 1), 0, 0),
                         memory_space=pltpu.SMEM),
            pl.BlockSpec((ch, gates.shape[1]), lambda i: (i, 0)),
            pl.BlockSpec((ch, d), lambda i: (i, 0)),
            mods.spec(_Mods.GATE2, row),
            pl.BlockSpec(memory_space=pl.ANY),
        ],
        out_specs=[
            pl.BlockSpec((ch, d), lambda i: (jnp.minimum(i, n_pc - 1), 0)),
            pl.BlockSpec((ch, d), lambda i: (jnp.maximum(i - n_pc, 0), 0)),
        ],
        out_shape=[jax.ShapeDtypeStruct((n_prompt, d), F32), jax.ShapeDtypeStruct((t - n_prompt, d), F32)],
        scratch_shapes=[pltpu.VMEM((2, TOP_K, ch, d), F32), pltpu.SemaphoreType.DMA((2,))],
        compiler_params=_params(("arbitrary",)),
        name="moe_combine",
    )(dest_rows, dest_rows, gates, x, mods.table, y)


def kernel(x_prompt, x_sample, c, cache_k, cache_v, c_ctx, ada_w, ada_b, norm1_g, norm2_g,
           attn_w_qkv, attn_w_o, attn_q_norm, attn_k_norm, attn_lambda, attn_subln_g,
           pool_w, pool_scale, ffn_w_gu, ffn_w_down,
           moe_w_router, moe_b_router, moe_w_gu, moe_w_down):
    b_ctx, l_ctx, d = x_prompt.shape
    b_dec, l_dec, _ = x_sample.shape
    depth = ada_w.shape[0]
    n_even, l_past, heads, v_dim = cache_k.shape[1:]
    head_dim = v_dim // 2
    assert head_dim == LANES and depth == 2 and l_dec % GRID_W == 0
    n_exp = moe_w_router.shape[-1]
    tp_, ts_ = b_ctx * l_ctx, b_dec * l_dec
    t_all = tp_ + ts_

    cond_rows = 8 * ((1 + b_dec + 7) // 8)
    cond = jnp.zeros((cond_rows, d), F32).at[0].set(c_ctx).at[1:1 + b_dec].set(c)
    mod_table = _adaln_mods(cond, ada_w, ada_b).reshape(depth * cond_rows * 6, 1, d)

    xp = x_prompt.reshape(tp_, d)
    xs = x_sample.reshape(ts_, d)

    layer, j = 0, 0
    mods = _Mods(mod_table, cond_rows, layer)
    g1 = norm1_g[layer].reshape(1, d)
    w_qkv = attn_w_qkv[j].astype(BF16)
    q_gain = (attn_q_norm[j] * (head_dim ** -0.5 * LOG2E)).reshape(1, head_dim)
    k_gain = attn_k_norm[j].reshape(1, head_dim)
    tables = _rope_tables(l_dec)
    prompt = dict(cond_base=0, tok_per_cond=tp_)
    sample = dict(cond_base=1, tok_per_cond=l_dec)

    q_p, k_p, v_p = _qkv_proj(xp, mods, g1, w_qkv, q_gain, k_gain, None, kv_dtype=F32, **prompt)
    q_s, k_s, v_s = _qkv_proj(xs, mods, g1, w_qkv, q_gain, k_gain, tables, kv_dtype=BF16, **sample)

    lam_params = attn_lambda[j].astype(F32)
    subln = attn_subln_g[j].reshape(1, v_dim)
    o_p, _ = _diff_attention(q_p, k_p, v_p, None, None, lam_params, subln,
                             batch=b_ctx, seq=l_ctx, layer=layer)
    ck = cache_k[:, j].reshape(b_dec * l_past, heads * v_dim)
    cv = cache_v[:, j].reshape(b_dec * l_past, heads * v_dim)
    later_weights = [attn_w_o[j], ffn_w_gu[j], ffn_w_down[j], pool_w[0], moe_w_gu[0], moe_w_down[0]]
    o_s, later_bf16 = _diff_attention(q_s, k_s, v_s, ck, cv, lam_params, subln,
                                      batch=b_dec, seq=l_dec, layer=layer,
                                      casts=[w.reshape(-1, w.shape[-1]) for w in later_weights])
    w_o, w_ffn_gu, w_ffn_down, w_pool, w_moe_gu, w_moe_down = [
        b.reshape(w.shape) for b, w in zip(later_bf16, later_weights)]

    x1 = _out_proj_residual(o_p, o_s, w_o, xp, xs, mods, tok_per_cond=l_dec)

    x2 = _dense_ffn(x1, mods, norm2_g[layer].reshape(1, d), w_ffn_gu, w_ffn_down,
                    n_prompt=tp_, tok_per_cond=l_dec)

    layer, j = 1, 0
    mods = _Mods(mod_table, cond_rows, layer)
    x3 = _pool_mixer(x2, mods, norm1_g[layer].reshape(1, d), w_pool,
                     pool_scale[j].reshape(1, d), n_prompt=tp_, prompt_seq=l_ctx, sample_seq=l_dec)

    g2 = norm2_g[layer].reshape(1, d)
    joint = dict(n_prompt=tp_, tok_per_cond=l_dec)
    route, counts = _moe_router(x3, mods, g2, moe_w_router[j], moe_b_router[j], **joint)
    dest_rows, gates, gemm_list, zero_tiles, n_rows = _dispatch_plan(route, counts, n_exp)
    rows_sorted = _moe_scatter(x3, mods, g2, dest_rows, zero_tiles, n_rows, **joint)
    y_sorted = _moe_experts(rows_sorted, w_moe_gu, w_moe_down, gemm_list)
    y_p, y_s = _moe_combine(y_sorted, dest_rows, gates, x3, mods, **joint)

    y_prompt = y_p.reshape(b_ctx, l_ctx, d)
    y_sample = y_s.reshape(b_dec, l_dec, d)
    state_k = k_p.reshape(b_ctx, 1, l_ctx, heads, v_dim)
    state_v = v_p.reshape(b_ctx, 1, l_ctx, heads, v_dim)
    return (y_prompt, y_sample, state_k, state_v)
```

```python
import functools
import math

import jax
import jax.numpy as jnp
from jax import lax
from jax.experimental import pallas as pl
from jax.experimental.pallas import tpu as pltpu

F32 = jnp.float32
BF16 = jnp.bfloat16

GRID_W = 64
ROPE_BASE = 10000.0
POOL_WINDOWS = (2, 4, 8, 16)
TOP_K = 2
EPS = 1e-6
LOG2E = 1.4426950408889634

LANES = 128
VMEM_LIMIT = 52 * 2**20

ROUTE_CHUNK = 256
GEMM_ROWS = 1024


def _lambda_init(layer):
    return 0.8 - 0.6 * math.exp(-0.3 * layer)


def _tile(n, pref, mult):
    best = None
    t = mult
    while t <= min(n, pref):
        if n % t == 0:
            best = t
        t += mult
    if best is None:
        raise ValueError(f"no tile for {n} (multiple of {mult}, <= {pref})")
    return best


def _params(semantics):
    return pltpu.CompilerParams(dimension_semantics=semantics, vmem_limit_bytes=VMEM_LIMIT)


def _dot(a, b):
    return jnp.dot(a, b, preferred_element_type=F32)


def _mask_bf16(m):
    return jnp.where(m, 1.0, 0.0).astype(BF16)


def _split_bf16(x):
    hi = x.astype(BF16)
    lo = (x - hi.astype(F32)).astype(BF16)
    return hi, lo


def _norm_modulate(x, g, scale, shift):
    ms = jnp.mean(x * x, axis=-1, keepdims=True)
    return x * lax.rsqrt(ms + EPS) * (g * (1.0 + scale)) + shift


def _adaln_kernel(cond_ref, w_ref, b_ref, o_ref):
    c = cond_ref[...]
    s = c * jax.nn.sigmoid(c)
    s_hi, s_lo = _split_bf16(s)
    w_hi, w_lo = _split_bf16(w_ref[...])
    o_ref[...] = _dot(s_hi, w_hi) + _dot(s_lo, w_hi) + _dot(s_hi, w_lo) + b_ref[...]


def _adaln_mods(cond, ada_w, ada_b):
    depth, d, n = ada_w.shape
    rows = cond.shape[0]
    tn = _tile(n, 512, LANES)
    return pl.pallas_call(
        _adaln_kernel,
        grid=(depth, n // tn),
        in_specs=[
            pl.BlockSpec((rows, d), lambda l, j: (0, 0)),
            pl.BlockSpec((None, d, tn), lambda l, j: (l, 0, j)),
            pl.BlockSpec((None, 1, tn), lambda l, j: (l, 0, j)),
        ],
        out_specs=pl.BlockSpec((None, rows, tn), lambda l, j: (l, 0, j)),
        out_shape=jax.ShapeDtypeStruct((depth, rows, n), F32),
        compiler_params=_params(("arbitrary", "arbitrary")),
        name="adaln_mods",
    )(cond, ada_w, ada_b.reshape(depth, 1, n))


class _Mods:
    SHIFT1, SCALE1, GATE1, SHIFT2, SCALE2, GATE2 = range(6)

    def __init__(self, table, rows, layer):
        self.table = table
        self.rows = rows
        self.layer = layer

    def spec(self, which, row_fn, width=None, col_fn=None):
        d = self.table.shape[-1]
        width = d if width is None else width
        base = self.layer * self.rows

        def index(*ids):
            col = 0 if col_fn is None else col_fn(*ids)
            return ((base + row_fn(*ids)) * 6 + which, 0, col)

        return pl.BlockSpec((None, 1, width), index)


def _qkv_kernel(*refs, rope, tn, n_sec):
    x_ref, sh_ref, sc_ref, g_ref, w_ref, qg_ref, kg_ref = refs[:7]
    rest = list(refs[7:])
    cos_ref, sin_ref = (rest.pop(0), rest.pop(0)) if rope else (None, None)
    q_ref, k_ref, v_ref, h_scr = rest
    j = pl.program_id(1)

    def qk_epilogue(gain_ref, out_ref, prologue=False):
        if prologue:
            h = _norm_modulate(x_ref[...], g_ref[...], sc_ref[...], sh_ref[...]).astype(BF16)
            h_scr[...] = h
        else:
            h = h_scr[...]
        acc = _dot(h, w_ref[...])
        r_idx = lax.broadcasted_iota(jnp.int32, (2 * LANES, 2 * LANES), 0)
        c_idx = lax.broadcasted_iota(jnp.int32, (2 * LANES, 2 * LANES), 1)
        if rope:
            swap = (r_idx >= LANES) & (c_idx >= LANES) & (r_idx - LANES == jnp.bitwise_xor(c_idx - LANES, LANES // 4))
            rhs = _mask_bf16(((r_idx < LANES) & (c_idx < LANES)) | swap)
        else:
            rhs = jnp.ones((2 * LANES, LANES), BF16)
        for g in range(tn // LANES):
            y = acc[:, g * LANES:(g + 1) * LANES]
            yg = y * gain_ref[...]
            if rope:
                res = _dot(jnp.concatenate([(y * y).astype(BF16), yg.astype(BF16)], axis=1), rhs)
                inv = lax.rsqrt(res[:, :LANES] * (1.0 / LANES) + EPS)
                out = (yg * cos_ref[...] + res[:, LANES:] * sin_ref[...]) * inv
            else:
                ss = _dot(jnp.concatenate(_split_bf16(y * y), axis=1), rhs)
                out = yg * lax.rsqrt(ss * (1.0 / LANES) + EPS)
            out_ref[:, g * LANES:(g + 1) * LANES] = out.astype(out_ref.dtype)

    @pl.when(j == 0)
    def _():
        qk_epilogue(qg_ref, q_ref, prologue=True)

    @pl.when((j > 0) & (j < n_sec))
    def _():
        qk_epilogue(qg_ref, q_ref)

    @pl.when((j >= n_sec) & (j < 2 * n_sec))
    def _():
        qk_epilogue(kg_ref, k_ref)

    @pl.when(j >= 2 * n_sec)
    def _():
        v_ref[...] = _dot(h_scr[...], w_ref[...]).astype(v_ref.dtype)


def _qkv_proj(x, mods, norm_g, w, q_gain, k_gain, tables, *, cond_base, tok_per_cond, kv_dtype):
    t, d = x.shape
    tm = _tile(math.gcd(t, tok_per_cond), 1024, 8)
    tn = _tile(d, 512, LANES)
    n_sec = d // tn
    rope = tables is not None
    row = lambda i, j: cond_base + (i * tm) // tok_per_cond
    in_specs = [
        pl.BlockSpec((tm, d), lambda i, j: (i, 0)),
        mods.spec(_Mods.SHIFT1, row),
        mods.spec(_Mods.SCALE1, row),
        pl.BlockSpec((1, d), lambda i, j: (0, 0)),
        pl.BlockSpec((d, tn), lambda i, j: (0, j)),
        pl.BlockSpec((1, LANES), lambda i, j: (0, 0)),
        pl.BlockSpec((1, LANES), lambda i, j: (0, 0)),
    ]
    args = [x, mods.table, mods.table, norm_g, w, q_gain, k_gain]
    if rope:
        nblk = tables[0].shape[0] // tm
        for tab in tables:
            in_specs.append(pl.BlockSpec((tm, LANES), lambda i, j: (i % nblk, 0)))
            args.append(tab)
    out_specs = [
        pl.BlockSpec((tm, tn), lambda i, j, s=s: (i, jnp.clip(j - s * n_sec, 0, n_sec - 1)))
        for s in range(3)
    ]
    return pl.pallas_call(
        functools.partial(_qkv_kernel, rope=rope, tn=tn, n_sec=n_sec),
        grid=(t // tm, 3 * n_sec),
        in_specs=in_specs,
        out_specs=out_specs,
        out_shape=[jax.ShapeDtypeStruct((t, d), BF16), jax.ShapeDtypeStruct((t, d), kv_dtype),
                   jax.ShapeDtypeStruct((t, d), kv_dtype)],
        scratch_shapes=[pltpu.VMEM((tm, d), BF16)],
        compiler_params=_params(("arbitrary", "arbitrary")),
        name="qkv_proj_rope" if rope else "qkv_proj",
    )(*args)


def _rope_tables(seq):
    n_freq = LANES // 4
    pos = jnp.arange(seq, dtype=jnp.int32)
    r = (pos // GRID_W).astype(F32)
    col = (pos % GRID_W).astype(F32)
    freqs = ROPE_BASE ** (-jnp.arange(n_freq, dtype=F32) / n_freq)
    cos_t, sin_t = [], []
    for p in (r, col):
        ang = p[:, None] * freqs
        cos, sin = jnp.cos(ang), jnp.sin(ang)
        cos_t += [cos, cos]
        sin_t += [-sin, sin]
    return jnp.concatenate(cos_t, axis=1), jnp.concatenate(sin_t, axis=1)


def _attn_kernel(*refs, n_cache, n_chunks, tk, lambda_init, cast_blocks, heads_per_step):
    n_cast = len(cast_blocks)
    lam_ref, g_ref, q_ref = refs[:3]
    n_in = 3 + (4 if n_cache else 2)
    if n_cache:
        kc_ref, vtc_ref, k_ref, vt_ref = refs[3:n_in]
    else:
        k_ref, vt_ref = refs[3:n_in]
    cast_in = refs[n_in:n_in + n_cast]
    o_ref = refs[n_in + n_cast]
    cast_out = refs[n_in + n_cast + 1:n_in + 2 * n_cast + 1]
    acc_ref = refs[-1]

    step = (pl.program_id(0) * pl.num_programs(1) + pl.program_id(1)) * pl.num_programs(2) + pl.program_id(2)
    for src, dst, n_blocks in zip(cast_in, cast_out, cast_blocks):
        @pl.when(step < n_blocks)
        def _(src=src, dst=dst):
            dst[...] = src[...].astype(dst.dtype)

    vd = 2 * LANES
    for hq in range(heads_per_step):
        cols = slice(hq * vd, (hq + 1) * vd)
        _attn_head(lam_ref, g_ref, q_ref.at[:, cols],
                   kc_ref.at[:, cols] if n_cache else None, vtc_ref.at[hq] if n_cache else None,
                   k_ref.at[:, cols], vt_ref.at[hq], o_ref.at[:, cols], acc_ref,
                   n_chunks=n_chunks, tk=tk, lambda_init=lambda_init)


def _attn_head(lam_ref, g_ref, q_ref, kc_ref, vtc_ref, k_ref, vt_ref, o_ref, acc_ref, *,
               n_chunks, tk, lambda_init):
    q = q_ref[...]
    qs = (q[:, :LANES], q[:, LANES:])

    def scores(kblk):
        return [lax.dot_general(kblk[:, c * LANES:(c + 1) * LANES], qs[c],
                                (((1,), (1,)), ((), ())), preferred_element_type=F32)
                for c in range(2)]

    def accumulate(sts, vt, stats):
        out = []
        for c in range(2):
            st = sts[c]
            m_cur = jnp.max(st, axis=0, keepdims=True)
            if stats is None:
                m_new = m_cur
                p = jnp.exp2(st - m_new)
                l_new = jnp.sum(p, axis=0, keepdims=True)
                acc_ref[c] = _dot(vt, p.astype(BF16))
            else:
                m_prev, l_prev = stats[c]
                m_new = jnp.maximum(m_prev, m_cur)
                alpha = jnp.exp2(m_prev - m_new)
                p = jnp.exp2(st - m_new)
                l_new = alpha * l_prev + jnp.sum(p, axis=0, keepdims=True)
                acc_ref[c] = alpha * acc_ref[c] + _dot(vt, p.astype(BF16))
            out.append((m_new, l_new))
        return out

    chunks = []
    if kc_ref is not None:
        chunks.append((lambda: kc_ref[...].astype(BF16), lambda: vtc_ref[...]))
    for i in range(n_chunks):
        chunks.append((lambda i=i: k_ref[i * tk:(i + 1) * tk, :].astype(BF16), lambda i=i: vt_ref[i]))
    stats = None
    sts = scores(chunks[0][0]())
    for n, (_, values) in enumerate(chunks):
        nxt = scores(chunks[n + 1][0]()) if n + 1 < len(chunks) else None
        stats = accumulate(sts, values(), stats)
        sts = nxt

    lp = lam_ref[...]
    lam = (jnp.exp(jnp.sum(lp[0:1] * lp[1:2], axis=-1, keepdims=True))
           - jnp.exp(jnp.sum(lp[2:3] * lp[3:4], axis=-1, keepdims=True)) + lambda_init)
    o_t = acc_ref[0] / stats[0][1] - lam * (acc_ref[1] / stats[1][1])
    o = o_t.T
    o = o * lax.rsqrt(jnp.mean(o * o, axis=-1, keepdims=True) + EPS) * g_ref[...]
    o_ref[...] = (o * (1.0 - lambda_init)).astype(o_ref.dtype)


def _chunked_transpose(v, batch, seq, heads, vd, tk):
    v = v.astype(BF16).reshape(batch, seq // tk, tk, heads, vd)
    return v.transpose(0, 3, 1, 4, 2).reshape(batch * heads, seq // tk, vd, tk)


def _cast_block_rows(rows, n_steps):
    blk = 16
    while rows % blk or rows // blk > n_steps:
        blk += 16
        if blk > rows:
            raise ValueError(f"no cast block for {rows} rows in {n_steps} steps")
    return blk


def _diff_attention(q, k, v, cache_k, cache_v, lam_params, subln_g, *, batch, seq, layer, casts=()):
    t, d = q.shape
    vd = subln_g.shape[-1]
    heads = d // vd
    n_cache = 0 if cache_k is None else cache_k.shape[0] // batch
    tq = _tile(seq, 512, LANES)
    tk = _tile(seq, 512, LANES)
    nq = seq // tq
    n_chunks = seq // tk
    hps = heads if seq <= tq else 2
    groups = heads // hps
    wide = hps * vd
    in_specs = [
        pl.BlockSpec(lam_params.shape, lambda b, h, i: (0, 0)),
        pl.BlockSpec((1, vd), lambda b, h, i: (0, 0)),
        pl.BlockSpec((tq, wide), lambda b, h, i: (b * nq + i, h)),
    ]
    args = [lam_params, subln_g, q]
    if n_cache:
        in_specs += [pl.BlockSpec((n_cache, wide), lambda b, h, i: (b, h)),
                     pl.BlockSpec((hps, None, vd, n_cache), lambda b, h, i: (b * groups + h, 0, 0, 0))]
        args += [cache_k, _chunked_transpose(cache_v, batch, n_cache, heads, vd, n_cache)]
    in_specs += [pl.BlockSpec((seq, wide), lambda b, h, i: (b, h)),
                 pl.BlockSpec((hps, n_chunks, vd, tk), lambda b, h, i: (b * groups + h, 0, 0, 0))]
    args += [k, _chunked_transpose(v, batch, seq, heads, vd, tk)]
    out_specs = [pl.BlockSpec((tq, wide), lambda b, h, i: (b * nq + i, h))]
    out_shape = [jax.ShapeDtypeStruct((t, d), BF16)]
    n_steps = batch * groups * nq
    cast_blocks = []
    for w in casts:
        rows, cols = w.shape
        blk = _cast_block_rows(rows, n_steps)
        n_blocks = rows // blk
        cast_blocks.append(n_blocks)
        spec = pl.BlockSpec((blk, cols), lambda b, h, i, n=n_blocks: (
            jnp.minimum((b * groups + h) * nq + i, n - 1), 0))
        in_specs.append(spec)
        args.append(w)
        out_specs.append(spec)
        out_shape.append(jax.ShapeDtypeStruct((rows, cols), BF16))
    res = pl.pallas_call(
        functools.partial(_attn_kernel, n_cache=n_cache, n_chunks=n_chunks, tk=tk,
                          lambda_init=_lambda_init(layer), cast_blocks=tuple(cast_blocks),
                          heads_per_step=hps),
        grid=(batch, groups, nq),
        in_specs=in_specs,
        out_specs=out_specs,
        out_shape=out_shape,
        scratch_shapes=[pltpu.VMEM((2, vd, tq), F32)],
        compiler_params=_params(("arbitrary", "arbitrary", "arbitrary")),
        name=f"diff_attention_{'latent' if n_cache else 'context'}",
    )(*args)
    return res[0], list(res[1:])


def _out_proj_kernel(op_ref, os_ref, w_ref, xp_ref, xs_ref, gate_ref, out_ref, *, n_prompt_tiles):
    is_prompt = pl.program_id(0) < n_prompt_tiles

    @pl.when(is_prompt)
    def _():
        out_ref[...] = xp_ref[...] + gate_ref[...] * _dot(op_ref[...], w_ref[...])

    @pl.when(jnp.logical_not(is_prompt))
    def _():
        out_ref[...] = xs_ref[...] + gate_ref[...] * _dot(os_ref[...], w_ref[...])


def _out_proj_residual(o_p, o_s, w, x_p, x_s, mods, *, tok_per_cond):
    n_prompt, d = x_p.shape
    t = n_prompt + x_s.shape[0]
    tm = _tile(math.gcd(n_prompt, tok_per_cond), 1024, 8)
    tn = _tile(d, 512, LANES)
    n_pt = n_prompt // tm
    row = _cond_row_joint(tm, n_prompt, tok_per_cond)
    p_rows = lambda i: jnp.minimum(i, n_pt - 1)
    s_rows = lambda i: jnp.maximum(i - n_pt, 0)
    return pl.pallas_call(
        functools.partial(_out_proj_kernel, n_prompt_tiles=n_pt),
        grid=(t // tm, d // tn),
        in_specs=[
            pl.BlockSpec((tm, d), lambda i, j: (p_rows(i), 0)),
            pl.BlockSpec((tm, d), lambda i, j: (s_rows(i), 0)),
            pl.BlockSpec((d, tn), lambda i, j: (0, j)),
            pl.BlockSpec((tm, tn), lambda i, j: (p_rows(i), jnp.where(i < n_pt, j, d // tn - 1))),
            pl.BlockSpec((tm, tn), lambda i, j: (s_rows(i), jnp.where(i < n_pt, 0, j))),
            mods.spec(_Mods.GATE1, row, width=tn, col_fn=lambda i, j: j),
        ],
        out_specs=pl.BlockSpec((tm, tn), lambda i, j: (i, j)),
        out_shape=jax.ShapeDtypeStruct((t, d), F32),
        compiler_params=_params(("arbitrary", "arbitrary")),
        name="attn_out_proj",
    )(o_p, o_s, w, x_p, x_s, mods.table)


def _ffn_kernel(x_ref, sh_ref, sc_ref, gate_ref, g_ref, wg_ref, wu_ref, wd_ref, o_ref, h_scr):
    f = pl.program_id(1)

    def swiglu_part(h):
        gte = _dot(h, wg_ref[...])
        up = _dot(h, wu_ref[...])
        act = (gte * jax.nn.sigmoid(gte) * up).astype(BF16)
        return _dot(act, wd_ref[...])

    @pl.when(f == 0)
    def _():
        h = _norm_modulate(x_ref[...], g_ref[...], sc_ref[...], sh_ref[...]).astype(BF16)
        h_scr[...] = h
        o_ref[...] = swiglu_part(h)

    @pl.when(f > 0)
    def _():
        o_ref[...] += swiglu_part(h_scr[...])

    @pl.when(f == pl.num_programs(1) - 1)
    def _():
        o_ref[...] = x_ref[...] + gate_ref[...] * o_ref[...]


def _cond_row_joint(tm, n_prompt, tok_per_cond):
    def row(i, *_):
        tok = i * tm
        return jnp.where(tok < n_prompt, 0, 1 + (tok - n_prompt) // tok_per_cond)
    return row


def _dense_ffn(x, mods, norm_g, w_gu, w_down, *, n_prompt, tok_per_cond):
    t, d = x.shape
    ff = w_down.shape[0]
    tm = _tile(math.gcd(n_prompt, tok_per_cond), 1024, 8)
    tf = _tile(ff, 256, LANES)
    nf = ff // tf
    row = _cond_row_joint(tm, n_prompt, tok_per_cond)
    return pl.pallas_call(
        _ffn_kernel,
        grid=(t // tm, nf),
        in_specs=[
            pl.BlockSpec((tm, d), lambda i, f: (i, 0)),
            mods.spec(_Mods.SHIFT2, row),
            mods.spec(_Mods.SCALE2, row),
            mods.spec(_Mods.GATE2, row),
            pl.BlockSpec((1, d), lambda i, f: (0, 0)),
            pl.BlockSpec((d, tf), lambda i, f: (0, f)),
            pl.BlockSpec((d, tf), lambda i, f: (0, nf + f)),
            pl.BlockSpec((tf, d), lambda i, f: (f, 0)),
        ],
        out_specs=pl.BlockSpec((tm, d), lambda i, f: (i, 0)),
        out_shape=jax.ShapeDtypeStruct((t, d), F32),
        scratch_shapes=[pltpu.VMEM((tm, d), BF16)],
        compiler_params=_params(("arbitrary", "arbitrary")),
        name="dense_ffn",
    )(x, mods.table, mods.table, mods.table, norm_g, w_gu, w_gu, w_down)


def _pool_kernel(x_ref, xp_ref, xn_ref, sh_ref, sc_ref, gate_ref, g_ref, w_ref, ps_ref, o_ref, *,
                 tp, halo, n_prompt_tiles, prompt_tiles_per_seq, sample_tiles_per_seq, group_dim):
    i = pl.program_id(0)
    is_prompt = i < n_prompt_tiles
    tiles_per_seq = jnp.where(is_prompt, prompt_tiles_per_seq, sample_tiles_per_seq)
    local = lax.rem(jnp.where(is_prompt, i, i - n_prompt_tiles), tiles_per_seq)
    first = local == 0
    last = local == tiles_per_seq - 1
    seq_len = tiles_per_seq * tp

    g, sc, sh = g_ref[...], sc_ref[...], sh_ref[...]
    x = x_ref[...]
    h = _norm_modulate(x, g, sc, sh)
    h_prev = jnp.where(first, 0.0, _norm_modulate(xp_ref[...], g, sc, sh))
    h_next = jnp.where(last, 0.0, _norm_modulate(xn_ref[...], g, sc, sh))
    d = h.shape[-1]
    h_halo = jnp.concatenate([h_prev, h_next, jnp.zeros((LANES - 2 * halo, d), F32)], axis=0)

    t_idx = lax.broadcasted_iota(jnp.int32, (tp, tp), 0)
    s_idx = lax.broadcasted_iota(jnp.int32, (tp, tp), 1)
    t_h = lax.broadcasted_iota(jnp.int32, (tp, LANES), 0)
    u_h = lax.broadcasted_iota(jnp.int32, (tp, LANES), 1)
    off_h = jnp.where(u_h < halo, u_h - halo, tp + u_h - halo)
    pos = local * tp + lax.broadcasted_iota(jnp.int32, (tp, 1), 0)

    for grp, win in enumerate(POOL_WINDOWS):
        lo, hi = win // 2, win // 2 - 1
        cols = slice(grp * group_dim, (grp + 1) * group_dim)
        band = _mask_bf16((s_idx >= t_idx - lo) & (s_idx <= t_idx + hi))
        band_h = _mask_bf16((u_h < 2 * halo) & (off_h >= t_h - lo) & (off_h <= t_h + hi))
        total = _dot(band, h[:, cols].astype(BF16)) + _dot(band_h, h_halo[:, cols].astype(BF16))
        cnt = (jnp.minimum(pos + hi, seq_len - 1) - jnp.maximum(pos - lo, 0) + 1).astype(F32)
        pooled = (total / cnt - h[:, cols]).astype(BF16)
        y = _dot(pooled, w_ref[grp]) * ps_ref[:, cols]
        o_ref[:, cols] = x[:, cols] + gate_ref[:, cols] * y


def _pool_mixer(x, mods, norm_g, pool_w, pool_scale, *, n_prompt, prompt_seq, sample_seq):
    t, d = x.shape
    halo = max(POOL_WINDOWS) // 2
    tp = _tile(math.gcd(prompt_seq, sample_seq), 256, 8)
    n_groups, group_dim = pool_w.shape[0], pool_w.shape[1]
    hb = tp // halo
    n_halo_blocks = t // halo
    row = _cond_row_joint(tp, n_prompt, sample_seq)
    return pl.pallas_call(
        functools.partial(_pool_kernel, tp=tp, halo=halo, n_prompt_tiles=n_prompt // tp,
                          prompt_tiles_per_seq=prompt_seq // tp,
                          sample_tiles_per_seq=sample_seq // tp, group_dim=group_dim),
        grid=(t // tp,),
        in_specs=[
            pl.BlockSpec((tp, d), lambda i: (i, 0)),
            pl.BlockSpec((halo, d), lambda i: (jnp.maximum(i * hb - 1, 0), 0)),
            pl.BlockSpec((halo, d), lambda i: (jnp.minimum((i + 1) * hb, n_halo_blocks - 1), 0)),
            mods.spec(_Mods.SHIFT1, row),
            mods.spec(_Mods.SCALE1, row),
            mods.spec(_Mods.GATE1, row),
            pl.BlockSpec((1, d), lambda i: (0, 0)),
            pl.BlockSpec((n_groups, group_dim, group_dim), lambda i: (0, 0, 0)),
            pl.BlockSpec((1, d), lambda i: (0, 0)),
        ],
        out_specs=pl.BlockSpec((tp, d), lambda i: (i, 0)),
        out_shape=jax.ShapeDtypeStruct((t, d), F32),
        compiler_params=_params(("arbitrary",)),
        name="pool_mixer",
    )(x, x, x, mods.table, mods.table, mods.table, norm_g, pool_w, pool_scale)


def _router_kernel(x_ref, sh_ref, sc_ref, g_ref, wr_ref, br_ref, route_ref, cnt_ref):
    h = _norm_modulate(x_ref[...], g_ref[...], sc_ref[...], sh_ref[...])
    h_hi, h_lo = _split_bf16(h)
    w_hi, w_lo = _split_bf16(wr_ref[...])
    logits = _dot(h_hi, w_hi) + _dot(h_lo, w_hi) + _dot(h_hi, w_lo) + br_ref[...]

    rows = logits.shape[0]
    lane = lax.broadcasted_iota(jnp.int32, logits.shape, 1).astype(F32)
    m1 = jnp.max(logits, axis=-1, keepdims=True)
    i1 = jnp.min(jnp.where(logits == m1, lane, float(LANES)), axis=-1, keepdims=True)
    oh1 = lane == i1
    rest = jnp.where(oh1, -jnp.inf, logits)
    m2 = jnp.max(rest, axis=-1, keepdims=True)
    i2 = jnp.min(jnp.where(rest == m2, lane, float(LANES)), axis=-1, keepdims=True)
    oh2 = lane == i2
    e = jnp.exp(m2 - m1)
    gate_a = 1.0 / (1.0 + e)
    gate_b = e / (1.0 + e)

    sel = jnp.where(oh1 | oh2, 1.0, 0.0)
    r_idx = lax.broadcasted_iota(jnp.int32, (rows, rows), 0)
    c_idx = lax.broadcasted_iota(jnp.int32, (rows, rows), 1)
    earlier = _mask_bf16(c_idx < r_idx)
    rank = _dot(earlier, sel.astype(BF16))
    rank_a = jnp.sum(jnp.where(oh1, rank, 0.0), axis=-1, keepdims=True)
    rank_b = jnp.sum(jnp.where(oh2, rank, 0.0), axis=-1, keepdims=True)

    route = jnp.zeros(logits.shape, F32)
    for k, val in enumerate((i1, i2, rank_a, rank_b, gate_a, gate_b)):
        route = jnp.where(lane == float(k), val, route)
    route_ref[...] = route
    cnt_ref[...] = jnp.sum(sel, axis=0, keepdims=True)


def _moe_router(x, mods, norm_g, w_router, b_router, *, n_prompt, tok_per_cond):
    t, d = x.shape
    n_exp = w_router.shape[1]
    ch = ROUTE_CHUNK
    wr = jnp.zeros((d, LANES), F32).at[:, :n_exp].set(w_router.astype(F32))
    br = jnp.full((1, LANES), -jnp.inf, F32).at[0, :n_exp].set(b_router.astype(F32))
    row = _cond_row_joint(ch, n_prompt, tok_per_cond)
    return pl.pallas_call(
        _router_kernel,
        grid=(t // ch,),
        in_specs=[
            pl.BlockSpec((ch, d), lambda i: (i, 0)),
            mods.spec(_Mods.SHIFT2, row),
            mods.spec(_Mods.SCALE2, row),
            pl.BlockSpec((1, d), lambda i: (0, 0)),
            pl.BlockSpec((d, LANES), lambda i: (0, 0)),
            pl.BlockSpec((1, LANES), lambda i: (0, 0)),
        ],
        out_specs=[
            pl.BlockSpec((ch, LANES), lambda i: (i, 0)),
            pl.BlockSpec((None, 1, LANES), lambda i: (i, 0, 0)),
        ],
        out_shape=[
            jax.ShapeDtypeStruct((t, LANES), F32),
            jax.ShapeDtypeStruct((t // ch, 1, LANES), F32),
        ],
        compiler_params=_params(("arbitrary",)),
        name="moe_router",
    )(x, mods.table, mods.table, norm_g, wr, br)


def _dispatch_plan(route, counts, n_exp):
    ch, gr = ROUTE_CHUNK, GEMM_ROWS
    t = route.shape[0]
    n_chunks = t // ch
    i32 = jnp.int32
    cnt = counts[:, 0, :n_exp].astype(i32)
    total = cnt.sum(0)
    padded = ((total + gr - 1) // gr) * gr
    off = jnp.cumsum(padded) - padded
    start = off[None, :] + jnp.cumsum(cnt, axis=0) - cnt

    def dest(choice):
        e = route[:, choice].astype(i32).reshape(n_chunks, ch, 1)
        hit = e == jnp.arange(n_exp, dtype=i32)
        base = jnp.sum(jnp.where(hit, start[:, None, :], 0), axis=-1)
        return base.reshape(t) + route[:, 2 + choice].astype(i32)

    dest_a, dest_b = dest(0), dest(1)

    n_gemm_tiles = (t * TOP_K) // gr + n_exp

    tile_lo = jnp.arange(n_gemm_tiles, dtype=i32) * gr
    used = tile_lo < (off + padded)[-1]
    ends = off + padded
    expert_of = jnp.minimum(jnp.sum(tile_lo[:, None] >= ends[None, :], axis=1), n_exp - 1).astype(i32)
    n_used = jnp.maximum(used.sum(), 1)
    clamp = jnp.minimum(jnp.arange(n_gemm_tiles, dtype=i32), n_used - 1).astype(i32)
    token_rows = jnp.clip((off + total)[expert_of] - tile_lo, 0, gr).astype(i32)
    gemm_list = (clamp, expert_of[clamp], used.astype(i32), token_rows[clamp])

    tail = jnp.arange(n_gemm_tiles - n_exp, n_gemm_tiles, dtype=i32)
    zero_rows = jnp.concatenate([jnp.maximum(ends - gr, 0), tail * gr]).astype(i32)
    zero_ok = jnp.concatenate([padded > 0, tail >= used.sum()]).astype(i32)

    dest_rows = jnp.stack([dest_a.reshape(n_chunks, ch), dest_b.reshape(n_chunks, ch)], axis=1)
    gates = jnp.concatenate([route[:, 4:6], jnp.zeros((t, 6), F32)], axis=1)
    return dest_rows, gates, gemm_list, (zero_rows, zero_ok), n_gemm_tiles * gr


def _wait_rows(src, dst, sem, n_rows):
    pltpu.make_async_copy(src.at[pl.ds(0, n_rows), :], dst.at[pl.ds(0, n_rows), :], sem).wait()


def _scatter_kernel(zrow_ref, zok_ref, dest_ref, x_ref, sh_ref, sc_ref, g_ref, out_hbm,
                    h_scr, z_scr, sem, zsem, *, tile_rows):
    ch = x_ref.shape[0]
    step, n_steps = pl.program_id(0), pl.num_programs(0)
    slot = lax.rem(step, 2)

    @pl.when(step == 0)
    def _():
        z_scr[...] = jnp.zeros(z_scr.shape, F32)
        zr = z_scr.shape[0]

        def zero_copy(n, part):
            row0 = pl.multiple_of(zrow_ref[n] + part * zr, zr)
            return pltpu.make_async_copy(z_scr, out_hbm.at[pl.ds(row0, zr), :], zsem)

        for n in range(zrow_ref.shape[0]):
            @pl.when(zok_ref[n] == 1)
            def _(n=n):
                for part in range(tile_rows // zr):
                    zero_copy(n, part).start()
        for n in range(zrow_ref.shape[0]):
            @pl.when(zok_ref[n] == 1)
            def _(n=n):
                for part in range(tile_rows // zr):
                    zero_copy(n, part).wait()

    h_scr[slot] = _norm_modulate(x_ref[...], g_ref[...], sc_ref[...], sh_ref[...])

    src_rows = h_scr.at[slot]
    for j in range(ch):
        for k in range(TOP_K):
            pltpu.make_async_copy(src_rows.at[pl.ds(j, 1), :],
                                  out_hbm.at[pl.ds(dest_ref[k, j], 1), :],
                                  sem.at[slot]).start(priority=k)

    def wait_slot(s):
        for _ in range(TOP_K):
            _wait_rows(h_scr.at[s], out_hbm, sem.at[s], ch)

    @pl.when(step > 0)
    def _():
        wait_slot(1 - slot)

    @pl.when(step == n_steps - 1)
    def _():
        wait_slot(slot)


def _moe_scatter(x, mods, norm_g, dest_rows, zero_tiles, n_rows, *, n_prompt, tok_per_cond):
    t, d = x.shape
    ch = ROUTE_CHUNK
    row = _cond_row_joint(ch, n_prompt, tok_per_cond)
    grid_spec = pltpu.PrefetchScalarGridSpec(
        num_scalar_prefetch=2,
        grid=(t // ch,),
        in_specs=[
            pl.BlockSpec((None, TOP_K, ch), lambda i, *_: (i, 0, 0), memory_space=pltpu.SMEM),
            pl.BlockSpec((ch, d), lambda i, *_: (i, 0)),
            mods.spec(_Mods.SHIFT2, row),
            mods.spec(_Mods.SCALE2, row),
            pl.BlockSpec((1, d), lambda i, *_: (0, 0)),
        ],
        out_specs=pl.BlockSpec(memory_space=pl.ANY),
        scratch_shapes=[pltpu.VMEM((2, ch, d), F32), pltpu.VMEM((ch, d), F32),
                        pltpu.SemaphoreType.DMA((2,)), pltpu.SemaphoreType.DMA(())],
    )
    return pl.pallas_call(
        functools.partial(_scatter_kernel, tile_rows=GEMM_ROWS),
        grid_spec=grid_spec,
        out_shape=jax.ShapeDtypeStruct((n_rows, d), F32),
        compiler_params=_params(("arbitrary",)),
        name="moe_scatter",
    )(*zero_tiles, dest_rows, x, mods.table, mods.table, norm_g)


def _expert_kernel(blk_ref, exp_ref, used_ref, rows_ref, x_ref, wg_ref, wu_ref, wd_ref, o_ref, x_scr, *,
                   sub_rows):
    r, f = pl.program_id(0), pl.program_id(1)
    tile_rows = o_ref.shape[0]

    def swiglu_part(x):
        gte = _dot(x, wg_ref[...])
        up = _dot(x, wu_ref[...])
        act = (gte * jax.nn.sigmoid(gte) * up).astype(BF16)
        return _dot(act, wd_ref[...])

    @pl.when(used_ref[r] == 1)
    def _():
        valid = rows_ref[r]
        full = valid > tile_rows - sub_rows

        @pl.when(full & (f == 0))
        def _():
            x = x_ref[...].astype(BF16)
            x_scr[...] = x
            o_ref[...] = swiglu_part(x)

        @pl.when(full & (f > 0))
        def _():
            o_ref[...] += swiglu_part(x_scr[...])

        @pl.when(jnp.logical_not(full))
        def _():
            @pl.when(f == 0)
            def _():
                o_ref[...] = jnp.zeros(o_ref.shape, F32)
                x_scr[...] = x_ref[...].astype(BF16)

            for sb in range(tile_rows // sub_rows - 1):
                @pl.when(valid > sb * sub_rows)
                def _(sb=sb):
                    rows = slice(sb * sub_rows, (sb + 1) * sub_rows)
                    o_ref[rows, :] += swiglu_part(x_scr[rows, :])


def _moe_experts(xs, w_gu, w_down, gemm_list):
    n_rows, d = xs.shape
    ff = w_down.shape[1]
    gr = GEMM_ROWS
    tf = _tile(ff, 256, LANES)
    nf = ff // tf
    n_tiles = gemm_list[0].shape[0]

    def fcol(f, used, r):
        return jnp.where(used[r] == 1, f, nf - 1)

    grid_spec = pltpu.PrefetchScalarGridSpec(
        num_scalar_prefetch=4,
        grid=(n_tiles, nf),
        in_specs=[
            pl.BlockSpec((gr, d), lambda r, f, blk, ex, used, rows: (blk[r], 0)),
            pl.BlockSpec((None, d, tf), lambda r, f, blk, ex, used, rows: (ex[r], 0, fcol(f, used, r))),
            pl.BlockSpec((None, d, tf), lambda r, f, blk, ex, used, rows: (ex[r], 0, nf + fcol(f, used, r))),
            pl.BlockSpec((None, tf, d), lambda r, f, blk, ex, used, rows: (ex[r], fcol(f, used, r), 0)),
        ],
        out_specs=pl.BlockSpec((gr, d), lambda r, f, blk, ex, used, rows: (blk[r], 0)),
        scratch_shapes=[pltpu.VMEM((gr, d), BF16)],
    )
    return pl.pallas_call(
        functools.partial(_expert_kernel, sub_rows=ROUTE_CHUNK),
        grid_spec=grid_spec,
        out_shape=jax.ShapeDtypeStruct((n_rows, d), F32),
        input_output_aliases={4: 0},
        compiler_params=_params(("arbitrary", "arbitrary")),
        name="moe_experts",
    )(*gemm_list, xs, w_gu, w_gu, w_down)


def _combine_kernel(dest_ref, dest_next_ref, gates_ref, x_ref, gate_ref, y_hbm, op_ref, os_ref,
                    y_scr, sem, *, n_prompt_chunks):
    ch = x_ref.shape[0]
    step, n_steps = pl.program_id(0), pl.num_programs(0)
    slot = lax.rem(step, 2)

    def gather(idx_ref, s):
        for j in range(ch):
            for k in range(TOP_K):
                pltpu.make_async_copy(y_hbm.at[pl.ds(idx_ref[k, j], 1), :],
                                      y_scr.at[s, k, pl.ds(j, 1), :], sem.at[s]).start(priority=k)

    @pl.when(step == 0)
    def _():
        gather(dest_ref, slot)

    @pl.when(step + 1 < n_steps)
    def _():
        gather(dest_next_ref, 1 - slot)

    for k in range(TOP_K):
        _wait_rows(y_hbm, y_scr.at[slot, k], sem.at[slot], ch)

    gates = gates_ref[...]
    mix = gates[:, 0:1] * y_scr[slot, 0] + gates[:, 1:2] * y_scr[slot, 1]
    out = x_ref[...] + gate_ref[...] * mix
    is_prompt = step < n_prompt_chunks

    @pl.when(is_prompt)
    def _():
        op_ref[...] = out

    @pl.when(jnp.logical_not(is_prompt))
    def _():
        os_ref[...] = out


def _moe_combine(y, dest_rows, gates, x, mods, *, n_prompt, tok_per_cond):
    t, d = x.shape
    ch = ROUTE_CHUNK
    n_pc = n_prompt // ch
    row = _cond_row_joint(ch, n_prompt, tok_per_cond)
    return pl.pallas_call(
        functools.partial(_combine_kernel, n_prompt_chunks=n_pc),
        grid=(t // ch,),
        in_specs=[
            pl.BlockSpec((None, TOP_K, ch), lambda i: (i, 0, 0), memory_space=pltpu.SMEM),
            pl.BlockSpec((None, TOP_K, ch), lambda i: (jnp.minimum(i + 1, t // ch - 1), 0, 0),
                         memory_space=pltpu.SMEM),
            pl.BlockSpec((ch, gates.shape[1]), lambda i: (i, 0)),
            pl.BlockSpec((ch, d), lambda i: (i, 0)),
            mods.spec(_Mods.GATE2, row),
            pl.BlockSpec(memory_space=pl.ANY),
        ],
        out_specs=[
            pl.BlockSpec((ch, d), lambda i: (jnp.minimum(i, n_pc - 1), 0)),
            pl.BlockSpec((ch, d), lambda i: (jnp.maximum(i - n_pc, 0), 0)),
        ],
        out_shape=[jax.ShapeDtypeStruct((n_prompt, d), F32), jax.ShapeDtypeStruct((t - n_prompt, d), F32)],
        scratch_shapes=[pltpu.VMEM((2, TOP_K, ch, d), F32), pltpu.SemaphoreType.DMA((2,))],
        compiler_params=_params(("arbitrary",)),
        name="moe_combine",
    )(dest_rows, dest_rows, gates, x, mods.table, y)


def kernel(x_prompt, x_sample, c, cache_k, cache_v, c_ctx, ada_w, ada_b, norm1_g, norm2_g,
           attn_w_qkv, attn_w_o, attn_q_norm, attn_k_norm, attn_lambda, attn_subln_g,
           pool_w, pool_scale, ffn_w_gu, ffn_w_down,
           moe_w_router, moe_b_router, moe_w_gu, moe_w_down):
    b_ctx, l_ctx, d = x_prompt.shape
    b_dec, l_dec, _ = x_sample.shape
    depth = ada_w.shape[0]
    l_past, heads, v_dim = cache_k.shape[2:]
    head_dim = v_dim // 2
    assert head_dim == LANES and depth == 2 and l_dec % GRID_W == 0
    n_exp = moe_w_router.shape[-1]
    tp_, ts_ = b_ctx * l_ctx, b_dec * l_dec

    cond_rows = 8 * ((1 + b_dec + 7) // 8)
    cond = jnp.zeros((cond_rows, d), F32).at[0].set(c_ctx).at[1:1 + b_dec].set(c)
    mod_table = _adaln_mods(cond, ada_w, ada_b).reshape(depth * cond_rows * 6, 1, d)

    xp = x_prompt.reshape(tp_, d)
    xs = x_sample.reshape(ts_, d)

    layer, j = 0, 0
    mods = _Mods(mod_table, cond_rows, layer)
    g1 = norm1_g[layer].reshape(1, d)
    w_qkv = attn_w_qkv[j].astype(BF16)
    q_gain = (attn_q_norm[j] * (head_dim ** -0.5 * LOG2E)).reshape(1, head_dim)
    k_gain = attn_k_norm[j].reshape(1, head_dim)
    tables = _rope_tables(l_dec)
    prompt = dict(cond_base=0, tok_per_cond=tp_)
    sample = dict(cond_base=1, tok_per_cond=l_dec)

    q_p, k_p, v_p = _qkv_proj(xp, mods, g1, w_qkv, q_gain, k_gain, None, kv_dtype=F32, **prompt)
    q_s, k_s, v_s = _qkv_proj(xs, mods, g1, w_qkv, q_gain, k_gain, tables, kv_dtype=BF16, **sample)

    lam_params = attn_lambda[j].astype(F32)
    subln = attn_subln_g[j].reshape(1, v_dim)
    o_p, _ = _diff_attention(q_p, k_p, v_p, None, None, lam_params, subln,
                             batch=b_ctx, seq=l_ctx, layer=layer)
    ck = cache_k[:, j].reshape(b_dec * l_past, heads * v_dim)
    cv = cache_v[:, j].reshape(b_dec * l_past, heads * v_dim)
    later_weights = [attn_w_o[j], ffn_w_gu[j], ffn_w_down[j], pool_w[0], moe_w_gu[0], moe_w_down[0]]
    o_s, later_bf16 = _diff_attention(q_s, k_s, v_s, ck, cv, lam_params, subln,
                                      batch=b_dec, seq=l_dec, layer=layer,
                                      casts=[w.reshape(-1, w.shape[-1]) for w in later_weights])
    w_o, w_ffn_gu, w_ffn_down, w_pool, w_moe_gu, w_moe_down = [
        b.reshape(w.shape) for b, w in zip(later_bf16, later_weights)]

    x1 = _out_proj_residual(o_p, o_s, w_o, xp, xs, mods, tok_per_cond=l_dec)

    x2 = _dense_ffn(x1, mods, norm2_g[layer].reshape(1, d), w_ffn_gu, w_ffn_down,
                    n_prompt=tp_, tok_per_cond=l_dec)

    layer, j = 1, 0
    mods = _Mods(mod_table, cond_rows, layer)
    x3 = _pool_mixer(x2, mods, norm1_g[layer].reshape(1, d), w_pool,
                     pool_scale[j].reshape(1, d), n_prompt=tp_, prompt_seq=l_ctx, sample_seq=l_dec)

    g2 = norm2_g[layer].reshape(1, d)
    joint = dict(n_prompt=tp_, tok_per_cond=l_dec)
    route, counts = _moe_router(x3, mods, g2, moe_w_router[j], moe_b_router[j], **joint)
    dest_rows, gates, gemm_list, zero_tiles, n_rows = _dispatch_plan(route, counts, n_exp)
    rows_sorted = _moe_scatter(x3, mods, g2, dest_rows, zero_tiles, n_rows, **joint)
    y_sorted = _moe_experts(rows_sorted, w_moe_gu, w_moe_down, gemm_list)
    y_p, y_s = _moe_combine(y_sorted, dest_rows, gates, x3, mods, **joint)

    y_prompt = y_p.reshape(b_ctx, l_ctx, d)
    y_sample = y_s.reshape(b_dec, l_dec, d)
    state_k = k_p.reshape(b_ctx, 1, l_ctx, heads, v_dim)
    state_v = v_p.reshape(b_ctx, 1, l_ctx, heads, v_dim)
    return (y_prompt, y_sample, state_k, state_v)
```

```python
import functools
import math

import jax
import jax.numpy as jnp
from jax import lax
from jax.experimental import pallas as pl
from jax.experimental.pallas import tpu as pltpu

F32 = jnp.float32
BF16 = jnp.bfloat16

GRID_W = 64
ROPE_BASE = 10000.0
POOL_WINDOWS = (2, 4, 8, 16)
TOP_K = 2
EPS = 1e-6
LOG2E = 1.4426950408889634

LANES = 128
VMEM_LIMIT = 52 * 2**20

ROUTE_CHUNK = 256
GEMM_ROWS = 1024


def _lambda_init(layer):
    return 0.8 - 0.6 * math.exp(-0.3 * layer)


def _tile(n, pref, mult):
    best = None
    t = mult
    while t <= min(n, pref):
        if n % t == 0:
            best = t
        t += mult
    if best is None:
        raise ValueError(f"no tile for {n} (multiple of {mult}, <= {pref})")
    return best


def _params(semantics):
    return pltpu.CompilerParams(dimension_semantics=semantics, vmem_limit_bytes=VMEM_LIMIT)


def _dot(a, b):
    return jnp.dot(a, b, preferred_element_type=F32)


def _mask_bf16(m):
    return jnp.where(m, 1.0, 0.0).astype(BF16)


def _split_bf16(x):
    hi = x.astype(BF16)
    lo = (x - hi.astype(F32)).astype(BF16)
    return hi, lo


def _norm_modulate(x, g, scale, shift):
    ms = jnp.mean(x * x, axis=-1, keepdims=True)
    return x * lax.rsqrt(ms + EPS) * (g * (1.0 + scale)) + shift


def _adaln_kernel(cond_ref, w_ref, b_ref, o_ref):
    c = cond_ref[...]
    s = c * jax.nn.sigmoid(c)
    s_hi, s_lo = _split_bf16(s)
    w_hi, w_lo = _split_bf16(w_ref[...])
    o_ref[...] = _dot(s_hi, w_hi) + _dot(s_lo, w_hi) + _dot(s_hi, w_lo) + b_ref[...]


def _adaln_mods(cond, ada_w, ada_b):
    depth, d, n = ada_w.shape
    rows = cond.shape[0]
    tn = _tile(n, 512, LANES)
    return pl.pallas_call(
        _adaln_kernel,
        grid=(depth, n // tn),
        in_specs=[
            pl.BlockSpec((rows, d), lambda l, j: (0, 0)),
            pl.BlockSpec((None, d, tn), lambda l, j: (l, 0, j)),
            pl.BlockSpec((None, 1, tn), lambda l, j: (l, 0, j)),
        ],
        out_specs=pl.BlockSpec((None, rows, tn), lambda l, j: (l, 0, j)),
        out_shape=jax.ShapeDtypeStruct((depth, rows, n), F32),
        compiler_params=_params(("arbitrary", "arbitrary")),
        name="adaln_mods",
    )(cond, ada_w, ada_b.reshape(depth, 1, n))


class _Mods:
    SHIFT1, SCALE1, GATE1, SHIFT2, SCALE2, GATE2 = range(6)

    def __init__(self, table, rows, layer):
        self.table = table
        self.rows = rows
        self.layer = layer

    def spec(self, which, row_fn, width=None, col_fn=None):
        d = self.table.shape[-1]
        width = d if width is None else width
        base = self.layer * self.rows

        def index(*ids):
            col = 0 if col_fn is None else col_fn(*ids)
            return ((base + row_fn(*ids)) * 6 + which, 0, col)

        return pl.BlockSpec((None, 1, width), index)


def _qkv_kernel(*refs, rope, tn, n_sec):
    x_ref, sh_ref, sc_ref, g_ref, w_ref, qg_ref, kg_ref = refs[:7]
    rest = list(refs[7:])
    cos_ref, sin_ref = (rest.pop(0), rest.pop(0)) if rope else (None, None)
    q_ref, k_ref, v_ref, h_scr = rest
    j = pl.program_id(1)

    def qk_epilogue(gain_ref, out_ref, prologue=False):
        if prologue:
            h = _norm_modulate(x_ref[...], g_ref[...], sc_ref[...], sh_ref[...]).astype(BF16)
            h_scr[...] = h
        else:
            h = h_scr[...]
        acc = _dot(h, w_ref[...])
        r_idx = lax.broadcasted_iota(jnp.int32, (2 * LANES, 2 * LANES), 0)
        c_idx = lax.broadcasted_iota(jnp.int32, (2 * LANES, 2 * LANES), 1)
        if rope:
            swap = (r_idx >= LANES) & (c_idx >= LANES) & (r_idx - LANES == jnp.bitwise_xor(c_idx - LANES, LANES // 4))
            rhs = _mask_bf16(((r_idx < LANES) & (c_idx < LANES)) | swap)
        else:
            rhs = jnp.ones((2 * LANES, LANES), BF16)
        for g in range(tn // LANES):
            y = acc[:, g * LANES:(g + 1) * LANES]
            yg = y * gain_ref[...]
            if rope:
                res = _dot(jnp.concatenate([(y * y).astype(BF16), yg.astype(BF16)], axis=1), rhs)
                inv = lax.rsqrt(res[:, :LANES] * (1.0 / LANES) + EPS)
                out = (yg * cos_ref[...] + res[:, LANES:] * sin_ref[...]) * inv
            else:
                ss = _dot(jnp.concatenate(_split_bf16(y * y), axis=1), rhs)
                out = yg * lax.rsqrt(ss * (1.0 / LANES) + EPS)
            out_ref[:, g * LANES:(g + 1) * LANES] = out.astype(out_ref.dtype)

    @pl.when(j == 0)
    def _():
        qk_epilogue(qg_ref, q_ref, prologue=True)

    @pl.when((j > 0) & (j < n_sec))
    def _():
        qk_epilogue(qg_ref, q_ref)

    @pl.when((j >= n_sec) & (j < 2 * n_sec))
    def _():
        qk_epilogue(kg_ref, k_ref)

    @pl.when(j >= 2 * n_sec)
    def _():
        v_ref[...] = _dot(h_scr[...], w_ref[...]).astype(v_ref.dtype)


def _qkv_proj(x, mods, norm_g, w, q_gain, k_gain, tables, *, cond_base, tok_per_cond, kv_dtype):
    t, d = x.shape
    tm = _tile(math.gcd(t, tok_per_cond), 1024, 8)
    tn = _tile(d, 512, LANES)
    n_sec = d // tn
    rope = tables is not None
    row = lambda i, j: cond_base + (i * tm) // tok_per_cond
    in_specs = [
        pl.BlockSpec((tm, d), lambda i, j: (i, 0)),
        mods.spec(_Mods.SHIFT1, row),
        mods.spec(_Mods.SCALE1, row),
        pl.BlockSpec((1, d), lambda i, j: (0, 0)),
        pl.BlockSpec((d, tn), lambda i, j: (0, j)),
        pl.BlockSpec((1, LANES), lambda i, j: (0, 0)),
        pl.BlockSpec((1, LANES), lambda i, j: (0, 0)),
    ]
    args = [x, mods.table, mods.table, norm_g, w, q_gain, k_gain]
    if rope:
        nblk = tables[0].shape[0] // tm
        for tab in tables:
            in_specs.append(pl.BlockSpec((tm, LANES), lambda i, j: (i % nblk, 0)))
            args.append(tab)
    out_specs = [
        pl.BlockSpec((tm, tn), lambda i, j, s=s: (i, jnp.clip(j - s * n_sec, 0, n_sec - 1)))
        for s in range(3)
    ]
    return pl.pallas_call(
        functools.partial(_qkv_kernel, rope=rope, tn=tn, n_sec=n_sec),
        grid=(t // tm, 3 * n_sec),
        in_specs=in_specs,
        out_specs=out_specs,
        out_shape=[jax.ShapeDtypeStruct((t, d), BF16), jax.ShapeDtypeStruct((t, d), kv_dtype),
                   jax.ShapeDtypeStruct((t, d), kv_dtype)],
        scratch_shapes=[pltpu.VMEM((tm, d), BF16)],
        compiler_params=_params(("arbitrary", "arbitrary")),
        name="qkv_proj_rope" if rope else "qkv_proj",
    )(*args)


def _rope_tables(seq):
    n_freq = LANES // 4
    pos = jnp.arange(seq, dtype=jnp.int32)
    r = (pos // GRID_W).astype(F32)
    col = (pos % GRID_W).astype(F32)
    freqs = ROPE_BASE ** (-jnp.arange(n_freq, dtype=F32) / n_freq)
    cos_t, sin_t = [], []
    for p in (r, col):
        ang = p[:, None] * freqs
        cos, sin = jnp.cos(ang), jnp.sin(ang)
        cos_t += [cos, cos]
        sin_t += [-sin, sin]
    return jnp.concatenate(cos_t, axis=1), jnp.concatenate(sin_t, axis=1)


def _attn_kernel(*refs, n_cache, n_chunks, tk, lambda_init, cast_blocks, heads_per_step):
    n_cast = len(cast_blocks)
    lam_ref, g_ref, q_ref = refs[:3]
    n_in = 3 + (4 if n_cache else 2)
    if n_cache:
        kc_ref, vtc_ref, k_ref, vt_ref = refs[3:n_in]
    else:
        k_ref, vt_ref = refs[3:n_in]
    cast_in = refs[n_in:n_in + n_cast]
    o_ref = refs[n_in + n_cast]
    cast_out = refs[n_in + n_cast + 1:n_in + 2 * n_cast + 1]
    acc_ref = refs[-1]

    step = (pl.program_id(0) * pl.num_programs(1) + pl.program_id(1)) * pl.num_programs(2) + pl.program_id(2)
    for src, dst, n_blocks in zip(cast_in, cast_out, cast_blocks):
        @pl.when(step < n_blocks)
        def _(src=src, dst=dst):
            dst[...] = src[...].astype(dst.dtype)

    vd = 2 * LANES
    for hq in range(heads_per_step):
        cols = slice(hq * vd, (hq + 1) * vd)
        _attn_head(lam_ref, g_ref, q_ref.at[:, cols],
                   kc_ref.at[:, cols] if n_cache else None, vtc_ref.at[hq] if n_cache else None,
                   k_ref.at[:, cols], vt_ref.at[hq], o_ref.at[:, cols], acc_ref,
                   n_chunks=n_chunks, tk=tk, lambda_init=lambda_init)


def _attn_head(lam_ref, g_ref, q_ref, kc_ref, vtc_ref, k_ref, vt_ref, o_ref, acc_ref, *,
               n_chunks, tk, lambda_init):
    q = q_ref[...]
    qs = (q[:, :LANES], q[:, LANES:])

    def scores(kblk):
        return [lax.dot_general(kblk[:, c * LANES:(c + 1) * LANES], qs[c],
                                (((1,), (1,)), ((), ())), preferred_element_type=F32)
                for c in range(2)]

    def accumulate(sts, vt, stats):
        out = []
        for c in range(2):
            st = sts[c]
            m_cur = jnp.max(st, axis=0, keepdims=True)
            if stats is None:
                m_new = m_cur
                p = jnp.exp2(st - m_new)
                l_new = jnp.sum(p, axis=0, keepdims=True)
                acc_ref[c] = _dot(vt, p.astype(BF16))
            else:
                m_prev, l_prev = stats[c]
                m_new = jnp.maximum(m_prev, m_cur)
                alpha = jnp.exp2(m_prev - m_new)
                p = jnp.exp2(st - m_new)
                l_new = alpha * l_prev + jnp.sum(p, axis=0, keepdims=True)
                acc_ref[c] = alpha * acc_ref[c] + _dot(vt, p.astype(BF16))
            out.append((m_new, l_new))
        return out

    chunks = []
    if kc_ref is not None:
        chunks.append((lambda: kc_ref[...].astype(BF16), lambda: vtc_ref[...]))
    for i in range(n_chunks):
        chunks.append((lambda i=i: k_ref[i * tk:(i + 1) * tk, :].astype(BF16), lambda i=i: vt_ref[i]))
    stats = None
    sts = scores(chunks[0][0]())
    for n, (_, values) in enumerate(chunks):
        nxt = scores(chunks[n + 1][0]()) if n + 1 < len(chunks) else None
        stats = accumulate(sts, values(), stats)
        sts = nxt

    lp = lam_ref[...]
    lam = (jnp.exp(jnp.sum(lp[0:1] * lp[1:2], axis=-1, keepdims=True))
           - jnp.exp(jnp.sum(lp[2:3] * lp[3:4], axis=-1, keepdims=True)) + lambda_init)
    o_t = acc_ref[0] / stats[0][1] - lam * (acc_ref[1] / stats[1][1])
    o = o_t.T
    o = o * lax.rsqrt(jnp.mean(o * o, axis=-1, keepdims=True) + EPS) * g_ref[...]
    o_ref[...] = (o * (1.0 - lambda_init)).astype(o_ref.dtype)


def _chunked_transpose(v, batch, seq, heads, vd, tk):
    v = v.astype(BF16).reshape(batch, seq // tk, tk, heads, vd)
    return v.transpose(0, 3, 1, 4, 2).reshape(batch * heads, seq // tk, vd, tk)


def _cast_block_rows(rows, n_steps):
    blk = 16
    while rows % blk or rows // blk > n_steps:
        blk += 16
        if blk > rows:
            raise ValueError(f"no cast block for {rows} rows in {n_steps} steps")
    return blk


def _diff_attention(q, k, v, cache_k, cache_v, lam_params, subln_g, *, batch, seq, layer, casts=()):
    t, d = q.shape
    vd = subln_g.shape[-1]
    heads = d // vd
    n_cache = 0 if cache_k is None else cache_k.shape[0] // batch
    tq = _tile(seq, 512, LANES)
    tk = _tile(seq, 512, LANES)
    nq = seq // tq
    n_chunks = seq // tk
    hps = heads if seq <= tq else 2
    groups = heads // hps
    wide = hps * vd
    in_specs = [
        pl.BlockSpec(lam_params.shape, lambda b, h, i: (0, 0)),
        pl.BlockSpec((1, vd), lambda b, h, i: (0, 0)),
        pl.BlockSpec((tq, wide), lambda b, h, i: (b * nq + i, h)),
    ]
    args = [lam_params, subln_g, q]
    if n_cache:
        in_specs += [pl.BlockSpec((n_cache, wide), lambda b, h, i: (b, h)),
                     pl.BlockSpec((hps, None, vd, n_cache), lambda b, h, i: (b * groups + h, 0, 0, 0))]
        args += [cache_k, _chunked_transpose(cache_v, batch, n_cache, heads, vd, n_cache)]
    in_specs += [pl.BlockSpec((seq, wide), lambda b, h, i: (b, h)),
                 pl.BlockSpec((hps, n_chunks, vd, tk), lambda b, h, i: (b * groups + h, 0, 0, 0))]
    args += [k, _chunked_transpose(v, batch, seq, heads, vd, tk)]
    out_specs = [pl.BlockSpec((tq, wide), lambda b, h, i: (b * nq + i, h))]
    out_shape = [jax.ShapeDtypeStruct((t, d), BF16)]
    n_steps = batch * groups * nq
    cast_blocks = []
    for w in casts:
        rows, cols = w.shape
        blk = _cast_block_rows(rows, n_steps)
        n_blocks = rows // blk
        cast_blocks.append(n_blocks)
        spec = pl.BlockSpec((blk, cols), lambda b, h, i, n=n_blocks: (
            jnp.minimum((b * groups + h) * nq + i, n - 1), 0))
        in_specs.append(spec)
        args.append(w)
        out_specs.append(spec)
        out_shape.append(jax.ShapeDtypeStruct((rows, cols), BF16))
    res = pl.pallas_call(
        functools.partial(_attn_kernel, n_cache=n_cache, n_chunks=n_chunks, tk=tk,
                          lambda_init=_lambda_init(layer), cast_blocks=tuple(cast_blocks),
                          heads_per_step=hps),
        grid=(batch, groups, nq),
        in_specs=in_specs,
        out_specs=out_specs,
        out_shape=out_shape,
        scratch_shapes=[pltpu.VMEM((2, vd, tq), F32)],
        compiler_params=_params(("arbitrary", "arbitrary", "arbitrary")),
        name=f"diff_attention_{'latent' if n_cache else 'context'}",
    )(*args)
    return res[0], list(res[1:])


def _out_proj_kernel(op_ref, os_ref, w_ref, xp_ref, xs_ref, gate_ref, out_ref, *, n_prompt_tiles):
    is_prompt = pl.program_id(0) < n_prompt_tiles

    @pl.when(is_prompt)
    def _():
        out_ref[...] = xp_ref[...] + gate_ref[...] * _dot(op_ref[...], w_ref[...])

    @pl.when(jnp.logical_not(is_prompt))
    def _():
        out_ref[...] = xs_ref[...] + gate_ref[...] * _dot(os_ref[...], w_ref[...])


def _out_proj_residual(o_p, o_s, w, x_p, x_s, mods, *, tok_per_cond):
    n_prompt, d = x_p.shape
    t = n_prompt + x_s.shape[0]
    tm = _tile(math.gcd(n_prompt, tok_per_cond), 256, 8)
    tn = d
    n_pt = n_prompt // tm
    row = _cond_row_joint(tm, n_prompt, tok_per_cond)
    p_rows = lambda i: jnp.minimum(i, n_pt - 1)
    s_rows = lambda i: jnp.maximum(i - n_pt, 0)
    return pl.pallas_call(
        functools.partial(_out_proj_kernel, n_prompt_tiles=n_pt),
        grid=(t // tm, d // tn),
        in_specs=[
            pl.BlockSpec((tm, d), lambda i, j: (p_rows(i), 0)),
            pl.BlockSpec((tm, d), lambda i, j: (s_rows(i), 0)),
            pl.BlockSpec((d, tn), lambda i, j: (0, j)),
            pl.BlockSpec((tm, tn), lambda i, j: (p_rows(i), jnp.where(i < n_pt, j, d // tn - 1))),
            pl.BlockSpec((tm, tn), lambda i, j: (s_rows(i), jnp.where(i < n_pt, 0, j))),
            mods.spec(_Mods.GATE1, row, width=tn, col_fn=lambda i, j: j),
        ],
        out_specs=pl.BlockSpec((tm, tn), lambda i, j: (i, j)),
        out_shape=jax.ShapeDtypeStruct((t, d), F32),
        compiler_params=_params(("arbitrary", "arbitrary")),
        name="attn_out_proj",
    )(o_p, o_s, w, x_p, x_s, mods.table)


def _ffn_kernel(x_ref, sh_ref, sc_ref, gate_ref, g_ref, wg_ref, wu_ref, wd_ref, o_ref, h_scr):
    f = pl.program_id(1)

    def swiglu_part(h):
        gte = _dot(h, wg_ref[...])
        up = _dot(h, wu_ref[...])
        act = (gte * jax.nn.sigmoid(gte) * up).astype(BF16)
        return _dot(act, wd_ref[...])

    @pl.when(f == 0)
    def _():
        h = _norm_modulate(x_ref[...], g_ref[...], sc_ref[...], sh_ref[...]).astype(BF16)
        h_scr[...] = h
        o_ref[...] = swiglu_part(h)

    @pl.when(f > 0)
    def _():
        o_ref[...] += swiglu_part(h_scr[...])

    @pl.when(f == pl.num_programs(1) - 1)
    def _():
        o_ref[...] = x_ref[...] + gate_ref[...] * o_ref[...]


def _cond_row_joint(tm, n_prompt, tok_per_cond):
    def row(i, *_):
        tok = i * tm
        return jnp.where(tok < n_prompt, 0, 1 + (tok - n_prompt) // tok_per_cond)
    return row


def _dense_ffn(x, mods, norm_g, w_gu, w_down, *, n_prompt, tok_per_cond):
    t, d = x.shape
    ff = w_down.shape[0]
    tm = _tile(math.gcd(n_prompt, tok_per_cond), 1024, 8)
    tf = _tile(ff, 256, LANES)
    nf = ff // tf
    row = _cond_row_joint(tm, n_prompt, tok_per_cond)
    return pl.pallas_call(
        _ffn_kernel,
        grid=(t // tm, nf),
        in_specs=[
            pl.BlockSpec((tm, d), lambda i, f: (i, 0)),
            mods.spec(_Mods.SHIFT2, row),
            mods.spec(_Mods.SCALE2, row),
            mods.spec(_Mods.GATE2, row),
            pl.BlockSpec((1, d), lambda i, f: (0, 0)),
            pl.BlockSpec((d, tf), lambda i, f: (0, f)),
            pl.BlockSpec((d, tf), lambda i, f: (0, nf + f)),
            pl.BlockSpec((tf, d), lambda i, f: (f, 0)),
        ],
        out_specs=pl.BlockSpec((tm, d), lambda i, f: (i, 0)),
        out_shape=jax.ShapeDtypeStruct((t, d), F32),
        scratch_shapes=[pltpu.VMEM((tm, d), BF16)],
        compiler_params=_params(("arbitrary", "arbitrary")),
        name="dense_ffn",
    )(x, mods.table, mods.table, mods.table, norm_g, w_gu, w_gu, w_down)


def _pool_kernel(x_ref, xp_ref, xn_ref, sh_ref, sc_ref, gate_ref, g_ref, w_ref, ps_ref, o_ref, *,
                 tp, halo, n_prompt_tiles, prompt_tiles_per_seq, sample_tiles_per_seq, group_dim):
    i = pl.program_id(0)
    is_prompt = i < n_prompt_tiles
    tiles_per_seq = jnp.where(is_prompt, prompt_tiles_per_seq, sample_tiles_per_seq)
    local = lax.rem(jnp.where(is_prompt, i, i - n_prompt_tiles), tiles_per_seq)
    first = local == 0
    last = local == tiles_per_seq - 1
    seq_len = tiles_per_seq * tp

    g, sc, sh = g_ref[...], sc_ref[...], sh_ref[...]
    x = x_ref[...]
    h = _norm_modulate(x, g, sc, sh)
    h_prev = jnp.where(first, 0.0, _norm_modulate(xp_ref[...], g, sc, sh))
    h_next = jnp.where(last, 0.0, _norm_modulate(xn_ref[...], g, sc, sh))
    d = h.shape[-1]
    h_halo = jnp.concatenate([h_prev, h_next, jnp.zeros((LANES - 2 * halo, d), F32)], axis=0)

    t_idx = lax.broadcasted_iota(jnp.int32, (tp, tp), 0)
    s_idx = lax.broadcasted_iota(jnp.int32, (tp, tp), 1)
    t_h = lax.broadcasted_iota(jnp.int32, (tp, LANES), 0)
    u_h = lax.broadcasted_iota(jnp.int32, (tp, LANES), 1)
    off_h = jnp.where(u_h < halo, u_h - halo, tp + u_h - halo)
    pos = local * tp + lax.broadcasted_iota(jnp.int32, (tp, 1), 0)

    for grp, win in enumerate(POOL_WINDOWS):
        lo, hi = win // 2, win // 2 - 1
        cols = slice(grp * group_dim, (grp + 1) * group_dim)
        band = _mask_bf16((s_idx >= t_idx - lo) & (s_idx <= t_idx + hi))
        band_h = _mask_bf16((u_h < 2 * halo) & (off_h >= t_h - lo) & (off_h <= t_h + hi))
        total = _dot(band, h[:, cols].astype(BF16)) + _dot(band_h, h_halo[:, cols].astype(BF16))
        cnt = (jnp.minimum(pos + hi, seq_len - 1) - jnp.maximum(pos - lo, 0) + 1).astype(F32)
        pooled = (total / cnt - h[:, cols]).astype(BF16)
        y = _dot(pooled, w_ref[grp]) * ps_ref[:, cols]
        o_ref[:, cols] = x[:, cols] + gate_ref[:, cols] * y


def _pool_mixer(x, mods, norm_g, pool_w, pool_scale, *, n_prompt, prompt_seq, sample_seq):
    t, d = x.shape
    halo = max(POOL_WINDOWS) // 2
    tp = _tile(math.gcd(prompt_seq, sample_seq), 256, 8)
    n_groups, group_dim = pool_w.shape[0], pool_w.shape[1]
    hb = tp // halo
    n_halo_blocks = t // halo
    row = _cond_row_joint(tp, n_prompt, sample_seq)
    return pl.pallas_call(
        functools.partial(_pool_kernel, tp=tp, halo=halo, n_prompt_tiles=n_prompt // tp,
                          prompt_tiles_per_seq=prompt_seq // tp,
                          sample_tiles_per_seq=sample_seq // tp, group_dim=group_dim),
        grid=(t // tp,),
        in_specs=[
            pl.BlockSpec((tp, d), lambda i: (i, 0)),
            pl.BlockSpec((halo, d), lambda i: (jnp.maximum(i * hb - 1, 0), 0)),
            pl.BlockSpec((halo, d), lambda i: (jnp.minimum((i + 1) * hb, n_halo_blocks - 1), 0)),
            mods.spec(_Mods.SHIFT1, row),
            mods.spec(_Mods.SCALE1, row),
            mods.spec(_Mods.GATE1, row),
            pl.BlockSpec((1, d), lambda i: (0, 0)),
            pl.BlockSpec((n_groups, group_dim, group_dim), lambda i: (0, 0, 0)),
            pl.BlockSpec((1, d), lambda i: (0, 0)),
        ],
        out_specs=pl.BlockSpec((tp, d), lambda i: (i, 0)),
        out_shape=jax.ShapeDtypeStruct((t, d), F32),
        compiler_params=_params(("arbitrary",)),
        name="pool_mixer",
    )(x, x, x, mods.table, mods.table, mods.table, norm_g, pool_w, pool_scale)


def _router_kernel(x_ref, sh_ref, sc_ref, g_ref, wr_ref, br_ref, route_ref, cnt_ref):
    h = _norm_modulate(x_ref[...], g_ref[...], sc_ref[...], sh_ref[...])
    h_hi, h_lo = _split_bf16(h)
    w_hi, w_lo = _split_bf16(wr_ref[...])
    logits = _dot(h_hi, w_hi) + _dot(h_lo, w_hi) + _dot(h_hi, w_lo) + br_ref[...]

    rows = logits.shape[0]
    lane = lax.broadcasted_iota(jnp.int32, logits.shape, 1).astype(F32)
    m1 = jnp.max(logits, axis=-1, keepdims=True)
    i1 = jnp.min(jnp.where(logits == m1, lane, float(LANES)), axis=-1, keepdims=True)
    oh1 = lane == i1
    rest = jnp.where(oh1, -jnp.inf, logits)
    m2 = jnp.max(rest, axis=-1, keepdims=True)
    i2 = jnp.min(jnp.where(rest == m2, lane, float(LANES)), axis=-1, keepdims=True)
    oh2 = lane == i2
    e = jnp.exp(m2 - m1)
    gate_a = 1.0 / (1.0 + e)
    gate_b = e / (1.0 + e)

    sel = jnp.where(oh1 | oh2, 1.0, 0.0)
    r_idx = lax.broadcasted_iota(jnp.int32, (rows, rows), 0)
    c_idx = lax.broadcasted_iota(jnp.int32, (rows, rows), 1)
    earlier = _mask_bf16(c_idx < r_idx)
    rank = _dot(earlier, sel.astype(BF16))
    rank_a = jnp.sum(jnp.where(oh1, rank, 0.0), axis=-1, keepdims=True)
    rank_b = jnp.sum(jnp.where(oh2, rank, 0.0), axis=-1, keepdims=True)

    route = jnp.zeros(logits.shape, F32)
    for k, val in enumerate((i1, i2, rank_a, rank_b, gate_a, gate_b)):
        route = jnp.where(lane == float(k), val, route)
    route_ref[...] = route
    cnt_ref[...] = jnp.sum(sel, axis=0, keepdims=True)


def _moe_router(x, mods, norm_g, w_router, b_router, *, n_prompt, tok_per_cond):
    t, d = x.shape
    n_exp = w_router.shape[1]
    ch = ROUTE_CHUNK
    wr = jnp.zeros((d, LANES), F32).at[:, :n_exp].set(w_router.astype(F32))
    br = jnp.full((1, LANES), -jnp.inf, F32).at[0, :n_exp].set(b_router.astype(F32))
    row = _cond_row_joint(ch, n_prompt, tok_per_cond)
    return pl.pallas_call(
        _router_kernel,
        grid=(t // ch,),
        in_specs=[
            pl.BlockSpec((ch, d), lambda i: (i, 0)),
            mods.spec(_Mods.SHIFT2, row),
            mods.spec(_Mods.SCALE2, row),
            pl.BlockSpec((1, d), lambda i: (0, 0)),
            pl.BlockSpec((d, LANES), lambda i: (0, 0)),
            pl.BlockSpec((1, LANES), lambda i: (0, 0)),
        ],
        out_specs=[
            pl.BlockSpec((ch, LANES), lambda i: (i, 0)),
            pl.BlockSpec((None, 1, LANES), lambda i: (i, 0, 0)),
        ],
        out_shape=[
            jax.ShapeDtypeStruct((t, LANES), F32),
            jax.ShapeDtypeStruct((t // ch, 1, LANES), F32),
        ],
        compiler_params=_params(("arbitrary",)),
        name="moe_router",
    )(x, mods.table, mods.table, norm_g, wr, br)


def _dispatch_plan(route, counts, n_exp):
    ch, gr = ROUTE_CHUNK, GEMM_ROWS
    t = route.shape[0]
    n_chunks = t // ch
    i32 = jnp.int32
    cnt = counts[:, 0, :n_exp].astype(i32)
    total = cnt.sum(0)
    padded = ((total + gr - 1) // gr) * gr
    off = jnp.cumsum(padded) - padded
    start = off[None, :] + jnp.cumsum(cnt, axis=0) - cnt

    def dest(choice):
        e = route[:, choice].astype(i32).reshape(n_chunks, ch, 1)
        hit = e == jnp.arange(n_exp, dtype=i32)
        base = jnp.sum(jnp.where(hit, start[:, None, :], 0), axis=-1)
        return base.reshape(t) + route[:, 2 + choice].astype(i32)

    dest_a, dest_b = dest(0), dest(1)

    n_gemm_tiles = (t * TOP_K) // gr + n_exp

    tile_lo = jnp.arange(n_gemm_tiles, dtype=i32) * gr
    used = tile_lo < (off + padded)[-1]
    ends = off + padded
    expert_of = jnp.minimum(jnp.sum(tile_lo[:, None] >= ends[None, :], axis=1), n_exp - 1).astype(i32)
    n_used = jnp.maximum(used.sum(), 1)
    clamp = jnp.minimum(jnp.arange(n_gemm_tiles, dtype=i32), n_used - 1).astype(i32)
    token_rows = jnp.clip((off + total)[expert_of] - tile_lo, 0, gr).astype(i32)
    gemm_list = (clamp, expert_of[clamp], used.astype(i32), token_rows[clamp])

    tail = jnp.arange(n_gemm_tiles - n_exp, n_gemm_tiles, dtype=i32)
    zero_rows = jnp.concatenate([jnp.maximum(ends - gr, 0), tail * gr]).astype(i32)
    zero_ok = jnp.concatenate([padded > 0, tail >= used.sum()]).astype(i32)

    dest_rows = jnp.stack([dest_a.reshape(n_chunks, ch), dest_b.reshape(n_chunks, ch)], axis=1)
    gates = jnp.concatenate([route[:, 4:6], jnp.zeros((t, 6), F32)], axis=1)
    return dest_rows, gates, gemm_list, (zero_rows, zero_ok), n_gemm_tiles * gr


def _wait_rows(src, dst, sem, n_rows):
    pltpu.make_async_copy(src.at[pl.ds(0, n_rows), :], dst.at[pl.ds(0, n_rows), :], sem).wait()


def _scatter_kernel(zrow_ref, zok_ref, dest_ref, x_ref, sh_ref, sc_ref, g_ref, out_hbm,
                    h_scr, z_scr, sem, zsem, *, tile_rows):
    ch = x_ref.shape[0]
    step, n_steps = pl.program_id(0), pl.num_programs(0)
    slot = lax.rem(step, 2)

    @pl.when(step == 0)
    def _():
        z_scr[...] = jnp.zeros(z_scr.shape, F32)
        zr = z_scr.shape[0]

        def zero_copy(n, part):
            row0 = pl.multiple_of(zrow_ref[n] + part * zr, zr)
            return pltpu.make_async_copy(z_scr, out_hbm.at[pl.ds(row0, zr), :], zsem)

        for n in range(zrow_ref.shape[0]):
            @pl.when(zok_ref[n] == 1)
            def _(n=n):
                for part in range(tile_rows // zr):
                    zero_copy(n, part).start()
        for n in range(zrow_ref.shape[0]):
            @pl.when(zok_ref[n] == 1)
            def _(n=n):
                for part in range(tile_rows // zr):
                    zero_copy(n, part).wait()

    h_scr[slot] = _norm_modulate(x_ref[...], g_ref[...], sc_ref[...], sh_ref[...])

    src_rows = h_scr.at[slot]
    for j in range(ch):
        for k in range(TOP_K):
            pltpu.make_async_copy(src_rows.at[pl.ds(j, 1), :],
                                  out_hbm.at[pl.ds(dest_ref[k, j], 1), :],
                                  sem.at[slot]).start(priority=k)

    def wait_slot(s):
        for _ in range(TOP_K):
            _wait_rows(h_scr.at[s], out_hbm, sem.at[s], ch)

    @pl.when(step > 0)
    def _():
        wait_slot(1 - slot)

    @pl.when(step == n_steps - 1)
    def _():
        wait_slot(slot)


def _moe_scatter(x, mods, norm_g, dest_rows, zero_tiles, n_rows, *, n_prompt, tok_per_cond):
    t, d = x.shape
    ch = ROUTE_CHUNK
    row = _cond_row_joint(ch, n_prompt, tok_per_cond)
    grid_spec = pltpu.PrefetchScalarGridSpec(
        num_scalar_prefetch=2,
        grid=(t // ch,),
        in_specs=[
            pl.BlockSpec((None, TOP_K, ch), lambda i, *_: (i, 0, 0), memory_space=pltpu.SMEM),
            pl.BlockSpec((ch, d), lambda i, *_: (i, 0)),
            mods.spec(_Mods.SHIFT2, row),
            mods.spec(_Mods.SCALE2, row),
            pl.BlockSpec((1, d), lambda i, *_: (0, 0)),
        ],
        out_specs=pl.BlockSpec(memory_space=pl.ANY),
        scratch_shapes=[pltpu.VMEM((2, ch, d), F32), pltpu.VMEM((ch, d), F32),
                        pltpu.SemaphoreType.DMA((2,)), pltpu.SemaphoreType.DMA(())],
    )
    return pl.pallas_call(
        functools.partial(_scatter_kernel, tile_rows=GEMM_ROWS),
        grid_spec=grid_spec,
        out_shape=jax.ShapeDtypeStruct((n_rows, d), F32),
        compiler_params=_params(("arbitrary",)),
        name="moe_scatter",
    )(*zero_tiles, dest_rows, x, mods.table, mods.table, norm_g)


def _expert_kernel(blk_ref, exp_ref, used_ref, rows_ref, x_ref, wg_ref, wu_ref, wd_ref, o_ref, x_scr, *,
                   sub_rows):
    r, f = pl.program_id(0), pl.program_id(1)
    tile_rows = o_ref.shape[0]

    def swiglu_part(x):
        gte = _dot(x, wg_ref[...])
        up = _dot(x, wu_ref[...])
        act = (gte * jax.nn.sigmoid(gte) * up).astype(BF16)
        return _dot(act, wd_ref[...])

    @pl.when(used_ref[r] == 1)
    def _():
        valid = rows_ref[r]
        full = valid > tile_rows - sub_rows

        @pl.when(full & (f == 0))
        def _():
            x = x_ref[...].astype(BF16)
            x_scr[...] = x
            o_ref[...] = swiglu_part(x)

        @pl.when(full & (f > 0))
        def _():
            o_ref[...] += swiglu_part(x_scr[...])

        @pl.when(jnp.logical_not(full))
        def _():
            @pl.when(f == 0)
            def _():
                o_ref[...] = jnp.zeros(o_ref.shape, F32)
                x_scr[...] = x_ref[...].astype(BF16)

            for sb in range(tile_rows // sub_rows - 1):
                @pl.when(valid > sb * sub_rows)
                def _(sb=sb):
                    rows = slice(sb * sub_rows, (sb + 1) * sub_rows)
                    o_ref[rows, :] += swiglu_part(x_scr[rows, :])


def _moe_experts(xs, w_gu, w_down, gemm_list):
    n_rows, d = xs.shape
    ff = w_down.shape[1]
    gr = GEMM_ROWS
    tf = _tile(ff, 256, LANES)
    nf = ff // tf
    n_tiles = gemm_list[0].shape[0]

    def fcol(f, used, r):
        return jnp.where(used[r] == 1, f, nf - 1)

    grid_spec = pltpu.PrefetchScalarGridSpec(
        num_scalar_prefetch=4,
        grid=(n_tiles, nf),
        in_specs=[
            pl.BlockSpec((gr, d), lambda r, f, blk, ex, used, rows: (blk[r], 0)),
            pl.BlockSpec((None, d, tf), lambda r, f, blk, ex, used, rows: (ex[r], 0, fcol(f, used, r))),
            pl.BlockSpec((None, d, tf), lambda r, f, blk, ex, used, rows: (ex[r], 0, nf + fcol(f, used, r))),
            pl.BlockSpec((None, tf, d), lambda r, f, blk, ex, used, rows: (ex[r], fcol(f, used, r), 0)),
        ],
        out_specs=pl.BlockSpec((gr, d), lambda r, f, blk, ex, used, rows: (blk[r], 0)),
        scratch_shapes=[pltpu.VMEM((gr, d), BF16)],
    )
    return pl.pallas_call(
        functools.partial(_expert_kernel, sub_rows=ROUTE_CHUNK),
        grid_spec=grid_spec,
        out_shape=jax.ShapeDtypeStruct((n_rows, d), F32),
        input_output_aliases={4: 0},
        compiler_params=_params(("arbitrary", "arbitrary")),
        name="moe_experts",
    )(*gemm_list, xs, w_gu, w_gu, w_down)


def _combine_kernel(dest_ref, dest_next_ref, gates_ref, x_ref, gate_ref, y_hbm, op_ref, os_ref,
                    y_scr, sem, *, n_prompt_chunks):
    ch = x_ref.shape[0]
    step, n_steps = pl.program_id(0), pl.num_programs(0)
    slot = lax.rem(step, 2)

    def gather(idx_ref, s):
        for j in range(ch):
            for k in range(TOP_K):
                pltpu.make_async_copy(y_hbm.at[pl.ds(idx_ref[k, j], 1), :],
                                      y_scr.at[s, k, pl.ds(j, 1), :], sem.at[s]).start(priority=k)

    @pl.when(step == 0)
    def _():
        gather(dest_ref, slot)

    @pl.when(step + 1 < n_steps)
    def _():
        gather(dest_next_ref, 1 - slot)

    for k in range(TOP_K):
        _wait_rows(y_hbm, y_scr.at[slot, k], sem.at[slot], ch)

    gates = gates_ref[...]
    mix = gates[:, 0:1] * y_scr[slot, 0] + gates[:, 1:2] * y_scr[slot, 1]
    out = x_ref[...] + gate_ref[...] * mix
    is_prompt = step < n_prompt_chunks

    @pl.when(is_prompt)
    def _():
        op_ref[...] = out

    @pl.when(jnp.logical_not(is_prompt))
    def _():
        os_ref[...] = out


def _moe_combine(y, dest_rows, gates, x, mods, *, n_prompt, tok_per_cond):
    t, d = x.shape
    ch = ROUTE_CHUNK
    n_pc = n_prompt // ch
    row = _cond_row_joint(ch, n_prompt, tok_per_cond)
    return pl.pallas_call(
        functools.partial(_combine_kernel, n_prompt_chunks=n_pc),
        grid=(t // ch,),
        in_specs=[
            pl.BlockSpec((None, TOP_K, ch), lambda i: (i, 0, 0), memory_space=pltpu.SMEM),
            pl.BlockSpec((None, TOP_K, ch), lambda i: (jnp.minimum(i + 1, t // ch - 1), 0, 0),
                         memory_space=pltpu.SMEM),
            pl.BlockSpec((ch, gates.shape[1]), lambda i: (i, 0)),
            pl.BlockSpec((ch, d), lambda i: (i, 0)),
            mods.spec(_Mods.GATE2, row),
            pl.BlockSpec(memory_space=pl.ANY),
        ],
        out_specs=[
            pl.BlockSpec((ch, d), lambda i: (jnp.minimum(i, n_pc - 1), 0)),
            pl.BlockSpec((ch, d), lambda i: (jnp.maximum(i - n_pc, 0), 0)),
        ],
        out_shape=[jax.ShapeDtypeStruct((n_prompt, d), F32), jax.ShapeDtypeStruct((t - n_prompt, d), F32)],
        scratch_shapes=[pltpu.VMEM((2, TOP_K, ch, d), F32), pltpu.SemaphoreType.DMA((2,))],
        compiler_params=_params(("arbitrary",)),
        name="moe_combine",
    )(dest_rows, dest_rows, gates, x, mods.table, y)


def kernel(x_prompt, x_sample, c, cache_k, cache_v, c_ctx, ada_w, ada_b, norm1_g, norm2_g,
           attn_w_qkv, attn_w_o, attn_q_norm, attn_k_norm, attn_lambda, attn_subln_g,
           pool_w, pool_scale, ffn_w_gu, ffn_w_down,
           moe_w_router, moe_b_router, moe_w_gu, moe_w_down):
    b_ctx, l_ctx, d = x_prompt.shape
    b_dec, l_dec, _ = x_sample.shape
    depth = ada_w.shape[0]
    l_past, heads, v_dim = cache_k.shape[2:]
    head_dim = v_dim // 2
    assert head_dim == LANES and depth == 2 and l_dec % GRID_W == 0
    n_exp = moe_w_router.shape[-1]
    tp_, ts_ = b_ctx * l_ctx, b_dec * l_dec

    cond_rows = 8 * ((1 + b_dec + 7) // 8)
    cond = jnp.zeros((cond_rows, d), F32).at[0].set(c_ctx).at[1:1 + b_dec].set(c)
    mod_table = _adaln_mods(cond, ada_w, ada_b).reshape(depth * cond_rows * 6, 1, d)

    xp = x_prompt.reshape(tp_, d)
    xs = x_sample.reshape(ts_, d)

    layer, j = 0, 0
    mods = _Mods(mod_table, cond_rows, layer)
    g1 = norm1_g[layer].reshape(1, d)
    w_qkv = attn_w_qkv[j].astype(BF16)
    q_gain = (attn_q_norm[j] * (head_dim ** -0.5 * LOG2E)).reshape(1, head_dim)
    k_gain = attn_k_norm[j].reshape(1, head_dim)
    tables = _rope_tables(l_dec)
    prompt = dict(cond_base=0, tok_per_cond=tp_)
    sample = dict(cond_base=1, tok_per_cond=l_dec)

    q_p, k_p, v_p = _qkv_proj(xp, mods, g1, w_qkv, q_gain, k_gain, None, kv_dtype=F32, **prompt)
    q_s, k_s, v_s = _qkv_proj(xs, mods, g1, w_qkv, q_gain, k_gain, tables, kv_dtype=BF16, **sample)

    lam_params = attn_lambda[j].astype(F32)
    subln = attn_subln_g[j].reshape(1, v_dim)
    o_p, _ = _diff_attention(q_p, k_p, v_p, None, None, lam_params, subln,
                             batch=b_ctx, seq=l_ctx, layer=layer)
    ck = cache_k[:, j].reshape(b_dec * l_past, heads * v_dim)
    cv = cache_v[:, j].reshape(b_dec * l_past, heads * v_dim)
    later_weights = [attn_w_o[j], ffn_w_gu[j], ffn_w_down[j], pool_w[0], moe_w_gu[0], moe_w_down[0]]
    o_s, later_bf16 = _diff_attention(q_s, k_s, v_s, ck, cv, lam_params, subln,
                                      batch=b_dec, seq=l_dec, layer=layer,
                                      casts=[w.reshape(-1, w.shape[-1]) for w in later_weights])
    w_o, w_ffn_gu, w_ffn_down, w_pool, w_moe_gu, w_moe_down = [
        b.reshape(w.shape) for b, w in zip(later_bf16, later_weights)]

    x1 = _out_proj_residual(o_p, o_s, w_o, xp, xs, mods, tok_per_cond=l_dec)

    x2 = _dense_ffn(x1, mods, norm2_g[layer].reshape(1, d), w_ffn_gu, w_ffn_down,
                    n_prompt=tp_, tok_per_cond=l_dec)

    layer, j = 1, 0
    mods = _Mods(mod_table, cond_rows, layer)
    x3 = _pool_mixer(x2, mods, norm1_g[layer].reshape(1, d), w_pool,
                     pool_scale[j].reshape(1, d), n_prompt=tp_, prompt_seq=l_ctx, sample_seq=l_dec)

    g2 = norm2_g[layer].reshape(1, d)
    joint = dict(n_prompt=tp_, tok_per_cond=l_dec)
    route, counts = _moe_router(x3, mods, g2, moe_w_router[j], moe_b_router[j], **joint)
    dest_rows, gates, gemm_list, zero_tiles, n_rows = _dispatch_plan(route, counts, n_exp)
    rows_sorted = _moe_scatter(x3, mods, g2, dest_rows, zero_tiles, n_rows, **joint)
    y_sorted = _moe_experts(rows_sorted, w_moe_gu, w_moe_down, gemm_list)
    y_p, y_s = _moe_combine(y_sorted, dest_rows, gates, x3, mods, **joint)

    y_prompt = y_p.reshape(b_ctx, l_ctx, d)
    y_sample = y_s.reshape(b_dec, l_dec, d)
    state_k = k_p.reshape(b_ctx, 1, l_ctx, heads, v_dim)
    state_v = v_p.reshape(b_ctx, 1, l_ctx, heads, v_dim)
    return (y_prompt, y_sample, state_k, state_v)
```

```python
import functools
import math

import jax
import jax.numpy as jnp
from jax import lax
from jax.experimental import pallas as pl
from jax.experimental.pallas import tpu as pltpu

F32 = jnp.float32
BF16 = jnp.bfloat16

GRID_W = 64
ROPE_BASE = 10000.0
POOL_WINDOWS = (2, 4, 8, 16)
TOP_K = 2
EPS = 1e-6
LOG2E = 1.4426950408889634

LANES = 128
SUBLANES = 8
VMEM_LIMIT = 52 * 2**20

MATMUL_ROWS = 1024
MATMUL_COLS = 512
FF_COLS = 256
ATTN_TILE = 512
OUT_PROJ_ROWS = 256
ROUTE_CHUNK = 256
GEMM_ROWS = 1024


def _lambda_init(layer):
    return 0.8 - 0.6 * math.exp(-0.3 * layer)


def _tile(n, pref, mult):
    best = None
    t = mult
    while t <= min(n, pref):
        if n % t == 0:
            best = t
        t += mult
    if best is None:
        raise ValueError(f"no tile for {n} (multiple of {mult}, <= {pref})")
    return best


def _params(semantics):
    return pltpu.CompilerParams(dimension_semantics=semantics, vmem_limit_bytes=VMEM_LIMIT)


def _dot(a, b):
    return jnp.dot(a, b, preferred_element_type=F32)


def _mask_bf16(m):
    return jnp.where(m, 1.0, 0.0).astype(BF16)


def _split_bf16(x):
    hi = x.astype(BF16)
    lo = (x - hi.astype(F32)).astype(BF16)
    return hi, lo


def _norm_modulate(x, g, scale, shift):
    ms = jnp.mean(x * x, axis=-1, keepdims=True)
    return x * lax.rsqrt(ms + EPS) * (g * (1.0 + scale)) + shift


def _adaln_kernel(cond_ref, w_ref, b_ref, o_ref):
    c = cond_ref[...]
    s = c * jax.nn.sigmoid(c)
    s_hi, s_lo = _split_bf16(s)
    w_hi, w_lo = _split_bf16(w_ref[...])
    o_ref[...] = _dot(s_hi, w_hi) + _dot(s_lo, w_hi) + _dot(s_hi, w_lo) + b_ref[...]


def _adaln_mods(cond, ada_w, ada_b):
    depth, d, n = ada_w.shape
    rows = cond.shape[0]
    tn = _tile(n, MATMUL_COLS, LANES)
    return pl.pallas_call(
        _adaln_kernel,
        grid=(depth, n // tn),
        in_specs=[
            pl.BlockSpec((rows, d), lambda l, j: (0, 0)),
            pl.BlockSpec((None, d, tn), lambda l, j: (l, 0, j)),
            pl.BlockSpec((None, 1, tn), lambda l, j: (l, 0, j)),
        ],
        out_specs=pl.BlockSpec((None, rows, tn), lambda l, j: (l, 0, j)),
        out_shape=jax.ShapeDtypeStruct((depth, rows, n), F32),
        compiler_params=_params(("arbitrary", "arbitrary")),
        name="adaln_mods",
    )(cond, ada_w, ada_b.reshape(depth, 1, n))


class _Mods:
    SHIFT1, SCALE1, GATE1, SHIFT2, SCALE2, GATE2 = range(6)

    def __init__(self, table, rows, layer):
        self.table = table
        self.rows = rows
        self.layer = layer

    def spec(self, which, row_fn, width=None, col_fn=None):
        d = self.table.shape[-1]
        width = d if width is None else width
        base = self.layer * self.rows

        def index(*ids):
            col = 0 if col_fn is None else col_fn(*ids)
            return ((base + row_fn(*ids)) * 6 + which, 0, col)

        return pl.BlockSpec((None, 1, width), index)


def _qkv_kernel(*refs, rope, tn, n_sec):
    x_ref, sh_ref, sc_ref, g_ref, w_ref, qg_ref, kg_ref = refs[:7]
    rest = list(refs[7:])
    cos_ref, sin_ref = (rest.pop(0), rest.pop(0)) if rope else (None, None)
    q_ref, k_ref, v_ref, h_scr = rest
    j = pl.program_id(1)

    def qk_epilogue(gain_ref, out_ref, prologue=False):
        if prologue:
            h = _norm_modulate(x_ref[...], g_ref[...], sc_ref[...], sh_ref[...]).astype(BF16)
            h_scr[...] = h
        else:
            h = h_scr[...]
        acc = _dot(h, w_ref[...])
        r_idx = lax.broadcasted_iota(jnp.int32, (2 * LANES, 2 * LANES), 0)
        c_idx = lax.broadcasted_iota(jnp.int32, (2 * LANES, 2 * LANES), 1)
        if rope:
            swap = (r_idx >= LANES) & (c_idx >= LANES) & (r_idx - LANES == jnp.bitwise_xor(c_idx - LANES, LANES // 4))
            rhs = _mask_bf16(((r_idx < LANES) & (c_idx < LANES)) | swap)
        else:
            rhs = jnp.ones((2 * LANES, LANES), BF16)
        for g in range(tn // LANES):
            y = acc[:, g * LANES:(g + 1) * LANES]
            yg = y * gain_ref[...]
            if rope:
                res = _dot(jnp.concatenate([(y * y).astype(BF16), yg.astype(BF16)], axis=1), rhs)
                inv = lax.rsqrt(res[:, :LANES] * (1.0 / LANES) + EPS)
                out = (yg * cos_ref[...] + res[:, LANES:] * sin_ref[...]) * inv
            else:
                ss = _dot(jnp.concatenate(_split_bf16(y * y), axis=1), rhs)
                out = yg * lax.rsqrt(ss * (1.0 / LANES) + EPS)
            out_ref[:, g * LANES:(g + 1) * LANES] = out.astype(out_ref.dtype)

    @pl.when(j == 0)
    def _():
        qk_epilogue(qg_ref, q_ref, prologue=True)

    @pl.when((j > 0) & (j < n_sec))
    def _():
        qk_epilogue(qg_ref, q_ref)

    @pl.when((j >= n_sec) & (j < 2 * n_sec))
    def _():
        qk_epilogue(kg_ref, k_ref)

    @pl.when(j >= 2 * n_sec)
    def _():
        v_ref[...] = _dot(h_scr[...], w_ref[...]).astype(v_ref.dtype)


def _qkv_proj(x, mods, norm_g, w, q_gain, k_gain, tables, *, cond_base, tok_per_cond, kv_dtype):
    t, d = x.shape
    tm = _tile(math.gcd(t, tok_per_cond), MATMUL_ROWS, SUBLANES)
    tn = _tile(d, MATMUL_COLS, LANES)
    n_sec = d // tn
    rope = tables is not None
    row = lambda i, j: cond_base + (i * tm) // tok_per_cond
    in_specs = [
        pl.BlockSpec((tm, d), lambda i, j: (i, 0)),
        mods.spec(_Mods.SHIFT1, row),
        mods.spec(_Mods.SCALE1, row),
        pl.BlockSpec((1, d), lambda i, j: (0, 0)),
        pl.BlockSpec((d, tn), lambda i, j: (0, j)),
        pl.BlockSpec((1, LANES), lambda i, j: (0, 0)),
        pl.BlockSpec((1, LANES), lambda i, j: (0, 0)),
    ]
    args = [x, mods.table, mods.table, norm_g, w, q_gain, k_gain]
    if rope:
        nblk = tables[0].shape[0] // tm
        for tab in tables:
            in_specs.append(pl.BlockSpec((tm, LANES), lambda i, j: (i % nblk, 0)))
            args.append(tab)
    out_specs = [
        pl.BlockSpec((tm, tn), lambda i, j, s=s: (i, jnp.clip(j - s * n_sec, 0, n_sec - 1)))
        for s in range(3)
    ]
    return pl.pallas_call(
        functools.partial(_qkv_kernel, rope=rope, tn=tn, n_sec=n_sec),
        grid=(t // tm, 3 * n_sec),
        in_specs=in_specs,
        out_specs=out_specs,
        out_shape=[jax.ShapeDtypeStruct((t, d), BF16), jax.ShapeDtypeStruct((t, d), kv_dtype),
                   jax.ShapeDtypeStruct((t, d), kv_dtype)],
        scratch_shapes=[pltpu.VMEM((tm, d), BF16)],
        compiler_params=_params(("arbitrary", "arbitrary")),
        name="qkv_proj_rope" if rope else "qkv_proj",
    )(*args)


def _rope_tables(seq):
    n_freq = LANES // 4
    pos = jnp.arange(seq, dtype=jnp.int32)
    r = (pos // GRID_W).astype(F32)
    col = (pos % GRID_W).astype(F32)
    freqs = ROPE_BASE ** (-jnp.arange(n_freq, dtype=F32) / n_freq)
    cos_t, sin_t = [], []
    for p in (r, col):
        ang = p[:, None] * freqs
        cos, sin = jnp.cos(ang), jnp.sin(ang)
        cos_t += [cos, cos]
        sin_t += [-sin, sin]
    return jnp.concatenate(cos_t, axis=1), jnp.concatenate(sin_t, axis=1)


def _attn_kernel(*refs, n_cache, n_chunks, tk, lambda_init, cast_blocks, heads_per_step):
    n_cast = len(cast_blocks)
    lam_ref, g_ref, q_ref = refs[:3]
    n_in = 3 + (4 if n_cache else 2)
    if n_cache:
        kc_ref, vtc_ref, k_ref, vt_ref = refs[3:n_in]
    else:
        k_ref, vt_ref = refs[3:n_in]
    cast_in = refs[n_in:n_in + n_cast]
    o_ref = refs[n_in + n_cast]
    cast_out = refs[n_in + n_cast + 1:n_in + 2 * n_cast + 1]
    acc_ref = refs[-1]

    step = (pl.program_id(0) * pl.num_programs(1) + pl.program_id(1)) * pl.num_programs(2) + pl.program_id(2)
    for src, dst, n_blocks in zip(cast_in, cast_out, cast_blocks):
        @pl.when(step < n_blocks)
        def _(src=src, dst=dst):
            dst[...] = src[...].astype(dst.dtype)

    vd = 2 * LANES
    for hq in range(heads_per_step):
        cols = slice(hq * vd, (hq + 1) * vd)
        _attn_head(lam_ref, g_ref, q_ref.at[:, cols],
                   kc_ref.at[:, cols] if n_cache else None, vtc_ref.at[hq] if n_cache else None,
                   k_ref.at[:, cols], vt_ref.at[hq], o_ref.at[:, cols], acc_ref,
                   n_chunks=n_chunks, tk=tk, lambda_init=lambda_init)


def _attn_head(lam_ref, g_ref, q_ref, kc_ref, vtc_ref, k_ref, vt_ref, o_ref, acc_ref, *,
               n_chunks, tk, lambda_init):
    q = q_ref[...]
    qs = (q[:, :LANES], q[:, LANES:])

    def scores(kblk):
        return [lax.dot_general(kblk[:, c * LANES:(c + 1) * LANES], qs[c],
                                (((1,), (1,)), ((), ())), preferred_element_type=F32)
                for c in range(2)]

    def accumulate(sts, vt, stats):
        out = []
        for c in range(2):
            st = sts[c]
            m_cur = jnp.max(st, axis=0, keepdims=True)
            if stats is None:
                m_new = m_cur
                p = jnp.exp2(st - m_new)
                l_new = jnp.sum(p, axis=0, keepdims=True)
                acc_ref[c] = _dot(vt, p.astype(BF16))
            else:
                m_prev, l_prev = stats[c]
                m_new = jnp.maximum(m_prev, m_cur)
                alpha = jnp.exp2(m_prev - m_new)
                p = jnp.exp2(st - m_new)
                l_new = alpha * l_prev + jnp.sum(p, axis=0, keepdims=True)
                acc_ref[c] = alpha * acc_ref[c] + _dot(vt, p.astype(BF16))
            out.append((m_new, l_new))
        return out

    chunks = []
    if kc_ref is not None:
        chunks.append((lambda: kc_ref[...].astype(BF16), lambda: vtc_ref[...]))
    for i in range(n_chunks):
        chunks.append((lambda i=i: k_ref[i * tk:(i + 1) * tk, :].astype(BF16), lambda i=i: vt_ref[i]))
    stats = None
    sts = scores(chunks[0][0]())
    for n, (_, values) in enumerate(chunks):
        nxt = scores(chunks[n + 1][0]()) if n + 1 < len(chunks) else None
        stats = accumulate(sts, values(), stats)
        sts = nxt

    lp = lam_ref[...]
    lam = (jnp.exp(jnp.sum(lp[0:1] * lp[1:2], axis=-1, keepdims=True))
           - jnp.exp(jnp.sum(lp[2:3] * lp[3:4], axis=-1, keepdims=True)) + lambda_init)
    o_t = acc_ref[0] / stats[0][1] - lam * (acc_ref[1] / stats[1][1])
    o = o_t.T
    o = o * lax.rsqrt(jnp.mean(o * o, axis=-1, keepdims=True) + EPS) * g_ref[...]
    o_ref[...] = (o * (1.0 - lambda_init)).astype(o_ref.dtype)


def _chunked_transpose(v, batch, seq, heads, vd, tk):
    v = v.astype(BF16).reshape(batch, seq // tk, tk, heads, vd)
    return v.transpose(0, 3, 1, 4, 2).reshape(batch * heads, seq // tk, vd, tk)


def _cast_block_rows(rows, n_steps):
    blk = 16
    while rows % blk or rows // blk > n_steps:
        blk += 16
        if blk > rows:
            raise ValueError(f"no cast block for {rows} rows in {n_steps} steps")
    return blk


def _diff_attention(q, k, v, cache_k, cache_v, lam_params, subln_g, *, batch, seq, layer, casts=()):
    t, d = q.shape
    vd = subln_g.shape[-1]
    heads = d // vd
    n_cache = 0 if cache_k is None else cache_k.shape[0] // batch
    tq = _tile(seq, ATTN_TILE, LANES)
    tk = _tile(seq, ATTN_TILE, LANES)
    nq = seq // tq
    n_chunks = seq // tk
    hps = heads if seq <= tq else 2
    groups = heads // hps
    wide = hps * vd
    in_specs = [
        pl.BlockSpec(lam_params.shape, lambda b, h, i: (0, 0)),
        pl.BlockSpec((1, vd), lambda b, h, i: (0, 0)),
        pl.BlockSpec((tq, wide), lambda b, h, i: (b * nq + i, h)),
    ]
    args = [lam_params, subln_g, q]
    if n_cache:
        in_specs += [pl.BlockSpec((n_cache, wide), lambda b, h, i: (b, h)),
                     pl.BlockSpec((hps, None, vd, n_cache), lambda b, h, i: (b * groups + h, 0, 0, 0))]
        args += [cache_k, _chunked_transpose(cache_v, batch, n_cache, heads, vd, n_cache)]
    in_specs += [pl.BlockSpec((seq, wide), lambda b, h, i: (b, h)),
                 pl.BlockSpec((hps, n_chunks, vd, tk), lambda b, h, i: (b * groups + h, 0, 0, 0))]
    args += [k, _chunked_transpose(v, batch, seq, heads, vd, tk)]
    out_specs = [pl.BlockSpec((tq, wide), lambda b, h, i: (b * nq + i, h))]
    out_shape = [jax.ShapeDtypeStruct((t, d), BF16)]
    n_steps = batch * groups * nq
    cast_blocks = []
    for w in casts:
        rows, cols = w.shape
        blk = _cast_block_rows(rows, n_steps)
        n_blocks = rows // blk
        cast_blocks.append(n_blocks)
        spec = pl.BlockSpec((blk, cols), lambda b, h, i, n=n_blocks: (
            jnp.minimum((b * groups + h) * nq + i, n - 1), 0))
        in_specs.append(spec)
        args.append(w)
        out_specs.append(spec)
        out_shape.append(jax.ShapeDtypeStruct((rows, cols), BF16))
    res = pl.pallas_call(
        functools.partial(_attn_kernel, n_cache=n_cache, n_chunks=n_chunks, tk=tk,
                          lambda_init=_lambda_init(layer), cast_blocks=tuple(cast_blocks),
                          heads_per_step=hps),
        grid=(batch, groups, nq),
        in_specs=in_specs,
        out_specs=out_specs,
        out_shape=out_shape,
        scratch_shapes=[pltpu.VMEM((2, vd, tq), F32)],
        compiler_params=_params(("arbitrary", "arbitrary", "arbitrary")),
        name=f"diff_attention_{'latent' if n_cache else 'context'}",
    )(*args)
    return res[0], list(res[1:])


def _out_proj_kernel(op_ref, os_ref, w_ref, xp_ref, xs_ref, gate_ref, out_ref, *, n_prompt_tiles):
    is_prompt = pl.program_id(0) < n_prompt_tiles

    @pl.when(is_prompt)
    def _():
        out_ref[...] = xp_ref[...] + gate_ref[...] * _dot(op_ref[...], w_ref[...])

    @pl.when(jnp.logical_not(is_prompt))
    def _():
        out_ref[...] = xs_ref[...] + gate_ref[...] * _dot(os_ref[...], w_ref[...])


def _out_proj_residual(o_p, o_s, w, x_p, x_s, mods, *, tok_per_cond):
    n_prompt, d = x_p.shape
    t = n_prompt + x_s.shape[0]
    tm = _tile(math.gcd(n_prompt, tok_per_cond), OUT_PROJ_ROWS, SUBLANES)
    tn = d
    n_pt = n_prompt // tm
    row = _cond_row_joint(tm, n_prompt, tok_per_cond)
    p_rows = lambda i: jnp.minimum(i, n_pt - 1)
    s_rows = lambda i: jnp.maximum(i - n_pt, 0)
    return pl.pallas_call(
        functools.partial(_out_proj_kernel, n_prompt_tiles=n_pt),
        grid=(t // tm, d // tn),
        in_specs=[
            pl.BlockSpec((tm, d), lambda i, j: (p_rows(i), 0)),
            pl.BlockSpec((tm, d), lambda i, j: (s_rows(i), 0)),
            pl.BlockSpec((d, tn), lambda i, j: (0, j)),
            pl.BlockSpec((tm, tn), lambda i, j: (p_rows(i), jnp.where(i < n_pt, j, d // tn - 1))),
            pl.BlockSpec((tm, tn), lambda i, j: (s_rows(i), jnp.where(i < n_pt, 0, j))),
            mods.spec(_Mods.GATE1, row, width=tn, col_fn=lambda i, j: j),
        ],
        out_specs=pl.BlockSpec((tm, tn), lambda i, j: (i, j)),
        out_shape=jax.ShapeDtypeStruct((t, d), F32),
        compiler_params=_params(("arbitrary", "arbitrary")),
        name="attn_out_proj",
    )(o_p, o_s, w, x_p, x_s, mods.table)


def _ffn_kernel(x_ref, sh_ref, sc_ref, gate_ref, g_ref, wg_ref, wu_ref, wd_ref, o_ref, h_scr):
    f = pl.program_id(1)

    def swiglu_part(h):
        gte = _dot(h, wg_ref[...])
        up = _dot(h, wu_ref[...])
        act = (gte * jax.nn.sigmoid(gte) * up).astype(BF16)
        return _dot(act, wd_ref[...])

    @pl.when(f == 0)
    def _():
        h = _norm_modulate(x_ref[...], g_ref[...], sc_ref[...], sh_ref[...]).astype(BF16)
        h_scr[...] = h
        o_ref[...] = swiglu_part(h)

    last = pl.num_programs(1) - 1

    @pl.when((f > 0) & (f < last))
    def _():
        o_ref[...] += swiglu_part(h_scr[...])

    @pl.when((f > 0) & (f == last))
    def _():
        o_ref[...] = x_ref[...] + gate_ref[...] * (o_ref[...] + swiglu_part(h_scr[...]))

    @pl.when((f == 0) & (f == last))
    def _():
        o_ref[...] = x_ref[...] + gate_ref[...] * o_ref[...]


def _cond_row_joint(tm, n_prompt, tok_per_cond):
    def row(i, *_):
        tok = i * tm
        return jnp.where(tok < n_prompt, 0, 1 + (tok - n_prompt) // tok_per_cond)
    return row


def _dense_ffn(x, mods, norm_g, w_gu, w_down, *, n_prompt, tok_per_cond):
    t, d = x.shape
    ff = w_down.shape[0]
    tm = _tile(math.gcd(n_prompt, tok_per_cond), MATMUL_ROWS, SUBLANES)
    tf = _tile(ff, FF_COLS, LANES)
    nf = ff // tf
    row = _cond_row_joint(tm, n_prompt, tok_per_cond)
    return pl.pallas_call(
        _ffn_kernel,
        grid=(t // tm, nf),
        in_specs=[
            pl.BlockSpec((tm, d), lambda i, f: (i, 0)),
            mods.spec(_Mods.SHIFT2, row),
            mods.spec(_Mods.SCALE2, row),
            mods.spec(_Mods.GATE2, row),
            pl.BlockSpec((1, d), lambda i, f: (0, 0)),
            pl.BlockSpec((d, tf), lambda i, f: (0, f)),
            pl.BlockSpec((d, tf), lambda i, f: (0, nf + f)),
            pl.BlockSpec((tf, d), lambda i, f: (f, 0)),
        ],
        out_specs=pl.BlockSpec((tm, d), lambda i, f: (i, 0)),
        out_shape=jax.ShapeDtypeStruct((t, d), F32),
        scratch_shapes=[pltpu.VMEM((tm, d), BF16)],
        compiler_params=_params(("arbitrary", "arbitrary")),
        name="dense_ffn",
    )(x, mods.table, mods.table, mods.table, norm_g, w_gu, w_gu, w_down)


def _pool_kernel(x_ref, xp_ref, xn_ref, sh_ref, sc_ref, gate_ref, g_ref, w_ref, ps_ref, o_ref, *,
                 tp, halo, n_prompt_tiles, prompt_tiles_per_seq, sample_tiles_per_seq, group_dim):
    i = pl.program_id(0)
    is_prompt = i < n_prompt_tiles
    tiles_per_seq = jnp.where(is_prompt, prompt_tiles_per_seq, sample_tiles_per_seq)
    local = lax.rem(jnp.where(is_prompt, i, i - n_prompt_tiles), tiles_per_seq)
    first = local == 0
    last = local == tiles_per_seq - 1
    seq_len = tiles_per_seq * tp

    g, sc, sh = g_ref[...], sc_ref[...], sh_ref[...]
    x = x_ref[...]
    h = _norm_modulate(x, g, sc, sh)
    h_prev = jnp.where(first, 0.0, _norm_modulate(xp_ref[...], g, sc, sh))
    h_next = jnp.where(last, 0.0, _norm_modulate(xn_ref[...], g, sc, sh))
    d = h.shape[-1]
    h_halo = jnp.concatenate([h_prev, h_next, jnp.zeros((LANES - 2 * halo, d), F32)], axis=0)

    t_idx = lax.broadcasted_iota(jnp.int32, (tp, tp), 0)
    s_idx = lax.broadcasted_iota(jnp.int32, (tp, tp), 1)
    t_h = lax.broadcasted_iota(jnp.int32, (tp, LANES), 0)
    u_h = lax.broadcasted_iota(jnp.int32, (tp, LANES), 1)
    off_h = jnp.where(u_h < halo, u_h - halo, tp + u_h - halo)
    pos = local * tp + lax.broadcasted_iota(jnp.int32, (tp, 1), 0)

    for grp, win in enumerate(POOL_WINDOWS):
        lo, hi = win // 2, win // 2 - 1
        cols = slice(grp * group_dim, (grp + 1) * group_dim)
        band = _mask_bf16((s_idx >= t_idx - lo) & (s_idx <= t_idx + hi))
        band_h = _mask_bf16((u_h < 2 * halo) & (off_h >= t_h - lo) & (off_h <= t_h + hi))
        total = _dot(band, h[:, cols].astype(BF16)) + _dot(band_h, h_halo[:, cols].astype(BF16))
        cnt = (jnp.minimum(pos + hi, seq_len - 1) - jnp.maximum(pos - lo, 0) + 1).astype(F32)
        pooled = (total / cnt - h[:, cols]).astype(BF16)
        y = _dot(pooled, w_ref[grp]) * ps_ref[:, cols]
        o_ref[:, cols] = x[:, cols] + gate_ref[:, cols] * y


def _pool_mixer(x, mods, norm_g, pool_w, pool_scale, *, n_prompt, prompt_seq, sample_seq):
    t, d = x.shape
    halo = max(POOL_WINDOWS) // 2
    tp = _tile(math.gcd(prompt_seq, sample_seq), ROUTE_CHUNK, SUBLANES)
    n_groups, group_dim = pool_w.shape[0], pool_w.shape[1]
    hb = tp // halo
    n_halo_blocks = t // halo
    row = _cond_row_joint(tp, n_prompt, sample_seq)
    return pl.pallas_call(
        functools.partial(_pool_kernel, tp=tp, halo=halo, n_prompt_tiles=n_prompt // tp,
                          prompt_tiles_per_seq=prompt_seq // tp,
                          sample_tiles_per_seq=sample_seq // tp, group_dim=group_dim),
        grid=(t // tp,),
        in_specs=[
            pl.BlockSpec((tp, d), lambda i: (i, 0)),
            pl.BlockSpec((halo, d), lambda i: (jnp.maximum(i * hb - 1, 0), 0)),
            pl.BlockSpec((halo, d), lambda i: (jnp.minimum((i + 1) * hb, n_halo_blocks - 1), 0)),
            mods.spec(_Mods.SHIFT1, row),
            mods.spec(_Mods.SCALE1, row),
            mods.spec(_Mods.GATE1, row),
            pl.BlockSpec((1, d), lambda i: (0, 0)),
            pl.BlockSpec((n_groups, group_dim, group_dim), lambda i: (0, 0, 0)),
            pl.BlockSpec((1, d), lambda i: (0, 0)),
        ],
        out_specs=pl.BlockSpec((tp, d), lambda i: (i, 0)),
        out_shape=jax.ShapeDtypeStruct((t, d), F32),
        compiler_params=_params(("arbitrary",)),
        name="pool_mixer",
    )(x, x, x, mods.table, mods.table, mods.table, norm_g, pool_w, pool_scale)


def _router_kernel(x_ref, sh_ref, sc_ref, g_ref, wr_ref, br_ref, route_ref, cnt_ref):
    h = _norm_modulate(x_ref[...], g_ref[...], sc_ref[...], sh_ref[...])
    h_hi, h_lo = _split_bf16(h)
    w_hi, w_lo = _split_bf16(wr_ref[...])
    logits = _dot(h_hi, w_hi) + _dot(h_lo, w_hi) + _dot(h_hi, w_lo) + br_ref[...]

    rows = logits.shape[0]
    lane = lax.broadcasted_iota(jnp.int32, logits.shape, 1).astype(F32)
    m1 = jnp.max(logits, axis=-1, keepdims=True)
    i1 = jnp.min(jnp.where(logits == m1, lane, float(LANES)), axis=-1, keepdims=True)
    oh1 = lane == i1
    rest = jnp.where(oh1, -jnp.inf, logits)
    m2 = jnp.max(rest, axis=-1, keepdims=True)
    i2 = jnp.min(jnp.where(rest == m2, lane, float(LANES)), axis=-1, keepdims=True)
    oh2 = lane == i2
    e = jnp.exp(m2 - m1)
    gate_a = 1.0 / (1.0 + e)
    gate_b = e / (1.0 + e)

    sel = jnp.where(oh1 | oh2, 1.0, 0.0)
    r_idx = lax.broadcasted_iota(jnp.int32, (rows, rows), 0)
    c_idx = lax.broadcasted_iota(jnp.int32, (rows, rows), 1)
    earlier = _mask_bf16(c_idx < r_idx)
    rank = _dot(earlier, sel.astype(BF16))
    rank_a = jnp.sum(jnp.where(oh1, rank, 0.0), axis=-1, keepdims=True)
    rank_b = jnp.sum(jnp.where(oh2, rank, 0.0), axis=-1, keepdims=True)

    route = jnp.zeros(logits.shape, F32)
    for k, val in enumerate((i1, i2, rank_a, rank_b, gate_a, gate_b)):
        route = jnp.where(lane == float(k), val, route)
    route_ref[...] = route
    cnt_ref[...] = jnp.sum(sel, axis=0, keepdims=True)


def _moe_router(x, mods, norm_g, w_router, b_router, *, n_prompt, tok_per_cond):
    t, d = x.shape
    n_exp = w_router.shape[1]
    ch = ROUTE_CHUNK
    wr = jnp.zeros((d, LANES), F32).at[:, :n_exp].set(w_router.astype(F32))
    br = jnp.full((1, LANES), -jnp.inf, F32).at[0, :n_exp].set(b_router.astype(F32))
    row = _cond_row_joint(ch, n_prompt, tok_per_cond)
    return pl.pallas_call(
        _router_kernel,
        grid=(t // ch,),
        in_specs=[
            pl.BlockSpec((ch, d), lambda i: (i, 0)),
            mods.spec(_Mods.SHIFT2, row),
            mods.spec(_Mods.SCALE2, row),
            pl.BlockSpec((1, d), lambda i: (0, 0)),
            pl.BlockSpec((d, LANES), lambda i: (0, 0)),
            pl.BlockSpec((1, LANES), lambda i: (0, 0)),
        ],
        out_specs=[
            pl.BlockSpec((ch, LANES), lambda i: (i, 0)),
            pl.BlockSpec((None, 1, LANES), lambda i: (i, 0, 0)),
        ],
        out_shape=[
            jax.ShapeDtypeStruct((t, LANES), F32),
            jax.ShapeDtypeStruct((t // ch, 1, LANES), F32),
        ],
        compiler_params=_params(("arbitrary",)),
        name="moe_router",
    )(x, mods.table, mods.table, norm_g, wr, br)


def _dispatch_plan(route, counts, n_exp):
    ch, gr = ROUTE_CHUNK, GEMM_ROWS
    t = route.shape[0]
    n_chunks = t // ch
    i32 = jnp.int32
    cnt = counts[:, 0, :n_exp].astype(i32)
    total = cnt.sum(0)
    padded = ((total + gr - 1) // gr) * gr
    off = jnp.cumsum(padded) - padded
    start = off[None, :] + jnp.cumsum(cnt, axis=0) - cnt

    def dest(choice):
        e = route[:, choice].astype(i32).reshape(n_chunks, ch, 1)
        hit = e == jnp.arange(n_exp, dtype=i32)
        base = jnp.sum(jnp.where(hit, start[:, None, :], 0), axis=-1)
        return base.reshape(t) + route[:, 2 + choice].astype(i32)

    dest_a, dest_b = dest(0), dest(1)

    n_gemm_tiles = (t * TOP_K) // gr + n_exp

    tile_lo = jnp.arange(n_gemm_tiles, dtype=i32) * gr
    used = tile_lo < (off + padded)[-1]
    ends = off + padded
    expert_of = jnp.minimum(jnp.sum(tile_lo[:, None] >= ends[None, :], axis=1), n_exp - 1).astype(i32)
    n_used = jnp.maximum(used.sum(), 1)
    clamp = jnp.minimum(jnp.arange(n_gemm_tiles, dtype=i32), n_used - 1).astype(i32)
    token_rows = jnp.clip((off + total)[expert_of] - tile_lo, 0, gr).astype(i32)
    gemm_list = (clamp, expert_of[clamp], used.astype(i32), token_rows[clamp])

    tail = jnp.arange(n_gemm_tiles - n_exp, n_gemm_tiles, dtype=i32)
    zero_rows = jnp.concatenate([jnp.maximum(ends - gr, 0), tail * gr]).astype(i32)
    zero_ok = jnp.concatenate([padded > 0, tail >= used.sum()]).astype(i32)

    dest_rows = jnp.stack([dest_a.reshape(n_chunks, ch), dest_b.reshape(n_chunks, ch)], axis=1)
    gates = jnp.concatenate([route[:, 4:6], jnp.zeros((t, 6), F32)], axis=1)
    return dest_rows, gates, gemm_list, (zero_rows, zero_ok), n_gemm_tiles * gr


def _wait_rows(src, dst, sem, n_rows):
    pltpu.make_async_copy(src.at[pl.ds(0, n_rows), :], dst.at[pl.ds(0, n_rows), :], sem).wait()


def _scatter_kernel(zrow_ref, zok_ref, dest_ref, x_ref, sh_ref, sc_ref, g_ref, out_hbm,
                    h_scr, z_scr, sem, zsem, *, tile_rows):
    ch = x_ref.shape[0]
    step, n_steps = pl.program_id(0), pl.num_programs(0)
    slot = lax.rem(step, 2)

    @pl.when(step == 0)
    def _():
        z_scr[...] = jnp.zeros(z_scr.shape, F32)
        zr = z_scr.shape[0]

        def zero_copy(n, part):
            row0 = pl.multiple_of(zrow_ref[n] + part * zr, zr)
            return pltpu.make_async_copy(z_scr, out_hbm.at[pl.ds(row0, zr), :], zsem)

        for n in range(zrow_ref.shape[0]):
            @pl.when(zok_ref[n] == 1)
            def _(n=n):
                for part in range(tile_rows // zr):
                    zero_copy(n, part).start()
        for n in range(zrow_ref.shape[0]):
            @pl.when(zok_ref[n] == 1)
            def _(n=n):
                for part in range(tile_rows // zr):
                    zero_copy(n, part).wait()

    h_scr[slot] = _norm_modulate(x_ref[...], g_ref[...], sc_ref[...], sh_ref[...])

    src_rows = h_scr.at[slot]
    for j in range(ch):
        for k in range(TOP_K):
            pltpu.make_async_copy(src_rows.at[pl.ds(j, 1), :],
                                  out_hbm.at[pl.ds(dest_ref[k, j], 1), :],
                                  sem.at[slot]).start(priority=k)

    def wait_slot(s):
        for _ in range(TOP_K):
            _wait_rows(h_scr.at[s], out_hbm, sem.at[s], ch)

    @pl.when(step > 0)
    def _():
        wait_slot(1 - slot)

    @pl.when(step == n_steps - 1)
    def _():
        wait_slot(slot)


def _moe_scatter(x, mods, norm_g, dest_rows, zero_tiles, n_rows, *, n_prompt, tok_per_cond):
    t, d = x.shape
    ch = ROUTE_CHUNK
    row = _cond_row_joint(ch, n_prompt, tok_per_cond)
    grid_spec = pltpu.PrefetchScalarGridSpec(
        num_scalar_prefetch=2,
        grid=(t // ch,),
        in_specs=[
            pl.BlockSpec((None, TOP_K, ch), lambda i, *_: (i, 0, 0), memory_space=pltpu.SMEM),
            pl.BlockSpec((ch, d), lambda i, *_: (i, 0)),
            mods.spec(_Mods.SHIFT2, row),
            mods.spec(_Mods.SCALE2, row),
            pl.BlockSpec((1, d), lambda i, *_: (0, 0)),
        ],
        out_specs=pl.BlockSpec(memory_space=pl.ANY),
        scratch_shapes=[pltpu.VMEM((2, ch, d), F32), pltpu.VMEM((ch, d), F32),
                        pltpu.SemaphoreType.DMA((2,)), pltpu.SemaphoreType.DMA(())],
    )
    return pl.pallas_call(
        functools.partial(_scatter_kernel, tile_rows=GEMM_ROWS),
        grid_spec=grid_spec,
        out_shape=jax.ShapeDtypeStruct((n_rows, d), F32),
        compiler_params=_params(("arbitrary",)),
        name="moe_scatter",
    )(*zero_tiles, dest_rows, x, mods.table, mods.table, norm_g)


def _expert_kernel(blk_ref, exp_ref, used_ref, rows_ref, x_ref, wg_ref, wu_ref, wd_ref, o_ref, x_scr, *,
                   sub_rows):
    r, f = pl.program_id(0), pl.program_id(1)
    tile_rows = o_ref.shape[0]

    def swiglu_part(x):
        gte = _dot(x, wg_ref[...])
        up = _dot(x, wu_ref[...])
        act = (gte * jax.nn.sigmoid(gte) * up).astype(BF16)
        return _dot(act, wd_ref[...])

    @pl.when(used_ref[r] == 1)
    def _():
        valid = rows_ref[r]
        full = valid > tile_rows - sub_rows

        @pl.when(full & (f == 0))
        def _():
            x = x_ref[...].astype(BF16)
            x_scr[...] = x
            o_ref[...] = swiglu_part(x)

        @pl.when(full & (f > 0))
        def _():
            o_ref[...] += swiglu_part(x_scr[...])

        @pl.when(jnp.logical_not(full))
        def _():
            @pl.when(f == 0)
            def _():
                o_ref[...] = jnp.zeros(o_ref.shape, F32)
                x_scr[...] = x_ref[...].astype(BF16)

            for sb in range(tile_rows // sub_rows - 1):
                @pl.when(valid > sb * sub_rows)
                def _(sb=sb):
                    rows = slice(sb * sub_rows, (sb + 1) * sub_rows)
                    o_ref[rows, :] += swiglu_part(x_scr[rows, :])


def _moe_experts(xs, w_gu, w_down, gemm_list):
    n_rows, d = xs.shape
    ff = w_down.shape[1]
    gr = GEMM_ROWS
    tf = _tile(ff, FF_COLS, LANES)
    nf = ff // tf
    n_tiles = gemm_list[0].shape[0]

    def fcol(f, used, r):
        return jnp.where(used[r] == 1, f, nf - 1)

    grid_spec = pltpu.PrefetchScalarGridSpec(
        num_scalar_prefetch=4,
        grid=(n_tiles, nf),
        in_specs=[
            pl.BlockSpec((gr, d), lambda r, f, blk, ex, used, rows: (blk[r], 0)),
            pl.BlockSpec((None, d, tf), lambda r, f, blk, ex, used, rows: (ex[r], 0, fcol(f, used, r))),
            pl.BlockSpec((None, d, tf), lambda r, f, blk, ex, used, rows: (ex[r], 0, nf + fcol(f, used, r))),
            pl.BlockSpec((None, tf, d), lambda r, f, blk, ex, used, rows: (ex[r], fcol(f, used, r), 0)),
        ],
        out_specs=pl.BlockSpec((gr, d), lambda r, f, blk, ex, used, rows: (blk[r], 0)),
        scratch_shapes=[pltpu.VMEM((gr, d), BF16)],
    )
    return pl.pallas_call(
        functools.partial(_expert_kernel, sub_rows=ROUTE_CHUNK),
        grid_spec=grid_spec,
        out_shape=jax.ShapeDtypeStruct((n_rows, d), F32),
        input_output_aliases={4: 0},
        compiler_params=_params(("arbitrary", "arbitrary")),
        name="moe_experts",
    )(*gemm_list, xs, w_gu, w_gu, w_down)


def _combine_kernel(dest_ref, dest_next_ref, gates_ref, x_ref, gate_ref, y_hbm, op_ref, os_ref,
                    y_scr, sem, *, n_prompt_chunks):
    ch = x_ref.shape[0]
    step, n_steps = pl.program_id(0), pl.num_programs(0)
    slot = lax.rem(step, 2)

    def gather(idx_ref, s):
        for j in range(ch):
            for k in range(TOP_K):
                pltpu.make_async_copy(y_hbm.at[pl.ds(idx_ref[k, j], 1), :],
                                      y_scr.at[s, k, pl.ds(j, 1), :], sem.at[s]).start(priority=k)

    @pl.when(step == 0)
    def _():
        gather(dest_ref, slot)

    @pl.when(step + 1 < n_steps)
    def _():
        gather(dest_next_ref, 1 - slot)

    for k in range(TOP_K):
        _wait_rows(y_hbm, y_scr.at[slot, k], sem.at[slot], ch)

    gates = gates_ref[...]
    mix = gates[:, 0:1] * y_scr[slot, 0] + gates[:, 1:2] * y_scr[slot, 1]
    out = x_ref[...] + gate_ref[...] * mix
    is_prompt = step < n_prompt_chunks

    @pl.when(is_prompt)
    def _():
        op_ref[...] = out

    @pl.when(jnp.logical_not(is_prompt))
    def _():
        os_ref[...] = out


def _moe_combine(y, dest_rows, gates, x, mods, *, n_prompt, tok_per_cond):
    t, d = x.shape
    ch = ROUTE_CHUNK
    n_pc = n_prompt // ch
    row = _cond_row_joint(ch, n_prompt, tok_per_cond)
    return pl.pallas_call(
        functools.partial(_combine_kernel, n_prompt_chunks=n_pc),
        grid=(t // ch,),
        in_specs=[
            pl.BlockSpec((None, TOP_K, ch), lambda i: (i, 0, 0), memory_space=pltpu.SMEM),
            pl.BlockSpec((None, TOP_K, ch), lambda i: (jnp.minimum(i + 1, t // ch - 1), 0, 0),
                         memory_space=pltpu.SMEM),
            pl.BlockSpec((ch, gates.shape[1]), lambda i: (i, 0)),
            pl.BlockSpec((ch, d), lambda i: (i, 0)),
            mods.spec(_Mods.GATE2, row),
            pl.BlockSpec(memory_space=pl.ANY),
        ],
        out_specs=[
            pl.BlockSpec((ch, d), lambda i: (jnp.minimum(i, n_pc - 1), 0)),
            pl.BlockSpec((ch, d), lambda i: (jnp.maximum(i - n_pc, 0), 0)),
        ],
        out_shape=[jax.ShapeDtypeStruct((n_prompt, d), F32), jax.ShapeDtypeStruct((t - n_prompt, d), F32)],
        scratch_shapes=[pltpu.VMEM((2, TOP_K, ch, d), F32), pltpu.SemaphoreType.DMA((2,))],
        compiler_params=_params(("arbitrary",)),
        name="moe_combine",
    )(dest_rows, dest_rows, gates, x, mods.table, y)


def kernel(x_prompt, x_sample, c, cache_k, cache_v, c_ctx, ada_w, ada_b, norm1_g, norm2_g,
           attn_w_qkv, attn_w_o, attn_q_norm, attn_k_norm, attn_lambda, attn_subln_g,
           pool_w, pool_scale, ffn_w_gu, ffn_w_down,
           moe_w_router, moe_b_router, moe_w_gu, moe_w_down):
    b_ctx, l_ctx, d = x_prompt.shape
    b_dec, l_dec, _ = x_sample.shape
    depth = ada_w.shape[0]
    l_past, heads, v_dim = cache_k.shape[2:]
    head_dim = v_dim // 2
    assert head_dim == LANES and depth == 2 and l_dec % GRID_W == 0
    n_exp = moe_w_router.shape[-1]
    tp_, ts_ = b_ctx * l_ctx, b_dec * l_dec

    cond_rows = 8 * ((1 + b_dec + 7) // 8)
    cond = jnp.zeros((cond_rows, d), F32).at[0].set(c_ctx).at[1:1 + b_dec].set(c)
    mod_table = _adaln_mods(cond, ada_w, ada_b).reshape(depth * cond_rows * 6, 1, d)

    xp = x_prompt.reshape(tp_, d)
    xs = x_sample.reshape(ts_, d)

    layer, j = 0, 0
    mods = _Mods(mod_table, cond_rows, layer)
    g1 = norm1_g[layer].reshape(1, d)
    w_qkv = attn_w_qkv[j].astype(BF16)
    q_gain = (attn_q_norm[j] * (head_dim ** -0.5 * LOG2E)).reshape(1, head_dim)
    k_gain = attn_k_norm[j].reshape(1, head_dim)
    tables = _rope_tables(l_dec)
    prompt = dict(cond_base=0, tok_per_cond=tp_)
    sample = dict(cond_base=1, tok_per_cond=l_dec)

    q_p, k_p, v_p = _qkv_proj(xp, mods, g1, w_qkv, q_gain, k_gain, None, kv_dtype=F32, **prompt)
    q_s, k_s, v_s = _qkv_proj(xs, mods, g1, w_qkv, q_gain, k_gain, tables, kv_dtype=BF16, **sample)

    lam_params = attn_lambda[j].astype(F32)
    subln = attn_subln_g[j].reshape(1, v_dim)
    o_p, _ = _diff_attention(q_p, k_p, v_p, None, None, lam_params, subln,
                             batch=b_ctx, seq=l_ctx, layer=layer)
    ck = cache_k[:, j].reshape(b_dec * l_past, heads * v_dim)
    cv = cache_v[:, j].reshape(b_dec * l_past, heads * v_dim)
    later_weights = [attn_w_o[j], ffn_w_gu[j], ffn_w_down[j], pool_w[0], moe_w_gu[0], moe_w_down[0]]
    o_s, later_bf16 = _diff_attention(q_s, k_s, v_s, ck, cv, lam_params, subln,
                                      batch=b_dec, seq=l_dec, layer=layer,
                                      casts=[w.reshape(-1, w.shape[-1]) for w in later_weights])
    w_o, w_ffn_gu, w_ffn_down, w_pool, w_moe_gu, w_moe_down = [
        b.reshape(w.shape) for b, w in zip(later_bf16, later_weights)]

    x1 = _out_proj_residual(o_p, o_s, w_o, xp, xs, mods, tok_per_cond=l_dec)

    x2 = _dense_ffn(x1, mods, norm2_g[layer].reshape(1, d), w_ffn_gu, w_ffn_down,
                    n_prompt=tp_, tok_per_cond=l_dec)

    layer, j = 1, 0
    mods = _Mods(mod_table, cond_rows, layer)
    x3 = _pool_mixer(x2, mods, norm1_g[layer].reshape(1, d), w_pool,
                     pool_scale[j].reshape(1, d), n_prompt=tp_, prompt_seq=l_ctx, sample_seq=l_dec)

    g2 = norm2_g[layer].reshape(1, d)
    joint = dict(n_prompt=tp_, tok_per_cond=l_dec)
    route, counts = _moe_router(x3, mods, g2, moe_w_router[j], moe_b_router[j], **joint)
    dest_rows, gates, gemm_list, zero_tiles, n_rows = _dispatch_plan(route, counts, n_exp)
    rows_sorted = _moe_scatter(x3, mods, g2, dest_rows, zero_tiles, n_rows, **joint)
    y_sorted = _moe_experts(rows_sorted, w_moe_gu, w_moe_down, gemm_list)
    y_p, y_s = _moe_combine(y_sorted, dest_rows, gates, x3, mods, **joint)

    y_prompt = y_p.reshape(b_ctx, l_ctx, d)
    y_sample = y_s.reshape(b_dec, l_dec, d)
    state_k = k_p.reshape(b_ctx, 1, l_ctx, heads, v_dim)
    state_v = v_p.reshape(b_ctx, 1, l_ctx, heads, v_dim)
    return (y_prompt, y_sample, state_k, state_v)
```

```python
import functools
import math

import jax
import jax.numpy as jnp
from jax import lax
from jax.experimental import pallas as pl
from jax.experimental.pallas import tpu as pltpu

F32 = jnp.float32
BF16 = jnp.bfloat16

GRID_W = 64
ROPE_BASE = 10000.0
POOL_WINDOWS = (2, 4, 8, 16)
TOP_K = 2
EPS = 1e-6
LOG2E = 1.4426950408889634

LANES = 128
VMEM_LIMIT = 52 * 2**20

ROUTE_CHUNK = 256
GEMM_ROWS = 1024


def _lambda_init(layer):
    return 0.8 - 0.6 * math.exp(-0.3 * layer)


def _tile(n, pref, mult):
    best = None
    t = mult
    while t <= min(n, pref):
        if n % t == 0:
            best = t
        t += mult
    if best is None:
        raise ValueError(f"no tile for {n} (multiple of {mult}, <= {pref})")
    return best


def _params(semantics):
    return pltpu.CompilerParams(dimension_semantics=semantics, vmem_limit_bytes=VMEM_LIMIT)


def _dot(a, b):
    return jnp.dot(a, b, preferred_element_type=F32)


def _mask_bf16(m):
    return jnp.where(m, 1.0, 0.0).astype(BF16)


def _split_bf16(x):
    hi = x.astype(BF16)
    lo = (x - hi.astype(F32)).astype(BF16)
    return hi, lo


def _norm_modulate(x, g, scale, shift):
    ms = jnp.mean(x * x, axis=-1, keepdims=True)
    return x * lax.rsqrt(ms + EPS) * (g * (1.0 + scale)) + shift


def _adaln_kernel(cond_ref, w_ref, b_ref, o_ref):
    c = cond_ref[...]
    s = c * jax.nn.sigmoid(c)
    s_hi, s_lo = _split_bf16(s)
    w_hi, w_lo = _split_bf16(w_ref[...])
    o_ref[...] = _dot(s_hi, w_hi) + _dot(s_lo, w_hi) + _dot(s_hi, w_lo) + b_ref[...]


def _adaln_mods(cond, ada_w, ada_b):
    depth, d, n = ada_w.shape
    rows = cond.shape[0]
    tn = _tile(n, 512, LANES)
    return pl.pallas_call(
        _adaln_kernel,
        grid=(depth, n // tn),
        in_specs=[
            pl.BlockSpec((rows, d), lambda l, j: (0, 0)),
            pl.BlockSpec((None, d, tn), lambda l, j: (l, 0, j)),
            pl.BlockSpec((None, 1, tn), lambda l, j: (l, 0, j)),
        ],
        out_specs=pl.BlockSpec((None, rows, tn), lambda l, j: (l, 0, j)),
        out_shape=jax.ShapeDtypeStruct((depth, rows, n), F32),
        compiler_params=_params(("arbitrary", "arbitrary")),
        name="adaln_mods",
    )(cond, ada_w, ada_b.reshape(depth, 1, n))


class _Mods:
    SHIFT1, SCALE1, GATE1, SHIFT2, SCALE2, GATE2 = range(6)

    def __init__(self, table, rows, layer):
        self.table = table
        self.rows = rows
        self.layer = layer

    def spec(self, which, row_fn, width=None, col_fn=None):
        d = self.table.shape[-1]
        width = d if width is None else width
        base = self.layer * self.rows

        def index(*ids):
            col = 0 if col_fn is None else col_fn(*ids)
            return ((base + row_fn(*ids)) * 6 + which, 0, col)

        return pl.BlockSpec((None, 1, width), index)


def _qkv_kernel(*refs, rope, tn, n_sec):
    x_ref, sh_ref, sc_ref, g_ref, w_ref, qg_ref, kg_ref = refs[:7]
    rest = list(refs[7:])
    cos_ref, sin_ref = (rest.pop(0), rest.pop(0)) if rope else (None, None)
    q_ref, k_ref, v_ref, h_scr = rest
    j = pl.program_id(1)

    def qk_epilogue(gain_ref, out_ref, prologue=False):
        if prologue:
            h = _norm_modulate(x_ref[...], g_ref[...], sc_ref[...], sh_ref[...]).astype(BF16)
            h_scr[...] = h
        else:
            h = h_scr[...]
        acc = _dot(h, w_ref[...])
        r_idx = lax.broadcasted_iota(jnp.int32, (2 * LANES, 2 * LANES), 0)
        c_idx = lax.broadcasted_iota(jnp.int32, (2 * LANES, 2 * LANES), 1)
        if rope:
            swap = (r_idx >= LANES) & (c_idx >= LANES) & (r_idx - LANES == jnp.bitwise_xor(c_idx - LANES, LANES // 4))
            rhs = _mask_bf16(((r_idx < LANES) & (c_idx < LANES)) | swap)
        else:
            rhs = jnp.ones((2 * LANES, LANES), BF16)
        for g in range(tn // LANES):
            y = acc[:, g * LANES:(g + 1) * LANES]
            yg = y * gain_ref[...]
            if rope:
                res = _dot(jnp.concatenate([(y * y).astype(BF16), yg.astype(BF16)], axis=1), rhs)
                inv = lax.rsqrt(res[:, :LANES] * (1.0 / LANES) + EPS)
                out = (yg * cos_ref[...] + res[:, LANES:] * sin_ref[...]) * inv
            else:
                ss = _dot(jnp.concatenate(_split_bf16(y * y), axis=1), rhs)
                out = yg * lax.rsqrt(ss * (1.0 / LANES) + EPS)
            out_ref[:, g * LANES:(g + 1) * LANES] = out.astype(out_ref.dtype)

    @pl.when(j == 0)
    def _():
        qk_epilogue(qg_ref, q_ref, prologue=True)

    @pl.when((j > 0) & (j < n_sec))
    def _():
        qk_epilogue(qg_ref, q_ref)

    @pl.when((j >= n_sec) & (j < 2 * n_sec))
    def _():
        qk_epilogue(kg_ref, k_ref)

    @pl.when(j >= 2 * n_sec)
    def _():
        v_ref[...] = _dot(h_scr[...], w_ref[...]).astype(v_ref.dtype)


def _qkv_proj(x, mods, norm_g, w, q_gain, k_gain, tables, *, cond_base, tok_per_cond, kv_dtype):
    t, d = x.shape
    tm = _tile(math.gcd(t, tok_per_cond), 1024, 8)
    tn = _tile(d, 512, LANES)
    n_sec = d // tn
    rope = tables is not None
    row = lambda i, j: cond_base + (i * tm) // tok_per_cond
    in_specs = [
        pl.BlockSpec((tm, d), lambda i, j: (i, 0)),
        mods.spec(_Mods.SHIFT1, row),
        mods.spec(_Mods.SCALE1, row),
        pl.BlockSpec((1, d), lambda i, j: (0, 0)),
        pl.BlockSpec((d, tn), lambda i, j: (0, j)),
        pl.BlockSpec((1, LANES), lambda i, j: (0, 0)),
        pl.BlockSpec((1, LANES), lambda i, j: (0, 0)),
    ]
    args = [x, mods.table, mods.table, norm_g, w, q_gain, k_gain]
    if rope:
        nblk = tables[0].shape[0] // tm
        for tab in tables:
            in_specs.append(pl.BlockSpec((tm, LANES), lambda i, j: (i % nblk, 0)))
            args.append(tab)
    out_specs = [
        pl.BlockSpec((tm, tn), lambda i, j, s=s: (i, jnp.clip(j - s * n_sec, 0, n_sec - 1)))
        for s in range(3)
    ]
    return pl.pallas_call(
        functools.partial(_qkv_kernel, rope=rope, tn=tn, n_sec=n_sec),
        grid=(t // tm, 3 * n_sec),
        in_specs=in_specs,
        out_specs=out_specs,
        out_shape=[jax.ShapeDtypeStruct((t, d), BF16), jax.ShapeDtypeStruct((t, d), kv_dtype),
                   jax.ShapeDtypeStruct((t, d), kv_dtype)],
        scratch_shapes=[pltpu.VMEM((tm, d), BF16)],
        compiler_params=_params(("arbitrary", "arbitrary")),
        name="qkv_proj_rope" if rope else "qkv_proj",
    )(*args)


def _rope_tables(seq):
    n_freq = LANES // 4
    pos = jnp.arange(seq, dtype=jnp.int32)
    r = (pos // GRID_W).astype(F32)
    col = (pos % GRID_W).astype(F32)
    freqs = ROPE_BASE ** (-jnp.arange(n_freq, dtype=F32) / n_freq)
    cos_t, sin_t = [], []
    for p in (r, col):
        ang = p[:, None] * freqs
        cos, sin = jnp.cos(ang), jnp.sin(ang)
        cos_t += [cos, cos]
        sin_t += [-sin, sin]
    return jnp.concatenate(cos_t, axis=1), jnp.concatenate(sin_t, axis=1)


def _attn_kernel(*refs, n_cache, n_chunks, tk, lambda_init, cast_blocks, heads_per_step):
    n_cast = len(cast_blocks)
    lam_ref, g_ref, q_ref = refs[:3]
    n_in = 3 + (4 if n_cache else 2)
    if n_cache:
        kc_ref, vtc_ref, k_ref, vt_ref = refs[3:n_in]
    else:
        k_ref, vt_ref = refs[3:n_in]
    cast_in = refs[n_in:n_in + n_cast]
    o_ref = refs[n_in + n_cast]
    cast_out = refs[n_in + n_cast + 1:n_in + 2 * n_cast + 1]
    acc_ref = refs[-1]

    step = (pl.program_id(0) * pl.num_programs(1) + pl.program_id(1)) * pl.num_programs(2) + pl.program_id(2)
    for src, dst, n_blocks in zip(cast_in, cast_out, cast_blocks):
        @pl.when(step < n_blocks)
        def _(src=src, dst=dst):
            dst[...] = src[...].astype(dst.dtype)

    vd = 2 * LANES
    for hq in range(heads_per_step):
        cols = slice(hq * vd, (hq + 1) * vd)
        _attn_head(lam_ref, g_ref, q_ref.at[:, cols],
                   kc_ref.at[:, cols] if n_cache else None, vtc_ref.at[hq] if n_cache else None,
                   k_ref.at[:, cols], vt_ref.at[hq], o_ref.at[:, cols], acc_ref,
                   n_chunks=n_chunks, tk=tk, lambda_init=lambda_init)


def _attn_head(lam_ref, g_ref, q_ref, kc_ref, vtc_ref, k_ref, vt_ref, o_ref, acc_ref, *,
               n_chunks, tk, lambda_init):
    q = q_ref[...]
    tq = q.shape[0]
    n_half = 2 if tq % (4 * LANES) == 0 else 1
    width = tq // n_half
    units = [(c, hf) for c in range(2) for hf in range(n_half)]
    qparts = {(c, hf): q[hf * width:(hf + 1) * width, c * LANES:(c + 1) * LANES] for c, hf in units}

    def scores(kblk):
        return [lax.dot_general(kblk[:, c * LANES:(c + 1) * LANES], qparts[c, hf],
                                (((1,), (1,)), ((), ())), preferred_element_type=F32)
                for c, hf in units]

    def accumulate(sts, vt, stats):
        out = []
        for u, (c, hf) in enumerate(units):
            st = sts[u]
            cols = slice(hf * width, (hf + 1) * width)
            m_cur = jnp.max(st, axis=0, keepdims=True)
            if stats is None:
                m_new = m_cur
                p = jnp.exp2(st - m_new)
                l_new = jnp.sum(p, axis=0, keepdims=True)
                acc_ref[c, :, cols] = _dot(vt, p.astype(BF16))
            else:
                m_prev, l_prev = stats[u]
                m_new = jnp.maximum(m_prev, m_cur)
                alpha = jnp.exp2(m_prev - m_new)
                p = jnp.exp2(st - m_new)
                l_new = alpha * l_prev + jnp.sum(p, axis=0, keepdims=True)
                acc_ref[c, :, cols] = alpha * acc_ref[c, :, cols] + _dot(vt, p.astype(BF16))
            out.append((m_new, l_new))
        return out

    chunks = []
    if kc_ref is not None:
        chunks.append((lambda: kc_ref[...].astype(BF16), lambda: vtc_ref[...]))
    for i in range(n_chunks):
        chunks.append((lambda i=i: k_ref[i * tk:(i + 1) * tk, :].astype(BF16), lambda i=i: vt_ref[i]))
    stats = None
    sts = scores(chunks[0][0]())
    for n, (_, values) in enumerate(chunks):
        nxt = scores(chunks[n + 1][0]()) if n + 1 < len(chunks) else None
        stats = accumulate(sts, values(), stats)
        sts = nxt

    lp = lam_ref[...]
    lam = (jnp.exp(jnp.sum(lp[0:1] * lp[1:2], axis=-1, keepdims=True))
           - jnp.exp(jnp.sum(lp[2:3] * lp[3:4], axis=-1, keepdims=True)) + lambda_init)
    sums = [jnp.concatenate([stats[c * n_half + hf][1] for hf in range(n_half)], axis=1)
            for c in range(2)]
    o_t = acc_ref[0] / sums[0] - lam * (acc_ref[1] / sums[1])
    o = o_t.T
    o = o * lax.rsqrt(jnp.mean(o * o, axis=-1, keepdims=True) + EPS) * g_ref[...]
    o_ref[...] = (o * (1.0 - lambda_init)).astype(o_ref.dtype)


def _chunked_transpose(v, batch, seq, heads, vd, tk):
    v = v.astype(BF16).reshape(batch, seq // tk, tk, heads, vd)
    return v.transpose(0, 3, 1, 4, 2).reshape(batch * heads, seq // tk, vd, tk)


def _cast_block_rows(rows, n_steps):
    blk = 16
    while rows % blk or rows // blk > n_steps:
        blk += 16
        if blk > rows:
            raise ValueError(f"no cast block for {rows} rows in {n_steps} steps")
    return blk


def _diff_attention(q, k, v, cache_k, cache_v, lam_params, subln_g, *, batch, seq, layer, casts=()):
    t, d = q.shape
    vd = subln_g.shape[-1]
    heads = d // vd
    n_cache = 0 if cache_k is None else cache_k.shape[0] // batch
    tq = _tile(seq, 512, LANES)
    tk = _tile(seq, 512, LANES)
    nq = seq // tq
    n_chunks = seq // tk
    hps = heads if seq <= tq else 2
    groups = heads // hps
    wide = hps * vd
    in_specs = [
        pl.BlockSpec(lam_params.shape, lambda b, h, i: (0, 0)),
        pl.BlockSpec((1, vd), lambda b, h, i: (0, 0)),
        pl.BlockSpec((tq, wide), lambda b, h, i: (b * nq + i, h)),
    ]
    args = [lam_params, subln_g, q]
    if n_cache:
        in_specs += [pl.BlockSpec((n_cache, wide), lambda b, h, i: (b, h)),
                     pl.BlockSpec((hps, None, vd, n_cache), lambda b, h, i: (b * groups + h, 0, 0, 0))]
        args += [cache_k, _chunked_transpose(cache_v, batch, n_cache, heads, vd, n_cache)]
    in_specs += [pl.BlockSpec((seq, wide), lambda b, h, i: (b, h)),
                 pl.BlockSpec((hps, n_chunks, vd, tk), lambda b, h, i: (b * groups + h, 0, 0, 0))]
    args += [k, _chunked_transpose(v, batch, seq, heads, vd, tk)]
    out_specs = [pl.BlockSpec((tq, wide), lambda b, h, i: (b * nq + i, h))]
    out_shape = [jax.ShapeDtypeStruct((t, d), BF16)]
    n_steps = batch * groups * nq
    cast_blocks = []
    for w in casts:
        rows, cols = w.shape
        blk = _cast_block_rows(rows, n_steps)
        n_blocks = rows // blk
        cast_blocks.append(n_blocks)
        spec = pl.BlockSpec((blk, cols), lambda b, h, i, n=n_blocks: (
            jnp.minimum((b * groups + h) * nq + i, n - 1), 0))
        in_specs.append(spec)
        args.append(w)
        out_specs.append(spec)
        out_shape.append(jax.ShapeDtypeStruct((rows, cols), BF16))
    res = pl.pallas_call(
        functools.partial(_attn_kernel, n_cache=n_cache, n_chunks=n_chunks, tk=tk,
                          lambda_init=_lambda_init(layer), cast_blocks=tuple(cast_blocks),
                          heads_per_step=hps),
        grid=(batch, groups, nq),
        in_specs=in_specs,
        out_specs=out_specs,
        out_shape=out_shape,
        scratch_shapes=[pltpu.VMEM((2, vd, tq), F32)],
        compiler_params=_params(("arbitrary", "arbitrary", "arbitrary")),
        name=f"diff_attention_{'latent' if n_cache else 'context'}",
    )(*args)
    return res[0], list(res[1:])


def _out_proj_kernel(op_ref, os_ref, w_ref, xp_ref, xs_ref, gate_ref, out_ref, *, n_prompt_tiles):
    is_prompt = pl.program_id(0) < n_prompt_tiles

    @pl.when(is_prompt)
    def _():
        out_ref[...] = xp_ref[...] + gate_ref[...] * _dot(op_ref[...], w_ref[...])

    @pl.when(jnp.logical_not(is_prompt))
    def _():
        out_ref[...] = xs_ref[...] + gate_ref[...] * _dot(os_ref[...], w_ref[...])


def _out_proj_residual(o_p, o_s, w, x_p, x_s, mods, *, tok_per_cond):
    n_prompt, d = x_p.shape
    t = n_prompt + x_s.shape[0]
    tm = _tile(math.gcd(n_prompt, tok_per_cond), 256, 8)
    tn = d
    n_pt = n_prompt // tm
    row = _cond_row_joint(tm, n_prompt, tok_per_cond)
    p_rows = lambda i: jnp.minimum(i, n_pt - 1)
    s_rows = lambda i: jnp.maximum(i - n_pt, 0)
    return pl.pallas_call(
        functools.partial(_out_proj_kernel, n_prompt_tiles=n_pt),
        grid=(t // tm, d // tn),
        in_specs=[
            pl.BlockSpec((tm, d), lambda i, j: (p_rows(i), 0)),
            pl.BlockSpec((tm, d), lambda i, j: (s_rows(i), 0)),
            pl.BlockSpec((d, tn), lambda i, j: (0, j)),
            pl.BlockSpec((tm, tn), lambda i, j: (p_rows(i), jnp.where(i < n_pt, j, d // tn - 1))),
            pl.BlockSpec((tm, tn), lambda i, j: (s_rows(i), jnp.where(i < n_pt, 0, j))),
            mods.spec(_Mods.GATE1, row, width=tn, col_fn=lambda i, j: j),
        ],
        out_specs=pl.BlockSpec((tm, tn), lambda i, j: (i, j)),
        out_shape=jax.ShapeDtypeStruct((t, d), F32),
        compiler_params=_params(("arbitrary", "arbitrary")),
        name="attn_out_proj",
    )(o_p, o_s, w, x_p, x_s, mods.table)


def _ffn_kernel(x_ref, sh_ref, sc_ref, gate_ref, g_ref, wg_ref, wu_ref, wd_ref, o_ref, h_scr):
    f = pl.program_id(1)

    def swiglu_part(h):
        gte = _dot(h, wg_ref[...])
        up = _dot(h, wu_ref[...])
        act = (gte * jax.nn.sigmoid(gte) * up).astype(BF16)
        return _dot(act, wd_ref[...])

    @pl.when(f == 0)
    def _():
        h = _norm_modulate(x_ref[...], g_ref[...], sc_ref[...], sh_ref[...]).astype(BF16)
        h_scr[...] = h
        o_ref[...] = swiglu_part(h)

    @pl.when(f > 0)
    def _():
        o_ref[...] += swiglu_part(h_scr[...])

    @pl.when(f == pl.num_programs(1) - 1)
    def _():
        o_ref[...] = x_ref[...] + gate_ref[...] * o_ref[...]


def _cond_row_joint(tm, n_prompt, tok_per_cond):
    def row(i, *_):
        tok = i * tm
        return jnp.where(tok < n_prompt, 0, 1 + (tok - n_prompt) // tok_per_cond)
    return row


def _dense_ffn(x, mods, norm_g, w_gu, w_down, *, n_prompt, tok_per_cond):
    t, d = x.shape
    ff = w_down.shape[0]
    tm = _tile(math.gcd(n_prompt, tok_per_cond), 1024, 8)
    tf = _tile(ff, 256, LANES)
    nf = ff // tf
    row = _cond_row_joint(tm, n_prompt, tok_per_cond)
    return pl.pallas_call(
        _ffn_kernel,
        grid=(t // tm, nf),
        in_specs=[
            pl.BlockSpec((tm, d), lambda i, f: (i, 0)),
            mods.spec(_Mods.SHIFT2, row),
            mods.spec(_Mods.SCALE2, row),
            mods.spec(_Mods.GATE2, row),
            pl.BlockSpec((1, d), lambda i, f: (0, 0)),
            pl.BlockSpec((d, tf), lambda i, f: (0, f)),
            pl.BlockSpec((d, tf), lambda i, f: (0, nf + f)),
            pl.BlockSpec((tf, d), lambda i, f: (f, 0)),
        ],
        out_specs=pl.BlockSpec((tm, d), lambda i, f: (i, 0)),
        out_shape=jax.ShapeDtypeStruct((t, d), F32),
        scratch_shapes=[pltpu.VMEM((tm, d), BF16)],
        compiler_params=_params(("arbitrary", "arbitrary")),
        name="dense_ffn",
    )(x, mods.table, mods.table, mods.table, norm_g, w_gu, w_gu, w_down)


def _pool_kernel(x_ref, xp_ref, xn_ref, sh_ref, sc_ref, gate_ref, g_ref, w_ref, ps_ref, o_ref, *,
                 tp, halo, n_prompt_tiles, prompt_tiles_per_seq, sample_tiles_per_seq, group_dim):
    i = pl.program_id(0)
    is_prompt = i < n_prompt_tiles
    tiles_per_seq = jnp.where(is_prompt, prompt_tiles_per_seq, sample_tiles_per_seq)
    local = lax.rem(jnp.where(is_prompt, i, i - n_prompt_tiles), tiles_per_seq)
    first = local == 0
    last = local == tiles_per_seq - 1
    seq_len = tiles_per_seq * tp

    g, sc, sh = g_ref[...], sc_ref[...], sh_ref[...]
    x = x_ref[...]
    h = _norm_modulate(x, g, sc, sh)
    h_prev = jnp.where(first, 0.0, _norm_modulate(xp_ref[...], g, sc, sh))
    h_next = jnp.where(last, 0.0, _norm_modulate(xn_ref[...], g, sc, sh))
    d = h.shape[-1]
    h_halo = jnp.concatenate([h_prev, h_next, jnp.zeros((LANES - 2 * halo, d), F32)], axis=0)

    t_idx = lax.broadcasted_iota(jnp.int32, (tp, tp), 0)
    s_idx = lax.broadcasted_iota(jnp.int32, (tp, tp), 1)
    t_h = lax.broadcasted_iota(jnp.int32, (tp, LANES), 0)
    u_h = lax.broadcasted_iota(jnp.int32, (tp, LANES), 1)
    off_h = jnp.where(u_h < halo, u_h - halo, tp + u_h - halo)
    pos = local * tp + lax.broadcasted_iota(jnp.int32, (tp, 1), 0)

    for grp, win in enumerate(POOL_WINDOWS):
        lo, hi = win // 2, win // 2 - 1
        cols = slice(grp * group_dim, (grp + 1) * group_dim)
        band = _mask_bf16((s_idx >= t_idx - lo) & (s_idx <= t_idx + hi))
        band_h = _mask_bf16((u_h < 2 * halo) & (off_h >= t_h - lo) & (off_h <= t_h + hi))
        total = _dot(band, h[:, cols].astype(BF16)) + _dot(band_h, h_halo[:, cols].astype(BF16))
        cnt = (jnp.minimum(pos + hi, seq_len - 1) - jnp.maximum(pos - lo, 0) + 1).astype(F32)
        pooled = (total / cnt - h[:, cols]).astype(BF16)
        y = _dot(pooled, w_ref[grp]) * ps_ref[:, cols]
        o_ref[:, cols] = x[:, cols] + gate_ref[:, cols] * y


def _pool_mixer(x, mods, norm_g, pool_w, pool_scale, *, n_prompt, prompt_seq, sample_seq):
    t, d = x.shape
    halo = max(POOL_WINDOWS) // 2
    tp = _tile(math.gcd(prompt_seq, sample_seq), 256, 8)
    n_groups, group_dim = pool_w.shape[0], pool_w.shape[1]
    hb = tp // halo
    n_halo_blocks = t // halo
    row = _cond_row_joint(tp, n_prompt, sample_seq)
    return pl.pallas_call(
        functools.partial(_pool_kernel, tp=tp, halo=halo, n_prompt_tiles=n_prompt // tp,
                          prompt_tiles_per_seq=prompt_seq // tp,
                          sample_tiles_per_seq=sample_seq // tp, group_dim=group_dim),
        grid=(t // tp,),
        in_specs=[
            pl.BlockSpec((tp, d), lambda i: (i, 0)),
            pl.BlockSpec((halo, d), lambda i: (jnp.maximum(i * hb - 1, 0), 0)),
            pl.BlockSpec((halo, d), lambda i: (jnp.minimum((i + 1) * hb, n_halo_blocks - 1), 0)),
            mods.spec(_Mods.SHIFT1, row),
            mods.spec(_Mods.SCALE1, row),
            mods.spec(_Mods.GATE1, row),
            pl.BlockSpec((1, d), lambda i: (0, 0)),
            pl.BlockSpec((n_groups, group_dim, group_dim), lambda i: (0, 0, 0)),
            pl.BlockSpec((1, d), lambda i: (0, 0)),
        ],
        out_specs=pl.BlockSpec((tp, d), lambda i: (i, 0)),
        out_shape=jax.ShapeDtypeStruct((t, d), F32),
        compiler_params=_params(("arbitrary",)),
        name="pool_mixer",
    )(x, x, x, mods.table, mods.table, mods.table, norm_g, pool_w, pool_scale)


def _router_kernel(x_ref, sh_ref, sc_ref, g_ref, wr_ref, br_ref, route_ref, cnt_ref):
    h = _norm_modulate(x_ref[...], g_ref[...], sc_ref[...], sh_ref[...])
    h_hi, h_lo = _split_bf16(h)
    w_hi, w_lo = _split_bf16(wr_ref[...])
    logits = _dot(h_hi, w_hi) + _dot(h_lo, w_hi) + _dot(h_hi, w_lo) + br_ref[...]

    rows = logits.shape[0]
    lane = lax.broadcasted_iota(jnp.int32, logits.shape, 1).astype(F32)
    m1 = jnp.max(logits, axis=-1, keepdims=True)
    i1 = jnp.min(jnp.where(logits == m1, lane, float(LANES)), axis=-1, keepdims=True)
    oh1 = lane == i1
    rest = jnp.where(oh1, -jnp.inf, logits)
    m2 = jnp.max(rest, axis=-1, keepdims=True)
    i2 = jnp.min(jnp.where(rest == m2, lane, float(LANES)), axis=-1, keepdims=True)
    oh2 = lane == i2
    e = jnp.exp(m2 - m1)
    gate_a = 1.0 / (1.0 + e)
    gate_b = e / (1.0 + e)

    sel = jnp.where(oh1 | oh2, 1.0, 0.0)
    r_idx = lax.broadcasted_iota(jnp.int32, (rows, rows), 0)
    c_idx = lax.broadcasted_iota(jnp.int32, (rows, rows), 1)
    earlier = _mask_bf16(c_idx < r_idx)
    rank = _dot(earlier, sel.astype(BF16))
    rank_a = jnp.sum(jnp.where(oh1, rank, 0.0), axis=-1, keepdims=True)
    rank_b = jnp.sum(jnp.where(oh2, rank, 0.0), axis=-1, keepdims=True)

    route = jnp.zeros(logits.shape, F32)
    for k, val in enumerate((i1, i2, rank_a, rank_b, gate_a, gate_b)):
        route = jnp.where(lane == float(k), val, route)
    route_ref[...] = route
    cnt_ref[...] = jnp.sum(sel, axis=0, keepdims=True)


def _moe_router(x, mods, norm_g, w_router, b_router, *, n_prompt, tok_per_cond):
    t, d = x.shape
    n_exp = w_router.shape[1]
    ch = ROUTE_CHUNK
    wr = jnp.zeros((d, LANES), F32).at[:, :n_exp].set(w_router.astype(F32))
    br = jnp.full((1, LANES), -jnp.inf, F32).at[0, :n_exp].set(b_router.astype(F32))
    row = _cond_row_joint(ch, n_prompt, tok_per_cond)
    return pl.pallas_call(
        _router_kernel,
        grid=(t // ch,),
        in_specs=[
            pl.BlockSpec((ch, d), lambda i: (i, 0)),
            mods.spec(_Mods.SHIFT2, row),
            mods.spec(_Mods.SCALE2, row),
            pl.BlockSpec((1, d), lambda i: (0, 0)),
            pl.BlockSpec((d, LANES), lambda i: (0, 0)),
            pl.BlockSpec((1, LANES), lambda i: (0, 0)),
        ],
        out_specs=[
            pl.BlockSpec((ch, LANES), lambda i: (i, 0)),
            pl.BlockSpec((None, 1, LANES), lambda i: (i, 0, 0)),
        ],
        out_shape=[
            jax.ShapeDtypeStruct((t, LANES), F32),
            jax.ShapeDtypeStruct((t // ch, 1, LANES), F32),
        ],
        compiler_params=_params(("arbitrary",)),
        name="moe_router",
    )(x, mods.table, mods.table, norm_g, wr, br)


def _dispatch_plan(route, counts, n_exp):
    ch, gr = ROUTE_CHUNK, GEMM_ROWS
    t = route.shape[0]
    n_chunks = t // ch
    i32 = jnp.int32
    cnt = counts[:, 0, :n_exp].astype(i32)
    total = cnt.sum(0)
    padded = ((total + gr - 1) // gr) * gr
    off = jnp.cumsum(padded) - padded
    start = off[None, :] + jnp.cumsum(cnt, axis=0) - cnt

    def dest(choice):
        e = route[:, choice].astype(i32).reshape(n_chunks, ch, 1)
        hit = e == jnp.arange(n_exp, dtype=i32)
        base = jnp.sum(jnp.where(hit, start[:, None, :], 0), axis=-1)
        return base.reshape(t) + route[:, 2 + choice].astype(i32)

    dest_a, dest_b = dest(0), dest(1)

    n_gemm_tiles = (t * TOP_K) // gr + n_exp

    tile_lo = jnp.arange(n_gemm_tiles, dtype=i32) * gr
    used = tile_lo < (off + padded)[-1]
    ends = off + padded
    expert_of = jnp.minimum(jnp.sum(tile_lo[:, None] >= ends[None, :], axis=1), n_exp - 1).astype(i32)
    n_used = jnp.maximum(used.sum(), 1)
    clamp = jnp.minimum(jnp.arange(n_gemm_tiles, dtype=i32), n_used - 1).astype(i32)
    token_rows = jnp.clip((off + total)[expert_of] - tile_lo, 0, gr).astype(i32)
    gemm_list = (clamp, expert_of[clamp], used.astype(i32), token_rows[clamp])

    tail = jnp.arange(n_gemm_tiles - n_exp, n_gemm_tiles, dtype=i32)
    zero_rows = jnp.concatenate([jnp.maximum(ends - gr, 0), tail * gr]).astype(i32)
    zero_ok = jnp.concatenate([padded > 0, tail >= used.sum()]).astype(i32)

    dest_rows = jnp.stack([dest_a.reshape(n_chunks, ch), dest_b.reshape(n_chunks, ch)], axis=1)
    gates = jnp.concatenate([route[:, 4:6], jnp.zeros((t, 6), F32)], axis=1)
    return dest_rows, gates, gemm_list, (zero_rows, zero_ok), n_gemm_tiles * gr


def _wait_rows(src, dst, sem, n_rows):
    pltpu.make_async_copy(src.at[pl.ds(0, n_rows), :], dst.at[pl.ds(0, n_rows), :], sem).wait()


def _scatter_kernel(zrow_ref, zok_ref, dest_ref, x_ref, sh_ref, sc_ref, g_ref, out_hbm,
                    h_scr, z_scr, sem, zsem, *, tile_rows):
    ch = x_ref.shape[0]
    step, n_steps = pl.program_id(0), pl.num_programs(0)
    slot = lax.rem(step, 2)

    @pl.when(step == 0)
    def _():
        z_scr[...] = jnp.zeros(z_scr.shape, F32)
        zr = z_scr.shape[0]

        def zero_copy(n, part):
            row0 = pl.multiple_of(zrow_ref[n] + part * zr, zr)
            return pltpu.make_async_copy(z_scr, out_hbm.at[pl.ds(row0, zr), :], zsem)

        for n in range(zrow_ref.shape[0]):
            @pl.when(zok_ref[n] == 1)
            def _(n=n):
                for part in range(tile_rows // zr):
                    zero_copy(n, part).start()
        for n in range(zrow_ref.shape[0]):
            @pl.when(zok_ref[n] == 1)
            def _(n=n):
                for part in range(tile_rows // zr):
                    zero_copy(n, part).wait()

    h_scr[slot] = _norm_modulate(x_ref[...], g_ref[...], sc_ref[...], sh_ref[...])

    src_rows = h_scr.at[slot]
    for j in range(ch):
        for k in range(TOP_K):
            pltpu.make_async_copy(src_rows.at[pl.ds(j, 1), :],
                                  out_hbm.at[pl.ds(dest_ref[k, j], 1), :],
                                  sem.at[slot]).start(priority=k)

    def wait_slot(s):
        for _ in range(TOP_K):
            _wait_rows(h_scr.at[s], out_hbm, sem.at[s], ch)

    @pl.when(step > 0)
    def _():
        wait_slot(1 - slot)

    @pl.when(step == n_steps - 1)
    def _():
        wait_slot(slot)


def _moe_scatter(x, mods, norm_g, dest_rows, zero_tiles, n_rows, *, n_prompt, tok_per_cond):
    t, d = x.shape
    ch = ROUTE_CHUNK
    row = _cond_row_joint(ch, n_prompt, tok_per_cond)
    grid_spec = pltpu.PrefetchScalarGridSpec(
        num_scalar_prefetch=2,
        grid=(t // ch,),
        in_specs=[
            pl.BlockSpec((None, TOP_K, ch), lambda i, *_: (i, 0, 0), memory_space=pltpu.SMEM),
            pl.BlockSpec((ch, d), lambda i, *_: (i, 0)),
            mods.spec(_Mods.SHIFT2, row),
            mods.spec(_Mods.SCALE2, row),
            pl.BlockSpec((1, d), lambda i, *_: (0, 0)),
        ],
        out_specs=pl.BlockSpec(memory_space=pl.ANY),
        scratch_shapes=[pltpu.VMEM((2, ch, d), F32), pltpu.VMEM((ch, d), F32),
                        pltpu.SemaphoreType.DMA((2,)), pltpu.SemaphoreType.DMA(())],
    )
    return pl.pallas_call(
        functools.partial(_scatter_kernel, tile_rows=GEMM_ROWS),
        grid_spec=grid_spec,
        out_shape=jax.ShapeDtypeStruct((n_rows, d), F32),
        compiler_params=_params(("arbitrary",)),
        name="moe_scatter",
    )(*zero_tiles, dest_rows, x, mods.table, mods.table, norm_g)


def _expert_kernel(blk_ref, exp_ref, used_ref, rows_ref, x_ref, wg_ref, wu_ref, wd_ref, o_ref, x_scr, *,
                   sub_rows):
    r, f = pl.program_id(0), pl.program_id(1)
    tile_rows = o_ref.shape[0]

    def swiglu_part(x):
        gte = _dot(x, wg_ref[...])
        up = _dot(x, wu_ref[...])
        act = (gte * jax.nn.sigmoid(gte) * up).astype(BF16)
        return _dot(act, wd_ref[...])

    @pl.when(used_ref[r] == 1)
    def _():
        valid = rows_ref[r]
        full = valid > tile_rows - sub_rows

        @pl.when(full & (f == 0))
        def _():
            x = x_ref[...].astype(BF16)
            x_scr[...] = x
            o_ref[...] = swiglu_part(x)

        @pl.when(full & (f > 0))
        def _():
            o_ref[...] += swiglu_part(x_scr[...])

        @pl.when(jnp.logical_not(full))
        def _():
            @pl.when(f == 0)
            def _():
                o_ref[...] = jnp.zeros(o_ref.shape, F32)
                x_scr[...] = x_ref[...].astype(BF16)

            for sb in range(tile_rows // sub_rows - 1):
                @pl.when(valid > sb * sub_rows)
                def _(sb=sb):
                    rows = slice(sb * sub_rows, (sb + 1) * sub_rows)
                    o_ref[rows, :] += swiglu_part(x_scr[rows, :])


def _moe_experts(xs, w_gu, w_down, gemm_list):
    n_rows, d = xs.shape
    ff = w_down.shape[1]
    gr = GEMM_ROWS
    tf = _tile(ff, 256, LANES)
    nf = ff // tf
    n_tiles = gemm_list[0].shape[0]

    def fcol(f, used, r):
        return jnp.where(used[r] == 1, f, nf - 1)

    grid_spec = pltpu.PrefetchScalarGridSpec(
        num_scalar_prefetch=4,
        grid=(n_tiles, nf),
        in_specs=[
            pl.BlockSpec((gr, d), lambda r, f, blk, ex, used, rows: (blk[r], 0)),
            pl.BlockSpec((None, d, tf), lambda r, f, blk, ex, used, rows: (ex[r], 0, fcol(f, used, r))),
            pl.BlockSpec((None, d, tf), lambda r, f, blk, ex, used, rows: (ex[r], 0, nf + fcol(f, used, r))),
            pl.BlockSpec((None, tf, d), lambda r, f, blk, ex, used, rows: (ex[r], fcol(f, used, r), 0)),
        ],
        out_specs=pl.BlockSpec((gr, d), lambda r, f, blk, ex, used, rows: (blk[r], 0)),
        scratch_shapes=[pltpu.VMEM((gr, d), BF16)],
    )
    return pl.pallas_call(
        functools.partial(_expert_kernel, sub_rows=ROUTE_CHUNK),
        grid_spec=grid_spec,
        out_shape=jax.ShapeDtypeStruct((n_rows, d), F32),
        input_output_aliases={4: 0},
        compiler_params=_params(("arbitrary", "arbitrary")),
        name="moe_experts",
    )(*gemm_list, xs, w_gu, w_gu, w_down)


def _combine_kernel(dest_ref, dest_next_ref, gates_ref, x_ref, gate_ref, y_hbm, op_ref, os_ref,
                    y_scr, sem, *, n_prompt_chunks):
    ch = x_ref.shape[0]
    step, n_steps = pl.program_id(0), pl.num_programs(0)
    slot = lax.rem(step, 2)

    def gather(idx_ref, s):
        for j in range(ch):
            for k in range(TOP_K):
                pltpu.make_async_copy(y_hbm.at[pl.ds(idx_ref[k, j], 1), :],
                                      y_scr.at[s, k, pl.ds(j, 1), :], sem.at[s]).start(priority=k)

    @pl.when(step == 0)
    def _():
        gather(dest_ref, slot)

    @pl.when(step + 1 < n_steps)
    def _():
        gather(dest_next_ref, 1 - slot)

    for k in range(TOP_K):
        _wait_rows(y_hbm, y_scr.at[slot, k], sem.at[slot], ch)

    gates = gates_ref[...]
    mix = gates[:, 0:1] * y_scr[slot, 0] + gates[:, 1:2] * y_scr[slot, 1]
    out = x_ref[...] + gate_ref[...] * mix
    is_prompt = step < n_prompt_chunks

    @pl.when(is_prompt)
    def _():
        op_ref[...] = out

    @pl.when(jnp.logical_not(is_prompt))
    def _():
        os_ref[...] = out


def _moe_combine(y, dest_rows, gates, x, mods, *, n_prompt, tok_per_cond):
    t, d = x.shape
    ch = ROUTE_CHUNK
    n_pc = n_prompt // ch
    row = _cond_row_joint(ch, n_prompt, tok_per_cond)
    return pl.pallas_call(
        functools.partial(_combine_kernel, n_prompt_chunks=n_pc),
        grid=(t // ch,),
        in_specs=[
            pl.BlockSpec((None, TOP_K, ch), lambda i: (i, 0, 0), memory_space=pltpu.SMEM),
            pl.BlockSpec((None, TOP_K, ch), lambda i: (jnp.minimum(i + 1, t // ch - 1), 0, 0),
                         memory_space=pltpu.SMEM),
            pl.BlockSpec((ch, gates.shape[1]), lambda i: (i, 0)),
            pl.BlockSpec((ch, d), lambda i: (i, 0)),
            mods.spec(_Mods.GATE2, row),
            pl.BlockSpec(memory_space=pl.ANY),
        ],
        out_specs=[
            pl.BlockSpec((ch, d), lambda i: (jnp.minimum(i, n_pc - 1), 0)),
            pl.BlockSpec((ch, d), lambda i: (jnp.maximum(i - n_pc, 0), 0)),
        ],
        out_shape=[jax.ShapeDtypeStruct((n_prompt, d), F32), jax.ShapeDtypeStruct((t - n_prompt, d), F32)],
        scratch_shapes=[pltpu.VMEM((2, TOP_K, ch, d), F32), pltpu.SemaphoreType.DMA((2,))],
        compiler_params=_params(("arbitrary",)),
        name="moe_combine",
    )(dest_rows, dest_rows, gates, x, mods.table, y)


def kernel(x_prompt, x_sample, c, cache_k, cache_v, c_ctx, ada_w, ada_b, norm1_g, norm2_g,
           attn_w_qkv, attn_w_o, attn_q_norm, attn_k_norm, attn_lambda, attn_subln_g,
           pool_w, pool_scale, ffn_w_gu, ffn_w_down,
           moe_w_router, moe_b_router, moe_w_gu, moe_w_down):
    b_ctx, l_ctx, d = x_prompt.shape
    b_dec, l_dec, _ = x_sample.shape
    depth = ada_w.shape[0]
    l_past, heads, v_dim = cache_k.shape[2:]
    head_dim = v_dim // 2
    assert head_dim == LANES and depth == 2 and l_dec % GRID_W == 0
    n_exp = moe_w_router.shape[-1]
    tp_, ts_ = b_ctx * l_ctx, b_dec * l_dec

    cond_rows = 8 * ((1 + b_dec + 7) // 8)
    cond = jnp.zeros((cond_rows, d), F32).at[0].set(c_ctx).at[1:1 + b_dec].set(c)
    mod_table = _adaln_mods(cond, ada_w, ada_b).reshape(depth * cond_rows * 6, 1, d)

    xp = x_prompt.reshape(tp_, d)
    xs = x_sample.reshape(ts_, d)

    layer, j = 0, 0
    mods = _Mods(mod_table, cond_rows, layer)
    g1 = norm1_g[layer].reshape(1, d)
    w_qkv = attn_w_qkv[j].astype(BF16)
    q_gain = (attn_q_norm[j] * (head_dim ** -0.5 * LOG2E)).reshape(1, head_dim)
    k_gain = attn_k_norm[j].reshape(1, head_dim)
    tables = _rope_tables(l_dec)
    prompt = dict(cond_base=0, tok_per_cond=tp_)
    sample = dict(cond_base=1, tok_per_cond=l_dec)

    q_p, k_p, v_p = _qkv_proj(xp, mods, g1, w_qkv, q_gain, k_gain, None, kv_dtype=F32, **prompt)
    q_s, k_s, v_s = _qkv_proj(xs, mods, g1, w_qkv, q_gain, k_gain, tables, kv_dtype=BF16, **sample)

    lam_params = attn_lambda[j].astype(F32)
    subln = attn_subln_g[j].reshape(1, v_dim)
    o_p, _ = _diff_attention(q_p, k_p, v_p, None, None, lam_params, subln,
                             batch=b_ctx, seq=l_ctx, layer=layer)
    ck = cache_k[:, j].reshape(b_dec * l_past, heads * v_dim)
    cv = cache_v[:, j].reshape(b_dec * l_past, heads * v_dim)
    later_weights = [attn_w_o[j], ffn_w_gu[j], ffn_w_down[j], pool_w[0], moe_w_gu[0], moe_w_down[0]]
    o_s, later_bf16 = _diff_attention(q_s, k_s, v_s, ck, cv, lam_params, subln,
                                      batch=b_dec, seq=l_dec, layer=layer,
                                      casts=[w.reshape(-1, w.shape[-1]) for w in later_weights])
    w_o, w_ffn_gu, w_ffn_down, w_pool, w_moe_gu, w_moe_down = [
        b.reshape(w.shape) for b, w in zip(later_bf16, later_weights)]

    x1 = _out_proj_residual(o_p, o_s, w_o, xp, xs, mods, tok_per_cond=l_dec)

    x2 = _dense_ffn(x1, mods, norm2_g[layer].reshape(1, d), w_ffn_gu, w_ffn_down,
                    n_prompt=tp_, tok_per_cond=l_dec)

    layer, j = 1, 0
    mods = _Mods(mod_table, cond_rows, layer)
    x3 = _pool_mixer(x2, mods, norm1_g[layer].reshape(1, d), w_pool,
                     pool_scale[j].reshape(1, d), n_prompt=tp_, prompt_seq=l_ctx, sample_seq=l_dec)

    g2 = norm2_g[layer].reshape(1, d)
    joint = dict(n_prompt=tp_, tok_per_cond=l_dec)
    route, counts = _moe_router(x3, mods, g2, moe_w_router[j], moe_b_router[j], **joint)
    dest_rows, gates, gemm_list, zero_tiles, n_rows = _dispatch_plan(route, counts, n_exp)
    rows_sorted = _moe_scatter(x3, mods, g2, dest_rows, zero_tiles, n_rows, **joint)
    y_sorted = _moe_experts(rows_sorted, w_moe_gu, w_moe_down, gemm_list)
    y_p, y_s = _moe_combine(y_sorted, dest_rows, gates, x3, mods, **joint)

    y_prompt = y_p.reshape(b_ctx, l_ctx, d)
    y_sample = y_s.reshape(b_dec, l_dec, d)
    state_k = k_p.reshape(b_ctx, 1, l_ctx, heads, v_dim)
    state_v = v_p.reshape(b_ctx, 1, l_ctx, heads, v_dim)
    return (y_prompt, y_sample, state_k, state_v)
```
